```python
import jax, jax.numpy as jnp
from jax import lax
import numpy as np

D_MODEL = 1024
BATCH = 8
SEQ = 16384
DEPTH = 4

N_META = 16
D_MIX = D_MODEL
CONV_CH = D_MIX // 2
CONV_GROUPS = 8
CONV_WIDTH = 31
DN_HEADS = 4
DN_HEAD_DIM = 128
DN_KEY = DN_HEADS * DN_HEAD_DIM
DN_VAL = DN_HEADS * DN_HEAD_DIM
DN_SHORT_CONV = 4
CHUNK = 64
D_FF = -(-8 * D_MODEL // (3 * 256)) * 256
IN_SIZES = [CONV_CH, CONV_CH, DN_KEY, DN_KEY, DN_VAL, DN_VAL, DN_HEADS, DN_HEADS]
D_IN = sum(IN_SIZES)
IN_SPLITS = [int(s) for s in np.cumsum(IN_SIZES)[:-1]]
NORM_EPS = 1e-6
LN_EPS = 1e-5

kernel_name = "hymba_conformer_gated_deltanet_hybrid"


def rmsnorm(x, w, eps=NORM_EPS):
    xf = x.astype(jnp.float32)
    y = xf * lax.rsqrt(jnp.mean(xf * xf, axis=-1, keepdims=True) + eps)
    return (y * w.astype(jnp.float32)).astype(x.dtype)


def layernorm(x, w, b, eps=LN_EPS):
    xf = x.astype(jnp.float32)
    mu = jnp.mean(xf, axis=-1, keepdims=True)
    var = jnp.mean(jnp.square(xf - mu), axis=-1, keepdims=True)
    y = (xf - mu) * lax.rsqrt(var + eps) * w.astype(jnp.float32) + b.astype(jnp.float32)
    return y.astype(x.dtype)


def l2norm(x, eps=1e-6):
    return x * lax.rsqrt(jnp.sum(x * x, axis=-1, keepdims=True) + eps)


def causal_depthwise_conv(x, w):
    K, C = w.shape
    xp = jnp.pad(x, ((0, 0), (K - 1, 0), (0, 0)))
    return lax.conv_general_dilated(xp, w[:, None, :].astype(x.dtype), window_strides=(1,), padding='VALID',
                                    dimension_numbers=('NWC', 'WIO', 'NWC'), feature_group_count=C)


def conformer_conv_module(h_val, h_gate, dw_w, dw_b, ln_w, ln_b):
    u = h_val * jax.nn.sigmoid(h_gate)
    u = causal_depthwise_conv(u, dw_w) + dw_b
    u = layernorm(u, ln_w, ln_b)
    return jax.nn.silu(u)


def chunk_gated_delta_rule(q, k, v, g, beta):
    Bsz, L, H, dk = k.shape
    dv = v.shape[-1]
    front = CHUNK - N_META
    back = (-(L + front)) % CHUNK

    def to_chunks(t):
        t = jnp.pad(t, [(0, 0), (front, back)] + [(0, 0)] * (t.ndim - 2))
        t = jnp.moveaxis(t, 2, 1)
        return t.reshape((Bsz, H, -1, CHUNK) + t.shape[3:])

    q, k, v, g, beta = (to_chunks(t) for t in (q, k, v, g, beta))
    g = jnp.cumsum(g, axis=-1)
    causal = jnp.tril(jnp.ones((CHUNK, CHUNK), dtype=bool))
    strict = jnp.tril(jnp.ones((CHUNK, CHUNK), dtype=bool), -1)
    decay = jnp.exp(jnp.where(causal, g[..., :, None] - g[..., None, :], -jnp.inf))

    k_beta = k * beta[..., None]
    v_beta = v * beta[..., None]
    lower = jnp.where(strict, jnp.einsum('bhncd,bhnsd->bhncs', k_beta, k) * decay, 0.0)
    a_mat = lower + jnp.eye(CHUNK, dtype=lower.dtype)
    solve = lambda rhs: lax.linalg.triangular_solve(a_mat, rhs, left_side=True, lower=True, unit_diagonal=True)
    u = solve(v_beta)
    w = solve(k_beta * jnp.exp(g)[..., None])

    attn_intra = jnp.where(causal, jnp.einsum('bhncd,bhnsd->bhncs', q, k) * decay, 0.0)
    q_dec = q * jnp.exp(g)[..., None]
    k_dec = k * jnp.exp(g[..., -1:] - g)[..., None]
    g_last = jnp.exp(g[..., -1])

    def step(S, xs):
        q_i, k_i, u_i, w_i, a_i, gl_i = xs
        v_new = u_i - jnp.einsum('bhcd,bhdv->bhcv', w_i, S)
        o_i = jnp.einsum('bhcd,bhdv->bhcv', q_i, S) + jnp.einsum('bhcs,bhsv->bhcv', a_i, v_new)
        S = S * gl_i[..., None, None] + jnp.einsum('bhcd,bhcv->bhdv', k_i, v_new)
        return S, o_i

    xs = (jnp.moveaxis(q_dec, 2, 0), jnp.moveaxis(k_dec, 2, 0), jnp.moveaxis(u, 2, 0),
          jnp.moveaxis(w, 2, 0), jnp.moveaxis(attn_intra, 2, 0), jnp.moveaxis(g_last, 2, 0))
    S0 = jnp.zeros((Bsz, H, dk, dv), jnp.float32)
    _, o = lax.scan(step, S0, xs)
    o = jnp.transpose(o, (1, 0, 3, 2, 4)).reshape(Bsz, -1, H, dv)
    return o[:, front:front + L]


def gated_deltanet(q, k, v, z, b, a, conv_w, A_log, dt_bias, norm_w):
    Bsz, L, _ = q.shape
    dtype = v.dtype
    qkv = jax.nn.silu(causal_depthwise_conv(jnp.concatenate([q, k, v], axis=-1), conv_w))
    q, k, v = jnp.split(qkv.astype(jnp.float32), [DN_KEY, 2 * DN_KEY], axis=-1)
    q = l2norm(q.reshape(Bsz, L, DN_HEADS, DN_HEAD_DIM)) * (DN_HEAD_DIM ** -0.5)
    k = l2norm(k.reshape(Bsz, L, DN_HEADS, DN_HEAD_DIM))
    v = v.reshape(Bsz, L, DN_HEADS, DN_HEAD_DIM)
    beta = jax.nn.sigmoid(b.astype(jnp.float32))
    g = -jnp.exp(A_log.astype(jnp.float32)) * jax.nn.softplus(a.astype(jnp.float32) + dt_bias.astype(jnp.float32))
    o = chunk_gated_delta_rule(q, k, v, g, beta)
    o = o * lax.rsqrt(jnp.mean(o * o, axis=-1, keepdims=True) + NORM_EPS) * norm_w.astype(jnp.float32)
    o = o * jax.nn.silu(z.astype(jnp.float32).reshape(Bsz, L, DN_HEADS, DN_HEAD_DIM))
    return o.reshape(Bsz, L, DN_VAL).astype(dtype)


def _fwd_setup_inputs(seed: int = 0) -> dict:
    key = jax.random.key(seed)
    ks = jax.random.split(key, 20)
    f = jnp.float32
    nrm = lambda k_, shape: jax.random.normal(k_, shape, f)
    dt = jnp.exp(jax.random.uniform(ks[10], (DEPTH, DN_HEADS), f, np.log(1e-3), np.log(1e-1)))
    return {
        "x": nrm(ks[0], (BATCH, SEQ, D_MODEL)),
        "meta_tokens": nrm(ks[1], (N_META, D_MODEL)),
        "norm_mix_w": 1.0 + 0.02 * nrm(ks[2], (DEPTH, D_MODEL)),
        "w_in": nrm(ks[3], (DEPTH, D_MODEL, D_IN)) * D_MODEL ** -0.5,
        "conv_dw_w": nrm(ks[4], (DEPTH, CONV_WIDTH, CONV_CH)) * CONV_WIDTH ** -0.5,
        "conv_dw_b": 0.02 * nrm(ks[5], (DEPTH, CONV_CH)),
        "conv_ln_w": 1.0 + 0.02 * nrm(ks[6], (DEPTH, CONV_CH)),
        "conv_ln_b": 0.02 * nrm(ks[7], (DEPTH, CONV_CH)),
        "dn_conv_w": nrm(ks[8], (DEPTH, DN_SHORT_CONV, 2 * DN_KEY + DN_VAL)) * DN_SHORT_CONV ** -0.5,
        "dn_A_log": jnp.log(jax.random.uniform(ks[9], (DEPTH, DN_HEADS), f, 1.0, 16.0)),
        "dn_dt_bias": dt + jnp.log(-jnp.expm1(-dt)),
        "dn_norm_w": 1.0 + 0.02 * nrm(ks[11], (DEPTH, DN_HEAD_DIM)),
        "w_out": nrm(ks[12], (DEPTH, D_MIX, D_MODEL)) * D_MIX ** -0.5,
        "norm_ffn_w": 1.0 + 0.02 * nrm(ks[13], (DEPTH, D_MODEL)),
        "ffn_w_gu": nrm(ks[14], (DEPTH, D_MODEL, 2 * D_FF)) * D_MODEL ** -0.5,
        "ffn_w_down": nrm(ks[15], (DEPTH, D_FF, D_MODEL)) * D_FF ** -0.5,
        "final_norm_w": 1.0 + 0.02 * nrm(ks[16], (D_MODEL,)),
    }


def _fwd_reference(x, meta_tokens, norm_mix_w, w_in, conv_dw_w, conv_dw_b, conv_ln_w, conv_ln_b,
              dn_conv_w, dn_A_log, dn_dt_bias, dn_norm_w, w_out, norm_ffn_w, ffn_w_gu, ffn_w_down,
              final_norm_w):
    Bsz = x.shape[0]
    meta = jnp.broadcast_to(meta_tokens[None].astype(x.dtype), (Bsz, N_META, D_MODEL))
    h = jnp.concatenate([meta, x], axis=1)
    for l in range(DEPTH):
        hn = rmsnorm(h, norm_mix_w[l])
        proj = jnp.einsum('bld,de->ble', hn, w_in[l])
        c_val, c_gate, q, k, v, z, b, a = jnp.split(proj, IN_SPLITS, axis=-1)
        y_conv = conformer_conv_module(c_val, c_gate, conv_dw_w[l], conv_dw_b[l], conv_ln_w[l], conv_ln_b[l])
        y_dn = gated_deltanet(q, k, v, z, b, a, dn_conv_w[l], dn_A_log[l], dn_dt_bias[l], dn_norm_w[l])
        y = jnp.concatenate([y_conv, y_dn], axis=-1)
        h = h + jnp.einsum('ble,ed->bld', y, w_out[l])
        hn = rmsnorm(h, norm_ffn_w[l])
        gate, up = jnp.split(jnp.einsum('bld,df->blf', hn, ffn_w_gu[l]), 2, axis=-1)
        h = h + jnp.einsum('blf,fd->bld', jax.nn.silu(gate) * up, ffn_w_down[l])
    out = rmsnorm(h, final_norm_w)
    return out[:, N_META:]


import jax as _jax
import jax.numpy as _jnp

TWIN_FORMAT = 'train_step'
FWD_PARAMS = ['x', 'meta_tokens', 'norm_mix_w', 'w_in', 'conv_dw_w', 'conv_dw_b', 'conv_ln_w', 'conv_ln_b', 'dn_conv_w', 'dn_A_log', 'dn_dt_bias', 'dn_norm_w', 'w_out', 'norm_ffn_w', 'ffn_w_gu', 'ffn_w_down', 'final_norm_w']
TWIN_WEIGHTS = ['meta_tokens', 'norm_mix_w', 'w_in', 'conv_dw_w', 'conv_dw_b', 'conv_ln_w', 'conv_ln_b', 'dn_conv_w', 'dn_A_log', 'dn_dt_bias', 'dn_norm_w', 'w_out', 'norm_ffn_w', 'ffn_w_gu', 'ffn_w_down', 'final_norm_w']
TWIN_DIFF_INPUT = 'x'
TWIN_INPUTS = ['x', 'meta_tokens', 'norm_mix_w', 'w_in', 'conv_dw_w', 'conv_dw_b', 'conv_ln_w', 'conv_ln_b', 'dn_conv_w', 'dn_A_log', 'dn_dt_bias', 'dn_norm_w', 'w_out', 'norm_ffn_w', 'ffn_w_gu', 'ffn_w_down', 'final_norm_w', 'loss_target', 'm_meta_tokens', 'm_norm_mix_w', 'm_w_in', 'm_conv_dw_w', 'm_conv_dw_b', 'm_conv_ln_w', 'm_conv_ln_b', 'm_dn_conv_w', 'm_dn_A_log', 'm_dn_dt_bias', 'm_dn_norm_w', 'm_w_out', 'm_norm_ffn_w', 'm_ffn_w_gu', 'm_ffn_w_down', 'm_final_norm_w', 'v_meta_tokens', 'v_norm_mix_w', 'v_w_in', 'v_conv_dw_w', 'v_conv_dw_b', 'v_conv_ln_w', 'v_conv_ln_b', 'v_dn_conv_w', 'v_dn_A_log', 'v_dn_dt_bias', 'v_dn_norm_w', 'v_w_out', 'v_norm_ffn_w', 'v_ffn_w_gu', 'v_ffn_w_down', 'v_final_norm_w']
TWIN_OUTPUTS = ['loss', 'grad_x', 'grad_meta_tokens', 'grad_norm_mix_w', 'grad_w_in', 'grad_conv_dw_w', 'grad_conv_dw_b', 'grad_conv_ln_w', 'grad_conv_ln_b', 'grad_dn_conv_w', 'grad_dn_A_log', 'grad_dn_dt_bias', 'grad_dn_norm_w', 'grad_w_out', 'grad_norm_ffn_w', 'grad_ffn_w_gu', 'grad_ffn_w_down', 'grad_final_norm_w', 'delta_meta_tokens', 'delta_norm_mix_w', 'delta_w_in', 'delta_conv_dw_w', 'delta_conv_dw_b', 'delta_conv_ln_w', 'delta_conv_ln_b', 'delta_dn_conv_w', 'delta_dn_A_log', 'delta_dn_dt_bias', 'delta_dn_norm_w', 'delta_w_out', 'delta_norm_ffn_w', 'delta_ffn_w_gu', 'delta_ffn_w_down', 'delta_final_norm_w', 'new_m_meta_tokens', 'new_m_norm_mix_w', 'new_m_w_in', 'new_m_conv_dw_w', 'new_m_conv_dw_b', 'new_m_conv_ln_w', 'new_m_conv_ln_b', 'new_m_dn_conv_w', 'new_m_dn_A_log', 'new_m_dn_dt_bias', 'new_m_dn_norm_w', 'new_m_w_out', 'new_m_norm_ffn_w', 'new_m_ffn_w_gu', 'new_m_ffn_w_down', 'new_m_final_norm_w', 'new_v_meta_tokens', 'new_v_norm_mix_w', 'new_v_w_in', 'new_v_conv_dw_w', 'new_v_conv_dw_b', 'new_v_conv_ln_w', 'new_v_conv_ln_b', 'new_v_dn_conv_w', 'new_v_dn_A_log', 'new_v_dn_dt_bias', 'new_v_dn_norm_w', 'new_v_w_out', 'new_v_norm_ffn_w', 'new_v_ffn_w_gu', 'new_v_ffn_w_down', 'new_v_final_norm_w']
TWIN_LEAF_KINDS = {'loss': 'loss', 'grad_x': 'grad_x', 'grad_meta_tokens': 'grad_w', 'grad_norm_mix_w': 'grad_w', 'grad_w_in': 'grad_w', 'grad_conv_dw_w': 'grad_w', 'grad_conv_dw_b': 'grad_w', 'grad_conv_ln_w': 'grad_w', 'grad_conv_ln_b': 'grad_w', 'grad_dn_conv_w': 'grad_w', 'grad_dn_A_log': 'grad_w', 'grad_dn_dt_bias': 'grad_w', 'grad_dn_norm_w': 'grad_w', 'grad_w_out': 'grad_w', 'grad_norm_ffn_w': 'grad_w', 'grad_ffn_w_gu': 'grad_w', 'grad_ffn_w_down': 'grad_w', 'grad_final_norm_w': 'grad_w', 'delta_meta_tokens': 'delta_w', 'delta_norm_mix_w': 'delta_w', 'delta_w_in': 'delta_w', 'delta_conv_dw_w': 'delta_w', 'delta_conv_dw_b': 'delta_w', 'delta_conv_ln_w': 'delta_w', 'delta_conv_ln_b': 'delta_w', 'delta_dn_conv_w': 'delta_w', 'delta_dn_A_log': 'delta_w', 'delta_dn_dt_bias': 'delta_w', 'delta_dn_norm_w': 'delta_w', 'delta_w_out': 'delta_w', 'delta_norm_ffn_w': 'delta_w', 'delta_ffn_w_gu': 'delta_w', 'delta_ffn_w_down': 'delta_w', 'delta_final_norm_w': 'delta_w', 'new_m_meta_tokens': 'new_m', 'new_m_norm_mix_w': 'new_m', 'new_m_w_in': 'new_m', 'new_m_conv_dw_w': 'new_m', 'new_m_conv_dw_b': 'new_m', 'new_m_conv_ln_w': 'new_m', 'new_m_conv_ln_b': 'new_m', 'new_m_dn_conv_w': 'new_m', 'new_m_dn_A_log': 'new_m', 'new_m_dn_dt_bias': 'new_m', 'new_m_dn_norm_w': 'new_m', 'new_m_w_out': 'new_m', 'new_m_norm_ffn_w': 'new_m', 'new_m_ffn_w_gu': 'new_m', 'new_m_ffn_w_down': 'new_m', 'new_m_final_norm_w': 'new_m', 'new_v_meta_tokens': 'new_v', 'new_v_norm_mix_w': 'new_v', 'new_v_w_in': 'new_v', 'new_v_conv_dw_w': 'new_v', 'new_v_conv_dw_b': 'new_v', 'new_v_conv_ln_w': 'new_v', 'new_v_conv_ln_b': 'new_v', 'new_v_dn_conv_w': 'new_v', 'new_v_dn_A_log': 'new_v', 'new_v_dn_dt_bias': 'new_v', 'new_v_dn_norm_w': 'new_v', 'new_v_w_out': 'new_v', 'new_v_norm_ffn_w': 'new_v', 'new_v_ffn_w_gu': 'new_v', 'new_v_ffn_w_down': 'new_v', 'new_v_final_norm_w': 'new_v'}


def _forward(args):
    return _fwd_reference(*[args[k] for k in FWD_PARAMS])


def _output_shape():
    def fwd():
        inp = _fwd_setup_inputs(0)
        return _fwd_reference(*[inp[k] for k in FWD_PARAMS])
    out = _jax.eval_shape(fwd)
    return out.shape, out.dtype

N_MICROBATCH = 1
ADAM_LR = 0.001
ADAM_B1 = 0.9
ADAM_B2 = 0.999
ADAM_EPS = 1e-08
ADAM_WD = 0.01
ADAM_STEP = 10
PER_EXAMPLE_BATCH_AXIS = {'x': 0, 'loss_target': 0}
SHARED_INPUTS = []
_WEIGHT_DTYPES = {'meta_tokens': _jnp.float32, 'norm_mix_w': _jnp.float32, 'w_in': _jnp.float32, 'conv_dw_w': _jnp.float32, 'conv_dw_b': _jnp.float32, 'conv_ln_w': _jnp.float32, 'conv_ln_b': _jnp.float32, 'dn_conv_w': _jnp.float32, 'dn_A_log': _jnp.float32, 'dn_dt_bias': _jnp.float32, 'dn_norm_w': _jnp.float32, 'w_out': _jnp.float32, 'norm_ffn_w': _jnp.float32, 'ffn_w_gu': _jnp.float32, 'ffn_w_down': _jnp.float32, 'final_norm_w': _jnp.float32}
MOMENT_SCALE = {'meta_tokens': 2.800624e-02, 'norm_mix_w': 2.816004e-01, 'w_in': 1.640107e-01, 'conv_dw_w': 2.069084e-01, 'conv_dw_b': 4.568890e-01, 'conv_ln_w': 2.455689e-01, 'conv_ln_b': 2.680522e-01, 'dn_conv_w': 1.544215e-01, 'dn_A_log': 1.589059e+00, 'dn_dt_bias': 1.503464e+00, 'dn_norm_w': 4.049173e-01, 'w_out': 2.063836e-01, 'norm_ffn_w': 2.449367e-01, 'ffn_w_gu': 1.033540e-01, 'ffn_w_down': 1.688630e-01, 'final_norm_w': 1.282932e+02}


def _to_microbatches(a, axis):
    t = _jnp.moveaxis(a, axis, 0)
    t = t.reshape((N_MICROBATCH, t.shape[0] // N_MICROBATCH) + t.shape[1:])
    return _jnp.moveaxis(t, 1, axis + 1)


def setup_inputs(seed: int = 0) -> dict:
    inp = _fwd_setup_inputs(seed)
    key = _jax.random.fold_in(_jax.random.key(seed), 7919)
    shape, _ = _output_shape()
    out = dict(inp)
    out["loss_target"] = _jax.random.normal(_jax.random.fold_in(key, 0), shape, _jnp.float32)
    for i, name in enumerate(TWIN_WEIGHTS):
        w = inp[name].astype(_jnp.float32)
        if MOMENT_SCALE is None:
            s = _jnp.sqrt(_jnp.mean(_jnp.square(w)) + 1e-30)
        else:
            s = MOMENT_SCALE[name]
        km, kv = _jax.random.split(_jax.random.fold_in(key, i + 1))
        out[name] = w
        out["m_" + name] = s * _jax.random.normal(km, w.shape, _jnp.float32)
        out["v_" + name] = (s * s) * _jax.random.uniform(kv, w.shape, _jnp.float32, 0.5, 1.5)
    if N_MICROBATCH > 1:
        for name, axis in PER_EXAMPLE_BATCH_AXIS.items():
            out[name] = _to_microbatches(out[name], axis)
    return {'x': out['x'], 'meta_tokens': out['meta_tokens'], 'norm_mix_w': out['norm_mix_w'], 'w_in': out['w_in'], 'conv_dw_w': out['conv_dw_w'], 'conv_dw_b': out['conv_dw_b'], 'conv_ln_w': out['conv_ln_w'], 'conv_ln_b': out['conv_ln_b'], 'dn_conv_w': out['dn_conv_w'], 'dn_A_log': out['dn_A_log'], 'dn_dt_bias': out['dn_dt_bias'], 'dn_norm_w': out['dn_norm_w'], 'w_out': out['w_out'], 'norm_ffn_w': out['norm_ffn_w'], 'ffn_w_gu': out['ffn_w_gu'], 'ffn_w_down': out['ffn_w_down'], 'final_norm_w': out['final_norm_w'], 'loss_target': out['loss_target'], 'm_meta_tokens': out['m_meta_tokens'], 'm_norm_mix_w': out['m_norm_mix_w'], 'm_w_in': out['m_w_in'], 'm_conv_dw_w': out['m_conv_dw_w'], 'm_conv_dw_b': out['m_conv_dw_b'], 'm_conv_ln_w': out['m_conv_ln_w'], 'm_conv_ln_b': out['m_conv_ln_b'], 'm_dn_conv_w': out['m_dn_conv_w'], 'm_dn_A_log': out['m_dn_A_log'], 'm_dn_dt_bias': out['m_dn_dt_bias'], 'm_dn_norm_w': out['m_dn_norm_w'], 'm_w_out': out['m_w_out'], 'm_norm_ffn_w': out['m_norm_ffn_w'], 'm_ffn_w_gu': out['m_ffn_w_gu'], 'm_ffn_w_down': out['m_ffn_w_down'], 'm_final_norm_w': out['m_final_norm_w'], 'v_meta_tokens': out['v_meta_tokens'], 'v_norm_mix_w': out['v_norm_mix_w'], 'v_w_in': out['v_w_in'], 'v_conv_dw_w': out['v_conv_dw_w'], 'v_conv_dw_b': out['v_conv_dw_b'], 'v_conv_ln_w': out['v_conv_ln_w'], 'v_conv_ln_b': out['v_conv_ln_b'], 'v_dn_conv_w': out['v_dn_conv_w'], 'v_dn_A_log': out['v_dn_A_log'], 'v_dn_dt_bias': out['v_dn_dt_bias'], 'v_dn_norm_w': out['v_dn_norm_w'], 'v_w_out': out['v_w_out'], 'v_norm_ffn_w': out['v_norm_ffn_w'], 'v_ffn_w_gu': out['v_ffn_w_gu'], 'v_ffn_w_down': out['v_ffn_w_down'], 'v_final_norm_w': out['v_final_norm_w']}


def _loss(weights, diff, rest, loss_target):
    with _jax.named_scope("forward"):
        args = {**rest, TWIN_DIFF_INPUT: diff, **{k: w.astype(_WEIGHT_DTYPES[k]) for k, w in weights.items()}}
        y = _forward(args)
    with _jax.named_scope("loss_head"):
        err = _jnp.square(y.astype(_jnp.float32) - loss_target)
        return 0.5 * _jnp.sum(_jnp.mean(err, axis=-1)) if err.ndim else 0.5 * err


def _adamw(w, g, m, v):
    m = ADAM_B1 * m + (1.0 - ADAM_B1) * g
    v = ADAM_B2 * v + (1.0 - ADAM_B2) * _jnp.square(g)
    m_hat = m / (1.0 - ADAM_B1 ** ADAM_STEP)
    v_hat = v / (1.0 - ADAM_B2 ** ADAM_STEP)
    delta = -ADAM_LR * (m_hat / (_jnp.sqrt(v_hat) + ADAM_EPS) + ADAM_WD * w)
    return delta, m, v


def reference(x, meta_tokens, norm_mix_w, w_in, conv_dw_w, conv_dw_b, conv_ln_w, conv_ln_b, dn_conv_w, dn_A_log, dn_dt_bias, dn_norm_w, w_out, norm_ffn_w, ffn_w_gu, ffn_w_down, final_norm_w, loss_target, m_meta_tokens, m_norm_mix_w, m_w_in, m_conv_dw_w, m_conv_dw_b, m_conv_ln_w, m_conv_ln_b, m_dn_conv_w, m_dn_A_log, m_dn_dt_bias, m_dn_norm_w, m_w_out, m_norm_ffn_w, m_ffn_w_gu, m_ffn_w_down, m_final_norm_w, v_meta_tokens, v_norm_mix_w, v_w_in, v_conv_dw_w, v_conv_dw_b, v_conv_ln_w, v_conv_ln_b, v_dn_conv_w, v_dn_A_log, v_dn_dt_bias, v_dn_norm_w, v_w_out, v_norm_ffn_w, v_ffn_w_gu, v_ffn_w_down, v_final_norm_w):
    given = dict(x=x, meta_tokens=meta_tokens, norm_mix_w=norm_mix_w, w_in=w_in, conv_dw_w=conv_dw_w, conv_dw_b=conv_dw_b, conv_ln_w=conv_ln_w, conv_ln_b=conv_ln_b, dn_conv_w=dn_conv_w, dn_A_log=dn_A_log, dn_dt_bias=dn_dt_bias, dn_norm_w=dn_norm_w, w_out=w_out, norm_ffn_w=norm_ffn_w, ffn_w_gu=ffn_w_gu, ffn_w_down=ffn_w_down, final_norm_w=final_norm_w, loss_target=loss_target, m_meta_tokens=m_meta_tokens, m_norm_mix_w=m_norm_mix_w, m_w_in=m_w_in, m_conv_dw_w=m_conv_dw_w, m_conv_dw_b=m_conv_dw_b, m_conv_ln_w=m_conv_ln_w, m_conv_ln_b=m_conv_ln_b, m_dn_conv_w=m_dn_conv_w, m_dn_A_log=m_dn_A_log, m_dn_dt_bias=m_dn_dt_bias, m_dn_norm_w=m_dn_norm_w, m_w_out=m_w_out, m_norm_ffn_w=m_norm_ffn_w, m_ffn_w_gu=m_ffn_w_gu, m_ffn_w_down=m_ffn_w_down, m_final_norm_w=m_final_norm_w, v_meta_tokens=v_meta_tokens, v_norm_mix_w=v_norm_mix_w, v_w_in=v_w_in, v_conv_dw_w=v_conv_dw_w, v_conv_dw_b=v_conv_dw_b, v_conv_ln_w=v_conv_ln_w, v_conv_ln_b=v_conv_ln_b, v_dn_conv_w=v_dn_conv_w, v_dn_A_log=v_dn_A_log, v_dn_dt_bias=v_dn_dt_bias, v_dn_norm_w=v_dn_norm_w, v_w_out=v_w_out, v_norm_ffn_w=v_norm_ffn_w, v_ffn_w_gu=v_ffn_w_gu, v_ffn_w_down=v_ffn_w_down, v_final_norm_w=v_final_norm_w)
    weights = {n: given[n] for n in TWIN_WEIGHTS}
    shared = {n: given[n] for n in SHARED_INPUTS}
    per_example = {n: given[n] for n in ['x']}
    grad_fn = _jax.value_and_grad(_loss, argnums=(0, 1))

    def one_microbatch(ex, loss_target):
        ex = dict(ex)
        diff = ex.pop(TWIN_DIFF_INPUT)
        return grad_fn(weights, diff, {**shared, **ex}, loss_target)

    if N_MICROBATCH == 1:
        loss, (grad_w, grad_x) = one_microbatch(per_example, given["loss_target"])
    else:
        def body(carry, xs):
            loss_sum, grad_sum = carry
            l_k, (gw_k, gx_k) = one_microbatch(xs[0], xs[1])
            with _jax.named_scope("update"):
                return (loss_sum + l_k, _jax.tree.map(_jnp.add, grad_sum, gw_k)), gx_k

        init = (_jnp.zeros((), _jnp.float32), _jax.tree.map(_jnp.zeros_like, weights))
        (loss, grad_w), grad_x = _jax.lax.scan(body, init, (per_example, given["loss_target"]))
    with _jax.named_scope("update"):
        delta_w, new_m, new_v = {}, {}, {}
        for n in TWIN_WEIGHTS:
            delta_w[n], new_m[n], new_v[n] = _adamw(weights[n], grad_w[n], given["m_" + n], given["v_" + n])
    return (loss, grad_x, *[grad_w[n] for n in TWIN_WEIGHTS], *[delta_w[n] for n in TWIN_WEIGHTS],
            *[new_m[n] for n in TWIN_WEIGHTS], *[new_v[n] for n in TWIN_WEIGHTS])
```

```python
import jax
import jax.numpy as jnp
from jax import lax
from jax.experimental import pallas as pl
from jax.experimental.pallas import tpu as pltpu

F32 = jnp.float32
MXU_DTYPE = jnp.bfloat16

D = 1024
N_META = 16
CHUNK = 64
FRONT = CHUNK - N_META
HEAD = CHUNK
CONV_CH = 512
CONV_W = 31
CONV_HALO = 32
NH = 4
DH = 128
DQ = NH * DH
DN_W = 4
DN_HALO = 8
DFF = 2816
PROJ_MAIN = 3072
D_IN = 3080
GATE_W = 128
NDEV = 8
NORM_EPS = 1e-6
LN_EPS = 1e-5
L2_EPS = 1e-6
VMEM_LIMIT_V7X = 48 * 1024 * 1024
ROW_TILE = 640
ROW_TILE_SMALL = 128
ADAM_ROWS = 256
MM_TILES = (1408, 1024, 640, 512, 256, 128)

ADAM_LR = 0.001
ADAM_B1 = 0.9
ADAM_B2 = 0.999
ADAM_EPS = 1e-08
ADAM_WD = 0.01
ADAM_STEP = 10

MESH_AXES = ("x", "y", "c")
NN = ((1,), (0,))
NT = ((1,), (1,))
TN = ((0,), (0,))


def _row_tile(t):
    return ROW_TILE if t % ROW_TILE == 0 else ROW_TILE_SMALL


def _padded_rows(seq):
    n = HEAD + seq
    tm = ROW_TILE if n >= 4 * ROW_TILE else ROW_TILE_SMALL
    return -(-n // tm) * tm


def _pick(n, prefs):
    for p in prefs:
        if n % p == 0:
            return p
    return n


def _cp(sem):
    return pltpu.CompilerParams(dimension_semantics=sem, vmem_limit_bytes=VMEM_LIMIT_V7X)


def _sigmoid(x):
    return 1.0 / (1.0 + jnp.exp(-x))


def _softplus(x):
    return jnp.maximum(x, 0.0) + jnp.log(1.0 + jnp.exp(-jnp.abs(x)))


def _valid_rows(i, tm, seq, width, first=FRONT):
    rows = i * tm + lax.broadcasted_iota(jnp.int32, (tm, width), 0)
    return jnp.logical_and(rows >= first, rows < HEAD + seq)


def _dot(a, b, dims):
    return lax.dot_general(a, b, (dims, ((), ())), preferred_element_type=F32)


def _split(x, n):
    out, r = [], x
    for _ in range(n):
        p = r.astype(MXU_DTYPE)
        out.append(p)
        r = r - p.astype(F32)
    return out


def _mm1(a, b, dims):
    return _dot(a.astype(MXU_DTYPE), b.astype(MXU_DTYPE), dims)


def _mm3(a, b, dims):
    a1, a2 = _split(a, 2)
    b1, b2 = _split(b, 2)
    return _dot(a1, b1, dims) + (_dot(a1, b2, dims) + _dot(a2, b1, dims))


def _mmx(e, b, dims):
    e = e.astype(MXU_DTYPE)
    b1, b2, b3 = _split(b, 3)
    return _dot(e, b1, dims) + (_dot(e, b2, dims) + _dot(e, b3, dims))


def mm(a, b, *, ta=False, tb=False, res=None, out_dtype=F32, name):
    (k_dim, m_dim) = a.shape if ta else a.shape[::-1]
    n_dim = b.shape[0] if tb else b.shape[1]
    assert (b.shape[1] if tb else b.shape[0]) == k_dim
    tm, tn, tk = (_pick(n, MM_TILES) for n in (m_dim, n_dim, k_dim))
    nk = k_dim // tk

    def body(*refs):
        if res is None:
            a_ref, b_ref, o_ref, acc_ref = refs
            r_ref = None
        else:
            a_ref, b_ref, r_ref, o_ref, acc_ref = refs
        k = pl.program_id(2)

        @pl.when(k == 0)
        def _():
            acc_ref[...] = jnp.zeros_like(acc_ref)

        dims = ((0,) if ta else (1,), (1,) if tb else (0,))
        acc_ref[...] += _mm1(a_ref[...], b_ref[...], dims)

        @pl.when(k == nk - 1)
        def _():
            out = acc_ref[...]
            if r_ref is not None:
                out = out + r_ref[...]
            o_ref[...] = out.astype(o_ref.dtype)

    a_spec = pl.BlockSpec((tk, tm), lambda i, j, k: (k, i)) if ta else pl.BlockSpec((tm, tk), lambda i, j, k: (i, k))
    b_spec = pl.BlockSpec((tn, tk), lambda i, j, k: (j, k)) if tb else pl.BlockSpec((tk, tn), lambda i, j, k: (k, j))
    o_spec = pl.BlockSpec((tm, tn), lambda i, j, k: (i, j))
    in_specs, args = [a_spec, b_spec], [a, b]
    if res is not None:
        in_specs.append(o_spec)
        args.append(res)
    return pl.pallas_call(
        body, grid=(m_dim // tm, n_dim // tn, nk), in_specs=in_specs, out_specs=o_spec,
        out_shape=jax.ShapeDtypeStruct((m_dim, n_dim), out_dtype),
        scratch_shapes=[pltpu.VMEM((tm, tn), F32)],
        compiler_params=_cp(("parallel", "parallel", "arbitrary")), name=name)(*args)


def rms_fwd(h, w, name):
    t = h.shape[0]
    tm = _row_tile(t)

    def body(h_ref, w_ref, o_ref):
        x = h_ref[...]
        r = lax.rsqrt(jnp.mean(x * x, axis=-1, keepdims=True) + NORM_EPS)
        o_ref[...] = (x * r * w_ref[...]).astype(o_ref.dtype)

    row = pl.BlockSpec((tm, D), lambda i: (i, 0))
    return pl.pallas_call(
        body, grid=(t // tm,), in_specs=[row, pl.BlockSpec((1, D), lambda i: (0, 0))], out_specs=row,
        out_shape=jax.ShapeDtypeStruct((t, D), MXU_DTYPE), compiler_params=_cp(("parallel",)), name=name)(h, w)


def rms_bwd(dy, h, w, dres, name):
    t = h.shape[0]
    tm = _row_tile(t)

    def body(dy_ref, h_ref, w_ref, dres_ref, dh_ref, dw_ref):
        i = pl.program_id(0)
        x = h_ref[...]
        r = lax.rsqrt(jnp.mean(x * x, axis=-1, keepdims=True) + NORM_EPS)
        xh = x * r
        g = dy_ref[...] * w_ref[...]
        dh_ref[...] = dres_ref[...] + r * (g - xh * jnp.mean(g * xh, axis=-1, keepdims=True))

        @pl.when(i == 0)
        def _():
            dw_ref[...] = jnp.zeros_like(dw_ref)

        dw_ref[...] += jnp.sum(dy_ref[...] * xh, axis=0, keepdims=True)

    row = pl.BlockSpec((tm, D), lambda i: (i, 0))
    vec = pl.BlockSpec((1, D), lambda i: (0, 0))
    return pl.pallas_call(
        body, grid=(t // tm,), in_specs=[row, row, vec, row], out_specs=[row, vec],
        out_shape=[jax.ShapeDtypeStruct((t, D), F32), jax.ShapeDtypeStruct((1, D), F32)],
        compiler_params=_cp(("arbitrary",)), name=name)(dy, h, w, dres)


def loss_bwd(h, tgt, w, seq, name):
    t = h.shape[0]
    tm = _row_tile(t)

    def body(h_ref, t_ref, w_ref, dh_ref, loss_ref, dw_ref):
        i = pl.program_id(0)
        x = h_ref[...]
        wv = w_ref[...]
        r = lax.rsqrt(jnp.mean(x * x, axis=-1, keepdims=True) + NORM_EPS)
        xh = x * r
        err = jnp.where(_valid_rows(i, tm, seq, D, HEAD), xh * wv - t_ref[...], 0.0)
        dy = err * (1.0 / D)
        g = dy * wv
        dh_ref[...] = r * (g - xh * jnp.mean(g * xh, axis=-1, keepdims=True))

        @pl.when(i == 0)
        def _():
            dw_ref[...] = jnp.zeros_like(dw_ref)
            loss_ref[...] = jnp.zeros_like(loss_ref)

        dw_ref[...] += jnp.sum(dy * xh, axis=0, keepdims=True)
        part = jnp.sum(jnp.sum(err * err, axis=1, keepdims=True), axis=0, keepdims=True) * (0.5 / D)
        loss_ref[...] += jnp.broadcast_to(part, loss_ref.shape)

    row = pl.BlockSpec((tm, D), lambda i: (i, 0))
    vec = pl.BlockSpec((1, D), lambda i: (0, 0))
    return pl.pallas_call(
        body, grid=(t // tm,), in_specs=[row, row, vec],
        out_specs=[row, pl.BlockSpec((1, 128), lambda i: (0, 0)), vec],
        out_shape=[jax.ShapeDtypeStruct((t, D), F32), jax.ShapeDtypeStruct((1, 128), F32),
                   jax.ShapeDtypeStruct((1, D), F32)],
        compiler_params=_cp(("arbitrary",)), name=name)(h, tgt, w)


def _layernorm_parts(u1):
    mu = jnp.mean(u1, axis=-1, keepdims=True)
    xc = u1 - mu
    rstd = lax.rsqrt(jnp.mean(xc * xc, axis=-1, keepdims=True) + LN_EPS)
    return xc * rstd, rstd


def conv_fwd(proj, w32, b, lw, lb, seq, name):
    t = proj.shape[0]
    tm = _row_tile(t)

    def body(cv_ref, cg_ref, w_ref, b_ref, lw_ref, lb_ref, y_ref, u1_ref, ext):
        i = pl.program_id(0)

        @pl.when(i == 0)
        def _():
            ext[0:CONV_HALO, :] = jnp.zeros((CONV_HALO, CONV_CH), F32)

        @pl.when(i > 0)
        def _():
            ext[0:CONV_HALO, :] = ext[tm:tm + CONV_HALO, :]

        ext[CONV_HALO:CONV_HALO + tm, :] = cv_ref[...] * _sigmoid(cg_ref[...])
        acc = jnp.broadcast_to(b_ref[...], (tm, CONV_CH))
        for j in range(CONV_W):
            acc = acc + w_ref[j:j + 1, :] * ext[pl.ds(CONV_HALO - (CONV_W - 1) + j, tm), :]
        u1_ref[...] = acc
        xh, _ = _layernorm_parts(acc)
        ln = xh * lw_ref[...] + lb_ref[...]
        y = ln * _sigmoid(ln)
        y_ref[...] = jnp.where(_valid_rows(i, tm, seq, CONV_CH), y, 0.0).astype(y_ref.dtype)

    half = lambda c: pl.BlockSpec((tm, CONV_CH), lambda i: (i, c))
    vec = pl.BlockSpec((1, CONV_CH), lambda i: (0, 0))
    return pl.pallas_call(
        body, grid=(t // tm,),
        in_specs=[half(0), half(1), pl.BlockSpec((CONV_HALO, CONV_CH), lambda i: (0, 0)), vec, vec, vec],
        out_specs=[half(0), half(0)],
        out_shape=[jax.ShapeDtypeStruct((t, D), MXU_DTYPE), jax.ShapeDtypeStruct((t, CONV_CH), F32)],
        scratch_shapes=[pltpu.VMEM((tm + CONV_HALO, CONV_CH), F32)],
        compiler_params=_cp(("arbitrary",)), name=name)(proj, proj, w32, b, lw, lb)


def conv_bwd(dy, u1, proj, w32, lw, lb, seq, name):
    t = proj.shape[0]
    tm = _row_tile(t)
    nt = t // tm
    per = tm // CONV_HALO

    def body(dy_ref, u1_ref, cv_ref, cg_ref, cvp_ref, cgp_ref, w_ref, lw_ref, lb_ref,
             dp_ref, dw_ref, db_ref, dlw_ref, dlb_ref, ext_d, ext_u):
        i = pl.program_id(0)
        tile = nt - 1 - i

        @pl.when(i == 0)
        def _():
            ext_d[tm:tm + CONV_HALO, :] = jnp.zeros((CONV_HALO, CONV_CH), F32)
            dw_ref[...] = jnp.zeros_like(dw_ref)
            db_ref[...] = jnp.zeros_like(db_ref)
            dlw_ref[...] = jnp.zeros_like(dlw_ref)
            dlb_ref[...] = jnp.zeros_like(dlb_ref)

        @pl.when(i > 0)
        def _():
            ext_d[tm:tm + CONV_HALO, :] = ext_d[0:CONV_HALO, :]

        xh, rstd = _layernorm_parts(u1_ref[...])
        lwv = lw_ref[...]
        ln = xh * lwv + lb_ref[...]
        sg = _sigmoid(ln)
        dln = jnp.where(_valid_rows(tile, tm, seq, CONV_CH), dy_ref[...], 0.0) * (sg * (1.0 + ln * (1.0 - sg)))
        dlw_ref[...] += jnp.sum(dln * xh, axis=0, keepdims=True)
        dlb_ref[...] += jnp.sum(dln, axis=0, keepdims=True)
        dxh = dln * lwv
        du1 = rstd * (dxh - jnp.mean(dxh, axis=-1, keepdims=True)
                      - xh * jnp.mean(dxh * xh, axis=-1, keepdims=True))
        db_ref[...] += jnp.sum(du1, axis=0, keepdims=True)
        ext_d[0:tm, :] = du1

        cv = cv_ref[...]
        sgc = _sigmoid(cg_ref[...])
        prev = cvp_ref[...] * _sigmoid(cgp_ref[...])
        ext_u[0:CONV_HALO, :] = jnp.where(tile > 0, prev, 0.0)
        ext_u[CONV_HALO:CONV_HALO + tm, :] = cv * sgc

        du0 = jnp.zeros((tm, CONV_CH), F32)
        for j in range(CONV_W):
            du0 = du0 + w_ref[j:j + 1, :] * ext_d[pl.ds(CONV_W - 1 - j, tm), :]
            dw_ref[j:j + 1, :] += jnp.sum(
                du1 * ext_u[pl.ds(CONV_HALO - (CONV_W - 1) + j, tm), :], axis=0, keepdims=True)
        dp_ref[:, 0:CONV_CH] = (du0 * sgc).astype(dp_ref.dtype)
        dp_ref[:, CONV_CH:2 * CONV_CH] = (du0 * cv * sgc * (1.0 - sgc)).astype(dp_ref.dtype)

    rev = lambda c: pl.BlockSpec((tm, CONV_CH), lambda i: (nt - 1 - i, c))
    prev = lambda c: pl.BlockSpec((CONV_HALO, CONV_CH), lambda i: (jnp.maximum((nt - 1 - i) * per - 1, 0), c))
    vec = pl.BlockSpec((1, CONV_CH), lambda i: (0, 0))
    wspec = pl.BlockSpec((CONV_HALO, CONV_CH), lambda i: (0, 0))
    return pl.pallas_call(
        body, grid=(nt,),
        in_specs=[rev(0), rev(0), rev(0), rev(1), prev(0), prev(1), wspec, vec, vec],
        out_specs=[pl.BlockSpec((tm, 2 * CONV_CH), lambda i: (nt - 1 - i, 0)), wspec, vec, vec, vec],
        out_shape=[jax.ShapeDtypeStruct((t, PROJ_MAIN), MXU_DTYPE), jax.ShapeDtypeStruct((CONV_HALO, CONV_CH), F32),
                   jax.ShapeDtypeStruct((1, CONV_CH), F32), jax.ShapeDtypeStruct((1, CONV_CH), F32),
                   jax.ShapeDtypeStruct((1, CONV_CH), F32)],
        scratch_shapes=[pltpu.VMEM((tm + CONV_HALO, CONV_CH), F32), pltpu.VMEM((tm + CONV_HALO, CONV_CH), F32)],
        compiler_params=_cp(("arbitrary",)), name=name)(dy, u1, proj, proj, proj, proj, w32, lw, lb)


def dn_pre_fwd(proj, w8, name):
    t = proj.shape[0]
    tm = _row_tile(t)

    def body(raw_ref, w_ref, o_ref, ext):
        g = pl.program_id(0)
        i = pl.program_id(1)

        @pl.when(i == 0)
        def _():
            ext[0:DN_HALO, :] = jnp.zeros((DN_HALO, DQ), F32)

        @pl.when(i > 0)
        def _():
            ext[0:DN_HALO, :] = ext[tm:tm + DN_HALO, :]

        ext[DN_HALO:DN_HALO + tm, :] = raw_ref[...]
        c = jnp.zeros((tm, DQ), F32)
        for j in range(DN_W):
            c = c + w_ref[j:j + 1, :] * ext[pl.ds(DN_HALO - (DN_W - 1) + j, tm), :]
        s = c * _sigmoid(c)
        scale = jnp.where(g == 0, DH ** -0.5, 1.0)
        for h in range(NH):
            sh = s[:, h * DH:(h + 1) * DH]
            r = lax.rsqrt(jnp.sum(sh * sh, axis=-1, keepdims=True) + L2_EPS)
            o_ref[:, h * DH:(h + 1) * DH] = jnp.where(g == 2, sh, sh * (r * scale))

    return pl.pallas_call(
        body, grid=(3, t // tm),
        in_specs=[pl.BlockSpec((tm, DQ), lambda g, i: (i, 2 + g)), pl.BlockSpec((DN_HALO, DQ), lambda g, i: (0, g))],
        out_specs=pl.BlockSpec((tm, DQ), lambda g, i: (i, g)),
        out_shape=jax.ShapeDtypeStruct((t, 3 * DQ), F32),
        scratch_shapes=[pltpu.VMEM((tm + DN_HALO, DQ), F32)],
        compiler_params=_cp(("arbitrary", "arbitrary")), name=name)(proj, w8)


def dn_pre_bwd(dproj, dqkv, proj, w8, name):
    t = proj.shape[0]
    tm = _row_tile(t)
    nt = t // tm
    per = tm // DN_HALO

    def body(dp_in, d_ref, raw_ref, rawp_ref, w_ref, dp_ref, dw_ref, ext_d, ext_r):
        del dp_in
        g = pl.program_id(0)
        i = pl.program_id(1)
        tile = nt - 1 - i

        @pl.when(i == 0)
        def _():
            ext_d[tm:tm + DN_HALO, :] = jnp.zeros((DN_HALO, DQ), F32)
            dw_ref[...] = jnp.zeros_like(dw_ref)

        @pl.when(i > 0)
        def _():
            ext_d[tm:tm + DN_HALO, :] = ext_d[0:DN_HALO, :]

        ext_r[0:DN_HALO, :] = jnp.where(tile > 0, rawp_ref[...], 0.0)
        ext_r[DN_HALO:DN_HALO + tm, :] = raw_ref[...]
        c = jnp.zeros((tm, DQ), F32)
        for j in range(DN_W):
            c = c + w_ref[j:j + 1, :] * ext_r[pl.ds(DN_HALO - (DN_W - 1) + j, tm), :]
        sg = _sigmoid(c)
        s = c * sg
        scale = jnp.where(g == 0, DH ** -0.5, 1.0)
        for h in range(NH):
            sl = slice(h * DH, (h + 1) * DH)
            sh = s[:, sl]
            dn = d_ref[:, sl]
            r = lax.rsqrt(jnp.sum(sh * sh, axis=-1, keepdims=True) + L2_EPS)
            unit = sh * r
            dsn = (r * scale) * (dn - unit * jnp.sum(dn * unit, axis=-1, keepdims=True))
            ds = jnp.where(g == 2, dn, dsn)
            ext_d[0:tm, sl] = ds * (sg[:, sl] * (1.0 + c[:, sl] * (1.0 - sg[:, sl])))
        dc = ext_d[0:tm, :]
        draw = jnp.zeros((tm, DQ), F32)
        for j in range(DN_W):
            draw = draw + w_ref[j:j + 1, :] * ext_d[pl.ds(DN_W - 1 - j, tm), :]
            dw_ref[j:j + 1, :] += jnp.sum(
                dc * ext_r[pl.ds(DN_HALO - (DN_W - 1) + j, tm), :], axis=0, keepdims=True)
        dp_ref[...] = draw.astype(dp_ref.dtype)

    return pl.pallas_call(
        body, grid=(3, nt),
        in_specs=[pl.BlockSpec(memory_space=pl.ANY),
                  pl.BlockSpec((tm, DQ), lambda g, i: (nt - 1 - i, g)),
                  pl.BlockSpec((tm, DQ), lambda g, i: (nt - 1 - i, 2 + g)),
                  pl.BlockSpec((DN_HALO, DQ), lambda g, i: (jnp.maximum((nt - 1 - i) * per - 1, 0), 2 + g)),
                  pl.BlockSpec((DN_HALO, DQ), lambda g, i: (0, g))],
        out_specs=[pl.BlockSpec((tm, DQ), lambda g, i: (nt - 1 - i, 2 + g)),
                   pl.BlockSpec((DN_HALO, DQ), lambda g, i: (0, g))],
        out_shape=[jax.ShapeDtypeStruct(dproj.shape, dproj.dtype), jax.ShapeDtypeStruct((DN_HALO, 3 * DQ), F32)],
        scratch_shapes=[pltpu.VMEM((tm + DN_HALO, DQ), F32), pltpu.VMEM((tm + DN_HALO, DQ), F32)],
        input_output_aliases={0: 0},
        compiler_params=_cp(("arbitrary", "arbitrary")), name=name)(dproj, dqkv, proj, proj, w8)


def gate_fwd(pg, alog, dtb, seq, name):
    t = pg.shape[0]
    tm = _row_tile(t)

    def body(x_ref, al_ref, dt_ref, o_ref):
        i = pl.program_id(0)
        x = x_ref[...]
        lane = lax.broadcasted_iota(jnp.int32, (tm, GATE_W), 1)
        gg = -jnp.exp(al_ref[...]) * _softplus(x + dt_ref[...])
        out = jnp.where(lane < NH, _sigmoid(x), jnp.where(lane < 2 * NH, gg, 0.0))
        o_ref[...] = jnp.where(_valid_rows(i, tm, seq, GATE_W), out, 0.0)

    row = pl.BlockSpec((tm, GATE_W), lambda i: (i, 0))
    vec = pl.BlockSpec((1, GATE_W), lambda i: (0, 0))
    return pl.pallas_call(
        body, grid=(t // tm,), in_specs=[row, vec, vec], out_specs=row,
        out_shape=jax.ShapeDtypeStruct((t, GATE_W), F32), compiler_params=_cp(("parallel",)), name=name)(pg, alog, dtb)


def gate_bwd(dbg, pg, alog, dtb, seq, name):
    t = pg.shape[0]
    tm = _row_tile(t)

    def body(d_ref, x_ref, al_ref, dt_ref, o_ref, dal_ref, ddt_ref):
        i = pl.program_id(0)
        x = x_ref[...]
        lane = lax.broadcasted_iota(jnp.int32, (tm, GATE_W), 1)
        d = jnp.where(_valid_rows(i, tm, seq, GATE_W), d_ref[...], 0.0)
        beta = _sigmoid(x)
        xs = x + dt_ref[...]
        e = -jnp.exp(al_ref[...])
        is_g = jnp.logical_and(lane >= NH, lane < 2 * NH)
        da = jnp.where(is_g, d * e * _sigmoid(xs), 0.0)
        dgg = jnp.where(is_g, d * e * _softplus(xs), 0.0)
        o_ref[...] = jnp.where(lane < NH, d * beta * (1.0 - beta), da).astype(o_ref.dtype)

        @pl.when(i == 0)
        def _():
            dal_ref[...] = jnp.zeros_like(dal_ref)
            ddt_ref[...] = jnp.zeros_like(ddt_ref)

        dal_ref[...] += jnp.sum(dgg, axis=0, keepdims=True)
        ddt_ref[...] += jnp.sum(da, axis=0, keepdims=True)

    row = pl.BlockSpec((tm, GATE_W), lambda i: (i, 0))
    vec = pl.BlockSpec((1, GATE_W), lambda i: (0, 0))
    return pl.pallas_call(
        body, grid=(t // tm,), in_specs=[row, row, vec, vec], out_specs=[row, vec, vec],
        out_shape=[jax.ShapeDtypeStruct((t, GATE_W), MXU_DTYPE), jax.ShapeDtypeStruct((1, GATE_W), F32),
                   jax.ShapeDtypeStruct((1, GATE_W), F32)],
        compiler_params=_cp(("arbitrary",)), name=name)(dbg, pg, alog, dtb)


def _chunk_masks():
    ii = lax.broadcasted_iota(jnp.int32, (CHUNK, CHUNK), 0)
    jj = lax.broadcasted_iota(jnp.int32, (CHUNK, CHUNK), 1)
    return ii, jj, ii >= jj, ii > jj


def _lane_col(x, lane, idx):
    return jnp.sum(jnp.where(lane == idx, x, 0.0), axis=1, keepdims=True)


def _head_terms(q, k, v, bg, gam_all, h, lane, low, strict):
    rowi = lax.broadcasted_iota(jnp.int32, (CHUNK, 1), 0)
    beta = _lane_col(bg, lane, h)
    gam = _lane_col(gam_all, lane, NH + h)
    onehot = jnp.where(lane == NH + h, 1.0, 0.0)
    gam_row = _mmx(onehot, gam_all, NT)
    dm = jnp.exp(jnp.where(low, gam - gam_row, -1e30))
    glast = jnp.sum(jnp.where(rowi == CHUNK - 1, gam, 0.0), axis=0, keepdims=True)
    eg = jnp.exp(gam)
    ekl = jnp.exp(glast - gam)
    kb = k * beta
    a_mat = jnp.where(strict, _mm1(kb, k, NT) * dm, 0.0)
    p_mat = jnp.where(low, _mm1(q, k, NT) * dm, 0.0)
    return dict(beta=beta, gam=gam, dm=dm, eg=eg, ekl=ekl, gl=jnp.exp(glast), kb=kb, vb=v * beta, kbg=kb * eg,
                a=a_mat, p=p_mat, qd=q * eg, kd=k * ekl, rowi=rowi)


def _unit_lower_inverse(a_mat, eye):
    n = -a_mat
    x = eye + n
    p = n
    for _ in range(5):
        p = _mm3(p, p, NN)
        x = x + _mm3(x, p, NN)
    return x


def delta_fwd(qkv, bg, name):
    t = qkv.shape[0]
    nc = t // CHUNK

    def body(q_ref, k_ref, v_ref, bg_ref, o_ref, sh_ref, mi_ref, s_ref):
        n = pl.program_id(0)

        @pl.when(n == 0)
        def _():
            s_ref[...] = jnp.zeros_like(s_ref)

        ii, jj, low, strict = _chunk_masks()
        eye = jnp.where(ii == jj, 1.0, 0.0)
        lane = lax.broadcasted_iota(jnp.int32, (CHUNK, GATE_W), 1)
        bg_v = bg_ref[...]
        gam_all = _mmx(jnp.where(low, 1.0, 0.0), bg_v, NN)
        for h in range(NH):
            sl = slice(h * DH, (h + 1) * DH)
            q, k, v = q_ref[:, sl], k_ref[:, sl], v_ref[:, sl]
            tm_ = _head_terms(q, k, v, bg_v, gam_all, h, lane, low, strict)
            m_inv = _unit_lower_inverse(tm_["a"], eye)
            u = _mm3(m_inv, tm_["vb"], NN)
            w = _mm3(m_inv, tm_["kbg"], NN)
            s = s_ref[h]
            vn = u - _mm1(w, s, NN)
            o_ref[:, sl] = _mm1(tm_["qd"], s, NN) + _mm1(tm_["p"], vn, NN)
            sh_ref[0, h] = s
            mi_ref[0, h] = m_inv
            s_ref[h] = tm_["gl"] * s + _mm1(tm_["kd"], vn, TN)

    col = lambda c: pl.BlockSpec((CHUNK, DQ), lambda n: (n, c))
    return pl.pallas_call(
        body, grid=(nc,),
        in_specs=[col(0), col(1), col(2), pl.BlockSpec((CHUNK, GATE_W), lambda n: (n, 0))],
        out_specs=[col(0), pl.BlockSpec((1, NH, DH, DH), lambda n: (n, 0, 0, 0)),
                   pl.BlockSpec((1, NH, CHUNK, CHUNK), lambda n: (n, 0, 0, 0))],
        out_shape=[jax.ShapeDtypeStruct((t, DQ), F32), jax.ShapeDtypeStruct((nc, NH, DH, DH), F32),
                   jax.ShapeDtypeStruct((nc, NH, CHUNK, CHUNK), F32)],
        scratch_shapes=[pltpu.VMEM((NH, DH, DH), F32)],
        compiler_params=_cp(("arbitrary",)), name=name)(qkv, qkv, qkv, bg)


def delta_bwd(qkv, bg, do, s_hist, m_hist, name):
    t = qkv.shape[0]
    nc = t // CHUNK

    def body(q_ref, k_ref, v_ref, bg_ref, do_ref, sh_ref, mi_ref, dq_ref, dk_ref, dv_ref, dbg_ref, ds_ref):
        n = pl.program_id(0)

        @pl.when(n == 0)
        def _():
            ds_ref[...] = jnp.zeros_like(ds_ref)

        ii, jj, low, strict = _chunk_masks()
        eye = jnp.where(ii == jj, 1.0, 0.0)
        lane = lax.broadcasted_iota(jnp.int32, (CHUNK, GATE_W), 1)
        bg_v = bg_ref[...]
        gam_all = _mmx(jnp.where(low, 1.0, 0.0), bg_v, NN)
        dbeta_all = jnp.zeros((CHUNK, GATE_W), F32)
        dgam_all = jnp.zeros((CHUNK, GATE_W), F32)
        for h in range(NH):
            sl = slice(h * DH, (h + 1) * DH)
            q, k, v, d_o = q_ref[:, sl], k_ref[:, sl], v_ref[:, sl], do_ref[:, sl]
            tm_ = _head_terms(q, k, v, bg_v, gam_all, h, lane, low, strict)
            beta, dm, eg, ekl, gl = tm_["beta"], tm_["dm"], tm_["eg"], tm_["ekl"], tm_["gl"]
            kb, kbg, qd, kd, a_mat, p_mat = tm_["kb"], tm_["kbg"], tm_["qd"], tm_["kd"], tm_["a"], tm_["p"]
            s = sh_ref[0, h]
            dsn = ds_ref[h]
            m_inv = mi_ref[0, h]
            u = _mm3(m_inv, tm_["vb"], NN)
            w = _mm3(m_inv, kbg, NN)
            vn = u - _mm1(w, s, NN)

            dvn = _mm1(p_mat, d_o, TN) + _mm1(kd, dsn, NN)
            dqd = _mm1(d_o, s, NT)
            dp = jnp.where(low, _mm1(d_o, vn, NT), 0.0)
            dkd = _mm1(vn, dsn, NT)
            dgl = jnp.sum(jnp.sum(s * dsn, axis=1, keepdims=True), axis=0, keepdims=True)
            dw = -_mm1(dvn, s, NT)
            ds_ref[h] = _mm1(qd, d_o, TN) + gl * dsn - _mm1(w, dvn, TN)
            dvb = _mm3(m_inv, dvn, TN)
            dkbg = _mm3(m_inv, dw, TN)
            da = -jnp.where(strict, _mm1(dvb, u, NT) + _mm1(dkbg, w, NT), 0.0)
            gm = da * dm
            hm = dp * dm
            dkb = _mm1(gm, k, NN) + dkbg * eg
            dk = _mm1(gm, kb, TN) + _mm1(hm, q, TN) + dkd * ekl + beta * dkb
            dq = _mm1(hm, k, NN) + dqd * eg
            em = da * a_mat + dp * p_mat
            col_sum = jnp.sum(_mm3(em, eye, TN), axis=1, keepdims=True)
            dkd_kd = jnp.sum(dkd * kd, axis=1, keepdims=True)
            dgam = (jnp.sum(em, axis=1, keepdims=True) - col_sum + jnp.sum(dqd * qd, axis=1, keepdims=True)
                    - dkd_kd + jnp.sum(dkbg * kbg, axis=1, keepdims=True))
            tail = jnp.sum(dkd_kd, axis=0, keepdims=True) + dgl * gl
            dgam = dgam + jnp.where(tm_["rowi"] == CHUNK - 1, tail, 0.0)
            dbeta = jnp.sum(dkb * k, axis=1, keepdims=True) + jnp.sum(dvb * v, axis=1, keepdims=True)
            dq_ref[:, sl] = dq
            dk_ref[:, sl] = dk
            dv_ref[:, sl] = beta * dvb
            dbeta_all = dbeta_all + jnp.where(lane == h, dbeta, 0.0)
            dgam_all = dgam_all + jnp.where(lane == NH + h, dgam, 0.0)
        dg_all = _mmx(jnp.where(ii <= jj, 1.0, 0.0), dgam_all, NN)
        dbg_ref[...] = jnp.where(lane < NH, dbeta_all, dg_all)

    col = lambda c: pl.BlockSpec((CHUNK, DQ), lambda n: (nc - 1 - n, c))
    gate = pl.BlockSpec((CHUNK, GATE_W), lambda n: (nc - 1 - n, 0))
    dq, dk, dv, dbg = pl.pallas_call(
        body, grid=(nc,),
        in_specs=[col(0), col(1), col(2), gate, col(0),
                  pl.BlockSpec((1, NH, DH, DH), lambda n: (nc - 1 - n, 0, 0, 0)),
                  pl.BlockSpec((1, NH, CHUNK, CHUNK), lambda n: (nc - 1 - n, 0, 0, 0))],
        out_specs=[col(0), col(0), col(0), gate],
        out_shape=[jax.ShapeDtypeStruct((t, DQ), F32)] * 3 + [jax.ShapeDtypeStruct((t, GATE_W), F32)],
        scratch_shapes=[pltpu.VMEM((NH, DH, DH), F32)],
        compiler_params=_cp(("arbitrary",)), name=name)(qkv, qkv, qkv, bg, do, s_hist, m_hist)
    return dq, dk, dv, dbg


def dn_post_fwd(ybuf, o, proj, nw, name):
    t = o.shape[0]
    tm = _row_tile(t)

    def body(y_in, o_ref, z_ref, nw_ref, y_ref):
        del y_in
        nwv = nw_ref[...]
        for h in range(NH):
            sl = slice(h * DH, (h + 1) * DH)
            oh = o_ref[:, sl]
            z = z_ref[:, sl]
            r = lax.rsqrt(jnp.mean(oh * oh, axis=-1, keepdims=True) + NORM_EPS)
            y_ref[:, sl] = (oh * r * nwv * (z * _sigmoid(z))).astype(y_ref.dtype)

    return pl.pallas_call(
        body, grid=(t // tm,),
        in_specs=[pl.BlockSpec(memory_space=pl.ANY), pl.BlockSpec((tm, DQ), lambda i: (i, 0)),
                  pl.BlockSpec((tm, DQ), lambda i: (i, 5)), pl.BlockSpec((1, DH), lambda i: (0, 0))],
        out_specs=pl.BlockSpec((tm, DQ), lambda i: (i, 1)),
        out_shape=jax.ShapeDtypeStruct(ybuf.shape, ybuf.dtype), input_output_aliases={0: 0},
        compiler_params=_cp(("parallel",)), name=name)(ybuf, o, proj, nw)


def dn_post_bwd(dproj, dy, o, proj, nw, name):
    t = o.shape[0]
    tm = _row_tile(t)

    def body(dp_in, dy_ref, o_ref, z_ref, nw_ref, do_ref, dp_ref, dnw_ref):
        del dp_in
        i = pl.program_id(0)
        nwv = nw_ref[...]
        acc = jnp.zeros((1, DH), F32)
        for h in range(NH):
            sl = slice(h * DH, (h + 1) * DH)
            oh = o_ref[:, sl]
            z = z_ref[:, sl]
            dyh = dy_ref[:, sl]
            r = lax.rsqrt(jnp.mean(oh * oh, axis=-1, keepdims=True) + NORM_EPS)
            xh = oh * r
            sg = _sigmoid(z)
            sz = z * sg
            dxh = dyh * nwv * sz
            do_ref[:, sl] = r * (dxh - xh * jnp.mean(dxh * xh, axis=-1, keepdims=True))
            dp_ref[:, sl] = (dyh * xh * nwv * (sg * (1.0 + z * (1.0 - sg)))).astype(dp_ref.dtype)
            acc = acc + jnp.sum(dyh * xh * sz, axis=0, keepdims=True)

        @pl.when(i == 0)
        def _():
            dnw_ref[...] = jnp.zeros_like(dnw_ref)

        dnw_ref[...] += acc

    vec = pl.BlockSpec((1, DH), lambda i: (0, 0))
    return pl.pallas_call(
        body, grid=(t // tm,),
        in_specs=[pl.BlockSpec(memory_space=pl.ANY), pl.BlockSpec((tm, DQ), lambda i: (i, 1)),
                  pl.BlockSpec((tm, DQ), lambda i: (i, 0)), pl.BlockSpec((tm, DQ), lambda i: (i, 5)), vec],
        out_specs=[pl.BlockSpec((tm, DQ), lambda i: (i, 0)), pl.BlockSpec((tm, DQ), lambda i: (i, 5)), vec],
        out_shape=[jax.ShapeDtypeStruct((t, DQ), F32), jax.ShapeDtypeStruct(dproj.shape, dproj.dtype),
                   jax.ShapeDtypeStruct((1, DH), F32)],
        input_output_aliases={0: 1}, compiler_params=_cp(("arbitrary",)), name=name)(dproj, dy, o, proj, nw)


def swiglu_fwd(gu, name):
    t = gu.shape[0]
    tm = _row_tile(t)
    tn = DFF // 2
    nb = DFF // tn

    def body(g_ref, u_ref, o_ref):
        g = g_ref[...]
        o_ref[...] = (g * _sigmoid(g) * u_ref[...]).astype(o_ref.dtype)

    return pl.pallas_call(
        body, grid=(t // tm, nb),
        in_specs=[pl.BlockSpec((tm, tn), lambda i, j: (i, j)), pl.BlockSpec((tm, tn), lambda i, j: (i, j + nb))],
        out_specs=pl.BlockSpec((tm, tn), lambda i, j: (i, j)),
        out_shape=jax.ShapeDtypeStruct((t, DFF), MXU_DTYPE),
        compiler_params=_cp(("parallel", "parallel")), name=name)(gu, gu)


def swiglu_bwd(gu, dact, name):
    t = gu.shape[0]
    tm = _row_tile(t)
    tn = DFF // 2
    nb = DFF // tn

    def body(g_ref, u_ref, d_ref, o_ref):
        j = pl.program_id(1)
        g = g_ref[...]
        sg = _sigmoid(g)
        d = d_ref[...]
        dgate = d * u_ref[...] * (sg * (1.0 + g * (1.0 - sg)))
        dup = d * (g * sg)
        o_ref[...] = jnp.where(j < nb, dgate, dup).astype(o_ref.dtype)

    return pl.pallas_call(
        body, grid=(t // tm, 2 * nb),
        in_specs=[pl.BlockSpec((tm, tn), lambda i, j: (i, j % nb)),
                  pl.BlockSpec((tm, tn), lambda i, j: (i, j % nb + nb)),
                  pl.BlockSpec((tm, tn), lambda i, j: (i, j % nb))],
        out_specs=pl.BlockSpec((tm, tn), lambda i, j: (i, j)),
        out_shape=jax.ShapeDtypeStruct((t, 2 * DFF), MXU_DTYPE),
        compiler_params=_cp(("parallel", "parallel")), name=name)(gu, gu, dact)


def _me_and_peers():
    mx, my, mc = lax.axis_index("x"), lax.axis_index("y"), lax.axis_index("c")
    me = 4 * mx + 2 * my + mc
    peers = []
    for kk in range(1, NDEV):
        px = 1 - mx if kk & 4 else mx
        py = 1 - my if kk & 2 else my
        pc = 1 - mc if kk & 1 else mc
        peers.append(((px, py, pc), 4 * px + 2 * py + pc))
    return me, peers


def all_gather(x, name):
    def body(x_ref, o_ref, send_sems, recv_sems, local_sem):
        me, peers = _me_and_peers()
        mine = pltpu.make_async_copy(x_ref, o_ref.at[me], local_sem)
        mine.start()
        sends = []
        for kk, (peer, _) in enumerate(peers):
            cp = pltpu.make_async_remote_copy(
                src_ref=x_ref, dst_ref=o_ref.at[me], send_sem=send_sems.at[kk], recv_sem=recv_sems.at[kk],
                device_id=peer, device_id_type=pl.DeviceIdType.MESH)
            cp.start()
            sends.append(cp)
        for kk, (peer, pidx) in enumerate(peers):
            pltpu.make_async_remote_copy(
                src_ref=x_ref, dst_ref=o_ref.at[pidx], send_sem=send_sems.at[kk], recv_sem=recv_sems.at[kk],
                device_id=peer, device_id_type=pl.DeviceIdType.MESH).wait_recv()
        for cp in sends:
            cp.wait_send()
        mine.wait()

    return pl.pallas_call(
        body, in_specs=[pl.BlockSpec(memory_space=pl.ANY)], out_specs=pl.BlockSpec(memory_space=pl.ANY),
        out_shape=jax.ShapeDtypeStruct((NDEV,) + x.shape, x.dtype),
        scratch_shapes=[pltpu.SemaphoreType.DMA((NDEV - 1,)), pltpu.SemaphoreType.DMA((NDEV - 1,)),
                        pltpu.SemaphoreType.DMA],
        name=name)(x)


def exchange(x, name):
    def body(x_ref, o_ref, send_sems, recv_sems, local_sem):
        me, peers = _me_and_peers()
        mine = pltpu.make_async_copy(x_ref.at[me], o_ref.at[me], local_sem)
        mine.start()
        sends = []
        for kk, (peer, pidx) in enumerate(peers):
            cp = pltpu.make_async_remote_copy(
                src_ref=x_ref.at[pidx], dst_ref=o_ref.at[me], send_sem=send_sems.at[kk], recv_sem=recv_sems.at[kk],
                device_id=peer, device_id_type=pl.DeviceIdType.MESH)
            cp.start()
            sends.append(cp)
        for kk, (peer, pidx) in enumerate(peers):
            pltpu.make_async_remote_copy(
                src_ref=x_ref.at[pidx], dst_ref=o_ref.at[pidx], send_sem=send_sems.at[kk], recv_sem=recv_sems.at[kk],
                device_id=peer, device_id_type=pl.DeviceIdType.MESH).wait_recv()
        for cp in sends:
            cp.wait_send()
        mine.wait()

    return pl.pallas_call(
        body, in_specs=[pl.BlockSpec(memory_space=pl.ANY)], out_specs=pl.BlockSpec(memory_space=pl.ANY),
        out_shape=jax.ShapeDtypeStruct(x.shape, x.dtype),
        scratch_shapes=[pltpu.SemaphoreType.DMA((NDEV - 1,)), pltpu.SemaphoreType.DMA((NDEV - 1,)),
                        pltpu.SemaphoreType.DMA],
        name=name)(x)


def adamw(recv, w, m, v, name):
    rows = w.shape[0]
    c1 = 1.0 - ADAM_B1 ** ADAM_STEP
    c2 = 1.0 - ADAM_B2 ** ADAM_STEP

    def body(r_ref, w_ref, m_ref, v_ref, g_ref, d_ref, m2_ref, v2_ref):
        g = r_ref[0]
        for j in range(1, NDEV):
            g = g + r_ref[j]
        m2 = ADAM_B1 * m_ref[...] + (1.0 - ADAM_B1) * g
        v2 = ADAM_B2 * v_ref[...] + (1.0 - ADAM_B2) * (g * g)
        g_ref[...] = g
        m2_ref[...] = m2
        v2_ref[...] = v2
        d_ref[...] = -ADAM_LR * ((m2 / c1) / (jnp.sqrt(v2 / c2) + ADAM_EPS) + ADAM_WD * w_ref[...])

    blk = pl.BlockSpec((ADAM_ROWS, D), lambda i: (i, 0))
    return pl.pallas_call(
        body, grid=(rows // ADAM_ROWS,),
        in_specs=[pl.BlockSpec((NDEV, ADAM_ROWS, D), lambda i: (0, i, 0)), blk, blk, blk],
        out_specs=[blk, blk, blk, blk], out_shape=[jax.ShapeDtypeStruct((rows, D), F32)] * 4,
        compiler_params=_cp(("parallel",)), name=name)(recv, w, m, v)


PARAM_KINDS = (
    ("meta_tokens", "col"), ("norm_mix_w", "rep"), ("w_in", "col"), ("conv_dw_w", "col"), ("conv_dw_b", "rep"),
    ("conv_ln_w", "rep"), ("conv_ln_b", "rep"), ("dn_conv_w", "col"), ("dn_A_log", "rep"), ("dn_dt_bias", "rep"),
    ("dn_norm_w", "rep"), ("w_out", "row"), ("norm_ffn_w", "rep"), ("ffn_w_gu", "col"), ("ffn_w_down", "row"),
    ("final_norm_w", "rep"))


def _pack_rows(flat_parts, row_multiple, axis):
    flat = jnp.concatenate(flat_parts, axis=axis)
    n = flat.shape[axis]
    total = -(-n // (row_multiple * D)) * (row_multiple * D)
    pad = [(0, 0)] * flat.ndim
    pad[axis] = (0, total - n)
    flat = jnp.pad(flat, pad)
    return flat.reshape(flat.shape[:axis] + (total // D, D))


def _unshard(g8, kind):
    if kind == "col":
        moved = jnp.moveaxis(g8, 0, -2)
        return moved.reshape(moved.shape[:-2] + (-1,))
    moved = jnp.moveaxis(g8, 0, 1)
    return moved.reshape((moved.shape[0], -1) + moved.shape[3:])


def _per_destination(full, kind):
    if kind == "col":
        split = full.reshape(full.shape[:-1] + (NDEV, full.shape[-1] // NDEV))
        return jnp.moveaxis(split, -2, 0).reshape(NDEV, -1)
    if kind == "row":
        split = full.reshape((full.shape[0], NDEV, full.shape[1] // NDEV) + full.shape[2:])
        return jnp.moveaxis(split, 1, 0).reshape(NDEV, -1)
    return jnp.broadcast_to(full.reshape(1, -1), (NDEV, full.size))


def _gather_group(shards, kinds, dtype, name):
    packed = _pack_rows([s.astype(dtype).reshape(-1) for s in shards], 16, 0)
    got = all_gather(packed, name).reshape(NDEV, -1)
    out, off = [], 0
    for s, kind in zip(shards, kinds):
        out.append(_unshard(got[:, off:off + s.size].reshape((NDEV,) + s.shape), kind))
        off += s.size
    return out


def _lane_row(vec4, width):
    return jnp.zeros((1, width), F32).at[0, NH:2 * NH].set(vec4)


def kernel(x, meta_tokens, norm_mix_w, w_in, conv_dw_w, conv_dw_b, conv_ln_w, conv_ln_b, dn_conv_w, dn_A_log, dn_dt_bias, dn_norm_w, w_out, norm_ffn_w, ffn_w_gu, ffn_w_down, final_norm_w, loss_target, m_meta_tokens, m_norm_mix_w, m_w_in, m_conv_dw_w, m_conv_dw_b, m_conv_ln_w, m_conv_ln_b, m_dn_conv_w, m_dn_A_log, m_dn_dt_bias, m_dn_norm_w, m_w_out, m_norm_ffn_w, m_ffn_w_gu, m_ffn_w_down, m_final_norm_w, v_meta_tokens, v_norm_mix_w, v_w_in, v_conv_dw_w, v_conv_dw_b, v_conv_ln_w, v_conv_ln_b, v_dn_conv_w, v_dn_A_log, v_dn_dt_bias, v_dn_norm_w, v_w_out, v_norm_ffn_w, v_ffn_w_gu, v_ffn_w_down, v_final_norm_w):
    weights = dict(meta_tokens=meta_tokens, norm_mix_w=norm_mix_w, w_in=w_in, conv_dw_w=conv_dw_w, conv_dw_b=conv_dw_b,
                   conv_ln_w=conv_ln_w, conv_ln_b=conv_ln_b, dn_conv_w=dn_conv_w, dn_A_log=dn_A_log,
                   dn_dt_bias=dn_dt_bias, dn_norm_w=dn_norm_w, w_out=w_out, norm_ffn_w=norm_ffn_w, ffn_w_gu=ffn_w_gu,
                   ffn_w_down=ffn_w_down, final_norm_w=final_norm_w)
    m_in = dict(meta_tokens=m_meta_tokens, norm_mix_w=m_norm_mix_w, w_in=m_w_in, conv_dw_w=m_conv_dw_w,
                conv_dw_b=m_conv_dw_b, conv_ln_w=m_conv_ln_w, conv_ln_b=m_conv_ln_b, dn_conv_w=m_dn_conv_w,
                dn_A_log=m_dn_A_log, dn_dt_bias=m_dn_dt_bias, dn_norm_w=m_dn_norm_w, w_out=m_w_out,
                norm_ffn_w=m_norm_ffn_w, ffn_w_gu=m_ffn_w_gu, ffn_w_down=m_ffn_w_down, final_norm_w=m_final_norm_w)
    v_in = dict(meta_tokens=v_meta_tokens, norm_mix_w=v_norm_mix_w, w_in=v_w_in, conv_dw_w=v_conv_dw_w,
                conv_dw_b=v_conv_dw_b, conv_ln_w=v_conv_ln_w, conv_ln_b=v_conv_ln_b, dn_conv_w=v_dn_conv_w,
                dn_A_log=v_dn_A_log, dn_dt_bias=v_dn_dt_bias, dn_norm_w=v_dn_norm_w, w_out=v_w_out,
                norm_ffn_w=v_norm_ffn_w, ffn_w_gu=v_ffn_w_gu, ffn_w_down=v_ffn_w_down, final_norm_w=v_final_norm_w)

    depth = w_in.shape[0]
    seq = x.shape[1]
    t = _padded_rows(seq)

    win_f, wout_f, wgu_f, wdown_f = _gather_group(
        [w_in, w_out, ffn_w_gu, ffn_w_down], ["col", "row", "col", "row"], MXU_DTYPE, "gather_matmul_weights")
    cdw_f, dcw_f, meta_f = _gather_group(
        [conv_dw_w, dn_conv_w, meta_tokens], ["col", "col", "col"], F32, "gather_small_weights")
    win_main = win_f[:, :, :PROJ_MAIN]
    win_gate = jnp.pad(win_f[:, :, PROJ_MAIN:], ((0, 0), (0, 0), (0, GATE_W - (D_IN - PROJ_MAIN))))
    cdw32 = jnp.pad(cdw_f, ((0, 0), (0, CONV_HALO - CONV_W), (0, 0)))
    dcw8 = jnp.pad(dcw_f, ((0, 0), (0, DN_HALO - DN_W), (0, 0)))

    h = jnp.concatenate([jnp.zeros((FRONT, D), F32), meta_f, x[0], jnp.zeros((t - HEAD - seq, D), F32)], axis=0)
    tgt = jnp.pad(loss_target[0], ((HEAD, t - HEAD - seq), (0, 0)))

    saved = []
    for l in range(depth):
        nmw, nfw = norm_mix_w[l][None], norm_ffn_w[l][None]
        cdb, clw, clb = conv_dw_b[l][None], conv_ln_w[l][None], conv_ln_b[l][None]
        alog, dtb, dnw = _lane_row(dn_A_log[l], GATE_W), _lane_row(dn_dt_bias[l], GATE_W), dn_norm_w[l][None]
        hn = rms_fwd(h, nmw, "rms_mix_fwd")
        proj = mm(hn, win_main[l], name="mm_proj")
        pg = mm(hn, win_gate[l], name="mm_proj_gate")
        ybuf, u1 = conv_fwd(proj, cdw32[l], cdb, clw, clb, seq, "conv_fwd")
        qkv = dn_pre_fwd(proj, dcw8[l], "dn_pre_fwd")
        bg = gate_fwd(pg, alog, dtb, seq, "gate_fwd")
        o, s_hist, m_hist = delta_fwd(qkv, bg, "delta_fwd")
        ybuf = dn_post_fwd(ybuf, o, proj, dnw, "dn_post_fwd")
        h_mid = mm(ybuf, wout_f[l], res=h, name="mm_out")
        hn2 = rms_fwd(h_mid, nfw, "rms_ffn_fwd")
        gu = mm(hn2, wgu_f[l], name="mm_gu")
        act = swiglu_fwd(gu, "swiglu_fwd")
        h_out = mm(act, wdown_f[l], res=h_mid, name="mm_down")
        saved.append(dict(h=h, hn=hn, proj=proj, pg=pg, ybuf=ybuf, u1=u1, qkv=qkv, bg=bg, o=o, s_hist=s_hist,
                          m_hist=m_hist, h_mid=h_mid, hn2=hn2, gu=gu, act=act, nmw=nmw, nfw=nfw, clw=clw, clb=clb,
                          alog=alog, dtb=dtb, dnw=dnw))
        h = h_out

    dh, loss_part, d_final = loss_bwd(h, tgt, final_norm_w[None], seq, "loss_bwd")
    loss = lax.psum(loss_part[0, 0], MESH_AXES)

    grads = {name: [None] * depth for name, _ in PARAM_KINDS}
    for l in reversed(range(depth)):
        s = saved[l]
        dact = mm(dh, wdown_f[l], tb=True, name="mm_down_dx")
        grads["ffn_w_down"][l] = mm(s["act"], dh, ta=True, name="mm_down_dw")
        dgu = swiglu_bwd(s["gu"], dact, "swiglu_bwd")
        dhn2 = mm(dgu, wgu_f[l], tb=True, name="mm_gu_dx")
        grads["ffn_w_gu"][l] = mm(s["hn2"], dgu, ta=True, name="mm_gu_dw")
        dh_mid, dnfw = rms_bwd(dhn2, s["h_mid"], s["nfw"], dh, "rms_ffn_bwd")
        dy = mm(dh_mid, wout_f[l], tb=True, name="mm_out_dx")
        grads["w_out"][l] = mm(s["ybuf"], dh_mid, ta=True, name="mm_out_dw")
        dproj, dcdw, dcdb, dclw, dclb = conv_bwd(dy, s["u1"], s["proj"], cdw32[l], s["clw"], s["clb"], seq, "conv_bwd")
        do, dproj, ddnw = dn_post_bwd(dproj, dy, s["o"], s["proj"], s["dnw"], "dn_post_bwd")
        dq, dk, dv, dbg = delta_bwd(s["qkv"], s["bg"], do, s["s_hist"], s["m_hist"], "delta_bwd")
        dqkv = jnp.concatenate([dq, dk, dv], axis=1)
        dproj, ddcw = dn_pre_bwd(dproj, dqkv, s["proj"], dcw8[l], "dn_pre_bwd")
        dpg, dalog, ddtb = gate_bwd(dbg, s["pg"], s["alog"], s["dtb"], seq, "gate_bwd")
        dhn_gate = mm(dpg, win_gate[l], tb=True, name="mm_proj_gate_dx")
        dhn = mm(dproj, win_main[l], tb=True, res=dhn_gate, name="mm_proj_dx")
        dwin_main = mm(s["hn"], dproj, ta=True, name="mm_proj_dw")
        dwin_gate = mm(s["hn"], dpg, ta=True, name="mm_proj_gate_dw")
        grads["w_in"][l] = jnp.concatenate([dwin_main, dwin_gate[:, :D_IN - PROJ_MAIN]], axis=1)
        dh, dnmw = rms_bwd(dhn, s["h"], s["nmw"], dh_mid, "rms_mix_bwd")
        grads["norm_mix_w"][l] = dnmw[0]
        grads["norm_ffn_w"][l] = dnfw[0]
        grads["conv_dw_w"][l] = dcdw[:CONV_W]
        grads["conv_dw_b"][l] = dcdb[0]
        grads["conv_ln_w"][l] = dclw[0]
        grads["conv_ln_b"][l] = dclb[0]
        grads["dn_conv_w"][l] = ddcw[:DN_W]
        grads["dn_A_log"][l] = dalog[0, NH:2 * NH]
        grads["dn_dt_bias"][l] = ddtb[0, NH:2 * NH]
        grads["dn_norm_w"][l] = ddnw[0]

    grad_x = dh[HEAD:HEAD + seq][None]
    full_grads = {name: (jnp.stack(g) if g[0] is not None else None) for name, g in grads.items()}
    full_grads["meta_tokens"] = dh[FRONT:HEAD]
    full_grads["final_norm_w"] = d_final[0]

    send = _pack_rows([_per_destination(full_grads[name], kind) for name, kind in PARAM_KINDS], ADAM_ROWS, 1)
    recv = exchange(send, "exchange_grads")
    pack_local = lambda tree: _pack_rows([tree[name].reshape(-1) for name, _ in PARAM_KINDS], ADAM_ROWS, 0)
    g_p, d_p, m_p, v_p = adamw(recv, pack_local(weights), pack_local(m_in), pack_local(v_in), "adamw")

    def unpack(packed):
        flat, out, off = packed.reshape(-1), [], 0
        for name, _ in PARAM_KINDS:
            wgt = weights[name]
            out.append(flat[off:off + wgt.size].reshape(wgt.shape))
            off += wgt.size
        return out

    return (loss, grad_x, *unpack(g_p), *unpack(d_p), *unpack(m_p), *unpack(v_p))
```

```python
import jax
import jax.numpy as jnp
from jax import lax
from jax.experimental import pallas as pl
from jax.experimental.pallas import tpu as pltpu

F32 = jnp.float32
MXU_DTYPE = jnp.bfloat16

D = 1024
N_META = 16
CHUNK = 64
CHUNK_LOG2 = 6
FRONT = CHUNK - N_META
HEAD = CHUNK
CONV_CH = 512
CONV_W = 31
CONV_HALO = 32
NH = 4
DH = 128
DQ = NH * DH
DN_W = 4
DN_HALO = 8
DFF = 2816
PROJ_MAIN = 3072
D_IN = 3080
GATE_W = 128
LANES = 128
NDEV = 8
NORM_EPS = 1e-6
LN_EPS = 1e-5
L2_EPS = 1e-6
VMEM_LIMIT_V7X = 48 * 1024 * 1024
ROW_TILE = 640
ROW_TILE_SMALL = 128
MM_TILES = (1408, 1024, 640, 512, 256, 128)
DELTA_CHUNKS = (4, 2, 1)
ADAM_BLOCK_BYTES = 8 * 1024 * 1024

ADAM_LR = 0.001
ADAM_B1 = 0.9
ADAM_B2 = 0.999
ADAM_EPS = 1e-08
ADAM_WD = 0.01
ADAM_STEP = 10

MESH_AXES = ("x", "y", "c")
NN = ((1,), (0,))
NT = ((1,), (1,))
TN = ((0,), (0,))

assert 1 << CHUNK_LOG2 == CHUNK


def _row_tile(t):
    return ROW_TILE if t % ROW_TILE == 0 else ROW_TILE_SMALL


def _padded_rows(seq):
    n = HEAD + seq
    tm = ROW_TILE if n >= 4 * ROW_TILE else ROW_TILE_SMALL
    return -(-n // tm) * tm


def _pick(n, prefs):
    for p in prefs:
        if n % p == 0:
            return p
    return n


def _lane_pad(n):
    return -(-n // LANES) * LANES


def _cp(sem):
    return pltpu.CompilerParams(dimension_semantics=sem, vmem_limit_bytes=VMEM_LIMIT_V7X)


def _sigmoid(x):
    return 1.0 / (1.0 + jnp.exp(-x))


def _softplus(x):
    return jnp.maximum(x, 0.0) + jnp.log(1.0 + jnp.exp(-jnp.abs(x)))


def _valid_rows(i, tm, seq, width, first=FRONT):
    rows = i * tm + lax.broadcasted_iota(jnp.int32, (tm, width), 0)
    return jnp.logical_and(rows >= first, rows < HEAD + seq)


def _dot(a, b, dims):
    return lax.dot_general(a, b, (dims, ((), ())), preferred_element_type=F32)


def _split(x, n):
    out, r = [], x
    for _ in range(n):
        p = r.astype(MXU_DTYPE)
        out.append(p)
        r = r - p.astype(F32)
    return out


def _mm1(a, b, dims):
    return _dot(a.astype(MXU_DTYPE), b.astype(MXU_DTYPE), dims)


def _mm1_many(a_list, b_list, dims):
    return [_mm1(a, b, dims) for a, b in zip(a_list, b_list)]


def _mm3_many(a_list, b_list, dims):
    sa = [_split(a, 2) for a in a_list]
    sb = [_split(b, 2) for b in b_list]
    hh = [_dot(x[0], y[0], dims) for x, y in zip(sa, sb)]
    hl = [_dot(x[0], y[1], dims) for x, y in zip(sa, sb)]
    lh = [_dot(x[1], y[0], dims) for x, y in zip(sa, sb)]
    return [p + (q + r) for p, q, r in zip(hh, hl, lh)]


def _mmx(e, b, dims):
    e = e.astype(MXU_DTYPE)
    b1, b2, b3 = _split(b, 3)
    return _dot(e, b1, dims) + (_dot(e, b2, dims) + _dot(e, b3, dims))


def mm(a, b, *, ta=False, tb=False, res=None, out_dtype=F32, name):
    (k_dim, m_dim) = a.shape if ta else a.shape[::-1]
    n_dim = b.shape[0] if tb else b.shape[1]
    assert (b.shape[1] if tb else b.shape[0]) == k_dim
    tm, tn, tk = (_pick(n, MM_TILES) for n in (m_dim, n_dim, k_dim))
    nk = k_dim // tk

    def body(*refs):
        if res is None:
            a_ref, b_ref, o_ref, acc_ref = refs
            r_ref = None
        else:
            a_ref, b_ref, r_ref, o_ref, acc_ref = refs
        k = pl.program_id(2)

        @pl.when(k == 0)
        def _():
            acc_ref[...] = jnp.zeros_like(acc_ref)

        dims = ((0,) if ta else (1,), (1,) if tb else (0,))
        acc_ref[...] += _mm1(a_ref[...], b_ref[...], dims)

        @pl.when(k == nk - 1)
        def _():
            out = acc_ref[...]
            if r_ref is not None:
                out = out + r_ref[...]
            o_ref[...] = out.astype(o_ref.dtype)

    a_spec = pl.BlockSpec((tk, tm), lambda i, j, k: (k, i)) if ta else pl.BlockSpec((tm, tk), lambda i, j, k: (i, k))
    b_spec = pl.BlockSpec((tn, tk), lambda i, j, k: (j, k)) if tb else pl.BlockSpec((tk, tn), lambda i, j, k: (k, j))
    o_spec = pl.BlockSpec((tm, tn), lambda i, j, k: (i, j))
    in_specs, args = [a_spec, b_spec], [a, b]
    if res is not None:
        in_specs.append(o_spec)
        args.append(res)
    return pl.pallas_call(
        body, grid=(m_dim // tm, n_dim // tn, nk), in_specs=in_specs, out_specs=o_spec,
        out_shape=jax.ShapeDtypeStruct((m_dim, n_dim), out_dtype),
        scratch_shapes=[pltpu.VMEM((tm, tn), F32)],
        compiler_params=_cp(("parallel", "parallel", "arbitrary")), name=name)(*args)


def rms_fwd(h, w, name):
    t = h.shape[0]
    tm = _row_tile(t)

    def body(h_ref, w_ref, o_ref):
        x = h_ref[...]
        r = lax.rsqrt(jnp.mean(x * x, axis=-1, keepdims=True) + NORM_EPS)
        o_ref[...] = (x * r * w_ref[...]).astype(o_ref.dtype)

    row = pl.BlockSpec((tm, D), lambda i: (i, 0))
    return pl.pallas_call(
        body, grid=(t // tm,), in_specs=[row, pl.BlockSpec((1, D), lambda i: (0, 0))], out_specs=row,
        out_shape=jax.ShapeDtypeStruct((t, D), MXU_DTYPE), compiler_params=_cp(("parallel",)), name=name)(h, w)


def rms_bwd(dy, h, w, dres, name):
    t = h.shape[0]
    tm = _row_tile(t)

    def body(dy_ref, h_ref, w_ref, dres_ref, dh_ref, dw_ref):
        i = pl.program_id(0)
        x = h_ref[...]
        r = lax.rsqrt(jnp.mean(x * x, axis=-1, keepdims=True) + NORM_EPS)
        xh = x * r
        g = dy_ref[...] * w_ref[...]
        dh_ref[...] = dres_ref[...] + r * (g - xh * jnp.mean(g * xh, axis=-1, keepdims=True))

        @pl.when(i == 0)
        def _():
            dw_ref[...] = jnp.zeros_like(dw_ref)

        dw_ref[...] += jnp.sum(dy_ref[...] * xh, axis=0, keepdims=True)

    row = pl.BlockSpec((tm, D), lambda i: (i, 0))
    vec = pl.BlockSpec((1, D), lambda i: (0, 0))
    return pl.pallas_call(
        body, grid=(t // tm,), in_specs=[row, row, vec, row], out_specs=[row, vec],
        out_shape=[jax.ShapeDtypeStruct((t, D), F32), jax.ShapeDtypeStruct((1, D), F32)],
        compiler_params=_cp(("arbitrary",)), name=name)(dy, h, w, dres)


def loss_bwd(h, tgt, w, seq, name):
    t = h.shape[0]
    tm = _row_tile(t)

    def body(h_ref, t_ref, w_ref, dh_ref, loss_ref, dw_ref):
        i = pl.program_id(0)
        x = h_ref[...]
        wv = w_ref[...]
        r = lax.rsqrt(jnp.mean(x * x, axis=-1, keepdims=True) + NORM_EPS)
        xh = x * r
        err = jnp.where(_valid_rows(i, tm, seq, D, HEAD), xh * wv - t_ref[...], 0.0)
        dy = err * (1.0 / D)
        g = dy * wv
        dh_ref[...] = r * (g - xh * jnp.mean(g * xh, axis=-1, keepdims=True))

        @pl.when(i == 0)
        def _():
            dw_ref[...] = jnp.zeros_like(dw_ref)
            loss_ref[...] = jnp.zeros_like(loss_ref)

        dw_ref[...] += jnp.sum(dy * xh, axis=0, keepdims=True)
        part = jnp.sum(jnp.sum(err * err, axis=1, keepdims=True), axis=0, keepdims=True) * (0.5 / D)
        loss_ref[...] += jnp.broadcast_to(part, loss_ref.shape)

    row = pl.BlockSpec((tm, D), lambda i: (i, 0))
    vec = pl.BlockSpec((1, D), lambda i: (0, 0))
    return pl.pallas_call(
        body, grid=(t // tm,), in_specs=[row, row, vec],
        out_specs=[row, pl.BlockSpec((1, LANES), lambda i: (0, 0)), vec],
        out_shape=[jax.ShapeDtypeStruct((t, D), F32), jax.ShapeDtypeStruct((1, LANES), F32),
                   jax.ShapeDtypeStruct((1, D), F32)],
        compiler_params=_cp(("arbitrary",)), name=name)(h, tgt, w)


def _layernorm_parts(u1):
    mu = jnp.mean(u1, axis=-1, keepdims=True)
    xc = u1 - mu
    rstd = lax.rsqrt(jnp.mean(xc * xc, axis=-1, keepdims=True) + LN_EPS)
    return xc * rstd, rstd


def conv_fwd(proj, w32, b, lw, lb, seq, name):
    t = proj.shape[0]
    tm = _row_tile(t)

    def body(cv_ref, cg_ref, w_ref, b_ref, lw_ref, lb_ref, y_ref, u1_ref, ext):
        i = pl.program_id(0)

        @pl.when(i == 0)
        def _():
            ext[0:CONV_HALO, :] = jnp.zeros((CONV_HALO, CONV_CH), F32)

        @pl.when(i > 0)
        def _():
            ext[0:CONV_HALO, :] = ext[tm:tm + CONV_HALO, :]

        ext[CONV_HALO:CONV_HALO + tm, :] = cv_ref[...] * _sigmoid(cg_ref[...])
        acc = jnp.broadcast_to(b_ref[...], (tm, CONV_CH))
        for j in range(CONV_W):
            acc = acc + w_ref[j:j + 1, :] * ext[pl.ds(CONV_HALO - (CONV_W - 1) + j, tm), :]
        u1_ref[...] = acc
        xh, _ = _layernorm_parts(acc)
        ln = xh * lw_ref[...] + lb_ref[...]
        y = ln * _sigmoid(ln)
        y_ref[...] = jnp.where(_valid_rows(i, tm, seq, CONV_CH), y, 0.0).astype(y_ref.dtype)

    half = lambda c: pl.BlockSpec((tm, CONV_CH), lambda i: (i, c))
    vec = pl.BlockSpec((1, CONV_CH), lambda i: (0, 0))
    return pl.pallas_call(
        body, grid=(t // tm,),
        in_specs=[half(0), half(1), pl.BlockSpec((CONV_HALO, CONV_CH), lambda i: (0, 0)), vec, vec, vec],
        out_specs=[half(0), half(0)],
        out_shape=[jax.ShapeDtypeStruct((t, D), MXU_DTYPE), jax.ShapeDtypeStruct((t, CONV_CH), F32)],
        scratch_shapes=[pltpu.VMEM((tm + CONV_HALO, CONV_CH), F32)],
        compiler_params=_cp(("arbitrary",)), name=name)(proj, proj, w32, b, lw, lb)


def conv_bwd(dy, u1, proj, w32, lw, lb, seq, name):
    t = proj.shape[0]
    tm = _row_tile(t)
    nt = t // tm
    per = tm // CONV_HALO

    def body(dy_ref, u1_ref, cv_ref, cg_ref, cvp_ref, cgp_ref, w_ref, lw_ref, lb_ref,
             dp_ref, dw_ref, db_ref, dlw_ref, dlb_ref, ext_d, ext_u):
        i = pl.program_id(0)
        tile = nt - 1 - i

        @pl.when(i == 0)
        def _():
            ext_d[tm:tm + CONV_HALO, :] = jnp.zeros((CONV_HALO, CONV_CH), F32)
            dw_ref[...] = jnp.zeros_like(dw_ref)
            db_ref[...] = jnp.zeros_like(db_ref)
            dlw_ref[...] = jnp.zeros_like(dlw_ref)
            dlb_ref[...] = jnp.zeros_like(dlb_ref)

        @pl.when(i > 0)
        def _():
            ext_d[tm:tm + CONV_HALO, :] = ext_d[0:CONV_HALO, :]

        xh, rstd = _layernorm_parts(u1_ref[...])
        lwv = lw_ref[...]
        ln = xh * lwv + lb_ref[...]
        sg = _sigmoid(ln)
        dln = jnp.where(_valid_rows(tile, tm, seq, CONV_CH), dy_ref[...], 0.0) * (sg * (1.0 + ln * (1.0 - sg)))
        dlw_ref[...] += jnp.sum(dln * xh, axis=0, keepdims=True)
        dlb_ref[...] += jnp.sum(dln, axis=0, keepdims=True)
        dxh = dln * lwv
        du1 = rstd * (dxh - jnp.mean(dxh, axis=-1, keepdims=True)
                      - xh * jnp.mean(dxh * xh, axis=-1, keepdims=True))
        db_ref[...] += jnp.sum(du1, axis=0, keepdims=True)
        ext_d[0:tm, :] = du1

        cv = cv_ref[...]
        sgc = _sigmoid(cg_ref[...])
        prev = cvp_ref[...] * _sigmoid(cgp_ref[...])
        ext_u[0:CONV_HALO, :] = jnp.where(tile > 0, prev, 0.0)
        ext_u[CONV_HALO:CONV_HALO + tm, :] = cv * sgc

        du0 = jnp.zeros((tm, CONV_CH), F32)
        for j in range(CONV_W):
            du0 = du0 + w_ref[j:j + 1, :] * ext_d[pl.ds(CONV_W - 1 - j, tm), :]
            dw_ref[j:j + 1, :] += jnp.sum(
                du1 * ext_u[pl.ds(CONV_HALO - (CONV_W - 1) + j, tm), :], axis=0, keepdims=True)
        dp_ref[:, 0:CONV_CH] = (du0 * sgc).astype(dp_ref.dtype)
        dp_ref[:, CONV_CH:2 * CONV_CH] = (du0 * cv * sgc * (1.0 - sgc)).astype(dp_ref.dtype)

    rev = lambda c: pl.BlockSpec((tm, CONV_CH), lambda i: (nt - 1 - i, c))
    prev = lambda c: pl.BlockSpec((CONV_HALO, CONV_CH), lambda i: (jnp.maximum((nt - 1 - i) * per - 1, 0), c))
    vec = pl.BlockSpec((1, CONV_CH), lambda i: (0, 0))
    wspec = pl.BlockSpec((CONV_HALO, CONV_CH), lambda i: (0, 0))
    return pl.pallas_call(
        body, grid=(nt,),
        in_specs=[rev(0), rev(0), rev(0), rev(1), prev(0), prev(1), wspec, vec, vec],
        out_specs=[pl.BlockSpec((tm, 2 * CONV_CH), lambda i: (nt - 1 - i, 0)), wspec, vec, vec, vec],
        out_shape=[jax.ShapeDtypeStruct((t, PROJ_MAIN), MXU_DTYPE), jax.ShapeDtypeStruct((CONV_HALO, CONV_CH), F32),
                   jax.ShapeDtypeStruct((1, CONV_CH), F32), jax.ShapeDtypeStruct((1, CONV_CH), F32),
                   jax.ShapeDtypeStruct((1, CONV_CH), F32)],
        scratch_shapes=[pltpu.VMEM((tm + CONV_HALO, CONV_CH), F32), pltpu.VMEM((tm + CONV_HALO, CONV_CH), F32)],
        compiler_params=_cp(("arbitrary",)), name=name)(dy, u1, proj, proj, proj, proj, w32, lw, lb)


def dn_pre_fwd(proj, w8, name):
    t = proj.shape[0]
    tm = _row_tile(t)

    def body(raw_ref, w_ref, o_ref, ext):
        g = pl.program_id(0)
        i = pl.program_id(1)

        @pl.when(i == 0)
        def _():
            ext[0:DN_HALO, :] = jnp.zeros((DN_HALO, DQ), F32)

        @pl.when(i > 0)
        def _():
            ext[0:DN_HALO, :] = ext[tm:tm + DN_HALO, :]

        ext[DN_HALO:DN_HALO + tm, :] = raw_ref[...]
        c = jnp.zeros((tm, DQ), F32)
        for j in range(DN_W):
            c = c + w_ref[j:j + 1, :] * ext[pl.ds(DN_HALO - (DN_W - 1) + j, tm), :]
        s = c * _sigmoid(c)
        scale = jnp.where(g == 0, DH ** -0.5, 1.0)
        for h in range(NH):
            sh = s[:, h * DH:(h + 1) * DH]
            r = lax.rsqrt(jnp.sum(sh * sh, axis=-1, keepdims=True) + L2_EPS)
            o_ref[:, h * DH:(h + 1) * DH] = jnp.where(g == 2, sh, sh * (r * scale))

    return pl.pallas_call(
        body, grid=(3, t // tm),
        in_specs=[pl.BlockSpec((tm, DQ), lambda g, i: (i, 2 + g)), pl.BlockSpec((DN_HALO, DQ), lambda g, i: (0, g))],
        out_specs=pl.BlockSpec((tm, DQ), lambda g, i: (i, g)),
        out_shape=jax.ShapeDtypeStruct((t, 3 * DQ), F32),
        scratch_shapes=[pltpu.VMEM((tm + DN_HALO, DQ), F32)],
        compiler_params=_cp(("arbitrary", "arbitrary")), name=name)(proj, w8)


def dn_pre_bwd(dproj, dqkv, proj, w8, name):
    t = proj.shape[0]
    tm = _row_tile(t)
    nt = t // tm
    per = tm // DN_HALO

    def body(dp_in, d_ref, raw_ref, rawp_ref, w_ref, dp_ref, dw_ref, ext_d, ext_r):
        del dp_in
        g = pl.program_id(0)
        i = pl.program_id(1)
        tile = nt - 1 - i

        @pl.when(i == 0)
        def _():
            ext_d[tm:tm + DN_HALO, :] = jnp.zeros((DN_HALO, DQ), F32)
            dw_ref[...] = jnp.zeros_like(dw_ref)

        @pl.when(i > 0)
        def _():
            ext_d[tm:tm + DN_HALO, :] = ext_d[0:DN_HALO, :]

        ext_r[0:DN_HALO, :] = jnp.where(tile > 0, rawp_ref[...], 0.0)
        ext_r[DN_HALO:DN_HALO + tm, :] = raw_ref[...]
        c = jnp.zeros((tm, DQ), F32)
        for j in range(DN_W):
            c = c + w_ref[j:j + 1, :] * ext_r[pl.ds(DN_HALO - (DN_W - 1) + j, tm), :]
        sg = _sigmoid(c)
        s = c * sg
        scale = jnp.where(g == 0, DH ** -0.5, 1.0)
        for h in range(NH):
            sl = slice(h * DH, (h + 1) * DH)
            sh = s[:, sl]
            dn = d_ref[:, sl]
            r = lax.rsqrt(jnp.sum(sh * sh, axis=-1, keepdims=True) + L2_EPS)
            unit = sh * r
            dsn = (r * scale) * (dn - unit * jnp.sum(dn * unit, axis=-1, keepdims=True))
            ds = jnp.where(g == 2, dn, dsn)
            ext_d[0:tm, sl] = ds * (sg[:, sl] * (1.0 + c[:, sl] * (1.0 - sg[:, sl])))
        dc = ext_d[0:tm, :]
        draw = jnp.zeros((tm, DQ), F32)
        for j in range(DN_W):
            draw = draw + w_ref[j:j + 1, :] * ext_d[pl.ds(DN_W - 1 - j, tm), :]
            dw_ref[j:j + 1, :] += jnp.sum(
                dc * ext_r[pl.ds(DN_HALO - (DN_W - 1) + j, tm), :], axis=0, keepdims=True)
        dp_ref[...] = draw.astype(dp_ref.dtype)

    return pl.pallas_call(
        body, grid=(3, nt),
        in_specs=[pl.BlockSpec(memory_space=pl.ANY),
                  pl.BlockSpec((tm, DQ), lambda g, i: (nt - 1 - i, g)),
                  pl.BlockSpec((tm, DQ), lambda g, i: (nt - 1 - i, 2 + g)),
                  pl.BlockSpec((DN_HALO, DQ), lambda g, i: (jnp.maximum((nt - 1 - i) * per - 1, 0), 2 + g)),
                  pl.BlockSpec((DN_HALO, DQ), lambda g, i: (0, g))],
        out_specs=[pl.BlockSpec((tm, DQ), lambda g, i: (nt - 1 - i, 2 + g)),
                   pl.BlockSpec((DN_HALO, DQ), lambda g, i: (0, g))],
        out_shape=[jax.ShapeDtypeStruct(dproj.shape, dproj.dtype), jax.ShapeDtypeStruct((DN_HALO, 3 * DQ), F32)],
        scratch_shapes=[pltpu.VMEM((tm + DN_HALO, DQ), F32), pltpu.VMEM((tm + DN_HALO, DQ), F32)],
        input_output_aliases={0: 0},
        compiler_params=_cp(("arbitrary", "arbitrary")), name=name)(dproj, dqkv, proj, proj, w8)


def gate_fwd(pg, alog, dtb, seq, name):
    t = pg.shape[0]
    tm = _row_tile(t)

    def body(x_ref, al_ref, dt_ref, o_ref):
        i = pl.program_id(0)
        x = x_ref[...]
        lane = lax.broadcasted_iota(jnp.int32, (tm, GATE_W), 1)
        gg = -jnp.exp(al_ref[...]) * _softplus(x + dt_ref[...])
        out = jnp.where(lane < NH, _sigmoid(x), jnp.where(lane < 2 * NH, gg, 0.0))
        o_ref[...] = jnp.where(_valid_rows(i, tm, seq, GATE_W), out, 0.0)

    row = pl.BlockSpec((tm, GATE_W), lambda i: (i, 0))
    vec = pl.BlockSpec((1, GATE_W), lambda i: (0, 0))
    return pl.pallas_call(
        body, grid=(t // tm,), in_specs=[row, vec, vec], out_specs=row,
        out_shape=jax.ShapeDtypeStruct((t, GATE_W), F32), compiler_params=_cp(("parallel",)), name=name)(pg, alog, dtb)


def gate_bwd(dbg, pg, alog, dtb, seq, name):
    t = pg.shape[0]
    tm = _row_tile(t)

    def body(d_ref, x_ref, al_ref, dt_ref, o_ref, dal_ref, ddt_ref):
        i = pl.program_id(0)
        x = x_ref[...]
        lane = lax.broadcasted_iota(jnp.int32, (tm, GATE_W), 1)
        d = jnp.where(_valid_rows(i, tm, seq, GATE_W), d_ref[...], 0.0)
        beta = _sigmoid(x)
        xs = x + dt_ref[...]
        e = -jnp.exp(al_ref[...])
        is_g = jnp.logical_and(lane >= NH, lane < 2 * NH)
        da = jnp.where(is_g, d * e * _sigmoid(xs), 0.0)
        dgg = jnp.where(is_g, d * e * _softplus(xs), 0.0)
        o_ref[...] = jnp.where(lane < NH, d * beta * (1.0 - beta), da).astype(o_ref.dtype)

        @pl.when(i == 0)
        def _():
            dal_ref[...] = jnp.zeros_like(dal_ref)
            ddt_ref[...] = jnp.zeros_like(ddt_ref)

        dal_ref[...] += jnp.sum(dgg, axis=0, keepdims=True)
        ddt_ref[...] += jnp.sum(da, axis=0, keepdims=True)

    row = pl.BlockSpec((tm, GATE_W), lambda i: (i, 0))
    vec = pl.BlockSpec((1, GATE_W), lambda i: (0, 0))
    return pl.pallas_call(
        body, grid=(t // tm,), in_specs=[row, row, vec, vec], out_specs=[row, vec, vec],
        out_shape=[jax.ShapeDtypeStruct((t, GATE_W), MXU_DTYPE), jax.ShapeDtypeStruct((1, GATE_W), F32),
                   jax.ShapeDtypeStruct((1, GATE_W), F32)],
        compiler_params=_cp(("arbitrary",)), name=name)(dbg, pg, alog, dtb)


def _chunk_masks():
    ii = lax.broadcasted_iota(jnp.int32, (CHUNK, CHUNK), 0)
    jj = lax.broadcasted_iota(jnp.int32, (CHUNK, CHUNK), 1)
    return ii, jj, ii >= jj, ii > jj


def _lane_col(x, lane, idx):
    return jnp.sum(jnp.where(lane == idx, x, 0.0), axis=1, keepdims=True)


def _delta_terms(q, k, v, bgs, nch, low, strict):
    idx = [(c, h) for c in range(nch) for h in range(NH)]
    lane = lax.broadcasted_iota(jnp.int32, (CHUNK, GATE_W), 1)
    rowi = lax.broadcasted_iota(jnp.int32, (CHUNK, 1), 0)
    r4 = lax.broadcasted_iota(jnp.int32, (NH * CHUNK, GATE_W), 0)
    l4 = lax.broadcasted_iota(jnp.int32, (NH * CHUNK, GATE_W), 1)
    sel = jnp.where(l4 == NH + jnp.right_shift(r4, CHUNK_LOG2), 1.0, 0.0)
    lowf = jnp.where(low, 1.0, 0.0)
    gam_all = [_mmx(lowf, b, NN) for b in bgs]
    gam_rows = [_mmx(sel, g, NT) for g in gam_all]
    beta = [_lane_col(bgs[c], lane, h) for c, h in idx]
    gam = [_lane_col(gam_all[c], lane, NH + h) for c, h in idx]
    dm = [jnp.exp(jnp.where(low, g - gam_rows[c][h * CHUNK:(h + 1) * CHUNK, :], -1e30))
          for g, (c, h) in zip(gam, idx)]
    glast = [jnp.sum(jnp.where(rowi == CHUNK - 1, g, 0.0), axis=0, keepdims=True) for g in gam]
    eg = [jnp.exp(g) for g in gam]
    ekl = [jnp.exp(gl - g) for gl, g in zip(glast, gam)]
    gl = [jnp.exp(x) for x in glast]
    kb = [x * b for x, b in zip(k, beta)]
    vb = [x * b for x, b in zip(v, beta)]
    kbg = [x * e for x, e in zip(kb, eg)]
    kk = _mm1_many(kb, k, NT)
    qk = _mm1_many(q, k, NT)
    a_mat = [jnp.where(strict, x * d, 0.0) for x, d in zip(kk, dm)]
    p_mat = [jnp.where(low, x * d, 0.0) for x, d in zip(qk, dm)]
    qd = [x * e for x, e in zip(q, eg)]
    kd = [x * e for x, e in zip(k, ekl)]
    return dict(idx=idx, beta=beta, dm=dm, eg=eg, ekl=ekl, gl=gl, kb=kb, vb=vb, kbg=kbg, a=a_mat, p=p_mat, qd=qd,
                kd=kd, lane=lane, rowi=rowi)


def _unit_lower_inverses(a_list, eye):
    p = [-a for a in a_list]
    x = [eye + n for n in p]
    for _ in range(CHUNK_LOG2 - 1):
        p = _mm3_many(p, p, NN)
        x = [xi + y for xi, y in zip(x, _mm3_many(x, p, NN))]
    return x


def _load_heads(ref, nch):
    return [ref[c * CHUNK:(c + 1) * CHUNK, h * DH:(h + 1) * DH] for c in range(nch) for h in range(NH)]


def delta_fwd(qkv, bg, name):
    t = qkv.shape[0]
    nc = t // CHUNK
    nch = _pick(nc, DELTA_CHUNKS)
    rows = nch * CHUNK

    def body(q_ref, k_ref, v_ref, bg_ref, o_ref, sh_ref, mi_ref, s_ref):
        n = pl.program_id(0)

        @pl.when(n == 0)
        def _():
            s_ref[...] = jnp.zeros_like(s_ref)

        ii, jj, low, strict = _chunk_masks()
        eye = jnp.where(ii == jj, 1.0, 0.0)
        q, k, v = _load_heads(q_ref, nch), _load_heads(k_ref, nch), _load_heads(v_ref, nch)
        bgs = [bg_ref[c * CHUNK:(c + 1) * CHUNK, :] for c in range(nch)]
        tm_ = _delta_terms(q, k, v, bgs, nch, low, strict)
        m_inv = _unit_lower_inverses(tm_["a"], eye)
        u = _mm3_many(m_inv, tm_["vb"], NN)
        w = _mm3_many(m_inv, tm_["kbg"], NN)
        for i, (c, h) in enumerate(tm_["idx"]):
            mi_ref[c, h] = m_inv[i]
        s = [s_ref[h] for h in range(NH)]
        for c in range(nch):
            pr = range(c * NH, (c + 1) * NH)
            ws = [_mm1(w[i], s[i - c * NH], NN) for i in pr]
            qs = [_mm1(tm_["qd"][i], s[i - c * NH], NN) for i in pr]
            vn = [u[i] - x for i, x in zip(pr, ws)]
            pv = [_mm1(tm_["p"][i], x, NN) for i, x in zip(pr, vn)]
            kv = [_mm1(tm_["kd"][i], x, TN) for i, x in zip(pr, vn)]
            for h in range(NH):
                o_ref[c * CHUNK:(c + 1) * CHUNK, h * DH:(h + 1) * DH] = qs[h] + pv[h]
                sh_ref[c, h] = s[h]
                s[h] = tm_["gl"][c * NH + h] * s[h] + kv[h]
        for h in range(NH):
            s_ref[h] = s[h]

    col = lambda c: pl.BlockSpec((rows, DQ), lambda n: (n, c))
    return pl.pallas_call(
        body, grid=(nc // nch,),
        in_specs=[col(0), col(1), col(2), pl.BlockSpec((rows, GATE_W), lambda n: (n, 0))],
        out_specs=[col(0), pl.BlockSpec((nch, NH, DH, DH), lambda n: (n, 0, 0, 0)),
                   pl.BlockSpec((nch, NH, CHUNK, CHUNK), lambda n: (n, 0, 0, 0))],
        out_shape=[jax.ShapeDtypeStruct((t, DQ), F32), jax.ShapeDtypeStruct((nc, NH, DH, DH), F32),
                   jax.ShapeDtypeStruct((nc, NH, CHUNK, CHUNK), F32)],
        scratch_shapes=[pltpu.VMEM((NH, DH, DH), F32)],
        compiler_params=_cp(("arbitrary",)), name=name)(qkv, qkv, qkv, bg)


def delta_bwd(qkv, bg, do, s_hist, m_hist, name):
    t = qkv.shape[0]
    nc = t // CHUNK
    nch = _pick(nc, DELTA_CHUNKS)
    rows = nch * CHUNK
    ng = nc // nch

    def body(q_ref, k_ref, v_ref, bg_ref, do_ref, sh_ref, mi_ref, dqkv_ref, dbg_ref, ds_ref):
        n = pl.program_id(0)

        @pl.when(n == 0)
        def _():
            ds_ref[...] = jnp.zeros_like(ds_ref)

        ii, jj, low, strict = _chunk_masks()
        eye = jnp.where(ii == jj, 1.0, 0.0)
        q, k, v = _load_heads(q_ref, nch), _load_heads(k_ref, nch), _load_heads(v_ref, nch)
        d_o = _load_heads(do_ref, nch)
        bgs = [bg_ref[c * CHUNK:(c + 1) * CHUNK, :] for c in range(nch)]
        tm_ = _delta_terms(q, k, v, bgs, nch, low, strict)
        idx, lane, rowi = tm_["idx"], tm_["lane"], tm_["rowi"]
        beta, dm, eg, ekl, gl = tm_["beta"], tm_["dm"], tm_["eg"], tm_["ekl"], tm_["gl"]
        kb, kbg, qd, kd, a_mat, p_mat = tm_["kb"], tm_["kbg"], tm_["qd"], tm_["kd"], tm_["a"], tm_["p"]
        s = [sh_ref[c, h] for c, h in idx]
        m_inv = [mi_ref[c, h] for c, h in idx]
        u = _mm3_many(m_inv, tm_["vb"], NN)
        w = _mm3_many(m_inv, kbg, NN)
        ws = _mm1_many(w, s, NN)
        vn = [x - y for x, y in zip(u, ws)]
        pdo = _mm1_many(p_mat, d_o, TN)
        qdo = _mm1_many(qd, d_o, TN)
        dqd = _mm1_many(d_o, s, NT)
        dp = [jnp.where(low, x, 0.0) for x in _mm1_many(d_o, vn, NT)]

        nprob = len(idx)
        dvn, dkd, dgl = [None] * nprob, [None] * nprob, [None] * nprob
        ds = [ds_ref[h] for h in range(NH)]
        for c in reversed(range(nch)):
            pr = list(range(c * NH, (c + 1) * NH))
            kds = [_mm1(kd[i], ds[i - c * NH], NN) for i in pr]
            for i, x in zip(pr, kds):
                dvn[i] = pdo[i] + x
            wdv = [_mm1(w[i], dvn[i], TN) for i in pr]
            for i in pr:
                h = i - c * NH
                dkd[i] = _mm1(vn[i], ds[h], NT)
                dgl[i] = jnp.sum(jnp.sum(s[i] * ds[h], axis=1, keepdims=True), axis=0, keepdims=True)
                ds[h] = qdo[i] + gl[i] * ds[h] - wdv[h]
        for h in range(NH):
            ds_ref[h] = ds[h]

        dw = [-x for x in _mm1_many(dvn, s, NT)]
        dvb = _mm3_many(m_inv, dvn, TN)
        dkbg = _mm3_many(m_inv, dw, TN)
        da1 = _mm1_many(dvb, u, NT)
        da2 = _mm1_many(dkbg, w, NT)
        da = [-jnp.where(strict, x + y, 0.0) for x, y in zip(da1, da2)]
        gm = [x * d for x, d in zip(da, dm)]
        hm = [x * d for x, d in zip(dp, dm)]
        gk = _mm1_many(gm, k, NN)
        gkb = _mm1_many(gm, kb, TN)
        hq = _mm1_many(hm, q, TN)
        hk = _mm1_many(hm, k, NN)
        em = [x * a + y * p for x, a, y, p in zip(da, a_mat, dp, p_mat)]
        em_t = _mm3_many(em, [eye] * nprob, TN)
        dbeta_all = [jnp.zeros((CHUNK, GATE_W), F32) for _ in range(nch)]
        dgam_all = [jnp.zeros((CHUNK, GATE_W), F32) for _ in range(nch)]
        for i, (c, h) in enumerate(idx):
            dkb = gk[i] + dkbg[i] * eg[i]
            dk = gkb[i] + hq[i] + dkd[i] * ekl[i] + beta[i] * dkb
            dq = hk[i] + dqd[i] * eg[i]
            dkd_kd = jnp.sum(dkd[i] * kd[i], axis=1, keepdims=True)
            dgam = (jnp.sum(em[i], axis=1, keepdims=True) - jnp.sum(em_t[i], axis=1, keepdims=True)
                    + jnp.sum(dqd[i] * qd[i], axis=1, keepdims=True) - dkd_kd
                    + jnp.sum(dkbg[i] * kbg[i], axis=1, keepdims=True))
            tail = jnp.sum(dkd_kd, axis=0, keepdims=True) + dgl[i] * gl[i]
            dgam = dgam + jnp.where(rowi == CHUNK - 1, tail, 0.0)
            dbeta = jnp.sum(dkb * k[i], axis=1, keepdims=True) + jnp.sum(dvb[i] * v[i], axis=1, keepdims=True)
            rs = slice(c * CHUNK, (c + 1) * CHUNK)
            dqkv_ref[rs, h * DH:(h + 1) * DH] = dq
            dqkv_ref[rs, DQ + h * DH:DQ + (h + 1) * DH] = dk
            dqkv_ref[rs, 2 * DQ + h * DH:2 * DQ + (h + 1) * DH] = beta[i] * dvb[i]
            dbeta_all[c] = dbeta_all[c] + jnp.where(lane == h, dbeta, 0.0)
            dgam_all[c] = dgam_all[c] + jnp.where(lane == NH + h, dgam, 0.0)
        upf = jnp.where(ii <= jj, 1.0, 0.0)
        for c in range(nch):
            dg_all = _mmx(upf, dgam_all[c], NN)
            dbg_ref[c * CHUNK:(c + 1) * CHUNK, :] = jnp.where(lane < NH, dbeta_all[c], dg_all)

    col = lambda c: pl.BlockSpec((rows, DQ), lambda n: (ng - 1 - n, c))
    gate = pl.BlockSpec((rows, GATE_W), lambda n: (ng - 1 - n, 0))
    return pl.pallas_call(
        body, grid=(ng,),
        in_specs=[col(0), col(1), col(2), gate, col(0),
                  pl.BlockSpec((nch, NH, DH, DH), lambda n: (ng - 1 - n, 0, 0, 0)),
                  pl.BlockSpec((nch, NH, CHUNK, CHUNK), lambda n: (ng - 1 - n, 0, 0, 0))],
        out_specs=[pl.BlockSpec((rows, 3 * DQ), lambda n: (ng - 1 - n, 0)), gate],
        out_shape=[jax.ShapeDtypeStruct((t, 3 * DQ), F32), jax.ShapeDtypeStruct((t, GATE_W), F32)],
        scratch_shapes=[pltpu.VMEM((NH, DH, DH), F32)],
        compiler_params=_cp(("arbitrary",)), name=name)(qkv, qkv, qkv, bg, do, s_hist, m_hist)


def dn_post_fwd(ybuf, o, proj, nw, name):
    t = o.shape[0]
    tm = _row_tile(t)

    def body(y_in, o_ref, z_ref, nw_ref, y_ref):
        del y_in
        nwv = nw_ref[...]
        for h in range(NH):
            sl = slice(h * DH, (h + 1) * DH)
            oh = o_ref[:, sl]
            z = z_ref[:, sl]
            r = lax.rsqrt(jnp.mean(oh * oh, axis=-1, keepdims=True) + NORM_EPS)
            y_ref[:, sl] = (oh * r * nwv * (z * _sigmoid(z))).astype(y_ref.dtype)

    return pl.pallas_call(
        body, grid=(t // tm,),
        in_specs=[pl.BlockSpec(memory_space=pl.ANY), pl.BlockSpec((tm, DQ), lambda i: (i, 0)),
                  pl.BlockSpec((tm, DQ), lambda i: (i, 5)), pl.BlockSpec((1, DH), lambda i: (0, 0))],
        out_specs=pl.BlockSpec((tm, DQ), lambda i: (i, 1)),
        out_shape=jax.ShapeDtypeStruct(ybuf.shape, ybuf.dtype), input_output_aliases={0: 0},
        compiler_params=_cp(("parallel",)), name=name)(ybuf, o, proj, nw)


def dn_post_bwd(dproj, dy, o, proj, nw, name):
    t = o.shape[0]
    tm = _row_tile(t)

    def body(dp_in, dy_ref, o_ref, z_ref, nw_ref, do_ref, dp_ref, dnw_ref):
        del dp_in
        i = pl.program_id(0)
        nwv = nw_ref[...]
        acc = jnp.zeros((1, DH), F32)
        for h in range(NH):
            sl = slice(h * DH, (h + 1) * DH)
            oh = o_ref[:, sl]
            z = z_ref[:, sl]
            dyh = dy_ref[:, sl]
            r = lax.rsqrt(jnp.mean(oh * oh, axis=-1, keepdims=True) + NORM_EPS)
            xh = oh * r
            sg = _sigmoid(z)
            sz = z * sg
            dxh = dyh * nwv * sz
            do_ref[:, sl] = r * (dxh - xh * jnp.mean(dxh * xh, axis=-1, keepdims=True))
            dp_ref[:, sl] = (dyh * xh * nwv * (sg * (1.0 + z * (1.0 - sg)))).astype(dp_ref.dtype)
            acc = acc + jnp.sum(dyh * xh * sz, axis=0, keepdims=True)

        @pl.when(i == 0)
        def _():
            dnw_ref[...] = jnp.zeros_like(dnw_ref)

        dnw_ref[...] += acc

    vec = pl.BlockSpec((1, DH), lambda i: (0, 0))
    return pl.pallas_call(
        body, grid=(t // tm,),
        in_specs=[pl.BlockSpec(memory_space=pl.ANY), pl.BlockSpec((tm, DQ), lambda i: (i, 1)),
                  pl.BlockSpec((tm, DQ), lambda i: (i, 0)), pl.BlockSpec((tm, DQ), lambda i: (i, 5)), vec],
        out_specs=[pl.BlockSpec((tm, DQ), lambda i: (i, 0)), pl.BlockSpec((tm, DQ), lambda i: (i, 5)), vec],
        out_shape=[jax.ShapeDtypeStruct((t, DQ), F32), jax.ShapeDtypeStruct(dproj.shape, dproj.dtype),
                   jax.ShapeDtypeStruct((1, DH), F32)],
        input_output_aliases={0: 1}, compiler_params=_cp(("arbitrary",)), name=name)(dproj, dy, o, proj, nw)


def swiglu_fwd(gu, name):
    t = gu.shape[0]
    tm = _row_tile(t)
    tn = DFF // 2
    nb = DFF // tn

    def body(g_ref, u_ref, o_ref):
        g = g_ref[...]
        o_ref[...] = (g * _sigmoid(g) * u_ref[...]).astype(o_ref.dtype)

    return pl.pallas_call(
        body, grid=(t // tm, nb),
        in_specs=[pl.BlockSpec((tm, tn), lambda i, j: (i, j)), pl.BlockSpec((tm, tn), lambda i, j: (i, j + nb))],
        out_specs=pl.BlockSpec((tm, tn), lambda i, j: (i, j)),
        out_shape=jax.ShapeDtypeStruct((t, DFF), MXU_DTYPE),
        compiler_params=_cp(("parallel", "parallel")), name=name)(gu, gu)


def swiglu_bwd(gu, dact, name):
    t = gu.shape[0]
    tm = _row_tile(t)
    tn = DFF // 2
    nb = DFF // tn

    def body(g_ref, u_ref, d_ref, o_ref):
        j = pl.program_id(1)
        g = g_ref[...]
        sg = _sigmoid(g)
        d = d_ref[...]
        dgate = d * u_ref[...] * (sg * (1.0 + g * (1.0 - sg)))
        dup = d * (g * sg)
        o_ref[...] = jnp.where(j < nb, dgate, dup).astype(o_ref.dtype)

    return pl.pallas_call(
        body, grid=(t // tm, 2 * nb),
        in_specs=[pl.BlockSpec((tm, tn), lambda i, j: (i, j % nb)),
                  pl.BlockSpec((tm, tn), lambda i, j: (i, j % nb + nb)),
                  pl.BlockSpec((tm, tn), lambda i, j: (i, j % nb))],
        out_specs=pl.BlockSpec((tm, tn), lambda i, j: (i, j)),
        out_shape=jax.ShapeDtypeStruct((t, 2 * DFF), MXU_DTYPE),
        compiler_params=_cp(("parallel", "parallel")), name=name)(gu, gu, dact)


def _shifted(first, second, s, lane):
    if s == 0:
        return first
    return jnp.where(lane < LANES - s, pltpu.roll(first, LANES - s, 1), pltpu.roll(second, LANES - s, 1))


def unshard_cols(g8, w, widths, name):
    _, r, wp = g8.shape
    rb = _pick(r, (256, 128, 64, 32, 16))

    def body(g_ref, *o_refs):
        lane = lax.broadcasted_iota(jnp.int32, (rb, LANES), 1)
        zeros = jnp.zeros((rb, LANES), F32)

        def src(j, ta):
            if j >= NDEV or ta * LANES >= wp:
                return zeros
            return g_ref[j, :, ta * LANES:(ta + 1) * LANES].astype(F32)

        base = 0
        for o_ref, width in zip(o_refs, widths):
            for b in range(width // LANES):
                c0 = base + b * LANES
                if c0 >= NDEV * w:
                    tile = zeros
                else:
                    j0, o0 = divmod(c0, w)
                    n0 = min(w - o0, LANES)
                    ta, s = divmod(o0, LANES)
                    tile = _shifted(src(j0, ta), src(j0, ta + 1), s, lane)
                    if n0 < LANES:
                        nxt = pltpu.roll(src(j0 + 1, 0), n0, 1) if j0 + 1 < NDEV else zeros
                        tile = jnp.where(lane < n0, tile, nxt)
                o_ref[:, b * LANES:(b + 1) * LANES] = tile.astype(o_ref.dtype)
            base += width

    return pl.pallas_call(
        body, grid=(r // rb,), in_specs=[pl.BlockSpec((NDEV, rb, wp), lambda i: (0, i, 0))],
        out_specs=[pl.BlockSpec((rb, width), lambda i: (i, 0)) for width in widths],
        out_shape=[jax.ShapeDtypeStruct((r, width), g8.dtype) for width in widths],
        compiler_params=_cp(("parallel",)), name=name)(g8)


def shard_cols(parts, w, name):
    r = parts[0].shape[0]
    wp = _lane_pad(w)
    rb = _pick(r, (256, 128, 64, 32, 16))
    tiles_of = [p.shape[1] // LANES for p in parts]

    def body(*refs):
        p_refs, o_ref = refs[:-1], refs[-1]
        lane = lax.broadcasted_iota(jnp.int32, (rb, LANES), 1)
        zeros = jnp.zeros((rb, LANES), F32)

        def glob(tile_idx):
            for p_ref, n_tiles in zip(p_refs, tiles_of):
                if tile_idx < n_tiles:
                    return p_ref[:, tile_idx * LANES:(tile_idx + 1) * LANES]
                tile_idx -= n_tiles
            return zeros

        for j in range(NDEV):
            for a in range(wp // LANES):
                nv = min(w - a * LANES, LANES)
                tb, s = divmod(w * j + a * LANES, LANES)
                tile = _shifted(glob(tb), glob(tb + 1), s, lane)
                if nv < LANES:
                    tile = jnp.where(lane < nv, tile, 0.0)
                o_ref[j, :, a * LANES:(a + 1) * LANES] = tile

    return pl.pallas_call(
        body, grid=(r // rb,), in_specs=[pl.BlockSpec((rb, p.shape[1]), lambda i: (i, 0)) for p in parts],
        out_specs=pl.BlockSpec((NDEV, rb, wp), lambda i: (0, i, 0)),
        out_shape=jax.ShapeDtypeStruct((NDEV, r, wp), F32),
        compiler_params=_cp(("parallel",)), name=name)(*parts)


def _me_and_peers():
    mx, my, mc = lax.axis_index("x"), lax.axis_index("y"), lax.axis_index("c")
    me = 4 * mx + 2 * my + mc
    peers = []
    for kk in range(1, NDEV):
        px = 1 - mx if kk & 4 else mx
        py = 1 - my if kk & 2 else my
        pc = 1 - mc if kk & 1 else mc
        peers.append(((px, py, pc), 4 * px + 2 * py + pc))
    return me, peers


def _push_to_all(xs, name, scatter):
    n = len(xs)
    npeer = NDEV - 1

    def body(*refs):
        x_refs, o_refs = refs[:n], refs[n:2 * n]
        send_sems, recv_sems, local_sems = refs[2 * n:]
        me, peers = _me_and_peers()
        local, sends = [], []
        for a in range(n):
            cp = pltpu.make_async_copy(x_refs[a].at[me] if scatter else x_refs[a], o_refs[a].at[me], local_sems.at[a])
            cp.start()
            local.append(cp)
        for a in range(n):
            for kk, (peer, pidx) in enumerate(peers):
                cp = pltpu.make_async_remote_copy(
                    src_ref=x_refs[a].at[pidx] if scatter else x_refs[a], dst_ref=o_refs[a].at[me],
                    send_sem=send_sems.at[a * npeer + kk], recv_sem=recv_sems.at[a * npeer + kk],
                    device_id=peer, device_id_type=pl.DeviceIdType.MESH)
                cp.start()
                sends.append(cp)
        for a in range(n):
            for kk, (peer, pidx) in enumerate(peers):
                pltpu.make_async_remote_copy(
                    src_ref=x_refs[a].at[pidx] if scatter else x_refs[a], dst_ref=o_refs[a].at[pidx],
                    send_sem=send_sems.at[a * npeer + kk], recv_sem=recv_sems.at[a * npeer + kk],
                    device_id=peer, device_id_type=pl.DeviceIdType.MESH).wait_recv()
        for cp in sends:
            cp.wait_send()
        for cp in local:
            cp.wait()

    any_spec = pl.BlockSpec(memory_space=pl.ANY)
    return pl.pallas_call(
        body, in_specs=[any_spec] * n, out_specs=[any_spec] * n,
        out_shape=[jax.ShapeDtypeStruct(x.shape if scatter else (NDEV,) + x.shape, x.dtype) for x in xs],
        scratch_shapes=[pltpu.SemaphoreType.DMA((n * npeer,)), pltpu.SemaphoreType.DMA((n * npeer,)),
                        pltpu.SemaphoreType.DMA((n,))],
        name=name)(*xs)


def adamw(recv, w, m, v, name):
    rows, cols = w.shape
    c1 = 1.0 - ADAM_B1 ** ADAM_STEP
    c2 = 1.0 - ADAM_B2 ** ADAM_STEP
    cap = ADAM_BLOCK_BYTES // (NDEV * cols * 4)
    rb = _pick(rows, [p for p in (2048, 1024, 512, 256, 128, 64, 32, 16, 8) if p <= cap])

    def body(r_ref, w_ref, m_ref, v_ref, g_ref, d_ref, m2_ref, v2_ref):
        g = r_ref[0]
        for j in range(1, NDEV):
            g = g + r_ref[j]
        m2 = ADAM_B1 * m_ref[...] + (1.0 - ADAM_B1) * g
        v2 = ADAM_B2 * v_ref[...] + (1.0 - ADAM_B2) * (g * g)
        g_ref[...] = g
        m2_ref[...] = m2
        v2_ref[...] = v2
        d_ref[...] = -ADAM_LR * ((m2 / c1) / (jnp.sqrt(v2 / c2) + ADAM_EPS) + ADAM_WD * w_ref[...])

    blk = pl.BlockSpec((rb, cols), lambda i: (i, 0))
    return pl.pallas_call(
        body, grid=(rows // rb,),
        in_specs=[pl.BlockSpec((NDEV, rb, cols), lambda i: (0, i, 0)), blk, blk, blk],
        out_specs=[blk, blk, blk, blk], out_shape=[jax.ShapeDtypeStruct((rows, cols), F32)] * 4,
        compiler_params=_cp(("parallel",)), name=name)(recv, w, m, v)


SMALL_SHARDED = ("meta_tokens", "conv_dw_w", "dn_conv_w")
SMALL_REPLICATED = ("norm_mix_w", "conv_dw_b", "conv_ln_w", "conv_ln_b", "dn_A_log", "dn_dt_bias", "dn_norm_w",
                    "norm_ffn_w", "final_norm_w")
PARAM_ORDER = ("meta_tokens", "norm_mix_w", "w_in", "conv_dw_w", "conv_dw_b", "conv_ln_w", "conv_ln_b", "dn_conv_w",
               "dn_A_log", "dn_dt_bias", "dn_norm_w", "w_out", "norm_ffn_w", "ffn_w_gu", "ffn_w_down", "final_norm_w")


def _pack_small(parts, axis):
    flat = jnp.concatenate(parts, axis=axis)
    n = flat.shape[axis]
    total = -(-n // (8 * LANES)) * (8 * LANES)
    pad = [(0, 0)] * flat.ndim
    pad[axis] = (0, total - n)
    flat = jnp.pad(flat, pad)
    return flat.reshape(flat.shape[:axis] + (total // LANES, LANES))


def _unshard_last(g8):
    moved = jnp.moveaxis(g8, 0, -2)
    return moved.reshape(moved.shape[:-2] + (-1,))


def _per_destination_last(full):
    split = full.reshape(full.shape[:-1] + (NDEV, full.shape[-1] // NDEV))
    return jnp.moveaxis(split, -2, 0).reshape(NDEV, -1)


def _rows_to_slots(full, depth):
    r = full.shape[1] // NDEV
    return jnp.moveaxis(full.reshape(depth, NDEV, r, full.shape[2]), 1, 0).reshape(NDEV, depth * r, full.shape[2])


def _slots_to_rows(g8, depth):
    r = g8.shape[1] // depth
    return jnp.moveaxis(g8.reshape(NDEV, depth, r, g8.shape[2]), 0, 1).reshape(depth, NDEV * r, g8.shape[2])


def _lane_row(vec4, width):
    return jnp.zeros((1, width), F32).at[0, NH:2 * NH].set(vec4)


def kernel(x, meta_tokens, norm_mix_w, w_in, conv_dw_w, conv_dw_b, conv_ln_w, conv_ln_b, dn_conv_w, dn_A_log, dn_dt_bias, dn_norm_w, w_out, norm_ffn_w, ffn_w_gu, ffn_w_down, final_norm_w, loss_target, m_meta_tokens, m_norm_mix_w, m_w_in, m_conv_dw_w, m_conv_dw_b, m_conv_ln_w, m_conv_ln_b, m_dn_conv_w, m_dn_A_log, m_dn_dt_bias, m_dn_norm_w, m_w_out, m_norm_ffn_w, m_ffn_w_gu, m_ffn_w_down, m_final_norm_w, v_meta_tokens, v_norm_mix_w, v_w_in, v_conv_dw_w, v_conv_dw_b, v_conv_ln_w, v_conv_ln_b, v_dn_conv_w, v_dn_A_log, v_dn_dt_bias, v_dn_norm_w, v_w_out, v_norm_ffn_w, v_ffn_w_gu, v_ffn_w_down, v_final_norm_w):
    weights = dict(meta_tokens=meta_tokens, norm_mix_w=norm_mix_w, w_in=w_in, conv_dw_w=conv_dw_w, conv_dw_b=conv_dw_b,
                   conv_ln_w=conv_ln_w, conv_ln_b=conv_ln_b, dn_conv_w=dn_conv_w, dn_A_log=dn_A_log,
                   dn_dt_bias=dn_dt_bias, dn_norm_w=dn_norm_w, w_out=w_out, norm_ffn_w=norm_ffn_w, ffn_w_gu=ffn_w_gu,
                   ffn_w_down=ffn_w_down, final_norm_w=final_norm_w)
    m_in = dict(meta_tokens=m_meta_tokens, norm_mix_w=m_norm_mix_w, w_in=m_w_in, conv_dw_w=m_conv_dw_w,
                conv_dw_b=m_conv_dw_b, conv_ln_w=m_conv_ln_w, conv_ln_b=m_conv_ln_b, dn_conv_w=m_dn_conv_w,
                dn_A_log=m_dn_A_log, dn_dt_bias=m_dn_dt_bias, dn_norm_w=m_dn_norm_w, w_out=m_w_out,
                norm_ffn_w=m_norm_ffn_w, ffn_w_gu=m_ffn_w_gu, ffn_w_down=m_ffn_w_down, final_norm_w=m_final_norm_w)
    v_in = dict(meta_tokens=v_meta_tokens, norm_mix_w=v_norm_mix_w, w_in=v_w_in, conv_dw_w=v_conv_dw_w,
                conv_dw_b=v_conv_dw_b, conv_ln_w=v_conv_ln_w, conv_ln_b=v_conv_ln_b, dn_conv_w=v_dn_conv_w,
                dn_A_log=v_dn_A_log, dn_dt_bias=v_dn_dt_bias, dn_norm_w=v_dn_norm_w, w_out=v_w_out,
                norm_ffn_w=v_norm_ffn_w, ffn_w_gu=v_ffn_w_gu, ffn_w_down=v_ffn_w_down, final_norm_w=v_final_norm_w)

    depth = w_in.shape[0]
    seq = x.shape[1]
    t = _padded_rows(seq)
    rows_d = depth * D
    win_w, gu_w = w_in.shape[2], ffn_w_gu.shape[2]
    win_wp, gu_wp = _lane_pad(win_w), _lane_pad(gu_w)

    def pad_cols(a, wp):
        return jnp.pad(a, ((0, 0), (0, 0), (0, wp - a.shape[2]))).reshape(rows_d, wp)

    def rows2d(a):
        return a.reshape(-1, a.shape[2])

    small_shards = [weights[n] for n in SMALL_SHARDED]
    g_win, g_gu, g_wout, g_down, g_small = _push_to_all(
        [pad_cols(w_in, win_wp).astype(MXU_DTYPE), pad_cols(ffn_w_gu, gu_wp).astype(MXU_DTYPE),
         rows2d(w_out).astype(MXU_DTYPE), rows2d(ffn_w_down).astype(MXU_DTYPE),
         _pack_small([s.reshape(-1) for s in small_shards], 0)], "gather_weights", scatter=False)
    win_main, win_gate = unshard_cols(g_win, win_w, [PROJ_MAIN, GATE_W], "unshard_w_in")
    win_main, win_gate = win_main.reshape(depth, D, PROJ_MAIN), win_gate.reshape(depth, D, GATE_W)
    wgu_f = unshard_cols(g_gu, gu_w, [2 * DFF], "unshard_w_gu")[0].reshape(depth, D, 2 * DFF)
    wout_f = _slots_to_rows(g_wout, depth)
    wdown_f = _slots_to_rows(g_down, depth)
    small_flat, off, small_full = g_small.reshape(NDEV, -1), 0, {}
    for n, s in zip(SMALL_SHARDED, small_shards):
        small_full[n] = _unshard_last(small_flat[:, off:off + s.size].reshape((NDEV,) + s.shape))
        off += s.size
    cdw32 = jnp.pad(small_full["conv_dw_w"], ((0, 0), (0, CONV_HALO - CONV_W), (0, 0)))
    dcw8 = jnp.pad(small_full["dn_conv_w"], ((0, 0), (0, DN_HALO - DN_W), (0, 0)))

    h = jnp.concatenate([jnp.zeros((FRONT, D), F32), small_full["meta_tokens"], x[0],
                         jnp.zeros((t - HEAD - seq, D), F32)], axis=0)
    tgt = jnp.pad(loss_target[0], ((HEAD, t - HEAD - seq), (0, 0)))

    saved = []
    for l in range(depth):
        nmw, nfw = norm_mix_w[l][None], norm_ffn_w[l][None]
        cdb, clw, clb = conv_dw_b[l][None], conv_ln_w[l][None], conv_ln_b[l][None]
        alog, dtb, dnw = _lane_row(dn_A_log[l], GATE_W), _lane_row(dn_dt_bias[l], GATE_W), dn_norm_w[l][None]
        hn = rms_fwd(h, nmw, "rms_mix_fwd")
        proj = mm(hn, win_main[l], name="mm_proj")
        pg = mm(hn, win_gate[l], name="mm_proj_gate")
        ybuf, u1 = conv_fwd(proj, cdw32[l], cdb, clw, clb, seq, "conv_fwd")
        qkv = dn_pre_fwd(proj, dcw8[l], "dn_pre_fwd")
        bg = gate_fwd(pg, alog, dtb, seq, "gate_fwd")
        o, s_hist, m_hist = delta_fwd(qkv, bg, "delta_fwd")
        ybuf = dn_post_fwd(ybuf, o, proj, dnw, "dn_post_fwd")
        h_mid = mm(ybuf, wout_f[l], res=h, name="mm_out")
        hn2 = rms_fwd(h_mid, nfw, "rms_ffn_fwd")
        gu = mm(hn2, wgu_f[l], name="mm_gu")
        act = swiglu_fwd(gu, "swiglu_fwd")
        h_out = mm(act, wdown_f[l], res=h_mid, name="mm_down")
        saved.append(dict(h=h, hn=hn, proj=proj, pg=pg, ybuf=ybuf, u1=u1, qkv=qkv, bg=bg, o=o, s_hist=s_hist,
                          m_hist=m_hist, h_mid=h_mid, hn2=hn2, gu=gu, act=act, nmw=nmw, nfw=nfw, clw=clw, clb=clb,
                          alog=alog, dtb=dtb, dnw=dnw))
        h = h_out

    dh, loss_part, d_final = loss_bwd(h, tgt, final_norm_w[None], seq, "loss_bwd")
    loss = lax.psum(loss_part[0, 0], MESH_AXES)

    per_layer = ("norm_mix_w", "win_main", "win_gate", "conv_dw_w", "conv_dw_b", "conv_ln_w", "conv_ln_b", "dn_conv_w",
                 "dn_A_log", "dn_dt_bias", "dn_norm_w", "w_out", "norm_ffn_w", "ffn_w_gu", "ffn_w_down")
    grads = {n: [None] * depth for n in per_layer}
    for l in reversed(range(depth)):
        s = saved[l]
        dact = mm(dh, wdown_f[l], tb=True, name="mm_down_dx")
        grads["ffn_w_down"][l] = mm(s["act"], dh, ta=True, name="mm_down_dw")
        dgu = swiglu_bwd(s["gu"], dact, "swiglu_bwd")
        dhn2 = mm(dgu, wgu_f[l], tb=True, name="mm_gu_dx")
        grads["ffn_w_gu"][l] = mm(s["hn2"], dgu, ta=True, name="mm_gu_dw")
        dh_mid, dnfw = rms_bwd(dhn2, s["h_mid"], s["nfw"], dh, "rms_ffn_bwd")
        dy = mm(dh_mid, wout_f[l], tb=True, name="mm_out_dx")
        grads["w_out"][l] = mm(s["ybuf"], dh_mid, ta=True, name="mm_out_dw")
        dproj, dcdw, dcdb, dclw, dclb = conv_bwd(dy, s["u1"], s["proj"], cdw32[l], s["clw"], s["clb"], seq, "conv_bwd")
        do, dproj, ddnw = dn_post_bwd(dproj, dy, s["o"], s["proj"], s["dnw"], "dn_post_bwd")
        dqkv, dbg = delta_bwd(s["qkv"], s["bg"], do, s["s_hist"], s["m_hist"], "delta_bwd")
        dproj, ddcw = dn_pre_bwd(dproj, dqkv, s["proj"], dcw8[l], "dn_pre_bwd")
        dpg, dalog, ddtb = gate_bwd(dbg, s["pg"], s["alog"], s["dtb"], seq, "gate_bwd")
        dhn_gate = mm(dpg, win_gate[l], tb=True, name="mm_proj_gate_dx")
        dhn = mm(dproj, win_main[l], tb=True, res=dhn_gate, name="mm_proj_dx")
        grads["win_main"][l] = mm(s["hn"], dproj, ta=True, name="mm_proj_dw")
        grads["win_gate"][l] = mm(s["hn"], dpg, ta=True, name="mm_proj_gate_dw")
        dh, dnmw = rms_bwd(dhn, s["h"], s["nmw"], dh_mid, "rms_mix_bwd")
        grads["norm_mix_w"][l] = dnmw[0]
        grads["norm_ffn_w"][l] = dnfw[0]
        grads["conv_dw_w"][l] = dcdw[:CONV_W]
        grads["conv_dw_b"][l] = dcdb[0]
        grads["conv_ln_w"][l] = dclw[0]
        grads["conv_ln_b"][l] = dclb[0]
        grads["dn_conv_w"][l] = ddcw[:DN_W]
        grads["dn_A_log"][l] = dalog[0, NH:2 * NH]
        grads["dn_dt_bias"][l] = ddtb[0, NH:2 * NH]
        grads["dn_norm_w"][l] = ddnw[0]

    grad_x = dh[HEAD:HEAD + seq][None]
    full = {n: jnp.stack(g) for n, g in grads.items()}
    full["meta_tokens"] = dh[FRONT:HEAD]
    full["final_norm_w"] = d_final[0]

    send_win = shard_cols([full["win_main"].reshape(rows_d, PROJ_MAIN), full["win_gate"].reshape(rows_d, GATE_W)],
                          win_w, "shard_w_in")
    send_gu = shard_cols([full["ffn_w_gu"].reshape(rows_d, 2 * DFF)], gu_w, "shard_w_gu")
    send_small = _pack_small(
        [_per_destination_last(full[n]) for n in SMALL_SHARDED]
        + [jnp.broadcast_to(full[n].reshape(1, -1), (NDEV, full[n].size)) for n in SMALL_REPLICATED], 1)
    r_win, r_gu, r_wout, r_down, r_small = _push_to_all(
        [send_win, send_gu, _rows_to_slots(full["w_out"], depth), _rows_to_slots(full["ffn_w_down"], depth),
         send_small], "exchange_grads", scatter=True)

    small_names = SMALL_SHARDED + SMALL_REPLICATED
    pack_local = lambda tree: _pack_small([tree[n].reshape(-1) for n in small_names], 0)
    results = {}

    def run_adamw(name, recv, prep, finish):
        outs = adamw(recv, prep(weights[name]), prep(m_in[name]), prep(v_in[name]), "adamw_" + name)
        results[name] = [finish(o) for o in outs]

    run_adamw("w_in", r_win, lambda a: pad_cols(a, win_wp),
              lambda o: o[:, :win_w].reshape(depth, D, win_w))
    run_adamw("ffn_w_gu", r_gu, lambda a: pad_cols(a, gu_wp), lambda o: o[:, :gu_w].reshape(depth, D, gu_w))
    run_adamw("w_out", r_wout, rows2d, lambda o: o.reshape(w_out.shape))
    run_adamw("ffn_w_down", r_down, rows2d, lambda o: o.reshape(ffn_w_down.shape))
    small_outs = adamw(r_small, pack_local(weights), pack_local(m_in), pack_local(v_in), "adamw_small")
    for kind in range(4):
        flat, off = small_outs[kind].reshape(-1), 0
        for n in small_names:
            wgt = weights[n]
            results.setdefault(n, [None] * 4)[kind] = flat[off:off + wgt.size].reshape(wgt.shape)
            off += wgt.size

    return (loss, grad_x, *[results[n][0] for n in PARAM_ORDER], *[results[n][1] for n in PARAM_ORDER],
            *[results[n][2] for n in PARAM_ORDER], *[results[n][3] for n in PARAM_ORDER])
```

```python
import jax
import jax.numpy as jnp
from jax import lax
from jax.experimental import pallas as pl
from jax.experimental.pallas import tpu as pltpu

F32 = jnp.float32
MXU_DTYPE = jnp.bfloat16

D = 1024
N_META = 16
CHUNK = 64
CHUNK_LOG2 = 6
FRONT = CHUNK - N_META
HEAD = CHUNK
CONV_CH = 512
CONV_W = 31
CONV_HALO = 32
NH = 4
DH = 128
DQ = NH * DH
DN_W = 4
DN_HALO = 8
DFF = 2816
PROJ_MAIN = 3072
D_IN = 3080
GATE_W = 128
LANES = 128
NDEV = 8
NORM_EPS = 1e-6
LN_EPS = 1e-5
L2_EPS = 1e-6
VMEM_LIMIT_V7X = 48 * 1024 * 1024
ROW_TILE = 640
ROW_TILE_SMALL = 128
MM_TILES = (1408, 1280, 1024, 640, 512, 256, 128)
GRAD_DTYPE = jnp.bfloat16
DELTA_CHUNKS = (4, 2, 1)
ADAM_BLOCK_BYTES = 8 * 1024 * 1024

ADAM_LR = 0.001
ADAM_B1 = 0.9
ADAM_B2 = 0.999
ADAM_EPS = 1e-08
ADAM_WD = 0.01
ADAM_STEP = 10

MESH_AXES = ("x", "y", "c")
NN = ((1,), (0,))
NT = ((1,), (1,))
TN = ((0,), (0,))

assert 1 << CHUNK_LOG2 == CHUNK


def _row_tile(t):
    return ROW_TILE if t % ROW_TILE == 0 else ROW_TILE_SMALL


def _padded_rows(seq):
    n = HEAD + seq
    tm = ROW_TILE if n >= 4 * ROW_TILE else ROW_TILE_SMALL
    return -(-n // tm) * tm


def _pick(n, prefs):
    for p in prefs:
        if n % p == 0:
            return p
    return n


def _lane_pad(n):
    return -(-n // LANES) * LANES


def _cp(sem):
    return pltpu.CompilerParams(dimension_semantics=sem, vmem_limit_bytes=VMEM_LIMIT_V7X)


def _sigmoid(x):
    return 1.0 / (1.0 + jnp.exp(-x))


def _softplus(x):
    return jnp.maximum(x, 0.0) + jnp.log(1.0 + jnp.exp(-jnp.abs(x)))


def _valid_rows(i, tm, seq, width, first=FRONT):
    rows = i * tm + lax.broadcasted_iota(jnp.int32, (tm, width), 0)
    return jnp.logical_and(rows >= first, rows < HEAD + seq)


def _dot(a, b, dims):
    return lax.dot_general(a, b, (dims, ((), ())), preferred_element_type=F32)


def _split(x, n):
    out, r = [], x
    for _ in range(n):
        p = r.astype(MXU_DTYPE)
        out.append(p)
        r = r - p.astype(F32)
    return out


def _mm1(a, b, dims):
    return _dot(a.astype(MXU_DTYPE), b.astype(MXU_DTYPE), dims)


def _mm1_many(a_list, b_list, dims):
    return [_mm1(a, b, dims) for a, b in zip(a_list, b_list)]


def _mm3_many(a_list, b_list, dims):
    sa = [_split(a, 2) for a in a_list]
    sb = [_split(b, 2) for b in b_list]
    hh = [_dot(x[0], y[0], dims) for x, y in zip(sa, sb)]
    hl = [_dot(x[0], y[1], dims) for x, y in zip(sa, sb)]
    lh = [_dot(x[1], y[0], dims) for x, y in zip(sa, sb)]
    return [p + (q + r) for p, q, r in zip(hh, hl, lh)]


def _mmx(e, b, dims):
    e = e.astype(MXU_DTYPE)
    b1, b2, b3 = _split(b, 3)
    return _dot(e, b1, dims) + (_dot(e, b2, dims) + _dot(e, b3, dims))


def mm(a, b, *, ta=False, tb=False, res=None, out_dtype=F32, name):
    (k_dim, m_dim) = a.shape if ta else a.shape[::-1]
    n_dim = b.shape[0] if tb else b.shape[1]
    assert (b.shape[1] if tb else b.shape[0]) == k_dim
    tm, tn, tk = (_pick(n, MM_TILES) for n in (m_dim, n_dim, k_dim))
    nk = k_dim // tk

    dims = ((0,) if ta else (1,), (1,) if tb else (0,))

    def body(*refs):
        a_ref, b_ref = refs[:2]
        r_ref = refs[2] if res is not None else None
        o_ref = refs[3] if res is not None else refs[2]

        def finish(out):
            if r_ref is not None:
                out = out + r_ref[...]
            o_ref[...] = out.astype(o_ref.dtype)

        if nk == 1:
            finish(_mm1(a_ref[...], b_ref[...], dims))
            return
        acc_ref = refs[-1]
        k = pl.program_id(2)

        @pl.when(k == 0)
        def _():
            acc_ref[...] = jnp.zeros_like(acc_ref)

        acc_ref[...] += _mm1(a_ref[...], b_ref[...], dims)

        @pl.when(k == nk - 1)
        def _():
            finish(acc_ref[...])

    a_spec = pl.BlockSpec((tk, tm), lambda i, j, k: (k, i)) if ta else pl.BlockSpec((tm, tk), lambda i, j, k: (i, k))
    b_spec = pl.BlockSpec((tn, tk), lambda i, j, k: (j, k)) if tb else pl.BlockSpec((tk, tn), lambda i, j, k: (k, j))
    o_spec = pl.BlockSpec((tm, tn), lambda i, j, k: (i, j))
    in_specs, args = [a_spec, b_spec], [a, b]
    if res is not None:
        in_specs.append(o_spec)
        args.append(res)
    return pl.pallas_call(
        body, grid=(m_dim // tm, n_dim // tn, nk), in_specs=in_specs, out_specs=o_spec,
        out_shape=jax.ShapeDtypeStruct((m_dim, n_dim), out_dtype),
        scratch_shapes=[pltpu.VMEM((tm, tn), F32)] if nk > 1 else [],
        compiler_params=_cp(("parallel", "parallel", "arbitrary")), name=name)(*args)


def rms_fwd(h, w, name):
    t = h.shape[0]
    tm = _row_tile(t)

    def body(h_ref, w_ref, o_ref):
        x = h_ref[...]
        r = lax.rsqrt(jnp.mean(x * x, axis=-1, keepdims=True) + NORM_EPS)
        o_ref[...] = (x * r * w_ref[...]).astype(o_ref.dtype)

    row = pl.BlockSpec((tm, D), lambda i: (i, 0))
    return pl.pallas_call(
        body, grid=(t // tm,), in_specs=[row, pl.BlockSpec((1, D), lambda i: (0, 0))], out_specs=row,
        out_shape=jax.ShapeDtypeStruct((t, D), MXU_DTYPE), compiler_params=_cp(("parallel",)), name=name)(h, w)


def rms_bwd(dy, h, w, dres, name):
    t = h.shape[0]
    tm = _row_tile(t)

    def body(dy_ref, h_ref, w_ref, dres_ref, dh_ref, dw_ref):
        i = pl.program_id(0)
        x = h_ref[...]
        r = lax.rsqrt(jnp.mean(x * x, axis=-1, keepdims=True) + NORM_EPS)
        xh = x * r
        g = dy_ref[...] * w_ref[...]
        dh_ref[...] = dres_ref[...] + r * (g - xh * jnp.mean(g * xh, axis=-1, keepdims=True))

        @pl.when(i == 0)
        def _():
            dw_ref[...] = jnp.zeros_like(dw_ref)

        dw_ref[...] += jnp.sum(dy_ref[...] * xh, axis=0, keepdims=True)

    row = pl.BlockSpec((tm, D), lambda i: (i, 0))
    vec = pl.BlockSpec((1, D), lambda i: (0, 0))
    return pl.pallas_call(
        body, grid=(t // tm,), in_specs=[row, row, vec, row], out_specs=[row, vec],
        out_shape=[jax.ShapeDtypeStruct((t, D), F32), jax.ShapeDtypeStruct((1, D), F32)],
        compiler_params=_cp(("arbitrary",)), name=name)(dy, h, w, dres)


def loss_bwd(h, tgt, w, seq, name):
    t = h.shape[0]
    tm = _row_tile(t)

    def body(h_ref, t_ref, w_ref, dh_ref, loss_ref, dw_ref):
        i = pl.program_id(0)
        x = h_ref[...]
        wv = w_ref[...]
        r = lax.rsqrt(jnp.mean(x * x, axis=-1, keepdims=True) + NORM_EPS)
        xh = x * r
        err = jnp.where(_valid_rows(i, tm, seq, D, HEAD), xh * wv - t_ref[...], 0.0)
        dy = err * (1.0 / D)
        g = dy * wv
        dh_ref[...] = r * (g - xh * jnp.mean(g * xh, axis=-1, keepdims=True))

        @pl.when(i == 0)
        def _():
            dw_ref[...] = jnp.zeros_like(dw_ref)
            loss_ref[...] = jnp.zeros_like(loss_ref)

        dw_ref[...] += jnp.sum(dy * xh, axis=0, keepdims=True)
        part = jnp.sum(jnp.sum(err * err, axis=1, keepdims=True), axis=0, keepdims=True) * (0.5 / D)
        loss_ref[...] += jnp.broadcast_to(part, loss_ref.shape)

    row = pl.BlockSpec((tm, D), lambda i: (i, 0))
    vec = pl.BlockSpec((1, D), lambda i: (0, 0))
    return pl.pallas_call(
        body, grid=(t // tm,), in_specs=[row, row, vec],
        out_specs=[row, pl.BlockSpec((1, LANES), lambda i: (0, 0)), vec],
        out_shape=[jax.ShapeDtypeStruct((t, D), F32), jax.ShapeDtypeStruct((1, LANES), F32),
                   jax.ShapeDtypeStruct((1, D), F32)],
        compiler_params=_cp(("arbitrary",)), name=name)(h, tgt, w)


def _layernorm_parts(u1):
    mu = jnp.mean(u1, axis=-1, keepdims=True)
    xc = u1 - mu
    rstd = lax.rsqrt(jnp.mean(xc * xc, axis=-1, keepdims=True) + LN_EPS)
    return xc * rstd, rstd


SUBLANES = 8


def _shift_copies(ext, sh, tm):
    for s in range(1, SUBLANES):
        sh[s - 1, :, :] = ext[pl.ds(s, tm + CONV_HALO - SUBLANES), :]


def _window(ext, sh, off, tm):
    s, m = off % SUBLANES, off // SUBLANES
    if s == 0:
        return ext[pl.ds(off, tm), :]
    return sh[s - 1, pl.ds(SUBLANES * m, tm), :]


def _shift_scratch(tm):
    return pltpu.VMEM((SUBLANES - 1, tm + CONV_HALO - SUBLANES, CONV_CH), F32)


def conv_fwd(proj, w32, b, lw, lb, seq, name):
    t = proj.shape[0]
    tm = _row_tile(t)

    def body(cv_ref, cg_ref, w_ref, b_ref, lw_ref, lb_ref, y_ref, u1_ref, ext, sh):
        i = pl.program_id(0)

        @pl.when(i == 0)
        def _():
            ext[0:CONV_HALO, :] = jnp.zeros((CONV_HALO, CONV_CH), F32)

        @pl.when(i > 0)
        def _():
            ext[0:CONV_HALO, :] = ext[tm:tm + CONV_HALO, :]

        ext[CONV_HALO:CONV_HALO + tm, :] = cv_ref[...] * _sigmoid(cg_ref[...])
        _shift_copies(ext, sh, tm)
        acc = jnp.broadcast_to(b_ref[...], (tm, CONV_CH))
        for j in range(CONV_W):
            acc = acc + w_ref[j:j + 1, :] * _window(ext, sh, CONV_HALO - (CONV_W - 1) + j, tm)
        u1_ref[...] = acc
        xh, _ = _layernorm_parts(acc)
        ln = xh * lw_ref[...] + lb_ref[...]
        y = ln * _sigmoid(ln)
        y_ref[...] = jnp.where(_valid_rows(i, tm, seq, CONV_CH), y, 0.0).astype(y_ref.dtype)

    half = lambda c: pl.BlockSpec((tm, CONV_CH), lambda i: (i, c))
    vec = pl.BlockSpec((1, CONV_CH), lambda i: (0, 0))
    return pl.pallas_call(
        body, grid=(t // tm,),
        in_specs=[half(0), half(1), pl.BlockSpec((CONV_HALO, CONV_CH), lambda i: (0, 0)), vec, vec, vec],
        out_specs=[half(0), half(0)],
        out_shape=[jax.ShapeDtypeStruct((t, D), MXU_DTYPE), jax.ShapeDtypeStruct((t, CONV_CH), F32)],
        scratch_shapes=[pltpu.VMEM((tm + CONV_HALO, CONV_CH), F32), _shift_scratch(tm)],
        compiler_params=_cp(("arbitrary",)), name=name)(proj, proj, w32, b, lw, lb)


def conv_bwd(dy, u1, proj, w32, lw, lb, seq, name):
    t = proj.shape[0]
    tm = _row_tile(t)
    nt = t // tm
    per = tm // CONV_HALO

    def body(dy_ref, u1_ref, cv_ref, cg_ref, cvp_ref, cgp_ref, w_ref, lw_ref, lb_ref,
             dp_ref, dw_ref, db_ref, dlw_ref, dlb_ref, ext_d, ext_u, sh_d, sh_u):
        i = pl.program_id(0)
        tile = nt - 1 - i

        @pl.when(i == 0)
        def _():
            ext_d[tm:tm + CONV_HALO, :] = jnp.zeros((CONV_HALO, CONV_CH), F32)
            dw_ref[...] = jnp.zeros_like(dw_ref)
            db_ref[...] = jnp.zeros_like(db_ref)
            dlw_ref[...] = jnp.zeros_like(dlw_ref)
            dlb_ref[...] = jnp.zeros_like(dlb_ref)

        @pl.when(i > 0)
        def _():
            ext_d[tm:tm + CONV_HALO, :] = ext_d[0:CONV_HALO, :]

        xh, rstd = _layernorm_parts(u1_ref[...])
        lwv = lw_ref[...]
        ln = xh * lwv + lb_ref[...]
        sg = _sigmoid(ln)
        dln = jnp.where(_valid_rows(tile, tm, seq, CONV_CH), dy_ref[...], 0.0) * (sg * (1.0 + ln * (1.0 - sg)))
        dlw_ref[...] += jnp.sum(dln * xh, axis=0, keepdims=True)
        dlb_ref[...] += jnp.sum(dln, axis=0, keepdims=True)
        dxh = dln * lwv
        du1 = rstd * (dxh - jnp.mean(dxh, axis=-1, keepdims=True)
                      - xh * jnp.mean(dxh * xh, axis=-1, keepdims=True))
        db_ref[...] += jnp.sum(du1, axis=0, keepdims=True)
        ext_d[0:tm, :] = du1

        cv = cv_ref[...]
        sgc = _sigmoid(cg_ref[...])
        prev = cvp_ref[...] * _sigmoid(cgp_ref[...])
        ext_u[0:CONV_HALO, :] = jnp.where(tile > 0, prev, 0.0)
        ext_u[CONV_HALO:CONV_HALO + tm, :] = cv * sgc

        _shift_copies(ext_d, sh_d, tm)
        _shift_copies(ext_u, sh_u, tm)
        du0 = jnp.zeros((tm, CONV_CH), F32)
        for j in range(CONV_W):
            du0 = du0 + w_ref[j:j + 1, :] * _window(ext_d, sh_d, CONV_W - 1 - j, tm)
            dw_ref[j:j + 1, :] += jnp.sum(
                du1 * _window(ext_u, sh_u, CONV_HALO - (CONV_W - 1) + j, tm), axis=0, keepdims=True)
        dp_ref[:, 0:CONV_CH] = (du0 * sgc).astype(dp_ref.dtype)
        dp_ref[:, CONV_CH:2 * CONV_CH] = (du0 * cv * sgc * (1.0 - sgc)).astype(dp_ref.dtype)

    rev = lambda c: pl.BlockSpec((tm, CONV_CH), lambda i: (nt - 1 - i, c))
    prev = lambda c: pl.BlockSpec((CONV_HALO, CONV_CH), lambda i: (jnp.maximum((nt - 1 - i) * per - 1, 0), c))
    vec = pl.BlockSpec((1, CONV_CH), lambda i: (0, 0))
    wspec = pl.BlockSpec((CONV_HALO, CONV_CH), lambda i: (0, 0))
    return pl.pallas_call(
        body, grid=(nt,),
        in_specs=[rev(0), rev(0), rev(0), rev(1), prev(0), prev(1), wspec, vec, vec],
        out_specs=[pl.BlockSpec((tm, 2 * CONV_CH), lambda i: (nt - 1 - i, 0)), wspec, vec, vec, vec],
        out_shape=[jax.ShapeDtypeStruct((t, PROJ_MAIN), MXU_DTYPE), jax.ShapeDtypeStruct((CONV_HALO, CONV_CH), F32),
                   jax.ShapeDtypeStruct((1, CONV_CH), F32), jax.ShapeDtypeStruct((1, CONV_CH), F32),
                   jax.ShapeDtypeStruct((1, CONV_CH), F32)],
        scratch_shapes=[pltpu.VMEM((tm + CONV_HALO, CONV_CH), F32), pltpu.VMEM((tm + CONV_HALO, CONV_CH), F32),
                        _shift_scratch(tm), _shift_scratch(tm)],
        compiler_params=_cp(("arbitrary",)), name=name)(dy, u1, proj, proj, proj, proj, w32, lw, lb)


def dn_pre_fwd(proj, w8, name):
    t = proj.shape[0]
    tm = _row_tile(t)

    def body(raw_ref, w_ref, o_ref, ext):
        g = pl.program_id(0)
        i = pl.program_id(1)

        @pl.when(i == 0)
        def _():
            ext[0:DN_HALO, :] = jnp.zeros((DN_HALO, DQ), F32)

        @pl.when(i > 0)
        def _():
            ext[0:DN_HALO, :] = ext[tm:tm + DN_HALO, :]

        ext[DN_HALO:DN_HALO + tm, :] = raw_ref[...]
        c = jnp.zeros((tm, DQ), F32)
        for j in range(DN_W):
            c = c + w_ref[j:j + 1, :] * ext[pl.ds(DN_HALO - (DN_W - 1) + j, tm), :]
        s = c * _sigmoid(c)
        scale = jnp.where(g == 0, DH ** -0.5, 1.0)
        for h in range(NH):
            sh = s[:, h * DH:(h + 1) * DH]
            r = lax.rsqrt(jnp.sum(sh * sh, axis=-1, keepdims=True) + L2_EPS)
            o_ref[:, h * DH:(h + 1) * DH] = jnp.where(g == 2, sh, sh * (r * scale))

    return pl.pallas_call(
        body, grid=(3, t // tm),
        in_specs=[pl.BlockSpec((tm, DQ), lambda g, i: (i, 2 + g)), pl.BlockSpec((DN_HALO, DQ), lambda g, i: (0, g))],
        out_specs=pl.BlockSpec((tm, DQ), lambda g, i: (i, g)),
        out_shape=jax.ShapeDtypeStruct((t, 3 * DQ), F32),
        scratch_shapes=[pltpu.VMEM((tm + DN_HALO, DQ), F32)],
        compiler_params=_cp(("arbitrary", "arbitrary")), name=name)(proj, w8)


def dn_pre_bwd(dproj, dqkv, proj, w8, name):
    t = proj.shape[0]
    tm = _row_tile(t)
    nt = t // tm
    per = tm // DN_HALO

    def body(dp_in, d_ref, raw_ref, rawp_ref, w_ref, dp_ref, dw_ref, ext_d, ext_r):
        del dp_in
        g = pl.program_id(0)
        i = pl.program_id(1)
        tile = nt - 1 - i

        @pl.when(i == 0)
        def _():
            ext_d[tm:tm + DN_HALO, :] = jnp.zeros((DN_HALO, DQ), F32)
            dw_ref[...] = jnp.zeros_like(dw_ref)

        @pl.when(i > 0)
        def _():
            ext_d[tm:tm + DN_HALO, :] = ext_d[0:DN_HALO, :]

        ext_r[0:DN_HALO, :] = jnp.where(tile > 0, rawp_ref[...], 0.0)
        ext_r[DN_HALO:DN_HALO + tm, :] = raw_ref[...]
        c = jnp.zeros((tm, DQ), F32)
        for j in range(DN_W):
            c = c + w_ref[j:j + 1, :] * ext_r[pl.ds(DN_HALO - (DN_W - 1) + j, tm), :]
        sg = _sigmoid(c)
        s = c * sg
        scale = jnp.where(g == 0, DH ** -0.5, 1.0)
        for h in range(NH):
            sl = slice(h * DH, (h + 1) * DH)
            sh = s[:, sl]
            dn = d_ref[:, sl]
            r = lax.rsqrt(jnp.sum(sh * sh, axis=-1, keepdims=True) + L2_EPS)
            unit = sh * r
            dsn = (r * scale) * (dn - unit * jnp.sum(dn * unit, axis=-1, keepdims=True))
            ds = jnp.where(g == 2, dn, dsn)
            ext_d[0:tm, sl] = ds * (sg[:, sl] * (1.0 + c[:, sl] * (1.0 - sg[:, sl])))
        dc = ext_d[0:tm, :]
        draw = jnp.zeros((tm, DQ), F32)
        for j in range(DN_W):
            draw = draw + w_ref[j:j + 1, :] * ext_d[pl.ds(DN_W - 1 - j, tm), :]
            dw_ref[j:j + 1, :] += jnp.sum(
                dc * ext_r[pl.ds(DN_HALO - (DN_W - 1) + j, tm), :], axis=0, keepdims=True)
        dp_ref[...] = draw.astype(dp_ref.dtype)

    return pl.pallas_call(
        body, grid=(3, nt),
        in_specs=[pl.BlockSpec(memory_space=pl.ANY),
                  pl.BlockSpec((tm, DQ), lambda g, i: (nt - 1 - i, g)),
                  pl.BlockSpec((tm, DQ), lambda g, i: (nt - 1 - i, 2 + g)),
                  pl.BlockSpec((DN_HALO, DQ), lambda g, i: (jnp.maximum((nt - 1 - i) * per - 1, 0), 2 + g)),
                  pl.BlockSpec((DN_HALO, DQ), lambda g, i: (0, g))],
        out_specs=[pl.BlockSpec((tm, DQ), lambda g, i: (nt - 1 - i, 2 + g)),
                   pl.BlockSpec((DN_HALO, DQ), lambda g, i: (0, g))],
        out_shape=[jax.ShapeDtypeStruct(dproj.shape, dproj.dtype), jax.ShapeDtypeStruct((DN_HALO, 3 * DQ), F32)],
        scratch_shapes=[pltpu.VMEM((tm + DN_HALO, DQ), F32), pltpu.VMEM((tm + DN_HALO, DQ), F32)],
        input_output_aliases={0: 0},
        compiler_params=_cp(("arbitrary", "arbitrary")), name=name)(dproj, dqkv, proj, proj, w8)


def gate_fwd(pg, alog, dtb, seq, name):
    t = pg.shape[0]
    tm = _row_tile(t)

    def body(x_ref, al_ref, dt_ref, o_ref):
        i = pl.program_id(0)
        x = x_ref[...]
        lane = lax.broadcasted_iota(jnp.int32, (tm, GATE_W), 1)
        gg = -jnp.exp(al_ref[...]) * _softplus(x + dt_ref[...])
        out = jnp.where(lane < NH, _sigmoid(x), jnp.where(lane < 2 * NH, gg, 0.0))
        o_ref[...] = jnp.where(_valid_rows(i, tm, seq, GATE_W), out, 0.0)

    row = pl.BlockSpec((tm, GATE_W), lambda i: (i, 0))
    vec = pl.BlockSpec((1, GATE_W), lambda i: (0, 0))
    return pl.pallas_call(
        body, grid=(t // tm,), in_specs=[row, vec, vec], out_specs=row,
        out_shape=jax.ShapeDtypeStruct((t, GATE_W), F32), compiler_params=_cp(("parallel",)), name=name)(pg, alog, dtb)


def gate_bwd(dbg, pg, alog, dtb, seq, name):
    t = pg.shape[0]
    tm = _row_tile(t)

    def body(d_ref, x_ref, al_ref, dt_ref, o_ref, dal_ref, ddt_ref):
        i = pl.program_id(0)
        x = x_ref[...]
        lane = lax.broadcasted_iota(jnp.int32, (tm, GATE_W), 1)
        d = jnp.where(_valid_rows(i, tm, seq, GATE_W), d_ref[...], 0.0)
        beta = _sigmoid(x)
        xs = x + dt_ref[...]
        e = -jnp.exp(al_ref[...])
        is_g = jnp.logical_and(lane >= NH, lane < 2 * NH)
        da = jnp.where(is_g, d * e * _sigmoid(xs), 0.0)
        dgg = jnp.where(is_g, d * e * _softplus(xs), 0.0)
        o_ref[...] = jnp.where(lane < NH, d * beta * (1.0 - beta), da).astype(o_ref.dtype)

        @pl.when(i == 0)
        def _():
            dal_ref[...] = jnp.zeros_like(dal_ref)
            ddt_ref[...] = jnp.zeros_like(ddt_ref)

        dal_ref[...] += jnp.sum(dgg, axis=0, keepdims=True)
        ddt_ref[...] += jnp.sum(da, axis=0, keepdims=True)

    row = pl.BlockSpec((tm, GATE_W), lambda i: (i, 0))
    vec = pl.BlockSpec((1, GATE_W), lambda i: (0, 0))
    return pl.pallas_call(
        body, grid=(t // tm,), in_specs=[row, row, vec, vec], out_specs=[row, vec, vec],
        out_shape=[jax.ShapeDtypeStruct((t, GATE_W), MXU_DTYPE), jax.ShapeDtypeStruct((1, GATE_W), F32),
                   jax.ShapeDtypeStruct((1, GATE_W), F32)],
        compiler_params=_cp(("arbitrary",)), name=name)(dbg, pg, alog, dtb)


def _chunk_masks():
    ii = lax.broadcasted_iota(jnp.int32, (CHUNK, CHUNK), 0)
    jj = lax.broadcasted_iota(jnp.int32, (CHUNK, CHUNK), 1)
    return ii, jj, ii >= jj, ii > jj


def _lane_col(x, lane, idx):
    return jnp.sum(jnp.where(lane == idx, x, 0.0), axis=1, keepdims=True)


def _delta_terms(q, k, v, bgs, nch, low, strict):
    idx = [(c, h) for c in range(nch) for h in range(NH)]
    lane = lax.broadcasted_iota(jnp.int32, (CHUNK, GATE_W), 1)
    rowi = lax.broadcasted_iota(jnp.int32, (CHUNK, 1), 0)
    r4 = lax.broadcasted_iota(jnp.int32, (NH * CHUNK, GATE_W), 0)
    l4 = lax.broadcasted_iota(jnp.int32, (NH * CHUNK, GATE_W), 1)
    sel = jnp.where(l4 == NH + jnp.right_shift(r4, CHUNK_LOG2), 1.0, 0.0)
    lowf = jnp.where(low, 1.0, 0.0)
    gam_all = [_mmx(lowf, b, NN) for b in bgs]
    gam_rows = [_mmx(sel, g, NT) for g in gam_all]
    beta = [_lane_col(bgs[c], lane, h) for c, h in idx]
    gam = [_lane_col(gam_all[c], lane, NH + h) for c, h in idx]
    dm = [jnp.exp(jnp.where(low, g - gam_rows[c][h * CHUNK:(h + 1) * CHUNK, :], -1e30))
          for g, (c, h) in zip(gam, idx)]
    glast = [jnp.sum(jnp.where(rowi == CHUNK - 1, g, 0.0), axis=0, keepdims=True) for g in gam]
    eg = [jnp.exp(g) for g in gam]
    ekl = [jnp.exp(gl - g) for gl, g in zip(glast, gam)]
    gl = [jnp.exp(x) for x in glast]
    kb = [x * b for x, b in zip(k, beta)]
    vb = [x * b for x, b in zip(v, beta)]
    kbg = [x * e for x, e in zip(kb, eg)]
    kk = _mm1_many(kb, k, NT)
    qk = _mm1_many(q, k, NT)
    a_mat = [jnp.where(strict, x * d, 0.0) for x, d in zip(kk, dm)]
    p_mat = [jnp.where(low, x * d, 0.0) for x, d in zip(qk, dm)]
    qd = [x * e for x, e in zip(q, eg)]
    kd = [x * e for x, e in zip(k, ekl)]
    return dict(idx=idx, beta=beta, dm=dm, eg=eg, ekl=ekl, gl=gl, kb=kb, vb=vb, kbg=kbg, a=a_mat, p=p_mat, qd=qd,
                kd=kd, lane=lane, rowi=rowi)


def _unit_lower_inverses(a_list, eye):
    p = [-a for a in a_list]
    x = [eye + n for n in p]
    for _ in range(CHUNK_LOG2 - 1):
        p = _mm3_many(p, p, NN)
        x = [xi + y for xi, y in zip(x, _mm3_many(x, p, NN))]
    return x


def _load_heads(ref, nch):
    return [ref[c * CHUNK:(c + 1) * CHUNK, h * DH:(h + 1) * DH] for c in range(nch) for h in range(NH)]


def delta_fwd(qkv, bg, name):
    t = qkv.shape[0]
    nc = t // CHUNK
    nch = _pick(nc, DELTA_CHUNKS)
    rows = nch * CHUNK

    def body(q_ref, k_ref, v_ref, bg_ref, o_ref, sh_ref, mi_ref, s_ref):
        n = pl.program_id(0)

        @pl.when(n == 0)
        def _():
            s_ref[...] = jnp.zeros_like(s_ref)

        ii, jj, low, strict = _chunk_masks()
        eye = jnp.where(ii == jj, 1.0, 0.0)
        q, k, v = _load_heads(q_ref, nch), _load_heads(k_ref, nch), _load_heads(v_ref, nch)
        bgs = [bg_ref[c * CHUNK:(c + 1) * CHUNK, :] for c in range(nch)]
        tm_ = _delta_terms(q, k, v, bgs, nch, low, strict)
        m_inv = _unit_lower_inverses(tm_["a"], eye)
        u = _mm3_many(m_inv, tm_["vb"], NN)
        w = _mm3_many(m_inv, tm_["kbg"], NN)
        for i, (c, h) in enumerate(tm_["idx"]):
            mi_ref[c, h] = m_inv[i]
        s = [s_ref[h] for h in range(NH)]
        for c in range(nch):
            pr = range(c * NH, (c + 1) * NH)
            ws = [_mm1(w[i], s[i - c * NH], NN) for i in pr]
            qs = [_mm1(tm_["qd"][i], s[i - c * NH], NN) for i in pr]
            vn = [u[i] - x for i, x in zip(pr, ws)]
            pv = [_mm1(tm_["p"][i], x, NN) for i, x in zip(pr, vn)]
            kv = [_mm1(tm_["kd"][i], x, TN) for i, x in zip(pr, vn)]
            for h in range(NH):
                o_ref[c * CHUNK:(c + 1) * CHUNK, h * DH:(h + 1) * DH] = qs[h] + pv[h]
                sh_ref[c, h] = s[h]
                s[h] = tm_["gl"][c * NH + h] * s[h] + kv[h]
        for h in range(NH):
            s_ref[h] = s[h]

    col = lambda c: pl.BlockSpec((rows, DQ), lambda n: (n, c))
    return pl.pallas_call(
        body, grid=(nc // nch,),
        in_specs=[col(0), col(1), col(2), pl.BlockSpec((rows, GATE_W), lambda n: (n, 0))],
        out_specs=[col(0), pl.BlockSpec((nch, NH, DH, DH), lambda n: (n, 0, 0, 0)),
                   pl.BlockSpec((nch, NH, CHUNK, CHUNK), lambda n: (n, 0, 0, 0))],
        out_shape=[jax.ShapeDtypeStruct((t, DQ), F32), jax.ShapeDtypeStruct((nc, NH, DH, DH), F32),
                   jax.ShapeDtypeStruct((nc, NH, CHUNK, CHUNK), F32)],
        scratch_shapes=[pltpu.VMEM((NH, DH, DH), F32)],
        compiler_params=_cp(("arbitrary",)), name=name)(qkv, qkv, qkv, bg)


def delta_bwd(qkv, bg, do, s_hist, m_hist, name):
    t = qkv.shape[0]
    nc = t // CHUNK
    nch = _pick(nc, DELTA_CHUNKS)
    rows = nch * CHUNK
    ng = nc // nch

    def body(q_ref, k_ref, v_ref, bg_ref, do_ref, sh_ref, mi_ref, dqkv_ref, dbg_ref, ds_ref):
        n = pl.program_id(0)

        @pl.when(n == 0)
        def _():
            ds_ref[...] = jnp.zeros_like(ds_ref)

        ii, jj, low, strict = _chunk_masks()
        eye = jnp.where(ii == jj, 1.0, 0.0)
        q, k, v = _load_heads(q_ref, nch), _load_heads(k_ref, nch), _load_heads(v_ref, nch)
        d_o = _load_heads(do_ref, nch)
        bgs = [bg_ref[c * CHUNK:(c + 1) * CHUNK, :] for c in range(nch)]
        tm_ = _delta_terms(q, k, v, bgs, nch, low, strict)
        idx, lane, rowi = tm_["idx"], tm_["lane"], tm_["rowi"]
        beta, dm, eg, ekl, gl = tm_["beta"], tm_["dm"], tm_["eg"], tm_["ekl"], tm_["gl"]
        kb, kbg, qd, kd, a_mat, p_mat = tm_["kb"], tm_["kbg"], tm_["qd"], tm_["kd"], tm_["a"], tm_["p"]
        s = [sh_ref[c, h] for c, h in idx]
        m_inv = [mi_ref[c, h] for c, h in idx]
        u = _mm3_many(m_inv, tm_["vb"], NN)
        w = _mm3_many(m_inv, kbg, NN)
        ws = _mm1_many(w, s, NN)
        vn = [x - y for x, y in zip(u, ws)]
        pdo = _mm1_many(p_mat, d_o, TN)
        qdo = _mm1_many(qd, d_o, TN)
        dqd = _mm1_many(d_o, s, NT)
        dp = [jnp.where(low, x, 0.0) for x in _mm1_many(d_o, vn, NT)]

        nprob = len(idx)
        dvn, dkd, dgl = [None] * nprob, [None] * nprob, [None] * nprob
        ds = [ds_ref[h] for h in range(NH)]
        for c in reversed(range(nch)):
            pr = list(range(c * NH, (c + 1) * NH))
            kds = [_mm1(kd[i], ds[i - c * NH], NN) for i in pr]
            for i, x in zip(pr, kds):
                dvn[i] = pdo[i] + x
            wdv = [_mm1(w[i], dvn[i], TN) for i in pr]
            for i in pr:
                h = i - c * NH
                dkd[i] = _mm1(vn[i], ds[h], NT)
                dgl[i] = jnp.sum(jnp.sum(s[i] * ds[h], axis=1, keepdims=True), axis=0, keepdims=True)
                ds[h] = qdo[i] + gl[i] * ds[h] - wdv[h]
        for h in range(NH):
            ds_ref[h] = ds[h]

        dw = [-x for x in _mm1_many(dvn, s, NT)]
        dvb = _mm3_many(m_inv, dvn, TN)
        dkbg = _mm3_many(m_inv, dw, TN)
        da1 = _mm1_many(dvb, u, NT)
        da2 = _mm1_many(dkbg, w, NT)
        da = [-jnp.where(strict, x + y, 0.0) for x, y in zip(da1, da2)]
        gm = [x * d for x, d in zip(da, dm)]
        hm = [x * d for x, d in zip(dp, dm)]
        gk = _mm1_many(gm, k, NN)
        gkb = _mm1_many(gm, kb, TN)
        hq = _mm1_many(hm, q, TN)
        hk = _mm1_many(hm, k, NN)
        em = [x * a + y * p for x, a, y, p in zip(da, a_mat, dp, p_mat)]
        em_t = _mm3_many(em, [eye] * nprob, TN)
        dbeta_all = [jnp.zeros((CHUNK, GATE_W), F32) for _ in range(nch)]
        dgam_all = [jnp.zeros((CHUNK, GATE_W), F32) for _ in range(nch)]
        for i, (c, h) in enumerate(idx):
            dkb = gk[i] + dkbg[i] * eg[i]
            dk = gkb[i] + hq[i] + dkd[i] * ekl[i] + beta[i] * dkb
            dq = hk[i] + dqd[i] * eg[i]
            dkd_kd = jnp.sum(dkd[i] * kd[i], axis=1, keepdims=True)
            dgam = (jnp.sum(em[i], axis=1, keepdims=True) - jnp.sum(em_t[i], axis=1, keepdims=True)
                    + jnp.sum(dqd[i] * qd[i], axis=1, keepdims=True) - dkd_kd
                    + jnp.sum(dkbg[i] * kbg[i], axis=1, keepdims=True))
            tail = jnp.sum(dkd_kd, axis=0, keepdims=True) + dgl[i] * gl[i]
            dgam = dgam + jnp.where(rowi == CHUNK - 1, tail, 0.0)
            dbeta = jnp.sum(dkb * k[i], axis=1, keepdims=True) + jnp.sum(dvb[i] * v[i], axis=1, keepdims=True)
            rs = slice(c * CHUNK, (c + 1) * CHUNK)
            dqkv_ref[rs, h * DH:(h + 1) * DH] = dq
            dqkv_ref[rs, DQ + h * DH:DQ + (h + 1) * DH] = dk
            dqkv_ref[rs, 2 * DQ + h * DH:2 * DQ + (h + 1) * DH] = beta[i] * dvb[i]
            dbeta_all[c] = dbeta_all[c] + jnp.where(lane == h, dbeta, 0.0)
            dgam_all[c] = dgam_all[c] + jnp.where(lane == NH + h, dgam, 0.0)
        upf = jnp.where(ii <= jj, 1.0, 0.0)
        for c in range(nch):
            dg_all = _mmx(upf, dgam_all[c], NN)
            dbg_ref[c * CHUNK:(c + 1) * CHUNK, :] = jnp.where(lane < NH, dbeta_all[c], dg_all)

    col = lambda c: pl.BlockSpec((rows, DQ), lambda n: (ng - 1 - n, c))
    gate = pl.BlockSpec((rows, GATE_W), lambda n: (ng - 1 - n, 0))
    return pl.pallas_call(
        body, grid=(ng,),
        in_specs=[col(0), col(1), col(2), gate, col(0),
                  pl.BlockSpec((nch, NH, DH, DH), lambda n: (ng - 1 - n, 0, 0, 0)),
                  pl.BlockSpec((nch, NH, CHUNK, CHUNK), lambda n: (ng - 1 - n, 0, 0, 0))],
        out_specs=[pl.BlockSpec((rows, 3 * DQ), lambda n: (ng - 1 - n, 0)), gate],
        out_shape=[jax.ShapeDtypeStruct((t, 3 * DQ), F32), jax.ShapeDtypeStruct((t, GATE_W), F32)],
        scratch_shapes=[pltpu.VMEM((NH, DH, DH), F32)],
        compiler_params=_cp(("arbitrary",)), name=name)(qkv, qkv, qkv, bg, do, s_hist, m_hist)


def dn_post_fwd(ybuf, o, proj, nw, name):
    t = o.shape[0]
    tm = _row_tile(t)

    def body(y_in, o_ref, z_ref, nw_ref, y_ref):
        del y_in
        nwv = nw_ref[...]
        for h in range(NH):
            sl = slice(h * DH, (h + 1) * DH)
            oh = o_ref[:, sl]
            z = z_ref[:, sl]
            r = lax.rsqrt(jnp.mean(oh * oh, axis=-1, keepdims=True) + NORM_EPS)
            y_ref[:, sl] = (oh * r * nwv * (z * _sigmoid(z))).astype(y_ref.dtype)

    return pl.pallas_call(
        body, grid=(t // tm,),
        in_specs=[pl.BlockSpec(memory_space=pl.ANY), pl.BlockSpec((tm, DQ), lambda i: (i, 0)),
                  pl.BlockSpec((tm, DQ), lambda i: (i, 5)), pl.BlockSpec((1, DH), lambda i: (0, 0))],
        out_specs=pl.BlockSpec((tm, DQ), lambda i: (i, 1)),
        out_shape=jax.ShapeDtypeStruct(ybuf.shape, ybuf.dtype), input_output_aliases={0: 0},
        compiler_params=_cp(("parallel",)), name=name)(ybuf, o, proj, nw)


def dn_post_bwd(dproj, dy, o, proj, nw, name):
    t = o.shape[0]
    tm = _row_tile(t)

    def body(dp_in, dy_ref, o_ref, z_ref, nw_ref, do_ref, dp_ref, dnw_ref):
        del dp_in
        i = pl.program_id(0)
        nwv = nw_ref[...]
        acc = jnp.zeros((1, DH), F32)
        for h in range(NH):
            sl = slice(h * DH, (h + 1) * DH)
            oh = o_ref[:, sl]
            z = z_ref[:, sl]
            dyh = dy_ref[:, sl]
            r = lax.rsqrt(jnp.mean(oh * oh, axis=-1, keepdims=True) + NORM_EPS)
            xh = oh * r
            sg = _sigmoid(z)
            sz = z * sg
            dxh = dyh * nwv * sz
            do_ref[:, sl] = r * (dxh - xh * jnp.mean(dxh * xh, axis=-1, keepdims=True))
            dp_ref[:, sl] = (dyh * xh * nwv * (sg * (1.0 + z * (1.0 - sg)))).astype(dp_ref.dtype)
            acc = acc + jnp.sum(dyh * xh * sz, axis=0, keepdims=True)

        @pl.when(i == 0)
        def _():
            dnw_ref[...] = jnp.zeros_like(dnw_ref)

        dnw_ref[...] += acc

    vec = pl.BlockSpec((1, DH), lambda i: (0, 0))
    return pl.pallas_call(
        body, grid=(t // tm,),
        in_specs=[pl.BlockSpec(memory_space=pl.ANY), pl.BlockSpec((tm, DQ), lambda i: (i, 1)),
                  pl.BlockSpec((tm, DQ), lambda i: (i, 0)), pl.BlockSpec((tm, DQ), lambda i: (i, 5)), vec],
        out_specs=[pl.BlockSpec((tm, DQ), lambda i: (i, 0)), pl.BlockSpec((tm, DQ), lambda i: (i, 5)), vec],
        out_shape=[jax.ShapeDtypeStruct((t, DQ), F32), jax.ShapeDtypeStruct(dproj.shape, dproj.dtype),
                   jax.ShapeDtypeStruct((1, DH), F32)],
        input_output_aliases={0: 1}, compiler_params=_cp(("arbitrary",)), name=name)(dproj, dy, o, proj, nw)


def swiglu_fwd(gate, up, name):
    t = gate.shape[0]
    tm = _row_tile(t)
    tn = DFF // 2

    def body(g_ref, u_ref, o_ref):
        g = g_ref[...]
        o_ref[...] = (g * _sigmoid(g) * u_ref[...]).astype(o_ref.dtype)

    blk = pl.BlockSpec((tm, tn), lambda i, j: (i, j))
    return pl.pallas_call(
        body, grid=(t // tm, DFF // tn), in_specs=[blk, blk], out_specs=blk,
        out_shape=jax.ShapeDtypeStruct((t, DFF), MXU_DTYPE),
        compiler_params=_cp(("parallel", "parallel")), name=name)(gate, up)


def swiglu_bwd(gate, up, dact, name):
    t = gate.shape[0]
    tm = _row_tile(t)
    tn = DFF // 2

    def body(g_ref, u_ref, d_ref, dg_ref, du_ref):
        g = g_ref[...]
        sg = _sigmoid(g)
        d = d_ref[...]
        dg_ref[...] = (d * u_ref[...] * (sg * (1.0 + g * (1.0 - sg)))).astype(dg_ref.dtype)
        du_ref[...] = (d * (g * sg)).astype(du_ref.dtype)

    blk = pl.BlockSpec((tm, tn), lambda i, j: (i, j))
    return pl.pallas_call(
        body, grid=(t // tm, DFF // tn), in_specs=[blk, blk, blk], out_specs=[blk, blk],
        out_shape=[jax.ShapeDtypeStruct((t, DFF), MXU_DTYPE)] * 2,
        compiler_params=_cp(("parallel", "parallel")), name=name)(gate, up, dact)


def _shifted(first, second, s, lane):
    if s == 0:
        return first
    return jnp.where(lane < LANES - s, pltpu.roll(first, LANES - s, 1), pltpu.roll(second, LANES - s, 1))


def unshard_cols(g8, w, widths, name):
    _, r, wp = g8.shape
    rb = _pick(r, (256, 128, 64, 32, 16))

    def body(g_ref, *o_refs):
        lane = lax.broadcasted_iota(jnp.int32, (rb, LANES), 1)
        zeros = jnp.zeros((rb, LANES), F32)

        def src(j, ta):
            if j >= NDEV or ta * LANES >= wp:
                return zeros
            return g_ref[j, :, ta * LANES:(ta + 1) * LANES].astype(F32)

        base = 0
        for o_ref, width in zip(o_refs, widths):
            for b in range(width // LANES):
                c0 = base + b * LANES
                if c0 >= NDEV * w:
                    tile = zeros
                else:
                    j0, o0 = divmod(c0, w)
                    n0 = min(w - o0, LANES)
                    ta, s = divmod(o0, LANES)
                    tile = _shifted(src(j0, ta), src(j0, ta + 1), s, lane)
                    if n0 < LANES:
                        nxt = pltpu.roll(src(j0 + 1, 0), n0, 1) if j0 + 1 < NDEV else zeros
                        tile = jnp.where(lane < n0, tile, nxt)
                o_ref[:, b * LANES:(b + 1) * LANES] = tile.astype(o_ref.dtype)
            base += width

    return pl.pallas_call(
        body, grid=(r // rb,), in_specs=[pl.BlockSpec((NDEV, rb, wp), lambda i: (0, i, 0))],
        out_specs=[pl.BlockSpec((rb, width), lambda i: (i, 0)) for width in widths],
        out_shape=[jax.ShapeDtypeStruct((r, width), g8.dtype) for width in widths],
        compiler_params=_cp(("parallel",)), name=name)(g8)


def shard_cols(parts, w, name):
    r = parts[0].shape[0]
    wp = _lane_pad(w)
    rb = _pick(r, (256, 128, 64, 32, 16))
    tiles_of = [p.shape[1] // LANES for p in parts]

    def body(*refs):
        p_refs, o_ref = refs[:-1], refs[-1]
        lane = lax.broadcasted_iota(jnp.int32, (rb, LANES), 1)
        zeros = jnp.zeros((rb, LANES), F32)

        def glob(tile_idx):
            for p_ref, n_tiles in zip(p_refs, tiles_of):
                if tile_idx < n_tiles:
                    return p_ref[:, tile_idx * LANES:(tile_idx + 1) * LANES]
                tile_idx -= n_tiles
            return zeros

        for j in range(NDEV):
            for a in range(wp // LANES):
                nv = min(w - a * LANES, LANES)
                tb, s = divmod(w * j + a * LANES, LANES)
                tile = _shifted(glob(tb), glob(tb + 1), s, lane)
                if nv < LANES:
                    tile = jnp.where(lane < nv, tile, 0.0)
                o_ref[j, :, a * LANES:(a + 1) * LANES] = tile.astype(o_ref.dtype)

    return pl.pallas_call(
        body, grid=(r // rb,), in_specs=[pl.BlockSpec((rb, p.shape[1]), lambda i: (i, 0)) for p in parts],
        out_specs=pl.BlockSpec((NDEV, rb, wp), lambda i: (0, i, 0)),
        out_shape=jax.ShapeDtypeStruct((NDEV, r, wp), GRAD_DTYPE),
        compiler_params=_cp(("parallel",)), name=name)(*parts)


def _me_and_peers():
    mx, my, mc = lax.axis_index("x"), lax.axis_index("y"), lax.axis_index("c")
    me = 4 * mx + 2 * my + mc
    peers = []
    for kk in range(1, NDEV):
        px = 1 - mx if kk & 4 else mx
        py = 1 - my if kk & 2 else my
        pc = 1 - mc if kk & 1 else mc
        peers.append(((px, py, pc), 4 * px + 2 * py + pc))
    return me, peers


def _push_to_all(xs, name, scatter):
    n = len(xs)
    npeer = NDEV - 1

    def body(*refs):
        x_refs, o_refs = refs[:n], refs[n:2 * n]
        send_sems, recv_sems, local_sems = refs[2 * n:]
        me, peers = _me_and_peers()
        local, sends = [], []
        for a in range(n):
            cp = pltpu.make_async_copy(x_refs[a].at[me] if scatter else x_refs[a], o_refs[a].at[me], local_sems.at[a])
            cp.start()
            local.append(cp)
        for a in range(n):
            for kk, (peer, pidx) in enumerate(peers):
                cp = pltpu.make_async_remote_copy(
                    src_ref=x_refs[a].at[pidx] if scatter else x_refs[a], dst_ref=o_refs[a].at[me],
                    send_sem=send_sems.at[a * npeer + kk], recv_sem=recv_sems.at[a * npeer + kk],
                    device_id=peer, device_id_type=pl.DeviceIdType.MESH)
                cp.start()
                sends.append(cp)
        for a in range(n):
            for kk, (peer, pidx) in enumerate(peers):
                pltpu.make_async_remote_copy(
                    src_ref=x_refs[a].at[pidx] if scatter else x_refs[a], dst_ref=o_refs[a].at[pidx],
                    send_sem=send_sems.at[a * npeer + kk], recv_sem=recv_sems.at[a * npeer + kk],
                    device_id=peer, device_id_type=pl.DeviceIdType.MESH).wait_recv()
        for cp in sends:
            cp.wait_send()
        for cp in local:
            cp.wait()

    any_spec = pl.BlockSpec(memory_space=pl.ANY)
    return pl.pallas_call(
        body, in_specs=[any_spec] * n, out_specs=[any_spec] * n,
        out_shape=[jax.ShapeDtypeStruct(x.shape if scatter else (NDEV,) + x.shape, x.dtype) for x in xs],
        scratch_shapes=[pltpu.SemaphoreType.DMA((n * npeer,)), pltpu.SemaphoreType.DMA((n * npeer,)),
                        pltpu.SemaphoreType.DMA((n,))],
        name=name)(*xs)


def adamw(recv, w, m, v, name):
    rows, cols = w.shape
    c1 = 1.0 - ADAM_B1 ** ADAM_STEP
    c2 = 1.0 - ADAM_B2 ** ADAM_STEP
    cap = ADAM_BLOCK_BYTES // (NDEV * cols * 4)
    rb = _pick(rows, [p for p in (2048, 1024, 512, 256, 128, 64, 32, 16, 8) if p <= cap])

    def body(r_ref, w_ref, m_ref, v_ref, g_ref, d_ref, m2_ref, v2_ref):
        g = r_ref[0].astype(F32)
        for j in range(1, NDEV):
            g = g + r_ref[j].astype(F32)
        m2 = ADAM_B1 * m_ref[...] + (1.0 - ADAM_B1) * g
        v2 = ADAM_B2 * v_ref[...] + (1.0 - ADAM_B2) * (g * g)
        g_ref[...] = g
        m2_ref[...] = m2
        v2_ref[...] = v2
        d_ref[...] = -ADAM_LR * ((m2 / c1) / (jnp.sqrt(v2 / c2) + ADAM_EPS) + ADAM_WD * w_ref[...])

    blk = pl.BlockSpec((rb, cols), lambda i: (i, 0))
    return pl.pallas_call(
        body, grid=(rows // rb,),
        in_specs=[pl.BlockSpec((NDEV, rb, cols), lambda i: (0, i, 0)), blk, blk, blk],
        out_specs=[blk, blk, blk, blk], out_shape=[jax.ShapeDtypeStruct((rows, cols), F32)] * 4,
        compiler_params=_cp(("parallel",)), name=name)(recv, w, m, v)


SMALL_SHARDED = ("meta_tokens", "conv_dw_w", "dn_conv_w")
SMALL_REPLICATED = ("norm_mix_w", "conv_dw_b", "conv_ln_w", "conv_ln_b", "dn_A_log", "dn_dt_bias", "dn_norm_w",
                    "norm_ffn_w", "final_norm_w")
PARAM_ORDER = ("meta_tokens", "norm_mix_w", "w_in", "conv_dw_w", "conv_dw_b", "conv_ln_w", "conv_ln_b", "dn_conv_w",
               "dn_A_log", "dn_dt_bias", "dn_norm_w", "w_out", "norm_ffn_w", "ffn_w_gu", "ffn_w_down", "final_norm_w")


def _pack_small(parts, axis):
    flat = jnp.concatenate(parts, axis=axis)
    n = flat.shape[axis]
    total = -(-n // (8 * LANES)) * (8 * LANES)
    pad = [(0, 0)] * flat.ndim
    pad[axis] = (0, total - n)
    flat = jnp.pad(flat, pad)
    return flat.reshape(flat.shape[:axis] + (total // LANES, LANES))


def _unshard_last(g8):
    moved = jnp.moveaxis(g8, 0, -2)
    return moved.reshape(moved.shape[:-2] + (-1,))


def _per_destination_last(full):
    split = full.reshape(full.shape[:-1] + (NDEV, full.shape[-1] // NDEV))
    return jnp.moveaxis(split, -2, 0).reshape(NDEV, -1)


def _rows_to_slots(full, depth):
    r = full.shape[1] // NDEV
    return jnp.moveaxis(full.reshape(depth, NDEV, r, full.shape[2]), 1, 0).reshape(NDEV, depth * r, full.shape[2])


def _slots_to_rows(g8, depth):
    r = g8.shape[1] // depth
    return jnp.moveaxis(g8.reshape(NDEV, depth, r, g8.shape[2]), 0, 1).reshape(depth, NDEV * r, g8.shape[2])


def _lane_row(vec4, width):
    return jnp.pad(vec4, (NH, width - 2 * NH))[None]


def kernel(x, meta_tokens, norm_mix_w, w_in, conv_dw_w, conv_dw_b, conv_ln_w, conv_ln_b, dn_conv_w, dn_A_log, dn_dt_bias, dn_norm_w, w_out, norm_ffn_w, ffn_w_gu, ffn_w_down, final_norm_w, loss_target, m_meta_tokens, m_norm_mix_w, m_w_in, m_conv_dw_w, m_conv_dw_b, m_conv_ln_w, m_conv_ln_b, m_dn_conv_w, m_dn_A_log, m_dn_dt_bias, m_dn_norm_w, m_w_out, m_norm_ffn_w, m_ffn_w_gu, m_ffn_w_down, m_final_norm_w, v_meta_tokens, v_norm_mix_w, v_w_in, v_conv_dw_w, v_conv_dw_b, v_conv_ln_w, v_conv_ln_b, v_dn_conv_w, v_dn_A_log, v_dn_dt_bias, v_dn_norm_w, v_w_out, v_norm_ffn_w, v_ffn_w_gu, v_ffn_w_down, v_final_norm_w):
    weights = dict(meta_tokens=meta_tokens, norm_mix_w=norm_mix_w, w_in=w_in, conv_dw_w=conv_dw_w, conv_dw_b=conv_dw_b,
                   conv_ln_w=conv_ln_w, conv_ln_b=conv_ln_b, dn_conv_w=dn_conv_w, dn_A_log=dn_A_log,
                   dn_dt_bias=dn_dt_bias, dn_norm_w=dn_norm_w, w_out=w_out, norm_ffn_w=norm_ffn_w, ffn_w_gu=ffn_w_gu,
                   ffn_w_down=ffn_w_down, final_norm_w=final_norm_w)
    m_in = dict(meta_tokens=m_meta_tokens, norm_mix_w=m_norm_mix_w, w_in=m_w_in, conv_dw_w=m_conv_dw_w,
                conv_dw_b=m_conv_dw_b, conv_ln_w=m_conv_ln_w, conv_ln_b=m_conv_ln_b, dn_conv_w=m_dn_conv_w,
                dn_A_log=m_dn_A_log, dn_dt_bias=m_dn_dt_bias, dn_norm_w=m_dn_norm_w, w_out=m_w_out,
                norm_ffn_w=m_norm_ffn_w, ffn_w_gu=m_ffn_w_gu, ffn_w_down=m_ffn_w_down, final_norm_w=m_final_norm_w)
    v_in = dict(meta_tokens=v_meta_tokens, norm_mix_w=v_norm_mix_w, w_in=v_w_in, conv_dw_w=v_conv_dw_w,
                conv_dw_b=v_conv_dw_b, conv_ln_w=v_conv_ln_w, conv_ln_b=v_conv_ln_b, dn_conv_w=v_dn_conv_w,
                dn_A_log=v_dn_A_log, dn_dt_bias=v_dn_dt_bias, dn_norm_w=v_dn_norm_w, w_out=v_w_out,
                norm_ffn_w=v_norm_ffn_w, ffn_w_gu=v_ffn_w_gu, ffn_w_down=v_ffn_w_down, final_norm_w=v_final_norm_w)

    depth = w_in.shape[0]
    seq = x.shape[1]
    t = _padded_rows(seq)
    rows_d = depth * D
    win_w, gu_w = w_in.shape[2], ffn_w_gu.shape[2]
    win_wp, gu_wp = _lane_pad(win_w), _lane_pad(gu_w)

    def pad_cols(a, wp):
        return jnp.pad(a, ((0, 0), (0, 0), (0, wp - a.shape[2]))).reshape(rows_d, wp)

    def rows2d(a):
        return a.reshape(-1, a.shape[2])

    small_shards = [weights[n] for n in SMALL_SHARDED]
    g_win, g_gu, g_wout, g_down, g_small = _push_to_all(
        [pad_cols(w_in, win_wp).astype(MXU_DTYPE), pad_cols(ffn_w_gu, gu_wp).astype(MXU_DTYPE),
         rows2d(w_out).astype(MXU_DTYPE), rows2d(ffn_w_down).astype(MXU_DTYPE),
         _pack_small([s.reshape(-1) for s in small_shards], 0)], "gather_weights", scatter=False)
    win_main, win_gate = unshard_cols(g_win, win_w, [PROJ_MAIN, GATE_W], "unshard_w_in")
    win_main, win_gate = win_main.reshape(depth, D, PROJ_MAIN), win_gate.reshape(depth, D, GATE_W)
    wg_f, wu_f = (a.reshape(depth, D, DFF) for a in unshard_cols(g_gu, gu_w, [DFF, DFF], "unshard_w_gu"))
    wout_f = _slots_to_rows(g_wout, depth)
    wdown_f = _slots_to_rows(g_down, depth)
    small_flat, off, small_full = g_small.reshape(NDEV, -1), 0, {}
    for n, s in zip(SMALL_SHARDED, small_shards):
        small_full[n] = _unshard_last(small_flat[:, off:off + s.size].reshape((NDEV,) + s.shape))
        off += s.size
    cdw32 = jnp.pad(small_full["conv_dw_w"], ((0, 0), (0, CONV_HALO - CONV_W), (0, 0)))
    dcw8 = jnp.pad(small_full["dn_conv_w"], ((0, 0), (0, DN_HALO - DN_W), (0, 0)))

    h = jnp.concatenate([jnp.zeros((FRONT, D), F32), small_full["meta_tokens"], x[0],
                         jnp.zeros((t - HEAD - seq, D), F32)], axis=0)
    tgt = jnp.pad(loss_target[0], ((HEAD, t - HEAD - seq), (0, 0)))

    saved = []
    for l in range(depth):
        nmw, nfw = norm_mix_w[l][None], norm_ffn_w[l][None]
        cdb, clw, clb = conv_dw_b[l][None], conv_ln_w[l][None], conv_ln_b[l][None]
        alog, dtb, dnw = _lane_row(dn_A_log[l], GATE_W), _lane_row(dn_dt_bias[l], GATE_W), dn_norm_w[l][None]
        hn = rms_fwd(h, nmw, "rms_mix_fwd")
        proj = mm(hn, win_main[l], name="mm_proj")
        pg = mm(hn, win_gate[l], name="mm_proj_gate")
        ybuf, u1 = conv_fwd(proj, cdw32[l], cdb, clw, clb, seq, "conv_fwd")
        qkv = dn_pre_fwd(proj, dcw8[l], "dn_pre_fwd")
        bg = gate_fwd(pg, alog, dtb, seq, "gate_fwd")
        o, s_hist, m_hist = delta_fwd(qkv, bg, "delta_fwd")
        ybuf = dn_post_fwd(ybuf, o, proj, dnw, "dn_post_fwd")
        h_mid = mm(ybuf, wout_f[l], res=h, name="mm_out")
        hn2 = rms_fwd(h_mid, nfw, "rms_ffn_fwd")
        gate = mm(hn2, wg_f[l], name="mm_gate")
        up = mm(hn2, wu_f[l], name="mm_up")
        act = swiglu_fwd(gate, up, "swiglu_fwd")
        h_out = mm(act, wdown_f[l], res=h_mid, name="mm_down")
        saved.append(dict(h=h, hn=hn, proj=proj, pg=pg, ybuf=ybuf, u1=u1, qkv=qkv, bg=bg, o=o, s_hist=s_hist,
                          m_hist=m_hist, h_mid=h_mid, hn2=hn2, gate=gate, up=up, act=act, nmw=nmw, nfw=nfw, clw=clw,
                          clb=clb, alog=alog, dtb=dtb, dnw=dnw))
        h = h_out

    dh, loss_part, d_final = loss_bwd(h, tgt, final_norm_w[None], seq, "loss_bwd")
    loss = lax.psum(loss_part[0, 0], MESH_AXES)

    per_layer = ("norm_mix_w", "win_main", "win_gate", "conv_dw_w", "conv_dw_b", "conv_ln_w", "conv_ln_b", "dn_conv_w",
                 "dn_A_log", "dn_dt_bias", "dn_norm_w", "w_out", "norm_ffn_w", "w_gate", "w_up", "ffn_w_down")
    grads = {n: [None] * depth for n in per_layer}
    for l in reversed(range(depth)):
        s = saved[l]
        dact = mm(dh, wdown_f[l], tb=True, name="mm_down_dx")
        grads["ffn_w_down"][l] = mm(s["act"], dh, ta=True, name="mm_down_dw")
        dgate, dup = swiglu_bwd(s["gate"], s["up"], dact, "swiglu_bwd")
        dhn2 = mm(dup, wu_f[l], tb=True, res=mm(dgate, wg_f[l], tb=True, name="mm_gate_dx"), name="mm_up_dx")
        grads["w_gate"][l] = mm(s["hn2"], dgate, ta=True, name="mm_gate_dw")
        grads["w_up"][l] = mm(s["hn2"], dup, ta=True, name="mm_up_dw")
        dh_mid, dnfw = rms_bwd(dhn2, s["h_mid"], s["nfw"], dh, "rms_ffn_bwd")
        dy = mm(dh_mid, wout_f[l], tb=True, name="mm_out_dx")
        grads["w_out"][l] = mm(s["ybuf"], dh_mid, ta=True, name="mm_out_dw")
        dproj, dcdw, dcdb, dclw, dclb = conv_bwd(dy, s["u1"], s["proj"], cdw32[l], s["clw"], s["clb"], seq, "conv_bwd")
        do, dproj, ddnw = dn_post_bwd(dproj, dy, s["o"], s["proj"], s["dnw"], "dn_post_bwd")
        dqkv, dbg = delta_bwd(s["qkv"], s["bg"], do, s["s_hist"], s["m_hist"], "delta_bwd")
        dproj, ddcw = dn_pre_bwd(dproj, dqkv, s["proj"], dcw8[l], "dn_pre_bwd")
        dpg, dalog, ddtb = gate_bwd(dbg, s["pg"], s["alog"], s["dtb"], seq, "gate_bwd")
        dhn_gate = mm(dpg, win_gate[l], tb=True, name="mm_proj_gate_dx")
        dhn = mm(dproj, win_main[l], tb=True, res=dhn_gate, name="mm_proj_dx")
        grads["win_main"][l] = mm(s["hn"], dproj, ta=True, name="mm_proj_dw")
        grads["win_gate"][l] = mm(s["hn"], dpg, ta=True, name="mm_proj_gate_dw")
        dh, dnmw = rms_bwd(dhn, s["h"], s["nmw"], dh_mid, "rms_mix_bwd")
        grads["norm_mix_w"][l] = dnmw[0]
        grads["norm_ffn_w"][l] = dnfw[0]
        grads["conv_dw_w"][l] = dcdw[:CONV_W]
        grads["conv_dw_b"][l] = dcdb[0]
        grads["conv_ln_w"][l] = dclw[0]
        grads["conv_ln_b"][l] = dclb[0]
        grads["dn_conv_w"][l] = ddcw[:DN_W]
        grads["dn_A_log"][l] = dalog[0, NH:2 * NH]
        grads["dn_dt_bias"][l] = ddtb[0, NH:2 * NH]
        grads["dn_norm_w"][l] = ddnw[0]

    grad_x = dh[HEAD:HEAD + seq][None]
    full = {n: jnp.stack(g) for n, g in grads.items()}
    full["meta_tokens"] = dh[FRONT:HEAD]
    full["final_norm_w"] = d_final[0]

    send_win = shard_cols([full["win_main"].reshape(rows_d, PROJ_MAIN), full["win_gate"].reshape(rows_d, GATE_W)],
                          win_w, "shard_w_in")
    send_gu = shard_cols([full["w_gate"].reshape(rows_d, DFF), full["w_up"].reshape(rows_d, DFF)], gu_w, "shard_w_gu")
    send_small = _pack_small(
        [_per_destination_last(full[n]) for n in SMALL_SHARDED]
        + [jnp.broadcast_to(full[n].reshape(1, -1), (NDEV, full[n].size)) for n in SMALL_REPLICATED], 1)
    r_win, r_gu, r_wout, r_down, r_small = _push_to_all(
        [send_win, send_gu, _rows_to_slots(full["w_out"], depth).astype(GRAD_DTYPE),
         _rows_to_slots(full["ffn_w_down"], depth).astype(GRAD_DTYPE), send_small], "exchange_grads", scatter=True)

    small_names = SMALL_SHARDED + SMALL_REPLICATED
    pack_local = lambda tree: _pack_small([tree[n].reshape(-1) for n in small_names], 0)
    results = {}

    def run_adamw(name, recv, prep, finish):
        outs = adamw(recv, prep(weights[name]), prep(m_in[name]), prep(v_in[name]), "adamw_" + name)
        results[name] = [finish(o) for o in outs]

    run_adamw("w_in", r_win, lambda a: pad_cols(a, win_wp),
              lambda o: o[:, :win_w].reshape(depth, D, win_w))
    run_adamw("ffn_w_gu", r_gu, lambda a: pad_cols(a, gu_wp), lambda o: o[:, :gu_w].reshape(depth, D, gu_w))
    run_adamw("w_out", r_wout, rows2d, lambda o: o.reshape(w_out.shape))
    run_adamw("ffn_w_down", r_down, rows2d, lambda o: o.reshape(ffn_w_down.shape))
    small_outs = adamw(r_small, pack_local(weights), pack_local(m_in), pack_local(v_in), "adamw_small")
    for kind in range(4):
        flat, off = small_outs[kind].reshape(-1), 0
        for n in small_names:
            wgt = weights[n]
            results.setdefault(n, [None] * 4)[kind] = flat[off:off + wgt.size].reshape(wgt.shape)
            off += wgt.size

    return (loss, grad_x, *[results[n][0] for n in PARAM_ORDER], *[results[n][1] for n in PARAM_ORDER],
            *[results[n][2] for n in PARAM_ORDER], *[results[n][3] for n in PARAM_ORDER])
```

```python
import jax
import jax.numpy as jnp
from jax import lax
from jax.experimental import pallas as pl
from jax.experimental.pallas import tpu as pltpu

F32 = jnp.float32
MXU_DTYPE = jnp.bfloat16

D = 1024
N_META = 16
CHUNK = 64
CHUNK_LOG2 = 6
INV_BASE_LOG2 = 3
FRONT = CHUNK - N_META
HEAD = CHUNK
CONV_CH = 512
CONV_W = 31
CONV_HALO = 32
NH = 4
DH = 128
DQ = NH * DH
DN_W = 4
DN_HALO = 8
DFF = 2816
PROJ_MAIN = 3072
D_IN = 3080
GATE_W = 128
LANES = 128
NDEV = 8
NORM_EPS = 1e-6
LN_EPS = 1e-5
L2_EPS = 1e-6
VMEM_LIMIT_V7X = 48 * 1024 * 1024
ROW_TILE = 640
ROW_TILE_SMALL = 128
MM_TILES = (1408, 1280, 1024, 640, 512, 256, 128)
GRAD_DTYPE = jnp.bfloat16
DELTA_CHUNKS = (4, 2, 1)
ADAM_BLOCK_BYTES = 8 * 1024 * 1024

ADAM_LR = 0.001
ADAM_B1 = 0.9
ADAM_B2 = 0.999
ADAM_EPS = 1e-08
ADAM_WD = 0.01
ADAM_STEP = 10

MESH_AXES = ("x", "y", "c")
NN = ((1,), (0,))
NT = ((1,), (1,))
TN = ((0,), (0,))

assert 1 << CHUNK_LOG2 == CHUNK


def _row_tile(t):
    return ROW_TILE if t % ROW_TILE == 0 else ROW_TILE_SMALL


def _padded_rows(seq):
    n = HEAD + seq
    tm = ROW_TILE if n >= 4 * ROW_TILE else ROW_TILE_SMALL
    return -(-n // tm) * tm


def _pick(n, prefs):
    for p in prefs:
        if n % p == 0:
            return p
    return n


def _lane_pad(n):
    return -(-n // LANES) * LANES


def _cp(sem):
    return pltpu.CompilerParams(dimension_semantics=sem, vmem_limit_bytes=VMEM_LIMIT_V7X)


def _sigmoid(x):
    return 1.0 / (1.0 + jnp.exp(-x))


def _softplus(x):
    return jnp.maximum(x, 0.0) + jnp.log(1.0 + jnp.exp(-jnp.abs(x)))


def _valid_rows(i, tm, seq, width, first=FRONT):
    rows = i * tm + lax.broadcasted_iota(jnp.int32, (tm, width), 0)
    return jnp.logical_and(rows >= first, rows < HEAD + seq)


def _dot(a, b, dims):
    return lax.dot_general(a, b, (dims, ((), ())), preferred_element_type=F32)


def _split(x, n):
    out, r = [], x
    for _ in range(n):
        p = r.astype(MXU_DTYPE)
        out.append(p)
        r = r - p.astype(F32)
    return out


def _mm1(a, b, dims):
    return _dot(a.astype(MXU_DTYPE), b.astype(MXU_DTYPE), dims)


def _mm1_many(a_list, b_list, dims):
    return [_mm1(a, b, dims) for a, b in zip(a_list, b_list)]


def _mm3_many(a_list, b_list, dims):
    sa = [_split(a, 2) for a in a_list]
    sb = [_split(b, 2) for b in b_list]
    hh = [_dot(x[0], y[0], dims) for x, y in zip(sa, sb)]
    hl = [_dot(x[0], y[1], dims) for x, y in zip(sa, sb)]
    lh = [_dot(x[1], y[0], dims) for x, y in zip(sa, sb)]
    return [p + (q + r) for p, q, r in zip(hh, hl, lh)]


def _mmx(e, b, dims):
    e = e.astype(MXU_DTYPE)
    b1, b2, b3 = _split(b, 3)
    return _dot(e, b1, dims) + (_dot(e, b2, dims) + _dot(e, b3, dims))


def mm(a, b, *, ta=False, tb=False, res=None, out_dtype=F32, name):
    (k_dim, m_dim) = a.shape if ta else a.shape[::-1]
    n_dim = b.shape[0] if tb else b.shape[1]
    assert (b.shape[1] if tb else b.shape[0]) == k_dim
    tm, tn, tk = (_pick(n, MM_TILES) for n in (m_dim, n_dim, k_dim))
    nk = k_dim // tk

    dims = ((0,) if ta else (1,), (1,) if tb else (0,))

    def body(*refs):
        a_ref, b_ref = refs[:2]
        r_ref = refs[2] if res is not None else None
        o_ref = refs[3] if res is not None else refs[2]

        def finish(out):
            if r_ref is not None:
                out = out + r_ref[...]
            o_ref[...] = out.astype(o_ref.dtype)

        if nk == 1:
            finish(_mm1(a_ref[...], b_ref[...], dims))
            return
        acc_ref = refs[-1]
        k = pl.program_id(2)

        @pl.when(k == 0)
        def _():
            acc_ref[...] = jnp.zeros_like(acc_ref)

        acc_ref[...] += _mm1(a_ref[...], b_ref[...], dims)

        @pl.when(k == nk - 1)
        def _():
            finish(acc_ref[...])

    a_spec = pl.BlockSpec((tk, tm), lambda i, j, k: (k, i)) if ta else pl.BlockSpec((tm, tk), lambda i, j, k: (i, k))
    b_spec = pl.BlockSpec((tn, tk), lambda i, j, k: (j, k)) if tb else pl.BlockSpec((tk, tn), lambda i, j, k: (k, j))
    o_spec = pl.BlockSpec((tm, tn), lambda i, j, k: (i, j))
    in_specs, args = [a_spec, b_spec], [a, b]
    if res is not None:
        in_specs.append(o_spec)
        args.append(res)
    return pl.pallas_call(
        body, grid=(m_dim // tm, n_dim // tn, nk), in_specs=in_specs, out_specs=o_spec,
        out_shape=jax.ShapeDtypeStruct((m_dim, n_dim), out_dtype),
        scratch_shapes=[pltpu.VMEM((tm, tn), F32)] if nk > 1 else [],
        compiler_params=_cp(("parallel", "parallel", "arbitrary")), name=name)(*args)


def rms_fwd(h, w, name):
    t = h.shape[0]
    tm = _row_tile(t)

    def body(h_ref, w_ref, o_ref):
        x = h_ref[...]
        r = lax.rsqrt(jnp.mean(x * x, axis=-1, keepdims=True) + NORM_EPS)
        o_ref[...] = (x * r * w_ref[...]).astype(o_ref.dtype)

    row = pl.BlockSpec((tm, D), lambda i: (i, 0))
    return pl.pallas_call(
        body, grid=(t // tm,), in_specs=[row, pl.BlockSpec((1, D), lambda i: (0, 0))], out_specs=row,
        out_shape=jax.ShapeDtypeStruct((t, D), MXU_DTYPE), compiler_params=_cp(("parallel",)), name=name)(h, w)


def rms_bwd(dy, h, w, dres, name):
    t = h.shape[0]
    tm = _row_tile(t)

    def body(dy_ref, h_ref, w_ref, dres_ref, dh_ref, dw_ref):
        i = pl.program_id(0)
        x = h_ref[...]
        r = lax.rsqrt(jnp.mean(x * x, axis=-1, keepdims=True) + NORM_EPS)
        xh = x * r
        g = dy_ref[...] * w_ref[...]
        dh_ref[...] = dres_ref[...] + r * (g - xh * jnp.mean(g * xh, axis=-1, keepdims=True))

        @pl.when(i == 0)
        def _():
            dw_ref[...] = jnp.zeros_like(dw_ref)

        dw_ref[...] += jnp.sum(dy_ref[...] * xh, axis=0, keepdims=True)

    row = pl.BlockSpec((tm, D), lambda i: (i, 0))
    vec = pl.BlockSpec((1, D), lambda i: (0, 0))
    return pl.pallas_call(
        body, grid=(t // tm,), in_specs=[row, row, vec, row], out_specs=[row, vec],
        out_shape=[jax.ShapeDtypeStruct((t, D), F32), jax.ShapeDtypeStruct((1, D), F32)],
        compiler_params=_cp(("arbitrary",)), name=name)(dy, h, w, dres)


def loss_bwd(h, tgt, w, seq, name):
    t = h.shape[0]
    tm = _row_tile(t)

    def body(h_ref, t_ref, w_ref, dh_ref, loss_ref, dw_ref):
        i = pl.program_id(0)
        x = h_ref[...]
        wv = w_ref[...]
        r = lax.rsqrt(jnp.mean(x * x, axis=-1, keepdims=True) + NORM_EPS)
        xh = x * r
        err = jnp.where(_valid_rows(i, tm, seq, D, HEAD), xh * wv - t_ref[...], 0.0)
        dy = err * (1.0 / D)
        g = dy * wv
        dh_ref[...] = r * (g - xh * jnp.mean(g * xh, axis=-1, keepdims=True))

        @pl.when(i == 0)
        def _():
            dw_ref[...] = jnp.zeros_like(dw_ref)
            loss_ref[...] = jnp.zeros_like(loss_ref)

        dw_ref[...] += jnp.sum(dy * xh, axis=0, keepdims=True)
        part = jnp.sum(jnp.sum(err * err, axis=1, keepdims=True), axis=0, keepdims=True) * (0.5 / D)
        loss_ref[...] += jnp.broadcast_to(part, loss_ref.shape)

    row = pl.BlockSpec((tm, D), lambda i: (i, 0))
    vec = pl.BlockSpec((1, D), lambda i: (0, 0))
    return pl.pallas_call(
        body, grid=(t // tm,), in_specs=[row, row, vec],
        out_specs=[row, pl.BlockSpec((1, LANES), lambda i: (0, 0)), vec],
        out_shape=[jax.ShapeDtypeStruct((t, D), F32), jax.ShapeDtypeStruct((1, LANES), F32),
                   jax.ShapeDtypeStruct((1, D), F32)],
        compiler_params=_cp(("arbitrary",)), name=name)(h, tgt, w)


def _layernorm_parts(u1):
    mu = jnp.mean(u1, axis=-1, keepdims=True)
    xc = u1 - mu
    rstd = lax.rsqrt(jnp.mean(xc * xc, axis=-1, keepdims=True) + LN_EPS)
    return xc * rstd, rstd


SUBLANES = 8


def _shift_copies(ext, sh, tm):
    for s in range(1, SUBLANES):
        sh[s - 1, :, :] = ext[pl.ds(s, tm + CONV_HALO - SUBLANES), :]


def _window(ext, sh, off, tm):
    s, m = off % SUBLANES, off // SUBLANES
    if s == 0:
        return ext[pl.ds(off, tm), :]
    return sh[s - 1, pl.ds(SUBLANES * m, tm), :]


def _shift_scratch(tm):
    return pltpu.VMEM((SUBLANES - 1, tm + CONV_HALO - SUBLANES, CONV_CH), F32)


def conv_fwd(proj, w32, b, lw, lb, seq, name):
    t = proj.shape[0]
    tm = _row_tile(t)

    def body(cv_ref, cg_ref, w_ref, b_ref, lw_ref, lb_ref, y_ref, u1_ref, ext, sh):
        i = pl.program_id(0)

        @pl.when(i == 0)
        def _():
            ext[0:CONV_HALO, :] = jnp.zeros((CONV_HALO, CONV_CH), F32)

        @pl.when(i > 0)
        def _():
            ext[0:CONV_HALO, :] = ext[tm:tm + CONV_HALO, :]

        ext[CONV_HALO:CONV_HALO + tm, :] = cv_ref[...] * _sigmoid(cg_ref[...])
        _shift_copies(ext, sh, tm)
        acc = jnp.broadcast_to(b_ref[...], (tm, CONV_CH))
        for j in range(CONV_W):
            acc = acc + w_ref[j:j + 1, :] * _window(ext, sh, CONV_HALO - (CONV_W - 1) + j, tm)
        u1_ref[...] = acc
        xh, _ = _layernorm_parts(acc)
        ln = xh * lw_ref[...] + lb_ref[...]
        y = ln * _sigmoid(ln)
        y_ref[...] = jnp.where(_valid_rows(i, tm, seq, CONV_CH), y, 0.0).astype(y_ref.dtype)

    half = lambda c: pl.BlockSpec((tm, CONV_CH), lambda i: (i, c))
    vec = pl.BlockSpec((1, CONV_CH), lambda i: (0, 0))
    return pl.pallas_call(
        body, grid=(t // tm,),
        in_specs=[half(0), half(1), pl.BlockSpec((CONV_HALO, CONV_CH), lambda i: (0, 0)), vec, vec, vec],
        out_specs=[half(0), half(0)],
        out_shape=[jax.ShapeDtypeStruct((t, D), MXU_DTYPE), jax.ShapeDtypeStruct((t, CONV_CH), F32)],
        scratch_shapes=[pltpu.VMEM((tm + CONV_HALO, CONV_CH), F32), _shift_scratch(tm)],
        compiler_params=_cp(("arbitrary",)), name=name)(proj, proj, w32, b, lw, lb)


def conv_bwd(dy, u1, proj, w32, lw, lb, seq, name):
    t = proj.shape[0]
    tm = _row_tile(t)
    nt = t // tm
    per = tm // CONV_HALO

    def body(dy_ref, u1_ref, cv_ref, cg_ref, cvp_ref, cgp_ref, w_ref, lw_ref, lb_ref,
             dp_ref, dw_ref, db_ref, dlw_ref, dlb_ref, ext_d, ext_u, sh_d, sh_u):
        i = pl.program_id(0)
        tile = nt - 1 - i

        @pl.when(i == 0)
        def _():
            ext_d[tm:tm + CONV_HALO, :] = jnp.zeros((CONV_HALO, CONV_CH), F32)
            dw_ref[...] = jnp.zeros_like(dw_ref)
            db_ref[...] = jnp.zeros_like(db_ref)
            dlw_ref[...] = jnp.zeros_like(dlw_ref)
            dlb_ref[...] = jnp.zeros_like(dlb_ref)

        @pl.when(i > 0)
        def _():
            ext_d[tm:tm + CONV_HALO, :] = ext_d[0:CONV_HALO, :]

        xh, rstd = _layernorm_parts(u1_ref[...])
        lwv = lw_ref[...]
        ln = xh * lwv + lb_ref[...]
        sg = _sigmoid(ln)
        dln = jnp.where(_valid_rows(tile, tm, seq, CONV_CH), dy_ref[...], 0.0) * (sg * (1.0 + ln * (1.0 - sg)))
        dlw_ref[...] += jnp.sum(dln * xh, axis=0, keepdims=True)
        dlb_ref[...] += jnp.sum(dln, axis=0, keepdims=True)
        dxh = dln * lwv
        du1 = rstd * (dxh - jnp.mean(dxh, axis=-1, keepdims=True)
                      - xh * jnp.mean(dxh * xh, axis=-1, keepdims=True))
        db_ref[...] += jnp.sum(du1, axis=0, keepdims=True)
        ext_d[0:tm, :] = du1

        cv = cv_ref[...]
        sgc = _sigmoid(cg_ref[...])
        prev = cvp_ref[...] * _sigmoid(cgp_ref[...])
        ext_u[0:CONV_HALO, :] = jnp.where(tile > 0, prev, 0.0)
        ext_u[CONV_HALO:CONV_HALO + tm, :] = cv * sgc

        _shift_copies(ext_d, sh_d, tm)
        _shift_copies(ext_u, sh_u, tm)
        du0 = jnp.zeros((tm, CONV_CH), F32)
        for j in range(CONV_W):
            du0 = du0 + w_ref[j:j + 1, :] * _window(ext_d, sh_d, CONV_W - 1 - j, tm)
            dw_ref[j:j + 1, :] += jnp.sum(
                du1 * _window(ext_u, sh_u, CONV_HALO - (CONV_W - 1) + j, tm), axis=0, keepdims=True)
        dp_ref[:, 0:CONV_CH] = (du0 * sgc).astype(dp_ref.dtype)
        dp_ref[:, CONV_CH:2 * CONV_CH] = (du0 * cv * sgc * (1.0 - sgc)).astype(dp_ref.dtype)

    rev = lambda c: pl.BlockSpec((tm, CONV_CH), lambda i: (nt - 1 - i, c))
    prev = lambda c: pl.BlockSpec((CONV_HALO, CONV_CH), lambda i: (jnp.maximum((nt - 1 - i) * per - 1, 0), c))
    vec = pl.BlockSpec((1, CONV_CH), lambda i: (0, 0))
    wspec = pl.BlockSpec((CONV_HALO, CONV_CH), lambda i: (0, 0))
    return pl.pallas_call(
        body, grid=(nt,),
        in_specs=[rev(0), rev(0), rev(0), rev(1), prev(0), prev(1), wspec, vec, vec],
        out_specs=[pl.BlockSpec((tm, 2 * CONV_CH), lambda i: (nt - 1 - i, 0)), wspec, vec, vec, vec],
        out_shape=[jax.ShapeDtypeStruct((t, PROJ_MAIN), MXU_DTYPE), jax.ShapeDtypeStruct((CONV_HALO, CONV_CH), F32),
                   jax.ShapeDtypeStruct((1, CONV_CH), F32), jax.ShapeDtypeStruct((1, CONV_CH), F32),
                   jax.ShapeDtypeStruct((1, CONV_CH), F32)],
        scratch_shapes=[pltpu.VMEM((tm + CONV_HALO, CONV_CH), F32), pltpu.VMEM((tm + CONV_HALO, CONV_CH), F32),
                        _shift_scratch(tm), _shift_scratch(tm)],
        compiler_params=_cp(("arbitrary",)), name=name)(dy, u1, proj, proj, proj, proj, w32, lw, lb)


def dn_pre_fwd(proj, w8, name):
    t = proj.shape[0]
    tm = _row_tile(t)

    def body(raw_ref, w_ref, o_ref, ext):
        g = pl.program_id(0)
        i = pl.program_id(1)

        @pl.when(i == 0)
        def _():
            ext[0:DN_HALO, :] = jnp.zeros((DN_HALO, DQ), F32)

        @pl.when(i > 0)
        def _():
            ext[0:DN_HALO, :] = ext[tm:tm + DN_HALO, :]

        ext[DN_HALO:DN_HALO + tm, :] = raw_ref[...]
        c = jnp.zeros((tm, DQ), F32)
        for j in range(DN_W):
            c = c + w_ref[j:j + 1, :] * ext[pl.ds(DN_HALO - (DN_W - 1) + j, tm), :]
        s = c * _sigmoid(c)
        scale = jnp.where(g == 0, DH ** -0.5, 1.0)
        for h in range(NH):
            sh = s[:, h * DH:(h + 1) * DH]
            r = lax.rsqrt(jnp.sum(sh * sh, axis=-1, keepdims=True) + L2_EPS)
            o_ref[:, h * DH:(h + 1) * DH] = jnp.where(g == 2, sh, sh * (r * scale))

    return pl.pallas_call(
        body, grid=(3, t // tm),
        in_specs=[pl.BlockSpec((tm, DQ), lambda g, i: (i, 2 + g)), pl.BlockSpec((DN_HALO, DQ), lambda g, i: (0, g))],
        out_specs=pl.BlockSpec((tm, DQ), lambda g, i: (i, g)),
        out_shape=jax.ShapeDtypeStruct((t, 3 * DQ), F32),
        scratch_shapes=[pltpu.VMEM((tm + DN_HALO, DQ), F32)],
        compiler_params=_cp(("arbitrary", "arbitrary")), name=name)(proj, w8)


def dn_pre_bwd(dproj, dqkv, proj, w8, name):
    t = proj.shape[0]
    tm = _row_tile(t)
    nt = t // tm
    per = tm // DN_HALO

    def body(dp_in, d_ref, raw_ref, rawp_ref, w_ref, dp_ref, dw_ref, ext_d, ext_r):
        del dp_in
        g = pl.program_id(0)
        i = pl.program_id(1)
        tile = nt - 1 - i

        @pl.when(i == 0)
        def _():
            ext_d[tm:tm + DN_HALO, :] = jnp.zeros((DN_HALO, DQ), F32)
            dw_ref[...] = jnp.zeros_like(dw_ref)

        @pl.when(i > 0)
        def _():
            ext_d[tm:tm + DN_HALO, :] = ext_d[0:DN_HALO, :]

        ext_r[0:DN_HALO, :] = jnp.where(tile > 0, rawp_ref[...], 0.0)
        ext_r[DN_HALO:DN_HALO + tm, :] = raw_ref[...]
        c = jnp.zeros((tm, DQ), F32)
        for j in range(DN_W):
            c = c + w_ref[j:j + 1, :] * ext_r[pl.ds(DN_HALO - (DN_W - 1) + j, tm), :]
        sg = _sigmoid(c)
        s = c * sg
        scale = jnp.where(g == 0, DH ** -0.5, 1.0)
        for h in range(NH):
            sl = slice(h * DH, (h + 1) * DH)
            sh = s[:, sl]
            dn = d_ref[:, sl]
            r = lax.rsqrt(jnp.sum(sh * sh, axis=-1, keepdims=True) + L2_EPS)
            unit = sh * r
            dsn = (r * scale) * (dn - unit * jnp.sum(dn * unit, axis=-1, keepdims=True))
            ds = jnp.where(g == 2, dn, dsn)
            ext_d[0:tm, sl] = ds * (sg[:, sl] * (1.0 + c[:, sl] * (1.0 - sg[:, sl])))
        dc = ext_d[0:tm, :]
        draw = jnp.zeros((tm, DQ), F32)
        for j in range(DN_W):
            draw = draw + w_ref[j:j + 1, :] * ext_d[pl.ds(DN_W - 1 - j, tm), :]
            dw_ref[j:j + 1, :] += jnp.sum(
                dc * ext_r[pl.ds(DN_HALO - (DN_W - 1) + j, tm), :], axis=0, keepdims=True)
        dp_ref[...] = draw.astype(dp_ref.dtype)

    return pl.pallas_call(
        body, grid=(3, nt),
        in_specs=[pl.BlockSpec(memory_space=pl.ANY),
                  pl.BlockSpec((tm, DQ), lambda g, i: (nt - 1 - i, g)),
                  pl.BlockSpec((tm, DQ), lambda g, i: (nt - 1 - i, 2 + g)),
                  pl.BlockSpec((DN_HALO, DQ), lambda g, i: (jnp.maximum((nt - 1 - i) * per - 1, 0), 2 + g)),
                  pl.BlockSpec((DN_HALO, DQ), lambda g, i: (0, g))],
        out_specs=[pl.BlockSpec((tm, DQ), lambda g, i: (nt - 1 - i, 2 + g)),
                   pl.BlockSpec((DN_HALO, DQ), lambda g, i: (0, g))],
        out_shape=[jax.ShapeDtypeStruct(dproj.shape, dproj.dtype), jax.ShapeDtypeStruct((DN_HALO, 3 * DQ), F32)],
        scratch_shapes=[pltpu.VMEM((tm + DN_HALO, DQ), F32), pltpu.VMEM((tm + DN_HALO, DQ), F32)],
        input_output_aliases={0: 0},
        compiler_params=_cp(("arbitrary", "arbitrary")), name=name)(dproj, dqkv, proj, proj, w8)


def gate_fwd(pg, alog, dtb, seq, name):
    t = pg.shape[0]
    tm = _row_tile(t)

    def body(x_ref, al_ref, dt_ref, o_ref):
        i = pl.program_id(0)
        x = x_ref[...]
        lane = lax.broadcasted_iota(jnp.int32, (tm, GATE_W), 1)
        gg = -jnp.exp(al_ref[...]) * _softplus(x + dt_ref[...])
        out = jnp.where(lane < NH, _sigmoid(x), jnp.where(lane < 2 * NH, gg, 0.0))
        o_ref[...] = jnp.where(_valid_rows(i, tm, seq, GATE_W), out, 0.0)

    row = pl.BlockSpec((tm, GATE_W), lambda i: (i, 0))
    vec = pl.BlockSpec((1, GATE_W), lambda i: (0, 0))
    return pl.pallas_call(
        body, grid=(t // tm,), in_specs=[row, vec, vec], out_specs=row,
        out_shape=jax.ShapeDtypeStruct((t, GATE_W), F32), compiler_params=_cp(("parallel",)), name=name)(pg, alog, dtb)


def gate_bwd(dbg, pg, alog, dtb, seq, name):
    t = pg.shape[0]
    tm = _row_tile(t)

    def body(d_ref, x_ref, al_ref, dt_ref, o_ref, dal_ref, ddt_ref):
        i = pl.program_id(0)
        x = x_ref[...]
        lane = lax.broadcasted_iota(jnp.int32, (tm, GATE_W), 1)
        d = jnp.where(_valid_rows(i, tm, seq, GATE_W), d_ref[...], 0.0)
        beta = _sigmoid(x)
        xs = x + dt_ref[...]
        e = -jnp.exp(al_ref[...])
        is_g = jnp.logical_and(lane >= NH, lane < 2 * NH)
        da = jnp.where(is_g, d * e * _sigmoid(xs), 0.0)
        dgg = jnp.where(is_g, d * e * _softplus(xs), 0.0)
        o_ref[...] = jnp.where(lane < NH, d * beta * (1.0 - beta), da).astype(o_ref.dtype)

        @pl.when(i == 0)
        def _():
            dal_ref[...] = jnp.zeros_like(dal_ref)
            ddt_ref[...] = jnp.zeros_like(ddt_ref)

        dal_ref[...] += jnp.sum(dgg, axis=0, keepdims=True)
        ddt_ref[...] += jnp.sum(da, axis=0, keepdims=True)

    row = pl.BlockSpec((tm, GATE_W), lambda i: (i, 0))
    vec = pl.BlockSpec((1, GATE_W), lambda i: (0, 0))
    return pl.pallas_call(
        body, grid=(t // tm,), in_specs=[row, row, vec, vec], out_specs=[row, vec, vec],
        out_shape=[jax.ShapeDtypeStruct((t, GATE_W), MXU_DTYPE), jax.ShapeDtypeStruct((1, GATE_W), F32),
                   jax.ShapeDtypeStruct((1, GATE_W), F32)],
        compiler_params=_cp(("arbitrary",)), name=name)(dbg, pg, alog, dtb)


def _chunk_masks():
    ii = lax.broadcasted_iota(jnp.int32, (CHUNK, CHUNK), 0)
    jj = lax.broadcasted_iota(jnp.int32, (CHUNK, CHUNK), 1)
    return ii, jj, ii >= jj, ii > jj


def _lane_col(x, lane, idx):
    return jnp.sum(jnp.where(lane == idx, x, 0.0), axis=1, keepdims=True)


def _delta_terms(q, k, v, bgs, nch, low, strict):
    idx = [(c, h) for c in range(nch) for h in range(NH)]
    lane = lax.broadcasted_iota(jnp.int32, (CHUNK, GATE_W), 1)
    rowi = lax.broadcasted_iota(jnp.int32, (CHUNK, 1), 0)
    r4 = lax.broadcasted_iota(jnp.int32, (NH * CHUNK, GATE_W), 0)
    l4 = lax.broadcasted_iota(jnp.int32, (NH * CHUNK, GATE_W), 1)
    sel = jnp.where(l4 == NH + jnp.right_shift(r4, CHUNK_LOG2), 1.0, 0.0)
    lowf = jnp.where(low, 1.0, 0.0)
    gam_all = [_mmx(lowf, b, NN) for b in bgs]
    gam_rows = [_mmx(sel, g, NT) for g in gam_all]
    beta = [_lane_col(bgs[c], lane, h) for c, h in idx]
    gam = [_lane_col(gam_all[c], lane, NH + h) for c, h in idx]
    dm = [jnp.exp(jnp.where(low, g - gam_rows[c][h * CHUNK:(h + 1) * CHUNK, :], -1e30))
          for g, (c, h) in zip(gam, idx)]
    glast = [jnp.sum(jnp.where(rowi == CHUNK - 1, g, 0.0), axis=0, keepdims=True) for g in gam]
    eg = [jnp.exp(g) for g in gam]
    ekl = [jnp.exp(gl - g) for gl, g in zip(glast, gam)]
    gl = [jnp.exp(x) for x in glast]
    kb = [x * b for x, b in zip(k, beta)]
    vb = [x * b for x, b in zip(v, beta)]
    kbg = [x * e for x, e in zip(kb, eg)]
    kk = _mm1_many(kb, k, NT)
    qk = _mm1_many(q, k, NT)
    a_mat = [jnp.where(strict, x * d, 0.0) for x, d in zip(kk, dm)]
    p_mat = [jnp.where(low, x * d, 0.0) for x, d in zip(qk, dm)]
    qd = [x * e for x, e in zip(q, eg)]
    kd = [x * e for x, e in zip(k, ekl)]
    return dict(idx=idx, beta=beta, dm=dm, eg=eg, ekl=ekl, gl=gl, kb=kb, vb=vb, kbg=kbg, a=a_mat, p=p_mat, qd=qd,
                kd=kd, lane=lane, rowi=rowi)


def _unit_lower_inverses(a_list, ii, jj, eye):
    def same(log2):
        return jnp.right_shift(ii, log2) == jnp.right_shift(jj, log2)

    n = [-jnp.where(same(INV_BASE_LOG2), a, 0.0) for a in a_list]
    x = [eye + v for v in n]
    p = n
    for _ in range(INV_BASE_LOG2 - 1):
        p = _mm1_many(p, p, NN)
        x = [xi + y for xi, y in zip(x, _mm1_many(x, p, NN))]
    for log2 in range(INV_BASE_LOG2, CHUNK_LOG2):
        off = jnp.logical_and(same(log2 + 1), jnp.logical_not(same(log2)))
        a_off = [jnp.where(off, a, 0.0) for a in a_list]
        x = [xi - y for xi, y in zip(x, _mm1_many(x, _mm1_many(a_off, x, NN), NN))]
    return x


def _transposes(xs, eye):
    e = eye.astype(MXU_DTYPE)
    parts = [_split(x, 2) for x in xs]
    return [_dot(p[0], e, TN) + _dot(p[1], e, TN) for p in parts]


def _load_heads(ref, nch):
    return [ref[c * CHUNK:(c + 1) * CHUNK, h * DH:(h + 1) * DH] for c in range(nch) for h in range(NH)]


def delta_fwd(qkv, bg, name, push=None):
    t = qkv.shape[0]
    nc = t // CHUNK
    nch = _pick(nc, DELTA_CHUNKS)
    rows = nch * CHUNK
    ng = nc // nch
    npush = 0 if push is None else len(push)

    def body(*refs):
        q_ref, k_ref, v_ref, bg_ref = refs[:4]
        x_refs = refs[4:4 + npush]
        o_ref, sh_ref, mi_ref, u_ref, w_ref = refs[4 + npush:9 + npush]
        got_refs = refs[9 + npush:9 + 2 * npush]
        s_ref = refs[9 + 2 * npush]
        n = pl.program_id(0)
        if npush:
            copies = _push_copies(x_refs, got_refs, *refs[10 + 2 * npush:], scatter=False)

            @pl.when(n == 0)
            def _():
                _push_start(copies)

        @pl.when(n == 0)
        def _():
            s_ref[...] = jnp.zeros_like(s_ref)

        ii, jj, low, strict = _chunk_masks()
        eye = jnp.where(ii == jj, 1.0, 0.0)
        q, k, v = _load_heads(q_ref, nch), _load_heads(k_ref, nch), _load_heads(v_ref, nch)
        bgs = [bg_ref[c * CHUNK:(c + 1) * CHUNK, :] for c in range(nch)]
        tm_ = _delta_terms(q, k, v, bgs, nch, low, strict)
        m_inv = _unit_lower_inverses(tm_["a"], ii, jj, eye)
        u = _mm3_many(m_inv, tm_["vb"], NN)
        w = _mm3_many(m_inv, tm_["kbg"], NN)
        for i, (c, h) in enumerate(tm_["idx"]):
            mi_ref[c, h] = m_inv[i]
            u_ref[c * CHUNK:(c + 1) * CHUNK, h * DH:(h + 1) * DH] = u[i]
            w_ref[c * CHUNK:(c + 1) * CHUNK, h * DH:(h + 1) * DH] = w[i]
        s = [s_ref[h] for h in range(NH)]
        for c in range(nch):
            pr = range(c * NH, (c + 1) * NH)
            ws = [_mm1(w[i], s[i - c * NH], NN) for i in pr]
            qs = [_mm1(tm_["qd"][i], s[i - c * NH], NN) for i in pr]
            vn = [u[i] - x for i, x in zip(pr, ws)]
            pv = [_mm1(tm_["p"][i], x, NN) for i, x in zip(pr, vn)]
            kv = [_mm1(tm_["kd"][i], x, TN) for i, x in zip(pr, vn)]
            for h in range(NH):
                o_ref[c * CHUNK:(c + 1) * CHUNK, h * DH:(h + 1) * DH] = qs[h] + pv[h]
                sh_ref[c, h] = s[h]
                s[h] = tm_["gl"][c * NH + h] * s[h] + kv[h]
        for h in range(NH):
            s_ref[h] = s[h]
        if npush:
            @pl.when(n == ng - 1)
            def _():
                _push_finish(copies)

    col = lambda c: pl.BlockSpec((rows, DQ), lambda n: (n, c))
    any_spec = pl.BlockSpec(memory_space=pl.ANY)
    pushed = [] if push is None else list(push)
    outs = pl.pallas_call(
        body, grid=(ng,),
        in_specs=[col(0), col(1), col(2), pl.BlockSpec((rows, GATE_W), lambda n: (n, 0))] + [any_spec] * npush,
        out_specs=[col(0), pl.BlockSpec((nch, NH, DH, DH), lambda n: (n, 0, 0, 0)),
                   pl.BlockSpec((nch, NH, CHUNK, CHUNK), lambda n: (n, 0, 0, 0)), col(0), col(0)]
        + [any_spec] * npush,
        out_shape=[jax.ShapeDtypeStruct((t, DQ), F32), jax.ShapeDtypeStruct((nc, NH, DH, DH), F32),
                   jax.ShapeDtypeStruct((nc, NH, CHUNK, CHUNK), F32), jax.ShapeDtypeStruct((t, DQ), F32),
                   jax.ShapeDtypeStruct((t, DQ), F32)] + _push_out_shapes(pushed, scatter=False),
        scratch_shapes=[pltpu.VMEM((NH, DH, DH), F32)] + (_push_sems(npush) if npush else []),
        compiler_params=_cp(("arbitrary",)), name=name)(qkv, qkv, qkv, bg, *pushed)
    return outs[:5], outs[5:]


def delta_bwd(qkv, bg, do, s_hist, m_hist, u_all, w_all, name, push=None):
    t = qkv.shape[0]
    nc = t // CHUNK
    nch = _pick(nc, DELTA_CHUNKS)
    rows = nch * CHUNK
    ng = nc // nch
    npush = 0 if push is None else len(push)

    def body(*refs):
        q_ref, k_ref, v_ref, bg_ref, do_ref, sh_ref, mi_ref, u_ref, w_ref = refs[:9]
        x_refs = refs[9:9 + npush]
        dqkv_ref, dbg_ref = refs[9 + npush:11 + npush]
        got_refs = refs[11 + npush:11 + 2 * npush]
        ds_ref = refs[11 + 2 * npush]
        n = pl.program_id(0)
        if npush:
            copies = _push_copies(x_refs, got_refs, *refs[12 + 2 * npush:], scatter=True)

            @pl.when(n == 0)
            def _():
                _push_start(copies)

        @pl.when(n == 0)
        def _():
            ds_ref[...] = jnp.zeros_like(ds_ref)

        ii, jj, low, strict = _chunk_masks()
        eye = jnp.where(ii == jj, 1.0, 0.0)
        q, k, v = _load_heads(q_ref, nch), _load_heads(k_ref, nch), _load_heads(v_ref, nch)
        d_o = _load_heads(do_ref, nch)
        bgs = [bg_ref[c * CHUNK:(c + 1) * CHUNK, :] for c in range(nch)]
        tm_ = _delta_terms(q, k, v, bgs, nch, low, strict)
        idx, lane, rowi = tm_["idx"], tm_["lane"], tm_["rowi"]
        beta, dm, eg, ekl, gl = tm_["beta"], tm_["dm"], tm_["eg"], tm_["ekl"], tm_["gl"]
        kb, kbg, qd, kd, a_mat, p_mat = tm_["kb"], tm_["kbg"], tm_["qd"], tm_["kd"], tm_["a"], tm_["p"]
        s = [sh_ref[c, h] for c, h in idx]
        m_inv = [mi_ref[c, h] for c, h in idx]
        u, w = _load_heads(u_ref, nch), _load_heads(w_ref, nch)
        ws = _mm1_many(w, s, NN)
        vn = [x - y for x, y in zip(u, ws)]
        pdo = _mm1_many(p_mat, d_o, TN)
        qdo = _mm1_many(qd, d_o, TN)
        dqd = _mm1_many(d_o, s, NT)
        dp = [jnp.where(low, x, 0.0) for x in _mm1_many(d_o, vn, NT)]

        nprob = len(idx)
        dvn, dkd, dgl = [None] * nprob, [None] * nprob, [None] * nprob
        ds = [ds_ref[h] for h in range(NH)]
        for c in reversed(range(nch)):
            pr = list(range(c * NH, (c + 1) * NH))
            kds = [_mm1(kd[i], ds[i - c * NH], NN) for i in pr]
            for i, x in zip(pr, kds):
                dvn[i] = pdo[i] + x
            wdv = [_mm1(w[i], dvn[i], TN) for i in pr]
            for i in pr:
                h = i - c * NH
                dkd[i] = _mm1(vn[i], ds[h], NT)
                dgl[i] = jnp.sum(jnp.sum(s[i] * ds[h], axis=1, keepdims=True), axis=0, keepdims=True)
                ds[h] = qdo[i] + gl[i] * ds[h] - wdv[h]
        for h in range(NH):
            ds_ref[h] = ds[h]

        dw = [-x for x in _mm1_many(dvn, s, NT)]
        dvb = _mm3_many(m_inv, dvn, TN)
        dkbg = _mm3_many(m_inv, dw, TN)
        da1 = _mm1_many(dvb, u, NT)
        da2 = _mm1_many(dkbg, w, NT)
        da = [-jnp.where(strict, x + y, 0.0) for x, y in zip(da1, da2)]
        gm = [x * d for x, d in zip(da, dm)]
        hm = [x * d for x, d in zip(dp, dm)]
        gk = _mm1_many(gm, k, NN)
        gkb = _mm1_many(gm, kb, TN)
        hq = _mm1_many(hm, q, TN)
        hk = _mm1_many(hm, k, NN)
        em = [x * a + y * p for x, a, y, p in zip(da, a_mat, dp, p_mat)]
        em_t = _transposes(em, eye)
        dbeta_all = [jnp.zeros((CHUNK, GATE_W), F32) for _ in range(nch)]
        dgam_all = [jnp.zeros((CHUNK, GATE_W), F32) for _ in range(nch)]
        for i, (c, h) in enumerate(idx):
            dkb = gk[i] + dkbg[i] * eg[i]
            dk = gkb[i] + hq[i] + dkd[i] * ekl[i] + beta[i] * dkb
            dq = hk[i] + dqd[i] * eg[i]
            dkd_kd = jnp.sum(dkd[i] * kd[i], axis=1, keepdims=True)
            dgam = (jnp.sum(em[i], axis=1, keepdims=True) - jnp.sum(em_t[i], axis=1, keepdims=True)
                    + jnp.sum(dqd[i] * qd[i], axis=1, keepdims=True) - dkd_kd
                    + jnp.sum(dkbg[i] * kbg[i], axis=1, keepdims=True))
            tail = jnp.sum(dkd_kd, axis=0, keepdims=True) + dgl[i] * gl[i]
            dgam = dgam + jnp.where(rowi == CHUNK - 1, tail, 0.0)
            dbeta = jnp.sum(dkb * k[i], axis=1, keepdims=True) + jnp.sum(dvb[i] * v[i], axis=1, keepdims=True)
            rs = slice(c * CHUNK, (c + 1) * CHUNK)
            dqkv_ref[rs, h * DH:(h + 1) * DH] = dq
            dqkv_ref[rs, DQ + h * DH:DQ + (h + 1) * DH] = dk
            dqkv_ref[rs, 2 * DQ + h * DH:2 * DQ + (h + 1) * DH] = beta[i] * dvb[i]
            dbeta_all[c] = dbeta_all[c] + jnp.where(lane == h, dbeta, 0.0)
            dgam_all[c] = dgam_all[c] + jnp.where(lane == NH + h, dgam, 0.0)
        upf = jnp.where(ii <= jj, 1.0, 0.0)
        for c in range(nch):
            dg_all = _mmx(upf, dgam_all[c], NN)
            dbg_ref[c * CHUNK:(c + 1) * CHUNK, :] = jnp.where(lane < NH, dbeta_all[c], dg_all)
        if npush:
            @pl.when(n == ng - 1)
            def _():
                _push_finish(copies)

    col = lambda c: pl.BlockSpec((rows, DQ), lambda n: (ng - 1 - n, c))
    gate = pl.BlockSpec((rows, GATE_W), lambda n: (ng - 1 - n, 0))
    any_spec = pl.BlockSpec(memory_space=pl.ANY)
    pushed = [] if push is None else list(push)
    outs = pl.pallas_call(
        body, grid=(ng,),
        in_specs=[col(0), col(1), col(2), gate, col(0),
                  pl.BlockSpec((nch, NH, DH, DH), lambda n: (ng - 1 - n, 0, 0, 0)),
                  pl.BlockSpec((nch, NH, CHUNK, CHUNK), lambda n: (ng - 1 - n, 0, 0, 0)), col(0), col(0)]
        + [any_spec] * npush,
        out_specs=[pl.BlockSpec((rows, 3 * DQ), lambda n: (ng - 1 - n, 0)), gate] + [any_spec] * npush,
        out_shape=[jax.ShapeDtypeStruct((t, 3 * DQ), F32), jax.ShapeDtypeStruct((t, GATE_W), F32)]
        + _push_out_shapes(pushed, scatter=True),
        scratch_shapes=[pltpu.VMEM((NH, DH, DH), F32)] + (_push_sems(npush) if npush else []),
        compiler_params=_cp(("arbitrary",)), name=name)(qkv, qkv, qkv, bg, do, s_hist, m_hist, u_all, w_all, *pushed)
    return outs[0], outs[1], outs[2:]


def dn_post_fwd(ybuf, o, proj, nw, name):
    t = o.shape[0]
    tm = _row_tile(t)

    def body(y_in, o_ref, z_ref, nw_ref, y_ref):
        del y_in
        nwv = nw_ref[...]
        for h in range(NH):
            sl = slice(h * DH, (h + 1) * DH)
            oh = o_ref[:, sl]
            z = z_ref[:, sl]
            r = lax.rsqrt(jnp.mean(oh * oh, axis=-1, keepdims=True) + NORM_EPS)
            y_ref[:, sl] = (oh * r * nwv * (z * _sigmoid(z))).astype(y_ref.dtype)

    return pl.pallas_call(
        body, grid=(t // tm,),
        in_specs=[pl.BlockSpec(memory_space=pl.ANY), pl.BlockSpec((tm, DQ), lambda i: (i, 0)),
                  pl.BlockSpec((tm, DQ), lambda i: (i, 5)), pl.BlockSpec((1, DH), lambda i: (0, 0))],
        out_specs=pl.BlockSpec((tm, DQ), lambda i: (i, 1)),
        out_shape=jax.ShapeDtypeStruct(ybuf.shape, ybuf.dtype), input_output_aliases={0: 0},
        compiler_params=_cp(("parallel",)), name=name)(ybuf, o, proj, nw)


def dn_post_bwd(dproj, dy, o, proj, nw, name):
    t = o.shape[0]
    tm = _row_tile(t)

    def body(dp_in, dy_ref, o_ref, z_ref, nw_ref, do_ref, dp_ref, dnw_ref):
        del dp_in
        i = pl.program_id(0)
        nwv = nw_ref[...]
        acc = jnp.zeros((1, DH), F32)
        for h in range(NH):
            sl = slice(h * DH, (h + 1) * DH)
            oh = o_ref[:, sl]
            z = z_ref[:, sl]
            dyh = dy_ref[:, sl]
            r = lax.rsqrt(jnp.mean(oh * oh, axis=-1, keepdims=True) + NORM_EPS)
            xh = oh * r
            sg = _sigmoid(z)
            sz = z * sg
            dxh = dyh * nwv * sz
            do_ref[:, sl] = r * (dxh - xh * jnp.mean(dxh * xh, axis=-1, keepdims=True))
            dp_ref[:, sl] = (dyh * xh * nwv * (sg * (1.0 + z * (1.0 - sg)))).astype(dp_ref.dtype)
            acc = acc + jnp.sum(dyh * xh * sz, axis=0, keepdims=True)

        @pl.when(i == 0)
        def _():
            dnw_ref[...] = jnp.zeros_like(dnw_ref)

        dnw_ref[...] += acc

    vec = pl.BlockSpec((1, DH), lambda i: (0, 0))
    return pl.pallas_call(
        body, grid=(t // tm,),
        in_specs=[pl.BlockSpec(memory_space=pl.ANY), pl.BlockSpec((tm, DQ), lambda i: (i, 1)),
                  pl.BlockSpec((tm, DQ), lambda i: (i, 0)), pl.BlockSpec((tm, DQ), lambda i: (i, 5)), vec],
        out_specs=[pl.BlockSpec((tm, DQ), lambda i: (i, 0)), pl.BlockSpec((tm, DQ), lambda i: (i, 5)), vec],
        out_shape=[jax.ShapeDtypeStruct((t, DQ), F32), jax.ShapeDtypeStruct(dproj.shape, dproj.dtype),
                   jax.ShapeDtypeStruct((1, DH), F32)],
        input_output_aliases={0: 1}, compiler_params=_cp(("arbitrary",)), name=name)(dproj, dy, o, proj, nw)


def swiglu_fwd(gate, up, name):
    t = gate.shape[0]
    tm = _row_tile(t)
    tn = DFF // 2

    def body(g_ref, u_ref, o_ref):
        g = g_ref[...]
        o_ref[...] = (g * _sigmoid(g) * u_ref[...]).astype(o_ref.dtype)

    blk = pl.BlockSpec((tm, tn), lambda i, j: (i, j))
    return pl.pallas_call(
        body, grid=(t // tm, DFF // tn), in_specs=[blk, blk], out_specs=blk,
        out_shape=jax.ShapeDtypeStruct((t, DFF), MXU_DTYPE),
        compiler_params=_cp(("parallel", "parallel")), name=name)(gate, up)


def swiglu_bwd(gate, up, dact, name):
    t = gate.shape[0]
    tm = _row_tile(t)
    tn = DFF // 2

    def body(g_ref, u_ref, d_ref, dg_ref, du_ref):
        g = g_ref[...]
        sg = _sigmoid(g)
        d = d_ref[...]
        dg_ref[...] = (d * u_ref[...] * (sg * (1.0 + g * (1.0 - sg)))).astype(dg_ref.dtype)
        du_ref[...] = (d * (g * sg)).astype(du_ref.dtype)

    blk = pl.BlockSpec((tm, tn), lambda i, j: (i, j))
    return pl.pallas_call(
        body, grid=(t // tm, DFF // tn), in_specs=[blk, blk, blk], out_specs=[blk, blk],
        out_shape=[jax.ShapeDtypeStruct((t, DFF), MXU_DTYPE)] * 2,
        compiler_params=_cp(("parallel", "parallel")), name=name)(gate, up, dact)


def _shifted(first, second, s, lane):
    if s == 0:
        return first
    return jnp.where(lane < LANES - s, pltpu.roll(first, LANES - s, 1), pltpu.roll(second, LANES - s, 1))


def unshard_cols(g8, w, widths, name):
    _, r, wp = g8.shape
    rb = _pick(r, (256, 128, 64, 32, 16))

    def body(g_ref, *o_refs):
        lane = lax.broadcasted_iota(jnp.int32, (rb, LANES), 1)
        zeros = jnp.zeros((rb, LANES), F32)

        def src(j, ta):
            if j >= NDEV or ta * LANES >= wp:
                return zeros
            return g_ref[j, :, ta * LANES:(ta + 1) * LANES].astype(F32)

        base = 0
        for o_ref, width in zip(o_refs, widths):
            for b in range(width // LANES):
                c0 = base + b * LANES
                if c0 >= NDEV * w:
                    tile = zeros
                else:
                    j0, o0 = divmod(c0, w)
                    n0 = min(w - o0, LANES)
                    ta, s = divmod(o0, LANES)
                    tile = _shifted(src(j0, ta), src(j0, ta + 1), s, lane)
                    if n0 < LANES:
                        nxt = pltpu.roll(src(j0 + 1, 0), n0, 1) if j0 + 1 < NDEV else zeros
                        tile = jnp.where(lane < n0, tile, nxt)
                o_ref[:, b * LANES:(b + 1) * LANES] = tile.astype(o_ref.dtype)
            base += width

    return pl.pallas_call(
        body, grid=(r // rb,), in_specs=[pl.BlockSpec((NDEV, rb, wp), lambda i: (0, i, 0))],
        out_specs=[pl.BlockSpec((rb, width), lambda i: (i, 0)) for width in widths],
        out_shape=[jax.ShapeDtypeStruct((r, width), g8.dtype) for width in widths],
        compiler_params=_cp(("parallel",)), name=name)(g8)


def shard_cols(parts, w, name):
    r = parts[0].shape[0]
    wp = _lane_pad(w)
    rb = _pick(r, (256, 128, 64, 32, 16))
    tiles_of = [p.shape[1] // LANES for p in parts]

    def body(*refs):
        p_refs, o_ref = refs[:-1], refs[-1]
        lane = lax.broadcasted_iota(jnp.int32, (rb, LANES), 1)
        zeros = jnp.zeros((rb, LANES), F32)

        def glob(tile_idx):
            for p_ref, n_tiles in zip(p_refs, tiles_of):
                if tile_idx < n_tiles:
                    return p_ref[:, tile_idx * LANES:(tile_idx + 1) * LANES].astype(F32)
                tile_idx -= n_tiles
            return zeros

        for j in range(NDEV):
            for a in range(wp // LANES):
                nv = min(w - a * LANES, LANES)
                tb, s = divmod(w * j + a * LANES, LANES)
                tile = _shifted(glob(tb), glob(tb + 1), s, lane)
                if nv < LANES:
                    tile = jnp.where(lane < nv, tile, 0.0)
                o_ref[j, :, a * LANES:(a + 1) * LANES] = tile.astype(o_ref.dtype)

    return pl.pallas_call(
        body, grid=(r // rb,), in_specs=[pl.BlockSpec((rb, p.shape[1]), lambda i: (i, 0)) for p in parts],
        out_specs=pl.BlockSpec((NDEV, rb, wp), lambda i: (0, i, 0)),
        out_shape=jax.ShapeDtypeStruct((NDEV, r, wp), GRAD_DTYPE),
        compiler_params=_cp(("parallel",)), name=name)(*parts)


def _me_and_peers():
    mx, my, mc = lax.axis_index("x"), lax.axis_index("y"), lax.axis_index("c")
    me = 4 * mx + 2 * my + mc
    peers = []
    for kk in range(1, NDEV):
        px = 1 - mx if kk & 4 else mx
        py = 1 - my if kk & 2 else my
        pc = 1 - mc if kk & 1 else mc
        peers.append(((px, py, pc), 4 * px + 2 * py + pc))
    return me, peers


def _push_copies(x_refs, o_refs, send_sems, recv_sems, local_sems, scatter):
    me, peers = _me_and_peers()
    npeer = NDEV - 1
    local, sends, recvs = [], [], []
    for a, (x_ref, o_ref) in enumerate(zip(x_refs, o_refs)):
        local.append(pltpu.make_async_copy(x_ref.at[me] if scatter else x_ref, o_ref.at[me], local_sems.at[a]))
        for kk, (peer, pidx) in enumerate(peers):
            src = x_ref.at[pidx] if scatter else x_ref
            sems = dict(send_sem=send_sems.at[a * npeer + kk], recv_sem=recv_sems.at[a * npeer + kk],
                        device_id=peer, device_id_type=pl.DeviceIdType.MESH)
            sends.append(pltpu.make_async_remote_copy(src_ref=src, dst_ref=o_ref.at[me], **sems))
            recvs.append(pltpu.make_async_remote_copy(src_ref=src, dst_ref=o_ref.at[pidx], **sems))
    return local, sends, recvs


def _push_start(copies):
    local, sends, _ = copies
    for cp in local + sends:
        cp.start()


def _push_finish(copies):
    local, sends, recvs = copies
    for cp in recvs:
        cp.wait_recv()
    for cp in sends:
        cp.wait_send()
    for cp in local:
        cp.wait()


def _push_out_shapes(xs, scatter):
    return [jax.ShapeDtypeStruct(x.shape if scatter else (NDEV,) + x.shape, x.dtype) for x in xs]


def _push_sems(n):
    return [pltpu.SemaphoreType.DMA((n * (NDEV - 1),)), pltpu.SemaphoreType.DMA((n * (NDEV - 1),)),
            pltpu.SemaphoreType.DMA((n,))]


def _push_to_all(xs, name, scatter):
    n = len(xs)

    def body(*refs):
        copies = _push_copies(refs[:n], refs[n:2 * n], *refs[2 * n:], scatter=scatter)
        _push_start(copies)
        _push_finish(copies)

    any_spec = pl.BlockSpec(memory_space=pl.ANY)
    return pl.pallas_call(
        body, in_specs=[any_spec] * n, out_specs=[any_spec] * n, out_shape=_push_out_shapes(xs, scatter),
        scratch_shapes=_push_sems(n), name=name)(*xs)


def adamw(recv, w, m, v, name):
    rows, cols = w.shape
    c1 = 1.0 - ADAM_B1 ** ADAM_STEP
    c2 = 1.0 - ADAM_B2 ** ADAM_STEP
    cap = ADAM_BLOCK_BYTES // (NDEV * cols * 4)
    rb = _pick(rows, [p for p in (2048, 1024, 512, 256, 128, 64, 32, 16, 8) if p <= cap])

    def body(r_ref, w_ref, m_ref, v_ref, g_ref, d_ref, m2_ref, v2_ref):
        g = r_ref[0].astype(F32)
        for j in range(1, NDEV):
            g = g + r_ref[j].astype(F32)
        m2 = ADAM_B1 * m_ref[...] + (1.0 - ADAM_B1) * g
        v2 = ADAM_B2 * v_ref[...] + (1.0 - ADAM_B2) * (g * g)
        g_ref[...] = g
        m2_ref[...] = m2
        v2_ref[...] = v2
        d_ref[...] = -ADAM_LR * ((m2 / c1) / (jnp.sqrt(v2 / c2) + ADAM_EPS) + ADAM_WD * w_ref[...])

    blk = pl.BlockSpec((rb, cols), lambda i: (i, 0))
    return pl.pallas_call(
        body, grid=(rows // rb,),
        in_specs=[pl.BlockSpec((NDEV, rb, cols), lambda i: (0, i, 0)), blk, blk, blk],
        out_specs=[blk, blk, blk, blk], out_shape=[jax.ShapeDtypeStruct((rows, cols), F32)] * 4,
        compiler_params=_cp(("parallel",)), name=name)(recv, w, m, v)


SMALL_SHARDED = ("meta_tokens", "conv_dw_w", "dn_conv_w")
SMALL_REPLICATED = ("norm_mix_w", "conv_dw_b", "conv_ln_w", "conv_ln_b", "dn_A_log", "dn_dt_bias", "dn_norm_w",
                    "norm_ffn_w", "final_norm_w")
PARAM_ORDER = ("meta_tokens", "norm_mix_w", "w_in", "conv_dw_w", "conv_dw_b", "conv_ln_w", "conv_ln_b", "dn_conv_w",
               "dn_A_log", "dn_dt_bias", "dn_norm_w", "w_out", "norm_ffn_w", "ffn_w_gu", "ffn_w_down", "final_norm_w")


def _pack_small(parts, axis):
    flat = jnp.concatenate(parts, axis=axis)
    n = flat.shape[axis]
    total = -(-n // (8 * LANES)) * (8 * LANES)
    pad = [(0, 0)] * flat.ndim
    pad[axis] = (0, total - n)
    flat = jnp.pad(flat, pad)
    return flat.reshape(flat.shape[:axis] + (total // LANES, LANES))


def _unshard_last(g8):
    moved = jnp.moveaxis(g8, 0, -2)
    return moved.reshape(moved.shape[:-2] + (-1,))


def _per_destination_last(full):
    split = full.reshape(full.shape[:-1] + (NDEV, full.shape[-1] // NDEV))
    return jnp.moveaxis(split, -2, 0).reshape(NDEV, -1)


def _lane_row(vec4, width):
    return jnp.pad(vec4, (NH, width - 2 * NH))[None]


def kernel(x, meta_tokens, norm_mix_w, w_in, conv_dw_w, conv_dw_b, conv_ln_w, conv_ln_b, dn_conv_w, dn_A_log, dn_dt_bias, dn_norm_w, w_out, norm_ffn_w, ffn_w_gu, ffn_w_down, final_norm_w, loss_target, m_meta_tokens, m_norm_mix_w, m_w_in, m_conv_dw_w, m_conv_dw_b, m_conv_ln_w, m_conv_ln_b, m_dn_conv_w, m_dn_A_log, m_dn_dt_bias, m_dn_norm_w, m_w_out, m_norm_ffn_w, m_ffn_w_gu, m_ffn_w_down, m_final_norm_w, v_meta_tokens, v_norm_mix_w, v_w_in, v_conv_dw_w, v_conv_dw_b, v_conv_ln_w, v_conv_ln_b, v_dn_conv_w, v_dn_A_log, v_dn_dt_bias, v_dn_norm_w, v_w_out, v_norm_ffn_w, v_ffn_w_gu, v_ffn_w_down, v_final_norm_w):
    weights = dict(meta_tokens=meta_tokens, norm_mix_w=norm_mix_w, w_in=w_in, conv_dw_w=conv_dw_w, conv_dw_b=conv_dw_b,
                   conv_ln_w=conv_ln_w, conv_ln_b=conv_ln_b, dn_conv_w=dn_conv_w, dn_A_log=dn_A_log,
                   dn_dt_bias=dn_dt_bias, dn_norm_w=dn_norm_w, w_out=w_out, norm_ffn_w=norm_ffn_w, ffn_w_gu=ffn_w_gu,
                   ffn_w_down=ffn_w_down, final_norm_w=final_norm_w)
    m_in = dict(meta_tokens=m_meta_tokens, norm_mix_w=m_norm_mix_w, w_in=m_w_in, conv_dw_w=m_conv_dw_w,
                conv_dw_b=m_conv_dw_b, conv_ln_w=m_conv_ln_w, conv_ln_b=m_conv_ln_b, dn_conv_w=m_dn_conv_w,
                dn_A_log=m_dn_A_log, dn_dt_bias=m_dn_dt_bias, dn_norm_w=m_dn_norm_w, w_out=m_w_out,
                norm_ffn_w=m_norm_ffn_w, ffn_w_gu=m_ffn_w_gu, ffn_w_down=m_ffn_w_down, final_norm_w=m_final_norm_w)
    v_in = dict(meta_tokens=v_meta_tokens, norm_mix_w=v_norm_mix_w, w_in=v_w_in, conv_dw_w=v_conv_dw_w,
                conv_dw_b=v_conv_dw_b, conv_ln_w=v_conv_ln_w, conv_ln_b=v_conv_ln_b, dn_conv_w=v_dn_conv_w,
                dn_A_log=v_dn_A_log, dn_dt_bias=v_dn_dt_bias, dn_norm_w=v_dn_norm_w, w_out=v_w_out,
                norm_ffn_w=v_norm_ffn_w, ffn_w_gu=v_ffn_w_gu, ffn_w_down=v_ffn_w_down, final_norm_w=v_final_norm_w)

    depth = w_in.shape[0]
    seq = x.shape[1]
    t = _padded_rows(seq)
    rows_d = depth * D
    win_w, gu_w = w_in.shape[2], ffn_w_gu.shape[2]
    win_wp, gu_wp = _lane_pad(win_w), _lane_pad(gu_w)

    def pad_cols(a, wp):
        return jnp.pad(a, ((0, 0), (0, 0), (0, wp - a.shape[2]))).reshape(rows_d, wp)

    def rows2d(a):
        return a.reshape(-1, a.shape[2])

    small_shards = [weights[n] for n in SMALL_SHARDED]
    win_p = pad_cols(w_in, win_wp).astype(MXU_DTYPE).reshape(depth, D, win_wp)
    gu_p = pad_cols(ffn_w_gu, gu_wp).astype(MXU_DTYPE).reshape(depth, D, gu_wp)
    wout_b, wdown_b = w_out.astype(MXU_DTYPE), ffn_w_down.astype(MXU_DTYPE)
    layer_shards = lambda l: [win_p[l], gu_p[l], wout_b[l], wdown_b[l]]

    def whole_weights(got):
        g_win, g_gu, g_wout, g_down = got
        main, gate_cols = unshard_cols(g_win, win_w, [PROJ_MAIN, GATE_W], "unshard_w_in")
        wg, wu = unshard_cols(g_gu, gu_w, [DFF, DFF], "unshard_w_gu")
        return dict(main=main, gate=gate_cols, wg=wg, wu=wu, out=g_wout.reshape(D, D), down=g_down.reshape(DFF, D))

    *got0, g_small = _push_to_all(layer_shards(0) + [_pack_small([s.reshape(-1) for s in small_shards], 0)],
                                  "gather_first", scatter=False)
    wts = [whole_weights(got0)]
    small_flat, off, small_full = g_small.reshape(NDEV, -1), 0, {}
    for n, s in zip(SMALL_SHARDED, small_shards):
        small_full[n] = _unshard_last(small_flat[:, off:off + s.size].reshape((NDEV,) + s.shape))
        off += s.size
    cdw32 = jnp.pad(small_full["conv_dw_w"], ((0, 0), (0, CONV_HALO - CONV_W), (0, 0)))
    dcw8 = jnp.pad(small_full["dn_conv_w"], ((0, 0), (0, DN_HALO - DN_W), (0, 0)))

    h = jnp.concatenate([jnp.zeros((FRONT, D), F32), small_full["meta_tokens"], x[0],
                         jnp.zeros((t - HEAD - seq, D), F32)], axis=0)
    tgt = jnp.pad(loss_target[0], ((HEAD, t - HEAD - seq), (0, 0)))

    saved = []
    for l in range(depth):
        nmw, nfw = norm_mix_w[l][None], norm_ffn_w[l][None]
        cdb, clw, clb = conv_dw_b[l][None], conv_ln_w[l][None], conv_ln_b[l][None]
        alog, dtb, dnw = _lane_row(dn_A_log[l], GATE_W), _lane_row(dn_dt_bias[l], GATE_W), dn_norm_w[l][None]
        wl = wts[l]
        hn = rms_fwd(h, nmw, "rms_mix_fwd")
        proj = mm(hn, wl["main"], name="mm_proj")
        pg = mm(hn, wl["gate"], name="mm_proj_gate")
        ybuf, u1 = conv_fwd(proj, cdw32[l], cdb, clw, clb, seq, "conv_fwd")
        qkv = dn_pre_fwd(proj, dcw8[l], "dn_pre_fwd")
        bg = gate_fwd(pg, alog, dtb, seq, "gate_fwd")
        if l + 1 < depth:
            (o, s_hist, m_hist, u_all, w_all), got = delta_fwd(qkv, bg, "delta_fwd_gather", push=layer_shards(l + 1))
            wts.append(whole_weights(got))
        else:
            (o, s_hist, m_hist, u_all, w_all), _ = delta_fwd(qkv, bg, "delta_fwd")
        ybuf = dn_post_fwd(ybuf, o, proj, dnw, "dn_post_fwd")
        h_mid = mm(ybuf, wl["out"], res=h, name="mm_out")
        hn2 = rms_fwd(h_mid, nfw, "rms_ffn_fwd")
        gate = mm(hn2, wl["wg"], name="mm_gate")
        up = mm(hn2, wl["wu"], name="mm_up")
        act = swiglu_fwd(gate, up, "swiglu_fwd")
        h_out = mm(act, wl["down"], res=h_mid, name="mm_down")
        saved.append(dict(h=h, hn=hn, proj=proj, pg=pg, ybuf=ybuf, u1=u1, qkv=qkv, bg=bg, o=o, s_hist=s_hist,
                          m_hist=m_hist, u_all=u_all, w_all=w_all, h_mid=h_mid, hn2=hn2, gate=gate, up=up, act=act,
                          nmw=nmw, nfw=nfw, clw=clw, clb=clb, alog=alog, dtb=dtb, dnw=dnw))
        h = h_out

    dh, loss_part, d_final = loss_bwd(h, tgt, final_norm_w[None], seq, "loss_bwd")
    loss = lax.psum(loss_part[0, 0], MESH_AXES)

    per_layer = ("norm_mix_w", "conv_dw_w", "conv_dw_b", "conv_ln_w", "conv_ln_b", "dn_conv_w", "dn_A_log", "dn_dt_bias",
                 "dn_norm_w", "norm_ffn_w")
    grads = {n: [None] * depth for n in per_layer}
    received = [None] * depth
    pending = None
    dw_mm = lambda a, b, name: mm(a, b, ta=True, out_dtype=GRAD_DTYPE, name=name)
    for l in reversed(range(depth)):
        s, wl = saved[l], wts[l]
        dact = mm(dh, wl["down"], tb=True, name="mm_down_dx")
        d_down = dw_mm(s["act"], dh, "mm_down_dw")
        dgate, dup = swiglu_bwd(s["gate"], s["up"], dact, "swiglu_bwd")
        dhn2 = mm(dup, wl["wu"], tb=True, res=mm(dgate, wl["wg"], tb=True, name="mm_gate_dx"), name="mm_up_dx")
        d_wg = dw_mm(s["hn2"], dgate, "mm_gate_dw")
        d_wu = dw_mm(s["hn2"], dup, "mm_up_dw")
        dh_mid, dnfw = rms_bwd(dhn2, s["h_mid"], s["nfw"], dh, "rms_ffn_bwd")
        dy = mm(dh_mid, wl["out"], tb=True, name="mm_out_dx")
        d_out = dw_mm(s["ybuf"], dh_mid, "mm_out_dw")
        dproj, dcdw, dcdb, dclw, dclb = conv_bwd(dy, s["u1"], s["proj"], cdw32[l], s["clw"], s["clb"], seq, "conv_bwd")
        do, dproj, ddnw = dn_post_bwd(dproj, dy, s["o"], s["proj"], s["dnw"], "dn_post_bwd")
        delta_args = (s["qkv"], s["bg"], do, s["s_hist"], s["m_hist"], s["u_all"], s["w_all"])
        if pending is None:
            dqkv, dbg, _ = delta_bwd(*delta_args, "delta_bwd")
        else:
            dqkv, dbg, received[l + 1] = delta_bwd(*delta_args, "delta_bwd_exchange", push=pending)
        dproj, ddcw = dn_pre_bwd(dproj, dqkv, s["proj"], dcw8[l], "dn_pre_bwd")
        dpg, dalog, ddtb = gate_bwd(dbg, s["pg"], s["alog"], s["dtb"], seq, "gate_bwd")
        dhn_gate = mm(dpg, wl["gate"], tb=True, name="mm_proj_gate_dx")
        dhn = mm(dproj, wl["main"], tb=True, res=dhn_gate, name="mm_proj_dx")
        d_main = dw_mm(s["hn"], dproj, "mm_proj_dw")
        d_gate_cols = dw_mm(s["hn"], dpg, "mm_proj_gate_dw")
        pending = [shard_cols([d_main, d_gate_cols], win_w, "shard_w_in"), shard_cols([d_wg, d_wu], gu_w, "shard_w_gu"),
                   d_out.reshape(NDEV, D // NDEV, D), d_down.reshape(NDEV, DFF // NDEV, D)]
        dh, dnmw = rms_bwd(dhn, s["h"], s["nmw"], dh_mid, "rms_mix_bwd")
        grads["norm_mix_w"][l] = dnmw[0]
        grads["norm_ffn_w"][l] = dnfw[0]
        grads["conv_dw_w"][l] = dcdw[:CONV_W]
        grads["conv_dw_b"][l] = dcdb[0]
        grads["conv_ln_w"][l] = dclw[0]
        grads["conv_ln_b"][l] = dclb[0]
        grads["dn_conv_w"][l] = ddcw[:DN_W]
        grads["dn_A_log"][l] = dalog[0, NH:2 * NH]
        grads["dn_dt_bias"][l] = ddtb[0, NH:2 * NH]
        grads["dn_norm_w"][l] = ddnw[0]

    grad_x = dh[HEAD:HEAD + seq][None]
    full = {n: jnp.stack(g) for n, g in grads.items()}
    full["meta_tokens"] = dh[FRONT:HEAD]
    full["final_norm_w"] = d_final[0]

    send_small = _pack_small(
        [_per_destination_last(full[n]) for n in SMALL_SHARDED]
        + [jnp.broadcast_to(full[n].reshape(1, -1), (NDEV, full[n].size)) for n in SMALL_REPLICATED], 1)
    *received[0], r_small = _push_to_all(pending + [send_small], "exchange_last", scatter=True)
    r_win, r_gu, r_wout, r_down = (jnp.concatenate([received[l][k] for l in range(depth)], axis=1) for k in range(4))

    small_names = SMALL_SHARDED + SMALL_REPLICATED
    pack_local = lambda tree: _pack_small([tree[n].reshape(-1) for n in small_names], 0)
    results = {}

    def run_adamw(name, recv, prep, finish):
        outs = adamw(recv, prep(weights[name]), prep(m_in[name]), prep(v_in[name]), "adamw_" + name)
        results[name] = [finish(o) for o in outs]

    run_adamw("w_in", r_win, lambda a: pad_cols(a, win_wp),
              lambda o: o[:, :win_w].reshape(depth, D, win_w))
    run_adamw("ffn_w_gu", r_gu, lambda a: pad_cols(a, gu_wp), lambda o: o[:, :gu_w].reshape(depth, D, gu_w))
    run_adamw("w_out", r_wout, rows2d, lambda o: o.reshape(w_out.shape))
    run_adamw("ffn_w_down", r_down, rows2d, lambda o: o.reshape(ffn_w_down.shape))
    small_outs = adamw(r_small, pack_local(weights), pack_local(m_in), pack_local(v_in), "adamw_small")
    for kind in range(4):
        flat, off = small_outs[kind].reshape(-1), 0
        for n in small_names:
            wgt = weights[n]
            results.setdefault(n, [None] * 4)[kind] = flat[off:off + wgt.size].reshape(wgt.shape)
            off += wgt.size

    return (loss, grad_x, *[results[n][0] for n in PARAM_ORDER], *[results[n][1] for n in PARAM_ORDER],
            *[results[n][2] for n in PARAM_ORDER], *[results[n][3] for n in PARAM_ORDER])
```

```python
import jax
import jax.numpy as jnp
from jax import lax
from jax.experimental import pallas as pl
from jax.experimental.pallas import tpu as pltpu

F32 = jnp.float32
MXU_DTYPE = jnp.bfloat16

D = 1024
N_META = 16
CHUNK = 64
CHUNK_LOG2 = 6
INV_BASE_LOG2 = 3
FRONT = CHUNK - N_META
HEAD = CHUNK
CONV_CH = 512
CONV_W = 31
CONV_HALO = 32
NH = 4
DH = 128
DQ = NH * DH
DN_W = 4
DN_HALO = 8
DFF = 2816
PROJ_MAIN = 3072
D_IN = 3080
GATE_W = 128
LANES = 128
NDEV = 8
NORM_EPS = 1e-6
LN_EPS = 1e-5
L2_EPS = 1e-6
VMEM_LIMIT_V7X = 48 * 1024 * 1024
ROW_TILE = 640
ROW_TILE_SMALL = 128
MM_TILES = (1408, 1280, 1024, 640, 512, 256, 128)
GRAD_DTYPE = jnp.bfloat16
DELTA_CHUNKS = (4, 2, 1)
ADAM_BLOCK_BYTES = 8 * 1024 * 1024

ADAM_LR = 0.001
ADAM_B1 = 0.9
ADAM_B2 = 0.999
ADAM_EPS = 1e-08
ADAM_WD = 0.01
ADAM_STEP = 10

MESH_AXES = ("x", "y", "c")
NN = ((1,), (0,))
NT = ((1,), (1,))
TN = ((0,), (0,))

assert 1 << CHUNK_LOG2 == CHUNK


def _row_tile(t):
    return ROW_TILE if t % ROW_TILE == 0 else ROW_TILE_SMALL


def _padded_rows(seq):
    n = HEAD + seq
    tm = ROW_TILE if n >= 4 * ROW_TILE else ROW_TILE_SMALL
    return -(-n // tm) * tm


def _pick(n, prefs):
    for p in prefs:
        if n % p == 0:
            return p
    return n


def _lane_pad(n):
    return -(-n // LANES) * LANES


def _cp(sem):
    return pltpu.CompilerParams(dimension_semantics=sem, vmem_limit_bytes=VMEM_LIMIT_V7X)


def _sigmoid(x):
    return 1.0 / (1.0 + jnp.exp(-x))


def _softplus(x):
    return jnp.maximum(x, 0.0) + jnp.log(1.0 + jnp.exp(-jnp.abs(x)))


def _valid_rows(i, tm, seq, width, first=FRONT):
    rows = i * tm + lax.broadcasted_iota(jnp.int32, (tm, width), 0)
    return jnp.logical_and(rows >= first, rows < HEAD + seq)


def _dot(a, b, dims):
    return lax.dot_general(a, b, (dims, ((), ())), preferred_element_type=F32)


def _split(x, n):
    out, r = [], x
    for _ in range(n):
        p = r.astype(MXU_DTYPE)
        out.append(p)
        r = r - p.astype(F32)
    return out


def _mm1(a, b, dims):
    return _dot(a.astype(MXU_DTYPE), b.astype(MXU_DTYPE), dims)


def _mm1_many(a_list, b_list, dims):
    return [_mm1(a, b, dims) for a, b in zip(a_list, b_list)]


def _mm3_many(a_list, b_list, dims):
    sa = [_split(a, 2) for a in a_list]
    sb = [_split(b, 2) for b in b_list]
    hh = [_dot(x[0], y[0], dims) for x, y in zip(sa, sb)]
    hl = [_dot(x[0], y[1], dims) for x, y in zip(sa, sb)]
    lh = [_dot(x[1], y[0], dims) for x, y in zip(sa, sb)]
    return [p + (q + r) for p, q, r in zip(hh, hl, lh)]


def _mmx(e, b, dims):
    e = e.astype(MXU_DTYPE)
    b1, b2, b3 = _split(b, 3)
    return _dot(e, b1, dims) + (_dot(e, b2, dims) + _dot(e, b3, dims))


def mm(a, b, *, ta=False, tb=False, a_fn=None, a_vecs=(), extras=(), out_fn=None, out_dtypes=(F32,), tile_cap=None,
       name):
    a_list = list(a) if isinstance(a, (list, tuple)) else [a]
    (k_dim, m_dim) = a_list[0].shape if ta else a_list[0].shape[::-1]
    n_dim = b.shape[0] if tb else b.shape[1]
    assert (b.shape[1] if tb else b.shape[0]) == k_dim
    capped = MM_TILES if tile_cap is None else tuple(p for p in MM_TILES if p <= tile_cap)
    tm = _pick(m_dim, MM_TILES if ta else capped)
    tk = _pick(k_dim, capped if ta else MM_TILES)
    tn = _pick(n_dim, MM_TILES)
    nk = k_dim // tk
    na, nv, ne, no = len(a_list), len(a_vecs), len(extras), len(out_dtypes)
    dims = ((0,) if ta else (1,), (1,) if tb else (0,))

    def body(*refs):
        a_refs, v_refs, b_ref = refs[:na], refs[na:na + nv], refs[na + nv]
        e_refs = refs[na + nv + 1:na + nv + 1 + ne]
        o_refs = refs[na + nv + 1 + ne:na + nv + 1 + ne + no]

        def left():
            tiles = [r[...] for r in a_refs]
            return tiles[0] if a_fn is None else a_fn(*tiles, *[v[...] for v in v_refs])

        def finish(acc):
            outs = (acc,) if out_fn is None else out_fn(acc, *[e[...] for e in e_refs])
            for o_ref, out in zip(o_refs, outs):
                o_ref[...] = out.astype(o_ref.dtype)

        if nk == 1:
            finish(_mm1(left(), b_ref[...], dims))
            return
        acc_ref = refs[-1]
        k = pl.program_id(2)

        @pl.when(k == 0)
        def _():
            acc_ref[...] = jnp.zeros_like(acc_ref)

        acc_ref[...] += _mm1(left(), b_ref[...], dims)

        @pl.when(k == nk - 1)
        def _():
            finish(acc_ref[...])

    if ta:
        a_spec = pl.BlockSpec((tk, tm), lambda i, j, k: (k, i))
        v_spec = pl.BlockSpec((1, tm), lambda i, j, k: (0, i))
    else:
        a_spec = pl.BlockSpec((tm, tk), lambda i, j, k: (i, k))
        v_spec = pl.BlockSpec((1, tk), lambda i, j, k: (0, k))
    b_spec = pl.BlockSpec((tn, tk), lambda i, j, k: (j, k)) if tb else pl.BlockSpec((tk, tn), lambda i, j, k: (k, j))
    o_spec = pl.BlockSpec((tm, tn), lambda i, j, k: (i, j))
    outs = pl.pallas_call(
        body, grid=(m_dim // tm, n_dim // tn, nk),
        in_specs=[a_spec] * na + [v_spec] * nv + [b_spec] + [o_spec] * ne, out_specs=[o_spec] * no,
        out_shape=[jax.ShapeDtypeStruct((m_dim, n_dim), dt) for dt in out_dtypes],
        scratch_shapes=[pltpu.VMEM((tm, tn), F32)] if nk > 1 else [],
        compiler_params=_cp(("parallel", "parallel", "arbitrary")), name=name)(*a_list, *a_vecs, b, *extras)
    return outs[0] if no == 1 else outs


def _rms_apply(x, w):
    assert x.shape[-1] == D
    return x * lax.rsqrt(jnp.mean(x * x, axis=-1, keepdims=True) + NORM_EPS) * w


def _swiglu(g, u):
    return g * _sigmoid(g) * u


def _swiglu_bwd(dact, g, u):
    sg = _sigmoid(g)
    return dact * u * (sg * (1.0 + g * (1.0 - sg))), dact * (g * sg)


def _add(acc, r):
    return (acc + r,)


def rms_bwd(dy, h, w, dres, name):
    t = h.shape[0]
    tm = _row_tile(t)

    def body(dy_ref, h_ref, w_ref, dres_ref, dh_ref, dw_ref):
        i = pl.program_id(0)
        x = h_ref[...]
        r = lax.rsqrt(jnp.mean(x * x, axis=-1, keepdims=True) + NORM_EPS)
        xh = x * r
        g = dy_ref[...] * w_ref[...]
        dh_ref[...] = dres_ref[...] + r * (g - xh * jnp.mean(g * xh, axis=-1, keepdims=True))

        @pl.when(i == 0)
        def _():
            dw_ref[...] = jnp.zeros_like(dw_ref)

        dw_ref[...] += jnp.sum(dy_ref[...] * xh, axis=0, keepdims=True)

    row = pl.BlockSpec((tm, D), lambda i: (i, 0))
    vec = pl.BlockSpec((1, D), lambda i: (0, 0))
    return pl.pallas_call(
        body, grid=(t // tm,), in_specs=[row, row, vec, row], out_specs=[row, vec],
        out_shape=[jax.ShapeDtypeStruct((t, D), F32), jax.ShapeDtypeStruct((1, D), F32)],
        compiler_params=_cp(("arbitrary",)), name=name)(dy, h, w, dres)


def loss_bwd(h, tgt, w, seq, name):
    t = h.shape[0]
    tm = _row_tile(t)

    def body(h_ref, t_ref, w_ref, dh_ref, loss_ref, dw_ref):
        i = pl.program_id(0)
        x = h_ref[...]
        wv = w_ref[...]
        r = lax.rsqrt(jnp.mean(x * x, axis=-1, keepdims=True) + NORM_EPS)
        xh = x * r
        err = jnp.where(_valid_rows(i, tm, seq, D, HEAD), xh * wv - t_ref[...], 0.0)
        dy = err * (1.0 / D)
        g = dy * wv
        dh_ref[...] = r * (g - xh * jnp.mean(g * xh, axis=-1, keepdims=True))

        @pl.when(i == 0)
        def _():
            dw_ref[...] = jnp.zeros_like(dw_ref)
            loss_ref[...] = jnp.zeros_like(loss_ref)

        dw_ref[...] += jnp.sum(dy * xh, axis=0, keepdims=True)
        part = jnp.sum(jnp.sum(err * err, axis=1, keepdims=True), axis=0, keepdims=True) * (0.5 / D)
        loss_ref[...] += jnp.broadcast_to(part, loss_ref.shape)

    row = pl.BlockSpec((tm, D), lambda i: (i, 0))
    vec = pl.BlockSpec((1, D), lambda i: (0, 0))
    return pl.pallas_call(
        body, grid=(t // tm,), in_specs=[row, row, vec],
        out_specs=[row, pl.BlockSpec((1, LANES), lambda i: (0, 0)), vec],
        out_shape=[jax.ShapeDtypeStruct((t, D), F32), jax.ShapeDtypeStruct((1, LANES), F32),
                   jax.ShapeDtypeStruct((1, D), F32)],
        compiler_params=_cp(("arbitrary",)), name=name)(h, tgt, w)


def _layernorm_parts(u1):
    mu = jnp.mean(u1, axis=-1, keepdims=True)
    xc = u1 - mu
    rstd = lax.rsqrt(jnp.mean(xc * xc, axis=-1, keepdims=True) + LN_EPS)
    return xc * rstd, rstd


SUBLANES = 8


def _shift_copies(ext, sh, tm):
    for s in range(1, SUBLANES):
        sh[s - 1, :, :] = ext[pl.ds(s, tm + CONV_HALO - SUBLANES), :]


def _window(ext, sh, off, tm):
    s, m = off % SUBLANES, off // SUBLANES
    if s == 0:
        return ext[pl.ds(off, tm), :]
    return sh[s - 1, pl.ds(SUBLANES * m, tm), :]


def _shift_scratch(tm):
    return pltpu.VMEM((SUBLANES - 1, tm + CONV_HALO - SUBLANES, CONV_CH), F32)


def conv_fwd(proj, w32, b, lw, lb, seq, name):
    t = proj.shape[0]
    tm = _row_tile(t)

    def body(cv_ref, cg_ref, w_ref, b_ref, lw_ref, lb_ref, y_ref, u1_ref, ext, sh):
        i = pl.program_id(0)

        @pl.when(i == 0)
        def _():
            ext[0:CONV_HALO, :] = jnp.zeros((CONV_HALO, CONV_CH), F32)

        @pl.when(i > 0)
        def _():
            ext[0:CONV_HALO, :] = ext[tm:tm + CONV_HALO, :]

        ext[CONV_HALO:CONV_HALO + tm, :] = cv_ref[...] * _sigmoid(cg_ref[...])
        _shift_copies(ext, sh, tm)
        acc = jnp.broadcast_to(b_ref[...], (tm, CONV_CH))
        for j in range(CONV_W):
            acc = acc + w_ref[j:j + 1, :] * _window(ext, sh, CONV_HALO - (CONV_W - 1) + j, tm)
        u1_ref[...] = acc
        xh, _ = _layernorm_parts(acc)
        ln = xh * lw_ref[...] + lb_ref[...]
        y = ln * _sigmoid(ln)
        y_ref[...] = jnp.where(_valid_rows(i, tm, seq, CONV_CH), y, 0.0).astype(y_ref.dtype)

    half = lambda c: pl.BlockSpec((tm, CONV_CH), lambda i: (i, c))
    vec = pl.BlockSpec((1, CONV_CH), lambda i: (0, 0))
    return pl.pallas_call(
        body, grid=(t // tm,),
        in_specs=[half(0), half(1), pl.BlockSpec((CONV_HALO, CONV_CH), lambda i: (0, 0)), vec, vec, vec],
        out_specs=[half(0), half(0)],
        out_shape=[jax.ShapeDtypeStruct((t, D), MXU_DTYPE), jax.ShapeDtypeStruct((t, CONV_CH), F32)],
        scratch_shapes=[pltpu.VMEM((tm + CONV_HALO, CONV_CH), F32), _shift_scratch(tm)],
        compiler_params=_cp(("arbitrary",)), name=name)(proj, proj, w32, b, lw, lb)


def conv_bwd(dy, u1, proj, w32, lw, lb, seq, name):
    t = proj.shape[0]
    tm = _row_tile(t)
    nt = t // tm
    per = tm // CONV_HALO

    def body(dy_ref, u1_ref, cv_ref, cg_ref, cvp_ref, cgp_ref, w_ref, lw_ref, lb_ref,
             dp_ref, dw_ref, db_ref, dlw_ref, dlb_ref, ext_d, ext_u, sh_d, sh_u):
        i = pl.program_id(0)
        tile = nt - 1 - i

        @pl.when(i == 0)
        def _():
            ext_d[tm:tm + CONV_HALO, :] = jnp.zeros((CONV_HALO, CONV_CH), F32)
            dw_ref[...] = jnp.zeros_like(dw_ref)
            db_ref[...] = jnp.zeros_like(db_ref)
            dlw_ref[...] = jnp.zeros_like(dlw_ref)
            dlb_ref[...] = jnp.zeros_like(dlb_ref)

        @pl.when(i > 0)
        def _():
            ext_d[tm:tm + CONV_HALO, :] = ext_d[0:CONV_HALO, :]

        xh, rstd = _layernorm_parts(u1_ref[...])
        lwv = lw_ref[...]
        ln = xh * lwv + lb_ref[...]
        sg = _sigmoid(ln)
        dln = jnp.where(_valid_rows(tile, tm, seq, CONV_CH), dy_ref[...], 0.0) * (sg * (1.0 + ln * (1.0 - sg)))
        dlw_ref[...] += jnp.sum(dln * xh, axis=0, keepdims=True)
        dlb_ref[...] += jnp.sum(dln, axis=0, keepdims=True)
        dxh = dln * lwv
        du1 = rstd * (dxh - jnp.mean(dxh, axis=-1, keepdims=True)
                      - xh * jnp.mean(dxh * xh, axis=-1, keepdims=True))
        db_ref[...] += jnp.sum(du1, axis=0, keepdims=True)
        ext_d[0:tm, :] = du1

        cv = cv_ref[...]
        sgc = _sigmoid(cg_ref[...])
        prev = cvp_ref[...] * _sigmoid(cgp_ref[...])
        ext_u[0:CONV_HALO, :] = jnp.where(tile > 0, prev, 0.0)
        ext_u[CONV_HALO:CONV_HALO + tm, :] = cv * sgc

        _shift_copies(ext_d, sh_d, tm)
        _shift_copies(ext_u, sh_u, tm)
        du0 = jnp.zeros((tm, CONV_CH), F32)
        for j in range(CONV_W):
            du0 = du0 + w_ref[j:j + 1, :] * _window(ext_d, sh_d, CONV_W - 1 - j, tm)
            dw_ref[j:j + 1, :] += jnp.sum(
                du1 * _window(ext_u, sh_u, CONV_HALO - (CONV_W - 1) + j, tm), axis=0, keepdims=True)
        dp_ref[:, 0:CONV_CH] = (du0 * sgc).astype(dp_ref.dtype)
        dp_ref[:, CONV_CH:2 * CONV_CH] = (du0 * cv * sgc * (1.0 - sgc)).astype(dp_ref.dtype)

    rev = lambda c: pl.BlockSpec((tm, CONV_CH), lambda i: (nt - 1 - i, c))
    prev = lambda c: pl.BlockSpec((CONV_HALO, CONV_CH), lambda i: (jnp.maximum((nt - 1 - i) * per - 1, 0), c))
    vec = pl.BlockSpec((1, CONV_CH), lambda i: (0, 0))
    wspec = pl.BlockSpec((CONV_HALO, CONV_CH), lambda i: (0, 0))
    return pl.pallas_call(
        body, grid=(nt,),
        in_specs=[rev(0), rev(0), rev(0), rev(1), prev(0), prev(1), wspec, vec, vec],
        out_specs=[pl.BlockSpec((tm, 2 * CONV_CH), lambda i: (nt - 1 - i, 0)), wspec, vec, vec, vec],
        out_shape=[jax.ShapeDtypeStruct((t, PROJ_MAIN), MXU_DTYPE), jax.ShapeDtypeStruct((CONV_HALO, CONV_CH), F32),
                   jax.ShapeDtypeStruct((1, CONV_CH), F32), jax.ShapeDtypeStruct((1, CONV_CH), F32),
                   jax.ShapeDtypeStruct((1, CONV_CH), F32)],
        scratch_shapes=[pltpu.VMEM((tm + CONV_HALO, CONV_CH), F32), pltpu.VMEM((tm + CONV_HALO, CONV_CH), F32),
                        _shift_scratch(tm), _shift_scratch(tm)],
        compiler_params=_cp(("arbitrary",)), name=name)(dy, u1, proj, proj, proj, proj, w32, lw, lb)


def dn_pre_fwd(proj, w8, name):
    t = proj.shape[0]
    tm = _row_tile(t)

    def body(raw_ref, w_ref, o_ref, ext):
        g = pl.program_id(0)
        i = pl.program_id(1)

        @pl.when(i == 0)
        def _():
            ext[0:DN_HALO, :] = jnp.zeros((DN_HALO, DQ), F32)

        @pl.when(i > 0)
        def _():
            ext[0:DN_HALO, :] = ext[tm:tm + DN_HALO, :]

        ext[DN_HALO:DN_HALO + tm, :] = raw_ref[...]
        c = jnp.zeros((tm, DQ), F32)
        for j in range(DN_W):
            c = c + w_ref[j:j + 1, :] * ext[pl.ds(DN_HALO - (DN_W - 1) + j, tm), :]
        s = c * _sigmoid(c)
        scale = jnp.where(g == 0, DH ** -0.5, 1.0)
        for h in range(NH):
            sh = s[:, h * DH:(h + 1) * DH]
            r = lax.rsqrt(jnp.sum(sh * sh, axis=-1, keepdims=True) + L2_EPS)
            o_ref[:, h * DH:(h + 1) * DH] = jnp.where(g == 2, sh, sh * (r * scale))

    return pl.pallas_call(
        body, grid=(3, t // tm),
        in_specs=[pl.BlockSpec((tm, DQ), lambda g, i: (i, 2 + g)), pl.BlockSpec((DN_HALO, DQ), lambda g, i: (0, g))],
        out_specs=pl.BlockSpec((tm, DQ), lambda g, i: (i, g)),
        out_shape=jax.ShapeDtypeStruct((t, 3 * DQ), F32),
        scratch_shapes=[pltpu.VMEM((tm + DN_HALO, DQ), F32)],
        compiler_params=_cp(("arbitrary", "arbitrary")), name=name)(proj, w8)


def dn_pre_bwd(dproj, dqkv, proj, w8, name):
    t = proj.shape[0]
    tm = _row_tile(t)
    nt = t // tm
    per = tm // DN_HALO

    def body(dp_in, d_ref, raw_ref, rawp_ref, w_ref, dp_ref, dw_ref, ext_d, ext_r):
        del dp_in
        g = pl.program_id(0)
        i = pl.program_id(1)
        tile = nt - 1 - i

        @pl.when(i == 0)
        def _():
            ext_d[tm:tm + DN_HALO, :] = jnp.zeros((DN_HALO, DQ), F32)
            dw_ref[...] = jnp.zeros_like(dw_ref)

        @pl.when(i > 0)
        def _():
            ext_d[tm:tm + DN_HALO, :] = ext_d[0:DN_HALO, :]

        ext_r[0:DN_HALO, :] = jnp.where(tile > 0, rawp_ref[...], 0.0)
        ext_r[DN_HALO:DN_HALO + tm, :] = raw_ref[...]
        c = jnp.zeros((tm, DQ), F32)
        for j in range(DN_W):
            c = c + w_ref[j:j + 1, :] * ext_r[pl.ds(DN_HALO - (DN_W - 1) + j, tm), :]
        sg = _sigmoid(c)
        s = c * sg
        scale = jnp.where(g == 0, DH ** -0.5, 1.0)
        for h in range(NH):
            sl = slice(h * DH, (h + 1) * DH)
            sh = s[:, sl]
            dn = d_ref[:, sl]
            r = lax.rsqrt(jnp.sum(sh * sh, axis=-1, keepdims=True) + L2_EPS)
            unit = sh * r
            dsn = (r * scale) * (dn - unit * jnp.sum(dn * unit, axis=-1, keepdims=True))
            ds = jnp.where(g == 2, dn, dsn)
            ext_d[0:tm, sl] = ds * (sg[:, sl] * (1.0 + c[:, sl] * (1.0 - sg[:, sl])))
        dc = ext_d[0:tm, :]
        draw = jnp.zeros((tm, DQ), F32)
        for j in range(DN_W):
            draw = draw + w_ref[j:j + 1, :] * ext_d[pl.ds(DN_W - 1 - j, tm), :]
            dw_ref[j:j + 1, :] += jnp.sum(
                dc * ext_r[pl.ds(DN_HALO - (DN_W - 1) + j, tm), :], axis=0, keepdims=True)
        dp_ref[...] = draw.astype(dp_ref.dtype)

    return pl.pallas_call(
        body, grid=(3, nt),
        in_specs=[pl.BlockSpec(memory_space=pl.ANY),
                  pl.BlockSpec((tm, DQ), lambda g, i: (nt - 1 - i, g)),
                  pl.BlockSpec((tm, DQ), lambda g, i: (nt - 1 - i, 2 + g)),
                  pl.BlockSpec((DN_HALO, DQ), lambda g, i: (jnp.maximum((nt - 1 - i) * per - 1, 0), 2 + g)),
                  pl.BlockSpec((DN_HALO, DQ), lambda g, i: (0, g))],
        out_specs=[pl.BlockSpec((tm, DQ), lambda g, i: (nt - 1 - i, 2 + g)),
                   pl.BlockSpec((DN_HALO, DQ), lambda g, i: (0, g))],
        out_shape=[jax.ShapeDtypeStruct(dproj.shape, dproj.dtype), jax.ShapeDtypeStruct((DN_HALO, 3 * DQ), F32)],
        scratch_shapes=[pltpu.VMEM((tm + DN_HALO, DQ), F32), pltpu.VMEM((tm + DN_HALO, DQ), F32)],
        input_output_aliases={0: 0},
        compiler_params=_cp(("arbitrary", "arbitrary")), name=name)(dproj, dqkv, proj, proj, w8)


def gate_fwd(pg, alog, dtb, seq, name):
    t = pg.shape[0]
    tm = _row_tile(t)

    def body(x_ref, al_ref, dt_ref, o_ref):
        i = pl.program_id(0)
        x = x_ref[...]
        lane = lax.broadcasted_iota(jnp.int32, (tm, GATE_W), 1)
        gg = -jnp.exp(al_ref[...]) * _softplus(x + dt_ref[...])
        out = jnp.where(lane < NH, _sigmoid(x), jnp.where(lane < 2 * NH, gg, 0.0))
        o_ref[...] = jnp.where(_valid_rows(i, tm, seq, GATE_W), out, 0.0)

    row = pl.BlockSpec((tm, GATE_W), lambda i: (i, 0))
    vec = pl.BlockSpec((1, GATE_W), lambda i: (0, 0))
    return pl.pallas_call(
        body, grid=(t // tm,), in_specs=[row, vec, vec], out_specs=row,
        out_shape=jax.ShapeDtypeStruct((t, GATE_W), F32), compiler_params=_cp(("parallel",)), name=name)(pg, alog, dtb)


def gate_bwd(dbg, pg, alog, dtb, seq, name):
    t = pg.shape[0]
    tm = _row_tile(t)

    def body(d_ref, x_ref, al_ref, dt_ref, o_ref, dal_ref, ddt_ref):
        i = pl.program_id(0)
        x = x_ref[...]
        lane = lax.broadcasted_iota(jnp.int32, (tm, GATE_W), 1)
        d = jnp.where(_valid_rows(i, tm, seq, GATE_W), d_ref[...], 0.0)
        beta = _sigmoid(x)
        xs = x + dt_ref[...]
        e = -jnp.exp(al_ref[...])
        is_g = jnp.logical_and(lane >= NH, lane < 2 * NH)
        da = jnp.where(is_g, d * e * _sigmoid(xs), 0.0)
        dgg = jnp.where(is_g, d * e * _softplus(xs), 0.0)
        o_ref[...] = jnp.where(lane < NH, d * beta * (1.0 - beta), da).astype(o_ref.dtype)

        @pl.when(i == 0)
        def _():
            dal_ref[...] = jnp.zeros_like(dal_ref)
            ddt_ref[...] = jnp.zeros_like(ddt_ref)

        dal_ref[...] += jnp.sum(dgg, axis=0, keepdims=True)
        ddt_ref[...] += jnp.sum(da, axis=0, keepdims=True)

    row = pl.BlockSpec((tm, GATE_W), lambda i: (i, 0))
    vec = pl.BlockSpec((1, GATE_W), lambda i: (0, 0))
    return pl.pallas_call(
        body, grid=(t // tm,), in_specs=[row, row, vec, vec], out_specs=[row, vec, vec],
        out_shape=[jax.ShapeDtypeStruct((t, GATE_W), MXU_DTYPE), jax.ShapeDtypeStruct((1, GATE_W), F32),
                   jax.ShapeDtypeStruct((1, GATE_W), F32)],
        compiler_params=_cp(("arbitrary",)), name=name)(dbg, pg, alog, dtb)


def _chunk_masks():
    ii = lax.broadcasted_iota(jnp.int32, (CHUNK, CHUNK), 0)
    jj = lax.broadcasted_iota(jnp.int32, (CHUNK, CHUNK), 1)
    return ii, jj, ii >= jj, ii > jj


def _lane_col(x, lane, idx):
    return jnp.sum(jnp.where(lane == idx, x, 0.0), axis=1, keepdims=True)


def _delta_terms(q, k, v, bgs, nch, low, strict):
    idx = [(c, h) for c in range(nch) for h in range(NH)]
    lane = lax.broadcasted_iota(jnp.int32, (CHUNK, GATE_W), 1)
    rowi = lax.broadcasted_iota(jnp.int32, (CHUNK, 1), 0)
    r4 = lax.broadcasted_iota(jnp.int32, (NH * CHUNK, GATE_W), 0)
    l4 = lax.broadcasted_iota(jnp.int32, (NH * CHUNK, GATE_W), 1)
    sel = jnp.where(l4 == NH + jnp.right_shift(r4, CHUNK_LOG2), 1.0, 0.0)
    lowf = jnp.where(low, 1.0, 0.0)
    gam_all = [_mmx(lowf, b, NN) for b in bgs]
    gam_rows = [_mmx(sel, g, NT) for g in gam_all]
    beta = [_lane_col(bgs[c], lane, h) for c, h in idx]
    gam = [_lane_col(gam_all[c], lane, NH + h) for c, h in idx]
    dm = [jnp.exp(jnp.where(low, g - gam_rows[c][h * CHUNK:(h + 1) * CHUNK, :], -1e30))
          for g, (c, h) in zip(gam, idx)]
    glast = [jnp.sum(jnp.where(rowi == CHUNK - 1, g, 0.0), axis=0, keepdims=True) for g in gam]
    eg = [jnp.exp(g) for g in gam]
    ekl = [jnp.exp(gl - g) for gl, g in zip(glast, gam)]
    gl = [jnp.exp(x) for x in glast]
    kb = [x * b for x, b in zip(k, beta)]
    vb = [x * b for x, b in zip(v, beta)]
    kbg = [x * e for x, e in zip(kb, eg)]
    kk = _mm1_many(kb, k, NT)
    qk = _mm1_many(q, k, NT)
    a_mat = [jnp.where(strict, x * d, 0.0) for x, d in zip(kk, dm)]
    p_mat = [jnp.where(low, x * d, 0.0) for x, d in zip(qk, dm)]
    qd = [x * e for x, e in zip(q, eg)]
    kd = [x * e for x, e in zip(k, ekl)]
    return dict(idx=idx, beta=beta, dm=dm, eg=eg, ekl=ekl, gl=gl, kb=kb, vb=vb, kbg=kbg, a=a_mat, p=p_mat, qd=qd,
                kd=kd, lane=lane, rowi=rowi)


def _unit_lower_inverses(a_list, ii, jj, eye):
    def same(log2):
        return jnp.right_shift(ii, log2) == jnp.right_shift(jj, log2)

    n = [-jnp.where(same(INV_BASE_LOG2), a, 0.0) for a in a_list]
    x = [eye + v for v in n]
    p = n
    for _ in range(INV_BASE_LOG2 - 1):
        p = _mm1_many(p, p, NN)
        x = [xi + y for xi, y in zip(x, _mm1_many(x, p, NN))]
    for log2 in range(INV_BASE_LOG2, CHUNK_LOG2):
        off = jnp.logical_and(same(log2 + 1), jnp.logical_not(same(log2)))
        a_off = [jnp.where(off, a, 0.0) for a in a_list]
        x = [xi - y for xi, y in zip(x, _mm1_many(x, _mm1_many(a_off, x, NN), NN))]
    return x


def _transposes(xs, eye):
    e = eye.astype(MXU_DTYPE)
    parts = [_split(x, 2) for x in xs]
    return [_dot(p[0], e, TN) + _dot(p[1], e, TN) for p in parts]


def _load_heads(ref, nch):
    return [ref[c * CHUNK:(c + 1) * CHUNK, h * DH:(h + 1) * DH] for c in range(nch) for h in range(NH)]


def delta_fwd(qkv, bg, name, push=None):
    t = qkv.shape[0]
    nc = t // CHUNK
    nch = _pick(nc, DELTA_CHUNKS)
    rows = nch * CHUNK
    ng = nc // nch
    npush = 0 if push is None else len(push)

    def body(*refs):
        q_ref, k_ref, v_ref, bg_ref = refs[:4]
        x_refs = refs[4:4 + npush]
        o_ref, sh_ref, mi_ref, u_ref, w_ref = refs[4 + npush:9 + npush]
        got_refs = refs[9 + npush:9 + 2 * npush]
        s_ref = refs[9 + 2 * npush]
        n = pl.program_id(0)
        if npush:
            copies = _push_copies(x_refs, got_refs, *refs[10 + 2 * npush:], scatter=False)

            @pl.when(n == 0)
            def _():
                _push_start(copies)

        @pl.when(n == 0)
        def _():
            s_ref[...] = jnp.zeros_like(s_ref)

        ii, jj, low, strict = _chunk_masks()
        eye = jnp.where(ii == jj, 1.0, 0.0)
        q, k, v = _load_heads(q_ref, nch), _load_heads(k_ref, nch), _load_heads(v_ref, nch)
        bgs = [bg_ref[c * CHUNK:(c + 1) * CHUNK, :] for c in range(nch)]
        tm_ = _delta_terms(q, k, v, bgs, nch, low, strict)
        m_inv = _unit_lower_inverses(tm_["a"], ii, jj, eye)
        u = _mm3_many(m_inv, tm_["vb"], NN)
        w = _mm3_many(m_inv, tm_["kbg"], NN)
        for i, (c, h) in enumerate(tm_["idx"]):
            mi_ref[c, h] = m_inv[i]
            u_ref[c * CHUNK:(c + 1) * CHUNK, h * DH:(h + 1) * DH] = u[i]
            w_ref[c * CHUNK:(c + 1) * CHUNK, h * DH:(h + 1) * DH] = w[i]
        s = [s_ref[h] for h in range(NH)]
        for c in range(nch):
            pr = range(c * NH, (c + 1) * NH)
            ws = [_mm1(w[i], s[i - c * NH], NN) for i in pr]
            qs = [_mm1(tm_["qd"][i], s[i - c * NH], NN) for i in pr]
            vn = [u[i] - x for i, x in zip(pr, ws)]
            pv = [_mm1(tm_["p"][i], x, NN) for i, x in zip(pr, vn)]
            kv = [_mm1(tm_["kd"][i], x, TN) for i, x in zip(pr, vn)]
            for h in range(NH):
                o_ref[c * CHUNK:(c + 1) * CHUNK, h * DH:(h + 1) * DH] = qs[h] + pv[h]
                sh_ref[c, h] = s[h]
                s[h] = tm_["gl"][c * NH + h] * s[h] + kv[h]
        for h in range(NH):
            s_ref[h] = s[h]
        if npush:
            @pl.when(n == ng - 1)
            def _():
                _push_finish(copies)

    col = lambda c: pl.BlockSpec((rows, DQ), lambda n: (n, c))
    any_spec = pl.BlockSpec(memory_space=pl.ANY)
    pushed = [] if push is None else list(push)
    outs = pl.pallas_call(
        body, grid=(ng,),
        in_specs=[col(0), col(1), col(2), pl.BlockSpec((rows, GATE_W), lambda n: (n, 0))] + [any_spec] * npush,
        out_specs=[col(0), pl.BlockSpec((nch, NH, DH, DH), lambda n: (n, 0, 0, 0)),
                   pl.BlockSpec((nch, NH, CHUNK, CHUNK), lambda n: (n, 0, 0, 0)), col(0), col(0)]
        + [any_spec] * npush,
        out_shape=[jax.ShapeDtypeStruct((t, DQ), F32), jax.ShapeDtypeStruct((nc, NH, DH, DH), F32),
                   jax.ShapeDtypeStruct((nc, NH, CHUNK, CHUNK), F32), jax.ShapeDtypeStruct((t, DQ), F32),
                   jax.ShapeDtypeStruct((t, DQ), F32)] + _push_out_shapes(pushed, scatter=False),
        scratch_shapes=[pltpu.VMEM((NH, DH, DH), F32)] + (_push_sems(npush) if npush else []),
        compiler_params=_cp(("arbitrary",)), name=name)(qkv, qkv, qkv, bg, *pushed)
    return outs[:5], outs[5:]


def delta_bwd(qkv, bg, do, s_hist, m_hist, u_all, w_all, name, push=None):
    t = qkv.shape[0]
    nc = t // CHUNK
    nch = _pick(nc, DELTA_CHUNKS)
    rows = nch * CHUNK
    ng = nc // nch
    npush = 0 if push is None else len(push)

    def body(*refs):
        q_ref, k_ref, v_ref, bg_ref, do_ref, sh_ref, mi_ref, u_ref, w_ref = refs[:9]
        x_refs = refs[9:9 + npush]
        dqkv_ref, dbg_ref = refs[9 + npush:11 + npush]
        got_refs = refs[11 + npush:11 + 2 * npush]
        ds_ref = refs[11 + 2 * npush]
        n = pl.program_id(0)
        if npush:
            copies = _push_copies(x_refs, got_refs, *refs[12 + 2 * npush:], scatter=True)

            @pl.when(n == 0)
            def _():
                _push_start(copies)

        @pl.when(n == 0)
        def _():
            ds_ref[...] = jnp.zeros_like(ds_ref)

        ii, jj, low, strict = _chunk_masks()
        eye = jnp.where(ii == jj, 1.0, 0.0)
        q, k, v = _load_heads(q_ref, nch), _load_heads(k_ref, nch), _load_heads(v_ref, nch)
        d_o = _load_heads(do_ref, nch)
        bgs = [bg_ref[c * CHUNK:(c + 1) * CHUNK, :] for c in range(nch)]
        tm_ = _delta_terms(q, k, v, bgs, nch, low, strict)
        idx, lane, rowi = tm_["idx"], tm_["lane"], tm_["rowi"]
        beta, dm, eg, ekl, gl = tm_["beta"], tm_["dm"], tm_["eg"], tm_["ekl"], tm_["gl"]
        kb, kbg, qd, kd, a_mat, p_mat = tm_["kb"], tm_["kbg"], tm_["qd"], tm_["kd"], tm_["a"], tm_["p"]
        s = [sh_ref[c, h] for c, h in idx]
        m_inv = [mi_ref[c, h] for c, h in idx]
        u, w = _load_heads(u_ref, nch), _load_heads(w_ref, nch)
        ws = _mm1_many(w, s, NN)
        vn = [x - y for x, y in zip(u, ws)]
        pdo = _mm1_many(p_mat, d_o, TN)
        qdo = _mm1_many(qd, d_o, TN)
        dqd = _mm1_many(d_o, s, NT)
        dp = [jnp.where(low, x, 0.0) for x in _mm1_many(d_o, vn, NT)]

        nprob = len(idx)
        dvn, dkd, dgl = [None] * nprob, [None] * nprob, [None] * nprob
        ds = [ds_ref[h] for h in range(NH)]
        for c in reversed(range(nch)):
            pr = list(range(c * NH, (c + 1) * NH))
            kds = [_mm1(kd[i], ds[i - c * NH], NN) for i in pr]
            for i, x in zip(pr, kds):
                dvn[i] = pdo[i] + x
            wdv = [_mm1(w[i], dvn[i], TN) for i in pr]
            for i in pr:
                h = i - c * NH
                dkd[i] = _mm1(vn[i], ds[h], NT)
                dgl[i] = jnp.sum(jnp.sum(s[i] * ds[h], axis=1, keepdims=True), axis=0, keepdims=True)
                ds[h] = qdo[i] + gl[i] * ds[h] - wdv[h]
        for h in range(NH):
            ds_ref[h] = ds[h]

        dw = [-x for x in _mm1_many(dvn, s, NT)]
        dvb = _mm3_many(m_inv, dvn, TN)
        dkbg = _mm3_many(m_inv, dw, TN)
        da1 = _mm1_many(dvb, u, NT)
        da2 = _mm1_many(dkbg, w, NT)
        da = [-jnp.where(strict, x + y, 0.0) for x, y in zip(da1, da2)]
        gm = [x * d for x, d in zip(da, dm)]
        hm = [x * d for x, d in zip(dp, dm)]
        gk = _mm1_many(gm, k, NN)
        gkb = _mm1_many(gm, kb, TN)
        hq = _mm1_many(hm, q, TN)
        hk = _mm1_many(hm, k, NN)
        em = [x * a + y * p for x, a, y, p in zip(da, a_mat, dp, p_mat)]
        em_t = _transposes(em, eye)
        dbeta_all = [jnp.zeros((CHUNK, GATE_W), F32) for _ in range(nch)]
        dgam_all = [jnp.zeros((CHUNK, GATE_W), F32) for _ in range(nch)]
        for i, (c, h) in enumerate(idx):
            dkb = gk[i] + dkbg[i] * eg[i]
            dk = gkb[i] + hq[i] + dkd[i] * ekl[i] + beta[i] * dkb
            dq = hk[i] + dqd[i] * eg[i]
            dkd_kd = jnp.sum(dkd[i] * kd[i], axis=1, keepdims=True)
            dgam = (jnp.sum(em[i], axis=1, keepdims=True) - jnp.sum(em_t[i], axis=1, keepdims=True)
                    + jnp.sum(dqd[i] * qd[i], axis=1, keepdims=True) - dkd_kd
                    + jnp.sum(dkbg[i] * kbg[i], axis=1, keepdims=True))
            tail = jnp.sum(dkd_kd, axis=0, keepdims=True) + dgl[i] * gl[i]
            dgam = dgam + jnp.where(rowi == CHUNK - 1, tail, 0.0)
            dbeta = jnp.sum(dkb * k[i], axis=1, keepdims=True) + jnp.sum(dvb[i] * v[i], axis=1, keepdims=True)
            rs = slice(c * CHUNK, (c + 1) * CHUNK)
            dqkv_ref[rs, h * DH:(h + 1) * DH] = dq
            dqkv_ref[rs, DQ + h * DH:DQ + (h + 1) * DH] = dk
            dqkv_ref[rs, 2 * DQ + h * DH:2 * DQ + (h + 1) * DH] = beta[i] * dvb[i]
            dbeta_all[c] = dbeta_all[c] + jnp.where(lane == h, dbeta, 0.0)
            dgam_all[c] = dgam_all[c] + jnp.where(lane == NH + h, dgam, 0.0)
        upf = jnp.where(ii <= jj, 1.0, 0.0)
        for c in range(nch):
            dg_all = _mmx(upf, dgam_all[c], NN)
            dbg_ref[c * CHUNK:(c + 1) * CHUNK, :] = jnp.where(lane < NH, dbeta_all[c], dg_all)
        if npush:
            @pl.when(n == ng - 1)
            def _():
                _push_finish(copies)

    col = lambda c: pl.BlockSpec((rows, DQ), lambda n: (ng - 1 - n, c))
    gate = pl.BlockSpec((rows, GATE_W), lambda n: (ng - 1 - n, 0))
    any_spec = pl.BlockSpec(memory_space=pl.ANY)
    pushed = [] if push is None else list(push)
    outs = pl.pallas_call(
        body, grid=(ng,),
        in_specs=[col(0), col(1), col(2), gate, col(0),
                  pl.BlockSpec((nch, NH, DH, DH), lambda n: (ng - 1 - n, 0, 0, 0)),
                  pl.BlockSpec((nch, NH, CHUNK, CHUNK), lambda n: (ng - 1 - n, 0, 0, 0)), col(0), col(0)]
        + [any_spec] * npush,
        out_specs=[pl.BlockSpec((rows, 3 * DQ), lambda n: (ng - 1 - n, 0)), gate] + [any_spec] * npush,
        out_shape=[jax.ShapeDtypeStruct((t, 3 * DQ), F32), jax.ShapeDtypeStruct((t, GATE_W), F32)]
        + _push_out_shapes(pushed, scatter=True),
        scratch_shapes=[pltpu.VMEM((NH, DH, DH), F32)] + (_push_sems(npush) if npush else []),
        compiler_params=_cp(("arbitrary",)), name=name)(qkv, qkv, qkv, bg, do, s_hist, m_hist, u_all, w_all, *pushed)
    return outs[0], outs[1], outs[2:]


def dn_post_fwd(ybuf, o, proj, nw, name):
    t = o.shape[0]
    tm = _row_tile(t)

    def body(y_in, o_ref, z_ref, nw_ref, y_ref):
        del y_in
        nwv = nw_ref[...]
        for h in range(NH):
            sl = slice(h * DH, (h + 1) * DH)
            oh = o_ref[:, sl]
            z = z_ref[:, sl]
            r = lax.rsqrt(jnp.mean(oh * oh, axis=-1, keepdims=True) + NORM_EPS)
            y_ref[:, sl] = (oh * r * nwv * (z * _sigmoid(z))).astype(y_ref.dtype)

    return pl.pallas_call(
        body, grid=(t // tm,),
        in_specs=[pl.BlockSpec(memory_space=pl.ANY), pl.BlockSpec((tm, DQ), lambda i: (i, 0)),
                  pl.BlockSpec((tm, DQ), lambda i: (i, 5)), pl.BlockSpec((1, DH), lambda i: (0, 0))],
        out_specs=pl.BlockSpec((tm, DQ), lambda i: (i, 1)),
        out_shape=jax.ShapeDtypeStruct(ybuf.shape, ybuf.dtype), input_output_aliases={0: 0},
        compiler_params=_cp(("parallel",)), name=name)(ybuf, o, proj, nw)


def dn_post_bwd(dproj, dy, o, proj, nw, name):
    t = o.shape[0]
    tm = _row_tile(t)

    def body(dp_in, dy_ref, o_ref, z_ref, nw_ref, do_ref, dp_ref, dnw_ref):
        del dp_in
        i = pl.program_id(0)
        nwv = nw_ref[...]
        acc = jnp.zeros((1, DH), F32)
        for h in range(NH):
            sl = slice(h * DH, (h + 1) * DH)
            oh = o_ref[:, sl]
            z = z_ref[:, sl]
            dyh = dy_ref[:, sl]
            r = lax.rsqrt(jnp.mean(oh * oh, axis=-1, keepdims=True) + NORM_EPS)
            xh = oh * r
            sg = _sigmoid(z)
            sz = z * sg
            dxh = dyh * nwv * sz
            do_ref[:, sl] = r * (dxh - xh * jnp.mean(dxh * xh, axis=-1, keepdims=True))
            dp_ref[:, sl] = (dyh * xh * nwv * (sg * (1.0 + z * (1.0 - sg)))).astype(dp_ref.dtype)
            acc = acc + jnp.sum(dyh * xh * sz, axis=0, keepdims=True)

        @pl.when(i == 0)
        def _():
            dnw_ref[...] = jnp.zeros_like(dnw_ref)

        dnw_ref[...] += acc

    vec = pl.BlockSpec((1, DH), lambda i: (0, 0))
    return pl.pallas_call(
        body, grid=(t // tm,),
        in_specs=[pl.BlockSpec(memory_space=pl.ANY), pl.BlockSpec((tm, DQ), lambda i: (i, 1)),
                  pl.BlockSpec((tm, DQ), lambda i: (i, 0)), pl.BlockSpec((tm, DQ), lambda i: (i, 5)), vec],
        out_specs=[pl.BlockSpec((tm, DQ), lambda i: (i, 0)), pl.BlockSpec((tm, DQ), lambda i: (i, 5)), vec],
        out_shape=[jax.ShapeDtypeStruct((t, DQ), F32), jax.ShapeDtypeStruct(dproj.shape, dproj.dtype),
                   jax.ShapeDtypeStruct((1, DH), F32)],
        input_output_aliases={0: 1}, compiler_params=_cp(("arbitrary",)), name=name)(dproj, dy, o, proj, nw)


def _shifted(first, second, s, lane):
    if s == 0:
        return first
    return jnp.where(lane < LANES - s, pltpu.roll(first, LANES - s, 1), pltpu.roll(second, LANES - s, 1))


def unshard_cols(g8, w, widths, name):
    _, r, wp = g8.shape
    rb = _pick(r, (256, 128, 64, 32, 16))

    def body(g_ref, *o_refs):
        lane = lax.broadcasted_iota(jnp.int32, (rb, LANES), 1)
        zeros = jnp.zeros((rb, LANES), F32)

        def src(j, ta):
            if j >= NDEV or ta * LANES >= wp:
                return zeros
            return g_ref[j, :, ta * LANES:(ta + 1) * LANES].astype(F32)

        base = 0
        for o_ref, width in zip(o_refs, widths):
            for b in range(width // LANES):
                c0 = base + b * LANES
                if c0 >= NDEV * w:
                    tile = zeros
                else:
                    j0, o0 = divmod(c0, w)
                    n0 = min(w - o0, LANES)
                    ta, s = divmod(o0, LANES)
                    tile = _shifted(src(j0, ta), src(j0, ta + 1), s, lane)
                    if n0 < LANES:
                        nxt = pltpu.roll(src(j0 + 1, 0), n0, 1) if j0 + 1 < NDEV else zeros
                        tile = jnp.where(lane < n0, tile, nxt)
                o_ref[:, b * LANES:(b + 1) * LANES] = tile.astype(o_ref.dtype)
            base += width

    return pl.pallas_call(
        body, grid=(r // rb,), in_specs=[pl.BlockSpec((NDEV, rb, wp), lambda i: (0, i, 0))],
        out_specs=[pl.BlockSpec((rb, width), lambda i: (i, 0)) for width in widths],
        out_shape=[jax.ShapeDtypeStruct((r, width), g8.dtype) for width in widths],
        compiler_params=_cp(("parallel",)), name=name)(g8)


def shard_cols(parts, w, name):
    r = parts[0].shape[0]
    wp = _lane_pad(w)
    rb = _pick(r, (256, 128, 64, 32, 16))
    tiles_of = [p.shape[1] // LANES for p in parts]

    def body(*refs):
        p_refs, o_ref = refs[:-1], refs[-1]
        lane = lax.broadcasted_iota(jnp.int32, (rb, LANES), 1)
        zeros = jnp.zeros((rb, LANES), F32)

        def glob(tile_idx):
            for p_ref, n_tiles in zip(p_refs, tiles_of):
                if tile_idx < n_tiles:
                    return p_ref[:, tile_idx * LANES:(tile_idx + 1) * LANES].astype(F32)
                tile_idx -= n_tiles
            return zeros

        for j in range(NDEV):
            for a in range(wp // LANES):
                nv = min(w - a * LANES, LANES)
                tb, s = divmod(w * j + a * LANES, LANES)
                tile = _shifted(glob(tb), glob(tb + 1), s, lane)
                if nv < LANES:
                    tile = jnp.where(lane < nv, tile, 0.0)
                o_ref[j, :, a * LANES:(a + 1) * LANES] = tile.astype(o_ref.dtype)

    return pl.pallas_call(
        body, grid=(r // rb,), in_specs=[pl.BlockSpec((rb, p.shape[1]), lambda i: (i, 0)) for p in parts],
        out_specs=pl.BlockSpec((NDEV, rb, wp), lambda i: (0, i, 0)),
        out_shape=jax.ShapeDtypeStruct((NDEV, r, wp), GRAD_DTYPE),
        compiler_params=_cp(("parallel",)), name=name)(*parts)


def _me_and_peers():
    mx, my, mc = lax.axis_index("x"), lax.axis_index("y"), lax.axis_index("c")
    me = 4 * mx + 2 * my + mc
    peers = []
    for kk in range(1, NDEV):
        px = 1 - mx if kk & 4 else mx
        py = 1 - my if kk & 2 else my
        pc = 1 - mc if kk & 1 else mc
        peers.append(((px, py, pc), 4 * px + 2 * py + pc))
    return me, peers


def _push_copies(x_refs, o_refs, send_sems, recv_sems, local_sems, scatter):
    me, peers = _me_and_peers()
    npeer = NDEV - 1
    local, sends, recvs = [], [], []
    for a, (x_ref, o_ref) in enumerate(zip(x_refs, o_refs)):
        local.append(pltpu.make_async_copy(x_ref.at[me] if scatter else x_ref, o_ref.at[me], local_sems.at[a]))
        for kk, (peer, pidx) in enumerate(peers):
            src = x_ref.at[pidx] if scatter else x_ref
            sems = dict(send_sem=send_sems.at[a * npeer + kk], recv_sem=recv_sems.at[a * npeer + kk],
                        device_id=peer, device_id_type=pl.DeviceIdType.MESH)
            sends.append(pltpu.make_async_remote_copy(src_ref=src, dst_ref=o_ref.at[me], **sems))
            recvs.append(pltpu.make_async_remote_copy(src_ref=src, dst_ref=o_ref.at[pidx], **sems))
    return local, sends, recvs


def _push_start(copies):
    local, sends, _ = copies
    for cp in local + sends:
        cp.start()


def _push_finish(copies):
    local, sends, recvs = copies
    for cp in recvs:
        cp.wait_recv()
    for cp in sends:
        cp.wait_send()
    for cp in local:
        cp.wait()


def _push_out_shapes(xs, scatter):
    return [jax.ShapeDtypeStruct(x.shape if scatter else (NDEV,) + x.shape, x.dtype) for x in xs]


def _push_sems(n):
    return [pltpu.SemaphoreType.DMA((n * (NDEV - 1),)), pltpu.SemaphoreType.DMA((n * (NDEV - 1),)),
            pltpu.SemaphoreType.DMA((n,))]


def _push_to_all(xs, name, scatter):
    n = len(xs)

    def body(*refs):
        copies = _push_copies(refs[:n], refs[n:2 * n], *refs[2 * n:], scatter=scatter)
        _push_start(copies)
        _push_finish(copies)

    any_spec = pl.BlockSpec(memory_space=pl.ANY)
    return pl.pallas_call(
        body, in_specs=[any_spec] * n, out_specs=[any_spec] * n, out_shape=_push_out_shapes(xs, scatter),
        scratch_shapes=_push_sems(n), name=name)(*xs)


def adamw(recv, w, m, v, name):
    rows, cols = w.shape
    c1 = 1.0 - ADAM_B1 ** ADAM_STEP
    c2 = 1.0 - ADAM_B2 ** ADAM_STEP
    cap = ADAM_BLOCK_BYTES // (NDEV * cols * 4)
    rb = _pick(rows, [p for p in (2048, 1024, 512, 256, 128, 64, 32, 16, 8) if p <= cap])

    def body(r_ref, w_ref, m_ref, v_ref, g_ref, d_ref, m2_ref, v2_ref):
        g = r_ref[0].astype(F32)
        for j in range(1, NDEV):
            g = g + r_ref[j].astype(F32)
        m2 = ADAM_B1 * m_ref[...] + (1.0 - ADAM_B1) * g
        v2 = ADAM_B2 * v_ref[...] + (1.0 - ADAM_B2) * (g * g)
        g_ref[...] = g
        m2_ref[...] = m2
        v2_ref[...] = v2
        d_ref[...] = -ADAM_LR * ((m2 / c1) / (jnp.sqrt(v2 / c2) + ADAM_EPS) + ADAM_WD * w_ref[...])

    blk = pl.BlockSpec((rb, cols), lambda i: (i, 0))
    return pl.pallas_call(
        body, grid=(rows // rb,),
        in_specs=[pl.BlockSpec((NDEV, rb, cols), lambda i: (0, i, 0)), blk, blk, blk],
        out_specs=[blk, blk, blk, blk], out_shape=[jax.ShapeDtypeStruct((rows, cols), F32)] * 4,
        compiler_params=_cp(("parallel",)), name=name)(recv, w, m, v)


SMALL_SHARDED = ("meta_tokens", "conv_dw_w", "dn_conv_w")
SMALL_REPLICATED = ("norm_mix_w", "conv_dw_b", "conv_ln_w", "conv_ln_b", "dn_A_log", "dn_dt_bias", "dn_norm_w",
                    "norm_ffn_w", "final_norm_w")
PARAM_ORDER = ("meta_tokens", "norm_mix_w", "w_in", "conv_dw_w", "conv_dw_b", "conv_ln_w", "conv_ln_b", "dn_conv_w",
               "dn_A_log", "dn_dt_bias", "dn_norm_w", "w_out", "norm_ffn_w", "ffn_w_gu", "ffn_w_down", "final_norm_w")


def _pack_small(parts, axis):
    flat = jnp.concatenate(parts, axis=axis)
    n = flat.shape[axis]
    total = -(-n // (8 * LANES)) * (8 * LANES)
    pad = [(0, 0)] * flat.ndim
    pad[axis] = (0, total - n)
    flat = jnp.pad(flat, pad)
    return flat.reshape(flat.shape[:axis] + (total // LANES, LANES))


def _unshard_last(g8):
    moved = jnp.moveaxis(g8, 0, -2)
    return moved.reshape(moved.shape[:-2] + (-1,))


def _per_destination_last(full):
    split = full.reshape(full.shape[:-1] + (NDEV, full.shape[-1] // NDEV))
    return jnp.moveaxis(split, -2, 0).reshape(NDEV, -1)


def _lane_row(vec4, width):
    return jnp.pad(vec4, (NH, width - 2 * NH))[None]


def kernel(x, meta_tokens, norm_mix_w, w_in, conv_dw_w, conv_dw_b, conv_ln_w, conv_ln_b, dn_conv_w, dn_A_log, dn_dt_bias, dn_norm_w, w_out, norm_ffn_w, ffn_w_gu, ffn_w_down, final_norm_w, loss_target, m_meta_tokens, m_norm_mix_w, m_w_in, m_conv_dw_w, m_conv_dw_b, m_conv_ln_w, m_conv_ln_b, m_dn_conv_w, m_dn_A_log, m_dn_dt_bias, m_dn_norm_w, m_w_out, m_norm_ffn_w, m_ffn_w_gu, m_ffn_w_down, m_final_norm_w, v_meta_tokens, v_norm_mix_w, v_w_in, v_conv_dw_w, v_conv_dw_b, v_conv_ln_w, v_conv_ln_b, v_dn_conv_w, v_dn_A_log, v_dn_dt_bias, v_dn_norm_w, v_w_out, v_norm_ffn_w, v_ffn_w_gu, v_ffn_w_down, v_final_norm_w):
    weights = dict(meta_tokens=meta_tokens, norm_mix_w=norm_mix_w, w_in=w_in, conv_dw_w=conv_dw_w, conv_dw_b=conv_dw_b,
                   conv_ln_w=conv_ln_w, conv_ln_b=conv_ln_b, dn_conv_w=dn_conv_w, dn_A_log=dn_A_log,
                   dn_dt_bias=dn_dt_bias, dn_norm_w=dn_norm_w, w_out=w_out, norm_ffn_w=norm_ffn_w, ffn_w_gu=ffn_w_gu,
                   ffn_w_down=ffn_w_down, final_norm_w=final_norm_w)
    m_in = dict(meta_tokens=m_meta_tokens, norm_mix_w=m_norm_mix_w, w_in=m_w_in, conv_dw_w=m_conv_dw_w,
                conv_dw_b=m_conv_dw_b, conv_ln_w=m_conv_ln_w, conv_ln_b=m_conv_ln_b, dn_conv_w=m_dn_conv_w,
                dn_A_log=m_dn_A_log, dn_dt_bias=m_dn_dt_bias, dn_norm_w=m_dn_norm_w, w_out=m_w_out,
                norm_ffn_w=m_norm_ffn_w, ffn_w_gu=m_ffn_w_gu, ffn_w_down=m_ffn_w_down, final_norm_w=m_final_norm_w)
    v_in = dict(meta_tokens=v_meta_tokens, norm_mix_w=v_norm_mix_w, w_in=v_w_in, conv_dw_w=v_conv_dw_w,
                conv_dw_b=v_conv_dw_b, conv_ln_w=v_conv_ln_w, conv_ln_b=v_conv_ln_b, dn_conv_w=v_dn_conv_w,
                dn_A_log=v_dn_A_log, dn_dt_bias=v_dn_dt_bias, dn_norm_w=v_dn_norm_w, w_out=v_w_out,
                norm_ffn_w=v_norm_ffn_w, ffn_w_gu=v_ffn_w_gu, ffn_w_down=v_ffn_w_down, final_norm_w=v_final_norm_w)

    depth = w_in.shape[0]
    seq = x.shape[1]
    t = _padded_rows(seq)
    rows_d = depth * D
    win_w, gu_w = w_in.shape[2], ffn_w_gu.shape[2]
    win_wp, gu_wp = _lane_pad(win_w), _lane_pad(gu_w)

    def pad_cols(a, wp):
        return jnp.pad(a, ((0, 0), (0, 0), (0, wp - a.shape[2]))).reshape(rows_d, wp)

    def rows2d(a):
        return a.reshape(-1, a.shape[2])

    small_shards = [weights[n] for n in SMALL_SHARDED]
    win_p = pad_cols(w_in, win_wp).astype(MXU_DTYPE).reshape(depth, D, win_wp)
    gu_p = pad_cols(ffn_w_gu, gu_wp).astype(MXU_DTYPE).reshape(depth, D, gu_wp)
    wout_b, wdown_b = w_out.astype(MXU_DTYPE), ffn_w_down.astype(MXU_DTYPE)
    layer_shards = lambda l: [win_p[l], gu_p[l], wout_b[l], wdown_b[l]]

    def whole_weights(got):
        g_win, g_gu, g_wout, g_down = got
        main, gate_cols = unshard_cols(g_win, win_w, [PROJ_MAIN, GATE_W], "unshard_w_in")
        wg, wu = unshard_cols(g_gu, gu_w, [DFF, DFF], "unshard_w_gu")
        return dict(main=main, gate=gate_cols, wg=wg, wu=wu, out=g_wout.reshape(D, D), down=g_down.reshape(DFF, D))

    *got0, g_small = _push_to_all(layer_shards(0) + [_pack_small([s.reshape(-1) for s in small_shards], 0)],
                                  "gather_first", scatter=False)
    wts = [whole_weights(got0)]
    small_flat, off, small_full = g_small.reshape(NDEV, -1), 0, {}
    for n, s in zip(SMALL_SHARDED, small_shards):
        small_full[n] = _unshard_last(small_flat[:, off:off + s.size].reshape((NDEV,) + s.shape))
        off += s.size
    cdw32 = jnp.pad(small_full["conv_dw_w"], ((0, 0), (0, CONV_HALO - CONV_W), (0, 0)))
    dcw8 = jnp.pad(small_full["dn_conv_w"], ((0, 0), (0, DN_HALO - DN_W), (0, 0)))

    h = jnp.concatenate([jnp.zeros((FRONT, D), F32), small_full["meta_tokens"], x[0],
                         jnp.zeros((t - HEAD - seq, D), F32)], axis=0)
    tgt = jnp.pad(loss_target[0], ((HEAD, t - HEAD - seq), (0, 0)))

    saved = []
    for l in range(depth):
        nmw, nfw = norm_mix_w[l][None], norm_ffn_w[l][None]
        cdb, clw, clb = conv_dw_b[l][None], conv_ln_w[l][None], conv_ln_b[l][None]
        alog, dtb, dnw = _lane_row(dn_A_log[l], GATE_W), _lane_row(dn_dt_bias[l], GATE_W), dn_norm_w[l][None]
        wl = wts[l]
        proj = mm([h], wl["main"], a_fn=_rms_apply, a_vecs=[nmw], name="mm_proj")
        pg = mm([h], wl["gate"], a_fn=_rms_apply, a_vecs=[nmw], name="mm_proj_gate")
        ybuf, u1 = conv_fwd(proj, cdw32[l], cdb, clw, clb, seq, "conv_fwd")
        qkv = dn_pre_fwd(proj, dcw8[l], "dn_pre_fwd")
        bg = gate_fwd(pg, alog, dtb, seq, "gate_fwd")
        if l + 1 < depth:
            (o, s_hist, m_hist, u_all, w_all), got = delta_fwd(qkv, bg, "delta_fwd_gather", push=layer_shards(l + 1))
            wts.append(whole_weights(got))
        else:
            (o, s_hist, m_hist, u_all, w_all), _ = delta_fwd(qkv, bg, "delta_fwd")
        ybuf = dn_post_fwd(ybuf, o, proj, dnw, "dn_post_fwd")
        h_mid = mm(ybuf, wl["out"], extras=[h], out_fn=_add, name="mm_out")
        gate = mm([h_mid], wl["wg"], a_fn=_rms_apply, a_vecs=[nfw], name="mm_gate")
        up = mm([h_mid], wl["wu"], a_fn=_rms_apply, a_vecs=[nfw], name="mm_up")
        h_out = mm([gate, up], wl["down"], a_fn=_swiglu, extras=[h_mid], out_fn=_add, tile_cap=ROW_TILE, name="mm_down")
        saved.append(dict(h=h, proj=proj, pg=pg, ybuf=ybuf, u1=u1, qkv=qkv, bg=bg, o=o, s_hist=s_hist,
                          m_hist=m_hist, u_all=u_all, w_all=w_all, h_mid=h_mid, gate=gate, up=up,
                          nmw=nmw, nfw=nfw, clw=clw, clb=clb, alog=alog, dtb=dtb, dnw=dnw))
        h = h_out

    dh, loss_part, d_final = loss_bwd(h, tgt, final_norm_w[None], seq, "loss_bwd")
    loss = lax.psum(loss_part[0, 0], MESH_AXES)

    per_layer = ("norm_mix_w", "conv_dw_w", "conv_dw_b", "conv_ln_w", "conv_ln_b", "dn_conv_w", "dn_A_log", "dn_dt_bias",
                 "dn_norm_w", "norm_ffn_w")
    grads = {n: [None] * depth for n in per_layer}
    received = [None] * depth
    pending = None
    dw_mm = lambda a, b, name, **kw: mm(a, b, ta=True, out_dtypes=(GRAD_DTYPE,), name=name, **kw)
    for l in reversed(range(depth)):
        s, wl = saved[l], wts[l]
        dgate, dup = mm(dh, wl["down"], tb=True, extras=[s["gate"], s["up"]], out_fn=_swiglu_bwd,
                        out_dtypes=(MXU_DTYPE, MXU_DTYPE), tile_cap=ROW_TILE, name="mm_down_dx")
        d_down = dw_mm([s["gate"], s["up"]], dh, "mm_down_dw", a_fn=_swiglu, tile_cap=ROW_TILE)
        dhn2 = mm(dup, wl["wu"], tb=True, extras=[mm(dgate, wl["wg"], tb=True, name="mm_gate_dx")], out_fn=_add,
                  name="mm_up_dx")
        d_wg = dw_mm([s["h_mid"]], dgate, "mm_gate_dw", a_fn=_rms_apply, a_vecs=[s["nfw"]])
        d_wu = dw_mm([s["h_mid"]], dup, "mm_up_dw", a_fn=_rms_apply, a_vecs=[s["nfw"]])
        dh_mid, dnfw = rms_bwd(dhn2, s["h_mid"], s["nfw"], dh, "rms_ffn_bwd")
        dy = mm(dh_mid, wl["out"], tb=True, name="mm_out_dx")
        d_out = dw_mm(s["ybuf"], dh_mid, "mm_out_dw")
        dproj, dcdw, dcdb, dclw, dclb = conv_bwd(dy, s["u1"], s["proj"], cdw32[l], s["clw"], s["clb"], seq, "conv_bwd")
        do, dproj, ddnw = dn_post_bwd(dproj, dy, s["o"], s["proj"], s["dnw"], "dn_post_bwd")
        delta_args = (s["qkv"], s["bg"], do, s["s_hist"], s["m_hist"], s["u_all"], s["w_all"])
        if pending is None:
            dqkv, dbg, _ = delta_bwd(*delta_args, "delta_bwd")
        else:
            dqkv, dbg, received[l + 1] = delta_bwd(*delta_args, "delta_bwd_exchange", push=pending)
        dproj, ddcw = dn_pre_bwd(dproj, dqkv, s["proj"], dcw8[l], "dn_pre_bwd")
        dpg, dalog, ddtb = gate_bwd(dbg, s["pg"], s["alog"], s["dtb"], seq, "gate_bwd")
        dhn_gate = mm(dpg, wl["gate"], tb=True, name="mm_proj_gate_dx")
        dhn = mm(dproj, wl["main"], tb=True, extras=[dhn_gate], out_fn=_add, name="mm_proj_dx")
        d_main = dw_mm([s["h"]], dproj, "mm_proj_dw", a_fn=_rms_apply, a_vecs=[s["nmw"]])
        d_gate_cols = dw_mm([s["h"]], dpg, "mm_proj_gate_dw", a_fn=_rms_apply, a_vecs=[s["nmw"]])
        pending = [shard_cols([d_main, d_gate_cols], win_w, "shard_w_in"), shard_cols([d_wg, d_wu], gu_w, "shard_w_gu"),
                   d_out.reshape(NDEV, D // NDEV, D), d_down.reshape(NDEV, DFF // NDEV, D)]
        dh, dnmw = rms_bwd(dhn, s["h"], s["nmw"], dh_mid, "rms_mix_bwd")
        grads["norm_mix_w"][l] = dnmw[0]
        grads["norm_ffn_w"][l] = dnfw[0]
        grads["conv_dw_w"][l] = dcdw[:CONV_W]
        grads["conv_dw_b"][l] = dcdb[0]
        grads["conv_ln_w"][l] = dclw[0]
        grads["conv_ln_b"][l] = dclb[0]
        grads["dn_conv_w"][l] = ddcw[:DN_W]
        grads["dn_A_log"][l] = dalog[0, NH:2 * NH]
        grads["dn_dt_bias"][l] = ddtb[0, NH:2 * NH]
        grads["dn_norm_w"][l] = ddnw[0]

    grad_x = dh[HEAD:HEAD + seq][None]
    full = {n: jnp.stack(g) for n, g in grads.items()}
    full["meta_tokens"] = dh[FRONT:HEAD]
    full["final_norm_w"] = d_final[0]

    send_small = _pack_small(
        [_per_destination_last(full[n]) for n in SMALL_SHARDED]
        + [jnp.broadcast_to(full[n].reshape(1, -1), (NDEV, full[n].size)) for n in SMALL_REPLICATED], 1)
    *received[0], r_small = _push_to_all(pending + [send_small], "exchange_last", scatter=True)
    r_win, r_gu, r_wout, r_down = (jnp.concatenate([received[l][k] for l in range(depth)], axis=1) for k in range(4))

    small_names = SMALL_SHARDED + SMALL_REPLICATED
    pack_local = lambda tree: _pack_small([tree[n].reshape(-1) for n in small_names], 0)
    results = {}

    def run_adamw(name, recv, prep, finish):
        outs = adamw(recv, prep(weights[name]), prep(m_in[name]), prep(v_in[name]), "adamw_" + name)
        results[name] = [finish(o) for o in outs]

    run_adamw("w_in", r_win, lambda a: pad_cols(a, win_wp),
              lambda o: o[:, :win_w].reshape(depth, D, win_w))
    run_adamw("ffn_w_gu", r_gu, lambda a: pad_cols(a, gu_wp), lambda o: o[:, :gu_w].reshape(depth, D, gu_w))
    run_adamw("w_out", r_wout, rows2d, lambda o: o.reshape(w_out.shape))
    run_adamw("ffn_w_down", r_down, rows2d, lambda o: o.reshape(ffn_w_down.shape))
    small_outs = adamw(r_small, pack_local(weights), pack_local(m_in), pack_local(v_in), "adamw_small")
    for kind in range(4):
        flat, off = small_outs[kind].reshape(-1), 0
        for n in small_names:
            wgt = weights[n]
            results.setdefault(n, [None] * 4)[kind] = flat[off:off + wgt.size].reshape(wgt.shape)
            off += wgt.size

    return (loss, grad_x, *[results[n][0] for n in PARAM_ORDER], *[results[n][1] for n in PARAM_ORDER],
            *[results[n][2] for n in PARAM_ORDER], *[results[n][3] for n in PARAM_ORDER])
```

```python
import jax
import jax.numpy as jnp
from jax import lax
from jax.experimental import pallas as pl
from jax.experimental.pallas import tpu as pltpu

F32 = jnp.float32
MXU_DTYPE = jnp.bfloat16

D = 1024
N_META = 16
CHUNK = 64
CHUNK_LOG2 = 6
INV_BASE_LOG2 = 3
FRONT = CHUNK - N_META
HEAD = CHUNK
CONV_CH = 512
CONV_W = 31
CONV_HALO = 32
NH = 4
DH = 128
DQ = NH * DH
DN_W = 4
DN_HALO = 8
DFF = 2816
PROJ_MAIN = 3072
D_IN = 3080
GATE_W = 128
LANES = 128
NDEV = 8
NORM_EPS = 1e-6
LN_EPS = 1e-5
L2_EPS = 1e-6
VMEM_LIMIT_V7X = 48 * 1024 * 1024
ROW_TILE = 640
ROW_TILE_SMALL = 128
MM_TILES = (1408, 1280, 1024, 640, 512, 256, 128)
MM_SUB = 4
GRAD_DTYPE = jnp.bfloat16
DELTA_CHUNKS = (4, 2, 1)
ADAM_BLOCK_BYTES = 8 * 1024 * 1024

ADAM_LR = 0.001
ADAM_B1 = 0.9
ADAM_B2 = 0.999
ADAM_EPS = 1e-08
ADAM_WD = 0.01
ADAM_STEP = 10

MESH_AXES = ("x", "y", "c")
NN = ((1,), (0,))
NT = ((1,), (1,))
TN = ((0,), (0,))

assert 1 << CHUNK_LOG2 == CHUNK


def _row_tile(t):
    return ROW_TILE if t % ROW_TILE == 0 else ROW_TILE_SMALL


def _padded_rows(seq):
    n = HEAD + seq
    tm = ROW_TILE if n >= 4 * ROW_TILE else ROW_TILE_SMALL
    return -(-n // tm) * tm


def _pick(n, prefs):
    for p in prefs:
        if n % p == 0:
            return p
    return n


def _lane_pad(n):
    return -(-n // LANES) * LANES


def _cp(sem):
    return pltpu.CompilerParams(dimension_semantics=sem, vmem_limit_bytes=VMEM_LIMIT_V7X)


def _sigmoid(x):
    return 1.0 / (1.0 + jnp.exp(-x))


def _softplus(x):
    return jnp.maximum(x, 0.0) + jnp.log(1.0 + jnp.exp(-jnp.abs(x)))


def _valid_rows(i, tm, seq, width, first=FRONT):
    rows = i * tm + lax.broadcasted_iota(jnp.int32, (tm, width), 0)
    return jnp.logical_and(rows >= first, rows < HEAD + seq)


def _dot(a, b, dims):
    return lax.dot_general(a, b, (dims, ((), ())), preferred_element_type=F32)


def _split(x, n):
    out, r = [], x
    for _ in range(n):
        p = r.astype(MXU_DTYPE)
        out.append(p)
        r = r - p.astype(F32)
    return out


def _mm1(a, b, dims):
    return _dot(a.astype(MXU_DTYPE), b.astype(MXU_DTYPE), dims)


def _mm1_many(a_list, b_list, dims):
    return [_mm1(a, b, dims) for a, b in zip(a_list, b_list)]


def _mm3_many(a_list, b_list, dims):
    sa = [_split(a, 2) for a in a_list]
    sb = [_split(b, 2) for b in b_list]
    hh = [_dot(x[0], y[0], dims) for x, y in zip(sa, sb)]
    hl = [_dot(x[0], y[1], dims) for x, y in zip(sa, sb)]
    lh = [_dot(x[1], y[0], dims) for x, y in zip(sa, sb)]
    return [p + (q + r) for p, q, r in zip(hh, hl, lh)]


def _mmx(e, b, dims):
    e = e.astype(MXU_DTYPE)
    b1, b2, b3 = _split(b, 3)
    return _dot(e, b1, dims) + (_dot(e, b2, dims) + _dot(e, b3, dims))


def mm(a, b, *, ta=False, tb=False, a_fn=None, a_vecs=(), extras=(), o_vecs=(), out_fn=None, out_dtypes=(F32,),
       sum_last=False, tile_cap=None, name):
    a_list = list(a) if isinstance(a, (list, tuple)) else [a]
    (k_dim, m_dim) = a_list[0].shape if ta else a_list[0].shape[::-1]
    n_dim = b.shape[0] if tb else b.shape[1]
    assert (b.shape[1] if tb else b.shape[0]) == k_dim
    capped = MM_TILES if tile_cap is None else tuple(p for p in MM_TILES if p <= tile_cap)
    tm = _pick(m_dim, MM_TILES if ta else capped)
    tk = _pick(k_dim, capped if ta else MM_TILES)
    tn = _pick(n_dim, MM_TILES)
    nk = k_dim // tk
    na, nv, ne, nov, no = len(a_list), len(a_vecs), len(extras), len(o_vecs), len(out_dtypes)
    dims = ((0,) if ta else (1,), (1,) if tb else (0,))
    hooked = a_fn is not None or out_fn is not None
    token_tile = tk if ta else tm
    nsub = MM_SUB if hooked and token_tile % (MM_SUB * 16) == 0 else 1
    sub = token_tile // nsub
    assert not sum_last or (tn == n_dim and not ta)

    def body(*refs):
        a_refs, v_refs, b_ref = refs[:na], refs[na:na + nv], refs[na + nv]
        e_refs = refs[na + nv + 1:na + nv + 1 + ne]
        ov_refs = refs[na + nv + 1 + ne:na + nv + 1 + ne + nov]
        o_refs = refs[na + nv + 1 + ne + nov:na + nv + 1 + ne + nov + no]
        sum_ref = refs[na + nv + 1 + ne + nov + no] if sum_last else None
        acc_ref = refs[-1] if nk > 1 else None
        k = pl.program_id(2)

        def left(rows):
            tiles = [r[rows, :] for r in a_refs]
            return tiles[0] if a_fn is None else a_fn(*tiles, *[v[...] for v in v_refs])

        def finish(acc, rows):
            args = [e[rows, :] for e in e_refs] + [v[...] for v in ov_refs]
            outs = (acc,) if out_fn is None else out_fn(acc, *args)
            for o_ref, out in zip(o_refs, outs):
                o_ref[rows, :] = out.astype(o_ref.dtype)
            if sum_last:
                sum_ref[...] += jnp.sum(outs[no], axis=0, keepdims=True)

        if sum_last:
            @pl.when(jnp.logical_and(pl.program_id(0) == 0, k == 0))
            def _():
                sum_ref[...] = jnp.zeros_like(sum_ref)

        if nk > 1:
            @pl.when(k == 0)
            def _():
                acc_ref[...] = jnp.zeros_like(acc_ref)

        if ta:
            part = None
            for r in range(nsub):
                rows = slice(r * sub, (r + 1) * sub)
                prod = _mm1(left(rows), b_ref[rows, :], dims)
                part = prod if part is None else part + prod
            if nk == 1:
                finish(part, slice(None))
            else:
                acc_ref[...] += part
        else:
            for r in range(nsub):
                rows = slice(r * sub, (r + 1) * sub)
                prod = _mm1(left(rows), b_ref[...], dims)
                if nk == 1:
                    finish(prod, rows)
                else:
                    acc_ref[rows, :] += prod

        if nk > 1:
            @pl.when(k == nk - 1)
            def _():
                if ta:
                    finish(acc_ref[...], slice(None))
                else:
                    for r in range(nsub):
                        rows = slice(r * sub, (r + 1) * sub)
                        finish(acc_ref[rows, :], rows)

    if ta:
        a_spec = pl.BlockSpec((tk, tm), lambda i, j, k: (k, i))
        v_spec = pl.BlockSpec((1, tm), lambda i, j, k: (0, i))
    else:
        a_spec = pl.BlockSpec((tm, tk), lambda i, j, k: (i, k))
        v_spec = pl.BlockSpec((1, tk), lambda i, j, k: (0, k))
    b_spec = pl.BlockSpec((tn, tk), lambda i, j, k: (j, k)) if tb else pl.BlockSpec((tk, tn), lambda i, j, k: (k, j))
    o_spec = pl.BlockSpec((tm, tn), lambda i, j, k: (i, j))
    ov_spec = pl.BlockSpec((1, tn), lambda i, j, k: (0, j))
    out_specs = [o_spec] * no + ([ov_spec] if sum_last else [])
    out_shape = [jax.ShapeDtypeStruct((m_dim, n_dim), dt) for dt in out_dtypes]
    if sum_last:
        out_shape.append(jax.ShapeDtypeStruct((1, n_dim), F32))
    outs = pl.pallas_call(
        body, grid=(m_dim // tm, n_dim // tn, nk),
        in_specs=[a_spec] * na + [v_spec] * nv + [b_spec] + [o_spec] * ne + [ov_spec] * nov, out_specs=out_specs,
        out_shape=out_shape, scratch_shapes=[pltpu.VMEM((tm, tn), F32)] if nk > 1 else [],
        compiler_params=_cp(("arbitrary",) * 3 if sum_last else ("parallel", "parallel", "arbitrary")),
        name=name)(*a_list, *a_vecs, b, *extras, *o_vecs)
    return outs[0] if len(outs) == 1 else outs


def _rms_apply(x, w):
    assert x.shape[-1] == D
    return x * lax.rsqrt(jnp.mean(x * x, axis=-1, keepdims=True) + NORM_EPS) * w


def _swiglu(g, u):
    return g * _sigmoid(g) * u


def _swiglu_bwd(dact, g, u):
    sg = _sigmoid(g)
    return dact * u * (sg * (1.0 + g * (1.0 - sg))), dact * (g * sg)


def _add(acc, r):
    return (acc + r,)


def _rms_bwd_tail(acc, dy_rest, x, dres, w):
    assert x.shape[-1] == D
    dy = acc + dy_rest
    r = lax.rsqrt(jnp.mean(x * x, axis=-1, keepdims=True) + NORM_EPS)
    xh = x * r
    g = dy * w
    return dres + r * (g - xh * jnp.mean(g * xh, axis=-1, keepdims=True)), dy * xh


def loss_bwd(h, tgt, w, seq, name):
    t = h.shape[0]
    tm = _row_tile(t)

    def body(h_ref, t_ref, w_ref, dh_ref, loss_ref, dw_ref):
        i = pl.program_id(0)
        x = h_ref[...]
        wv = w_ref[...]
        r = lax.rsqrt(jnp.mean(x * x, axis=-1, keepdims=True) + NORM_EPS)
        xh = x * r
        err = jnp.where(_valid_rows(i, tm, seq, D, HEAD), xh * wv - t_ref[...], 0.0)
        dy = err * (1.0 / D)
        g = dy * wv
        dh_ref[...] = r * (g - xh * jnp.mean(g * xh, axis=-1, keepdims=True))

        @pl.when(i == 0)
        def _():
            dw_ref[...] = jnp.zeros_like(dw_ref)
            loss_ref[...] = jnp.zeros_like(loss_ref)

        dw_ref[...] += jnp.sum(dy * xh, axis=0, keepdims=True)
        part = jnp.sum(jnp.sum(err * err, axis=1, keepdims=True), axis=0, keepdims=True) * (0.5 / D)
        loss_ref[...] += jnp.broadcast_to(part, loss_ref.shape)

    row = pl.BlockSpec((tm, D), lambda i: (i, 0))
    vec = pl.BlockSpec((1, D), lambda i: (0, 0))
    return pl.pallas_call(
        body, grid=(t // tm,), in_specs=[row, row, vec],
        out_specs=[row, pl.BlockSpec((1, LANES), lambda i: (0, 0)), vec],
        out_shape=[jax.ShapeDtypeStruct((t, D), F32), jax.ShapeDtypeStruct((1, LANES), F32),
                   jax.ShapeDtypeStruct((1, D), F32)],
        compiler_params=_cp(("arbitrary",)), name=name)(h, tgt, w)


def _layernorm_parts(u1):
    mu = jnp.mean(u1, axis=-1, keepdims=True)
    xc = u1 - mu
    rstd = lax.rsqrt(jnp.mean(xc * xc, axis=-1, keepdims=True) + LN_EPS)
    return xc * rstd, rstd


SUBLANES = 8


def _shift_copies(ext, sh, tm):
    for s in range(1, SUBLANES):
        sh[s - 1, :, :] = ext[pl.ds(s, tm + CONV_HALO - SUBLANES), :]


def _window(ext, sh, off, tm):
    s, m = off % SUBLANES, off // SUBLANES
    if s == 0:
        return ext[pl.ds(off, tm), :]
    return sh[s - 1, pl.ds(SUBLANES * m, tm), :]


def _shift_scratch(tm):
    return pltpu.VMEM((SUBLANES - 1, tm + CONV_HALO - SUBLANES, CONV_CH), F32)


def conv_fwd(proj, w32, b, lw, lb, seq, name):
    t = proj.shape[0]
    tm = _row_tile(t)

    def body(cv_ref, cg_ref, w_ref, b_ref, lw_ref, lb_ref, y_ref, u1_ref, ext, sh):
        i = pl.program_id(0)

        @pl.when(i == 0)
        def _():
            ext[0:CONV_HALO, :] = jnp.zeros((CONV_HALO, CONV_CH), F32)

        @pl.when(i > 0)
        def _():
            ext[0:CONV_HALO, :] = ext[tm:tm + CONV_HALO, :]

        ext[CONV_HALO:CONV_HALO + tm, :] = cv_ref[...] * _sigmoid(cg_ref[...])
        _shift_copies(ext, sh, tm)
        acc = jnp.broadcast_to(b_ref[...], (tm, CONV_CH))
        for j in range(CONV_W):
            acc = acc + w_ref[j:j + 1, :] * _window(ext, sh, CONV_HALO - (CONV_W - 1) + j, tm)
        u1_ref[...] = acc
        xh, _ = _layernorm_parts(acc)
        ln = xh * lw_ref[...] + lb_ref[...]
        y = ln * _sigmoid(ln)
        y_ref[...] = jnp.where(_valid_rows(i, tm, seq, CONV_CH), y, 0.0).astype(y_ref.dtype)

    half = lambda c: pl.BlockSpec((tm, CONV_CH), lambda i: (i, c))
    vec = pl.BlockSpec((1, CONV_CH), lambda i: (0, 0))
    return pl.pallas_call(
        body, grid=(t // tm,),
        in_specs=[half(0), half(1), pl.BlockSpec((CONV_HALO, CONV_CH), lambda i: (0, 0)), vec, vec, vec],
        out_specs=[half(0), half(0)],
        out_shape=[jax.ShapeDtypeStruct((t, D), MXU_DTYPE), jax.ShapeDtypeStruct((t, CONV_CH), F32)],
        scratch_shapes=[pltpu.VMEM((tm + CONV_HALO, CONV_CH), F32), _shift_scratch(tm)],
        compiler_params=_cp(("arbitrary",)), name=name)(proj, proj, w32, b, lw, lb)


def conv_bwd(dy, u1, proj, w32, lw, lb, seq, name):
    t = proj.shape[0]
    tm = _row_tile(t)
    nt = t // tm
    per = tm // CONV_HALO

    def body(dy_ref, u1_ref, cv_ref, cg_ref, cvp_ref, cgp_ref, w_ref, lw_ref, lb_ref,
             dp_ref, dw_ref, db_ref, dlw_ref, dlb_ref, ext_d, ext_u, sh_d, sh_u):
        i = pl.program_id(0)
        tile = nt - 1 - i

        @pl.when(i == 0)
        def _():
            ext_d[tm:tm + CONV_HALO, :] = jnp.zeros((CONV_HALO, CONV_CH), F32)
            dw_ref[...] = jnp.zeros_like(dw_ref)
            db_ref[...] = jnp.zeros_like(db_ref)
            dlw_ref[...] = jnp.zeros_like(dlw_ref)
            dlb_ref[...] = jnp.zeros_like(dlb_ref)

        @pl.when(i > 0)
        def _():
            ext_d[tm:tm + CONV_HALO, :] = ext_d[0:CONV_HALO, :]

        xh, rstd = _layernorm_parts(u1_ref[...])
        lwv = lw_ref[...]
        ln = xh * lwv + lb_ref[...]
        sg = _sigmoid(ln)
        dln = jnp.where(_valid_rows(tile, tm, seq, CONV_CH), dy_ref[...], 0.0) * (sg * (1.0 + ln * (1.0 - sg)))
        dlw_ref[...] += jnp.sum(dln * xh, axis=0, keepdims=True)
        dlb_ref[...] += jnp.sum(dln, axis=0, keepdims=True)
        dxh = dln * lwv
        du1 = rstd * (dxh - jnp.mean(dxh, axis=-1, keepdims=True)
                      - xh * jnp.mean(dxh * xh, axis=-1, keepdims=True))
        db_ref[...] += jnp.sum(du1, axis=0, keepdims=True)
        ext_d[0:tm, :] = du1

        cv = cv_ref[...]
        sgc = _sigmoid(cg_ref[...])
        prev = cvp_ref[...] * _sigmoid(cgp_ref[...])
        ext_u[0:CONV_HALO, :] = jnp.where(tile > 0, prev, 0.0)
        ext_u[CONV_HALO:CONV_HALO + tm, :] = cv * sgc

        _shift_copies(ext_d, sh_d, tm)
        _shift_copies(ext_u, sh_u, tm)
        du0 = jnp.zeros((tm, CONV_CH), F32)
        for j in range(CONV_W):
            du0 = du0 + w_ref[j:j + 1, :] * _window(ext_d, sh_d, CONV_W - 1 - j, tm)
            dw_ref[j:j + 1, :] += jnp.sum(
                du1 * _window(ext_u, sh_u, CONV_HALO - (CONV_W - 1) + j, tm), axis=0, keepdims=True)
        dp_ref[:, 0:CONV_CH] = (du0 * sgc).astype(dp_ref.dtype)
        dp_ref[:, CONV_CH:2 * CONV_CH] = (du0 * cv * sgc * (1.0 - sgc)).astype(dp_ref.dtype)

    rev = lambda c: pl.BlockSpec((tm, CONV_CH), lambda i: (nt - 1 - i, c))
    prev = lambda c: pl.BlockSpec((CONV_HALO, CONV_CH), lambda i: (jnp.maximum((nt - 1 - i) * per - 1, 0), c))
    vec = pl.BlockSpec((1, CONV_CH), lambda i: (0, 0))
    wspec = pl.BlockSpec((CONV_HALO, CONV_CH), lambda i: (0, 0))
    return pl.pallas_call(
        body, grid=(nt,),
        in_specs=[rev(0), rev(0), rev(0), rev(1), prev(0), prev(1), wspec, vec, vec],
        out_specs=[pl.BlockSpec((tm, 2 * CONV_CH), lambda i: (nt - 1 - i, 0)), wspec, vec, vec, vec],
        out_shape=[jax.ShapeDtypeStruct((t, PROJ_MAIN), MXU_DTYPE), jax.ShapeDtypeStruct((CONV_HALO, CONV_CH), F32),
                   jax.ShapeDtypeStruct((1, CONV_CH), F32), jax.ShapeDtypeStruct((1, CONV_CH), F32),
                   jax.ShapeDtypeStruct((1, CONV_CH), F32)],
        scratch_shapes=[pltpu.VMEM((tm + CONV_HALO, CONV_CH), F32), pltpu.VMEM((tm + CONV_HALO, CONV_CH), F32),
                        _shift_scratch(tm), _shift_scratch(tm)],
        compiler_params=_cp(("arbitrary",)), name=name)(dy, u1, proj, proj, proj, proj, w32, lw, lb)


def dn_pre_fwd(proj, w8, name):
    t = proj.shape[0]
    tm = _row_tile(t)

    def body(raw_ref, w_ref, o_ref, ext):
        g = pl.program_id(0)
        i = pl.program_id(1)

        @pl.when(i == 0)
        def _():
            ext[0:DN_HALO, :] = jnp.zeros((DN_HALO, DQ), F32)

        @pl.when(i > 0)
        def _():
            ext[0:DN_HALO, :] = ext[tm:tm + DN_HALO, :]

        ext[DN_HALO:DN_HALO + tm, :] = raw_ref[...]
        c = jnp.zeros((tm, DQ), F32)
        for j in range(DN_W):
            c = c + w_ref[j:j + 1, :] * ext[pl.ds(DN_HALO - (DN_W - 1) + j, tm), :]
        s = c * _sigmoid(c)
        scale = jnp.where(g == 0, DH ** -0.5, 1.0)
        for h in range(NH):
            sh = s[:, h * DH:(h + 1) * DH]
            r = lax.rsqrt(jnp.sum(sh * sh, axis=-1, keepdims=True) + L2_EPS)
            o_ref[:, h * DH:(h + 1) * DH] = jnp.where(g == 2, sh, sh * (r * scale))

    return pl.pallas_call(
        body, grid=(3, t // tm),
        in_specs=[pl.BlockSpec((tm, DQ), lambda g, i: (i, 2 + g)), pl.BlockSpec((DN_HALO, DQ), lambda g, i: (0, g))],
        out_specs=pl.BlockSpec((tm, DQ), lambda g, i: (i, g)),
        out_shape=jax.ShapeDtypeStruct((t, 3 * DQ), F32),
        scratch_shapes=[pltpu.VMEM((tm + DN_HALO, DQ), F32)],
        compiler_params=_cp(("arbitrary", "arbitrary")), name=name)(proj, w8)


def dn_pre_bwd(dproj, dqkv, proj, w8, name):
    t = proj.shape[0]
    tm = _row_tile(t)
    nt = t // tm
    per = tm // DN_HALO

    def body(dp_in, d_ref, raw_ref, rawp_ref, w_ref, dp_ref, dw_ref, ext_d, ext_r):
        del dp_in
        g = pl.program_id(0)
        i = pl.program_id(1)
        tile = nt - 1 - i

        @pl.when(i == 0)
        def _():
            ext_d[tm:tm + DN_HALO, :] = jnp.zeros((DN_HALO, DQ), F32)
            dw_ref[...] = jnp.zeros_like(dw_ref)

        @pl.when(i > 0)
        def _():
            ext_d[tm:tm + DN_HALO, :] = ext_d[0:DN_HALO, :]

        ext_r[0:DN_HALO, :] = jnp.where(tile > 0, rawp_ref[...], 0.0)
        ext_r[DN_HALO:DN_HALO + tm, :] = raw_ref[...]
        c = jnp.zeros((tm, DQ), F32)
        for j in range(DN_W):
            c = c + w_ref[j:j + 1, :] * ext_r[pl.ds(DN_HALO - (DN_W - 1) + j, tm), :]
        sg = _sigmoid(c)
        s = c * sg
        scale = jnp.where(g == 0, DH ** -0.5, 1.0)
        for h in range(NH):
            sl = slice(h * DH, (h + 1) * DH)
            sh = s[:, sl]
            dn = d_ref[:, sl]
            r = lax.rsqrt(jnp.sum(sh * sh, axis=-1, keepdims=True) + L2_EPS)
            unit = sh * r
            dsn = (r * scale) * (dn - unit * jnp.sum(dn * unit, axis=-1, keepdims=True))
            ds = jnp.where(g == 2, dn, dsn)
            ext_d[0:tm, sl] = ds * (sg[:, sl] * (1.0 + c[:, sl] * (1.0 - sg[:, sl])))
        dc = ext_d[0:tm, :]
        draw = jnp.zeros((tm, DQ), F32)
        for j in range(DN_W):
            draw = draw + w_ref[j:j + 1, :] * ext_d[pl.ds(DN_W - 1 - j, tm), :]
            dw_ref[j:j + 1, :] += jnp.sum(
                dc * ext_r[pl.ds(DN_HALO - (DN_W - 1) + j, tm), :], axis=0, keepdims=True)
        dp_ref[...] = draw.astype(dp_ref.dtype)

    return pl.pallas_call(
        body, grid=(3, nt),
        in_specs=[pl.BlockSpec(memory_space=pl.ANY),
                  pl.BlockSpec((tm, DQ), lambda g, i: (nt - 1 - i, g)),
                  pl.BlockSpec((tm, DQ), lambda g, i: (nt - 1 - i, 2 + g)),
                  pl.BlockSpec((DN_HALO, DQ), lambda g, i: (jnp.maximum((nt - 1 - i) * per - 1, 0), 2 + g)),
                  pl.BlockSpec((DN_HALO, DQ), lambda g, i: (0, g))],
        out_specs=[pl.BlockSpec((tm, DQ), lambda g, i: (nt - 1 - i, 2 + g)),
                   pl.BlockSpec((DN_HALO, DQ), lambda g, i: (0, g))],
        out_shape=[jax.ShapeDtypeStruct(dproj.shape, dproj.dtype), jax.ShapeDtypeStruct((DN_HALO, 3 * DQ), F32)],
        scratch_shapes=[pltpu.VMEM((tm + DN_HALO, DQ), F32), pltpu.VMEM((tm + DN_HALO, DQ), F32)],
        input_output_aliases={0: 0},
        compiler_params=_cp(("arbitrary", "arbitrary")), name=name)(dproj, dqkv, proj, proj, w8)


def gate_fwd(pg, alog, dtb, seq, name):
    t = pg.shape[0]
    tm = _row_tile(t)

    def body(x_ref, al_ref, dt_ref, o_ref):
        i = pl.program_id(0)
        x = x_ref[...]
        lane = lax.broadcasted_iota(jnp.int32, (tm, GATE_W), 1)
        gg = -jnp.exp(al_ref[...]) * _softplus(x + dt_ref[...])
        out = jnp.where(lane < NH, _sigmoid(x), jnp.where(lane < 2 * NH, gg, 0.0))
        o_ref[...] = jnp.where(_valid_rows(i, tm, seq, GATE_W), out, 0.0)

    row = pl.BlockSpec((tm, GATE_W), lambda i: (i, 0))
    vec = pl.BlockSpec((1, GATE_W), lambda i: (0, 0))
    return pl.pallas_call(
        body, grid=(t // tm,), in_specs=[row, vec, vec], out_specs=row,
        out_shape=jax.ShapeDtypeStruct((t, GATE_W), F32), compiler_params=_cp(("parallel",)), name=name)(pg, alog, dtb)


def gate_bwd(dbg, pg, alog, dtb, seq, name):
    t = pg.shape[0]
    tm = _row_tile(t)

    def body(d_ref, x_ref, al_ref, dt_ref, o_ref, dal_ref, ddt_ref):
        i = pl.program_id(0)
        x = x_ref[...]
        lane = lax.broadcasted_iota(jnp.int32, (tm, GATE_W), 1)
        d = jnp.where(_valid_rows(i, tm, seq, GATE_W), d_ref[...], 0.0)
        beta = _sigmoid(x)
        xs = x + dt_ref[...]
        e = -jnp.exp(al_ref[...])
        is_g = jnp.logical_and(lane >= NH, lane < 2 * NH)
        da = jnp.where(is_g, d * e * _sigmoid(xs), 0.0)
        dgg = jnp.where(is_g, d * e * _softplus(xs), 0.0)
        o_ref[...] = jnp.where(lane < NH, d * beta * (1.0 - beta), da).astype(o_ref.dtype)

        @pl.when(i == 0)
        def _():
            dal_ref[...] = jnp.zeros_like(dal_ref)
            ddt_ref[...] = jnp.zeros_like(ddt_ref)

        dal_ref[...] += jnp.sum(dgg, axis=0, keepdims=True)
        ddt_ref[...] += jnp.sum(da, axis=0, keepdims=True)

    row = pl.BlockSpec((tm, GATE_W), lambda i: (i, 0))
    vec = pl.BlockSpec((1, GATE_W), lambda i: (0, 0))
    return pl.pallas_call(
        body, grid=(t // tm,), in_specs=[row, row, vec, vec], out_specs=[row, vec, vec],
        out_shape=[jax.ShapeDtypeStruct((t, GATE_W), MXU_DTYPE), jax.ShapeDtypeStruct((1, GATE_W), F32),
                   jax.ShapeDtypeStruct((1, GATE_W), F32)],
        compiler_params=_cp(("arbitrary",)), name=name)(dbg, pg, alog, dtb)


def _chunk_masks():
    ii = lax.broadcasted_iota(jnp.int32, (CHUNK, CHUNK), 0)
    jj = lax.broadcasted_iota(jnp.int32, (CHUNK, CHUNK), 1)
    return ii, jj, ii >= jj, ii > jj


def _lane_col(x, lane, idx):
    return jnp.sum(jnp.where(lane == idx, x, 0.0), axis=1, keepdims=True)


def _delta_terms(q, k, v, bgs, nch, low, strict):
    idx = [(c, h) for c in range(nch) for h in range(NH)]
    lane = lax.broadcasted_iota(jnp.int32, (CHUNK, GATE_W), 1)
    rowi = lax.broadcasted_iota(jnp.int32, (CHUNK, 1), 0)
    r4 = lax.broadcasted_iota(jnp.int32, (NH * CHUNK, GATE_W), 0)
    l4 = lax.broadcasted_iota(jnp.int32, (NH * CHUNK, GATE_W), 1)
    sel = jnp.where(l4 == NH + jnp.right_shift(r4, CHUNK_LOG2), 1.0, 0.0)
    lowf = jnp.where(low, 1.0, 0.0)
    gam_all = [_mmx(lowf, b, NN) for b in bgs]
    gam_rows = [_mmx(sel, g, NT) for g in gam_all]
    beta = [_lane_col(bgs[c], lane, h) for c, h in idx]
    gam = [_lane_col(gam_all[c], lane, NH + h) for c, h in idx]
    dm = [jnp.exp(jnp.where(low, g - gam_rows[c][h * CHUNK:(h + 1) * CHUNK, :], -1e30))
          for g, (c, h) in zip(gam, idx)]
    glast = [jnp.sum(jnp.where(rowi == CHUNK - 1, g, 0.0), axis=0, keepdims=True) for g in gam]
    eg = [jnp.exp(g) for g in gam]
    ekl = [jnp.exp(gl - g) for gl, g in zip(glast, gam)]
    gl = [jnp.exp(x) for x in glast]
    kb = [x * b for x, b in zip(k, beta)]
    vb = [x * b for x, b in zip(v, beta)]
    kbg = [x * e for x, e in zip(kb, eg)]
    kk = _mm1_many(kb, k, NT)
    qk = _mm1_many(q, k, NT)
    a_mat = [jnp.where(strict, x * d, 0.0) for x, d in zip(kk, dm)]
    p_mat = [jnp.where(low, x * d, 0.0) for x, d in zip(qk, dm)]
    qd = [x * e for x, e in zip(q, eg)]
    kd = [x * e for x, e in zip(k, ekl)]
    return dict(idx=idx, beta=beta, dm=dm, eg=eg, ekl=ekl, gl=gl, kb=kb, vb=vb, kbg=kbg, a=a_mat, p=p_mat, qd=qd,
                kd=kd, lane=lane, rowi=rowi)


def _unit_lower_inverses(a_list, ii, jj, eye):
    def same(log2):
        return jnp.right_shift(ii, log2) == jnp.right_shift(jj, log2)

    n = [-jnp.where(same(INV_BASE_LOG2), a, 0.0) for a in a_list]
    x = [eye + v for v in n]
    p = n
    for _ in range(INV_BASE_LOG2 - 1):
        p = _mm1_many(p, p, NN)
        x = [xi + y for xi, y in zip(x, _mm1_many(x, p, NN))]
    for log2 in range(INV_BASE_LOG2, CHUNK_LOG2):
        off = jnp.logical_and(same(log2 + 1), jnp.logical_not(same(log2)))
        a_off = [jnp.where(off, a, 0.0) for a in a_list]
        x = [xi - y for xi, y in zip(x, _mm1_many(x, _mm1_many(a_off, x, NN), NN))]
    return x


def _transposes(xs, eye):
    e = eye.astype(MXU_DTYPE)
    parts = [_split(x, 2) for x in xs]
    return [_dot(p[0], e, TN) + _dot(p[1], e, TN) for p in parts]


def _load_heads(ref, nch):
    return [ref[c * CHUNK:(c + 1) * CHUNK, h * DH:(h + 1) * DH] for c in range(nch) for h in range(NH)]


def delta_fwd(qkv, bg, name, push=None):
    t = qkv.shape[0]
    nc = t // CHUNK
    nch = _pick(nc, DELTA_CHUNKS)
    rows = nch * CHUNK
    ng = nc // nch
    npush = 0 if push is None else len(push)

    def body(*refs):
        q_ref, k_ref, v_ref, bg_ref = refs[:4]
        x_refs = refs[4:4 + npush]
        o_ref, sh_ref, mi_ref, u_ref, w_ref = refs[4 + npush:9 + npush]
        got_refs = refs[9 + npush:9 + 2 * npush]
        s_ref = refs[9 + 2 * npush]
        n = pl.program_id(0)
        if npush:
            copies = _push_copies(x_refs, got_refs, *refs[10 + 2 * npush:], scatter=False)

            @pl.when(n == 0)
            def _():
                _push_start(copies)

        @pl.when(n == 0)
        def _():
            s_ref[...] = jnp.zeros_like(s_ref)

        ii, jj, low, strict = _chunk_masks()
        eye = jnp.where(ii == jj, 1.0, 0.0)
        q, k, v = _load_heads(q_ref, nch), _load_heads(k_ref, nch), _load_heads(v_ref, nch)
        bgs = [bg_ref[c * CHUNK:(c + 1) * CHUNK, :] for c in range(nch)]
        tm_ = _delta_terms(q, k, v, bgs, nch, low, strict)
        m_inv = _unit_lower_inverses(tm_["a"], ii, jj, eye)
        u = _mm3_many(m_inv, tm_["vb"], NN)
        w = _mm3_many(m_inv, tm_["kbg"], NN)
        for i, (c, h) in enumerate(tm_["idx"]):
            mi_ref[c, h] = m_inv[i]
            u_ref[c * CHUNK:(c + 1) * CHUNK, h * DH:(h + 1) * DH] = u[i]
            w_ref[c * CHUNK:(c + 1) * CHUNK, h * DH:(h + 1) * DH] = w[i]
        s = [s_ref[h] for h in range(NH)]
        for c in range(nch):
            pr = range(c * NH, (c + 1) * NH)
            ws = [_mm1(w[i], s[i - c * NH], NN) for i in pr]
            qs = [_mm1(tm_["qd"][i], s[i - c * NH], NN) for i in pr]
            vn = [u[i] - x for i, x in zip(pr, ws)]
            pv = [_mm1(tm_["p"][i], x, NN) for i, x in zip(pr, vn)]
            kv = [_mm1(tm_["kd"][i], x, TN) for i, x in zip(pr, vn)]
            for h in range(NH):
                o_ref[c * CHUNK:(c + 1) * CHUNK, h * DH:(h + 1) * DH] = qs[h] + pv[h]
                sh_ref[c, h] = s[h]
                s[h] = tm_["gl"][c * NH + h] * s[h] + kv[h]
        for h in range(NH):
            s_ref[h] = s[h]
        if npush:
            @pl.when(n == ng - 1)
            def _():
                _push_finish(copies)

    col = lambda c: pl.BlockSpec((rows, DQ), lambda n: (n, c))
    any_spec = pl.BlockSpec(memory_space=pl.ANY)
    pushed = [] if push is None else list(push)
    outs = pl.pallas_call(
        body, grid=(ng,),
        in_specs=[col(0), col(1), col(2), pl.BlockSpec((rows, GATE_W), lambda n: (n, 0))] + [any_spec] * npush,
        out_specs=[col(0), pl.BlockSpec((nch, NH, DH, DH), lambda n: (n, 0, 0, 0)),
                   pl.BlockSpec((nch, NH, CHUNK, CHUNK), lambda n: (n, 0, 0, 0)), col(0), col(0)]
        + [any_spec] * npush,
        out_shape=[jax.ShapeDtypeStruct((t, DQ), F32), jax.ShapeDtypeStruct((nc, NH, DH, DH), F32),
                   jax.ShapeDtypeStruct((nc, NH, CHUNK, CHUNK), F32), jax.ShapeDtypeStruct((t, DQ), F32),
                   jax.ShapeDtypeStruct((t, DQ), F32)] + _push_out_shapes(pushed, scatter=False),
        scratch_shapes=[pltpu.VMEM((NH, DH, DH), F32)] + (_push_sems(npush) if npush else []),
        compiler_params=_cp(("arbitrary",)), name=name)(qkv, qkv, qkv, bg, *pushed)
    return outs[:5], outs[5:]


def delta_bwd(qkv, bg, do, s_hist, m_hist, u_all, w_all, name, push=None):
    t = qkv.shape[0]
    nc = t // CHUNK
    nch = _pick(nc, DELTA_CHUNKS)
    rows = nch * CHUNK
    ng = nc // nch
    npush = 0 if push is None else len(push)

    def body(*refs):
        q_ref, k_ref, v_ref, bg_ref, do_ref, sh_ref, mi_ref, u_ref, w_ref = refs[:9]
        x_refs = refs[9:9 + npush]
        dqkv_ref, dbg_ref = refs[9 + npush:11 + npush]
        got_refs = refs[11 + npush:11 + 2 * npush]
        ds_ref = refs[11 + 2 * npush]
        n = pl.program_id(0)
        if npush:
            copies = _push_copies(x_refs, got_refs, *refs[12 + 2 * npush:], scatter=True)

            @pl.when(n == 0)
            def _():
                _push_start(copies)

        @pl.when(n == 0)
        def _():
            ds_ref[...] = jnp.zeros_like(ds_ref)

        ii, jj, low, strict = _chunk_masks()
        eye = jnp.where(ii == jj, 1.0, 0.0)
        q, k, v = _load_heads(q_ref, nch), _load_heads(k_ref, nch), _load_heads(v_ref, nch)
        d_o = _load_heads(do_ref, nch)
        bgs = [bg_ref[c * CHUNK:(c + 1) * CHUNK, :] for c in range(nch)]
        tm_ = _delta_terms(q, k, v, bgs, nch, low, strict)
        idx, lane, rowi = tm_["idx"], tm_["lane"], tm_["rowi"]
        beta, dm, eg, ekl, gl = tm_["beta"], tm_["dm"], tm_["eg"], tm_["ekl"], tm_["gl"]
        kb, kbg, qd, kd, a_mat, p_mat = tm_["kb"], tm_["kbg"], tm_["qd"], tm_["kd"], tm_["a"], tm_["p"]
        s = [sh_ref[c, h] for c, h in idx]
        m_inv = [mi_ref[c, h] for c, h in idx]
        u, w = _load_heads(u_ref, nch), _load_heads(w_ref, nch)
        ws = _mm1_many(w, s, NN)
        vn = [x - y for x, y in zip(u, ws)]
        pdo = _mm1_many(p_mat, d_o, TN)
        qdo = _mm1_many(qd, d_o, TN)
        dqd = _mm1_many(d_o, s, NT)
        dp = [jnp.where(low, x, 0.0) for x in _mm1_many(d_o, vn, NT)]

        nprob = len(idx)
        dvn, dkd, dgl = [None] * nprob, [None] * nprob, [None] * nprob
        ds = [ds_ref[h] for h in range(NH)]
        for c in reversed(range(nch)):
            pr = list(range(c * NH, (c + 1) * NH))
            kds = [_mm1(kd[i], ds[i - c * NH], NN) for i in pr]
            for i, x in zip(pr, kds):
                dvn[i] = pdo[i] + x
            wdv = [_mm1(w[i], dvn[i], TN) for i in pr]
            for i in pr:
                h = i - c * NH
                dkd[i] = _mm1(vn[i], ds[h], NT)
                dgl[i] = jnp.sum(jnp.sum(s[i] * ds[h], axis=1, keepdims=True), axis=0, keepdims=True)
                ds[h] = qdo[i] + gl[i] * ds[h] - wdv[h]
        for h in range(NH):
            ds_ref[h] = ds[h]

        dw = [-x for x in _mm1_many(dvn, s, NT)]
        dvb = _mm3_many(m_inv, dvn, TN)
        dkbg = _mm3_many(m_inv, dw, TN)
        da1 = _mm1_many(dvb, u, NT)
        da2 = _mm1_many(dkbg, w, NT)
        da = [-jnp.where(strict, x + y, 0.0) for x, y in zip(da1, da2)]
        gm = [x * d for x, d in zip(da, dm)]
        hm = [x * d for x, d in zip(dp, dm)]
        gk = _mm1_many(gm, k, NN)
        gkb = _mm1_many(gm, kb, TN)
        hq = _mm1_many(hm, q, TN)
        hk = _mm1_many(hm, k, NN)
        em = [x * a + y * p for x, a, y, p in zip(da, a_mat, dp, p_mat)]
        em_t = _transposes(em, eye)
        dbeta_all = [jnp.zeros((CHUNK, GATE_W), F32) for _ in range(nch)]
        dgam_all = [jnp.zeros((CHUNK, GATE_W), F32) for _ in range(nch)]
        for i, (c, h) in enumerate(idx):
            dkb = gk[i] + dkbg[i] * eg[i]
            dk = gkb[i] + hq[i] + dkd[i] * ekl[i] + beta[i] * dkb
            dq = hk[i] + dqd[i] * eg[i]
            dkd_kd = jnp.sum(dkd[i] * kd[i], axis=1, keepdims=True)
            dgam = (jnp.sum(em[i], axis=1, keepdims=True) - jnp.sum(em_t[i], axis=1, keepdims=True)
                    + jnp.sum(dqd[i] * qd[i], axis=1, keepdims=True) - dkd_kd
                    + jnp.sum(dkbg[i] * kbg[i], axis=1, keepdims=True))
            tail = jnp.sum(dkd_kd, axis=0, keepdims=True) + dgl[i] * gl[i]
            dgam = dgam + jnp.where(rowi == CHUNK - 1, tail, 0.0)
            dbeta = jnp.sum(dkb * k[i], axis=1, keepdims=True) + jnp.sum(dvb[i] * v[i], axis=1, keepdims=True)
            rs = slice(c * CHUNK, (c + 1) * CHUNK)
            dqkv_ref[rs, h * DH:(h + 1) * DH] = dq
            dqkv_ref[rs, DQ + h * DH:DQ + (h + 1) * DH] = dk
            dqkv_ref[rs, 2 * DQ + h * DH:2 * DQ + (h + 1) * DH] = beta[i] * dvb[i]
            dbeta_all[c] = dbeta_all[c] + jnp.where(lane == h, dbeta, 0.0)
            dgam_all[c] = dgam_all[c] + jnp.where(lane == NH + h, dgam, 0.0)
        upf = jnp.where(ii <= jj, 1.0, 0.0)
        for c in range(nch):
            dg_all = _mmx(upf, dgam_all[c], NN)
            dbg_ref[c * CHUNK:(c + 1) * CHUNK, :] = jnp.where(lane < NH, dbeta_all[c], dg_all)
        if npush:
            @pl.when(n == ng - 1)
            def _():
                _push_finish(copies)

    col = lambda c: pl.BlockSpec((rows, DQ), lambda n: (ng - 1 - n, c))
    gate = pl.BlockSpec((rows, GATE_W), lambda n: (ng - 1 - n, 0))
    any_spec = pl.BlockSpec(memory_space=pl.ANY)
    pushed = [] if push is None else list(push)
    outs = pl.pallas_call(
        body, grid=(ng,),
        in_specs=[col(0), col(1), col(2), gate, col(0),
                  pl.BlockSpec((nch, NH, DH, DH), lambda n: (ng - 1 - n, 0, 0, 0)),
                  pl.BlockSpec((nch, NH, CHUNK, CHUNK), lambda n: (ng - 1 - n, 0, 0, 0)), col(0), col(0)]
        + [any_spec] * npush,
        out_specs=[pl.BlockSpec((rows, 3 * DQ), lambda n: (ng - 1 - n, 0)), gate] + [any_spec] * npush,
        out_shape=[jax.ShapeDtypeStruct((t, 3 * DQ), F32), jax.ShapeDtypeStruct((t, GATE_W), F32)]
        + _push_out_shapes(pushed, scatter=True),
        scratch_shapes=[pltpu.VMEM((NH, DH, DH), F32)] + (_push_sems(npush) if npush else []),
        compiler_params=_cp(("arbitrary",)), name=name)(qkv, qkv, qkv, bg, do, s_hist, m_hist, u_all, w_all, *pushed)
    return outs[0], outs[1], outs[2:]


def dn_post_fwd(ybuf, o, proj, nw, name):
    t = o.shape[0]
    tm = _row_tile(t)

    def body(y_in, o_ref, z_ref, nw_ref, y_ref):
        del y_in
        nwv = nw_ref[...]
        for h in range(NH):
            sl = slice(h * DH, (h + 1) * DH)
            oh = o_ref[:, sl]
            z = z_ref[:, sl]
            r = lax.rsqrt(jnp.mean(oh * oh, axis=-1, keepdims=True) + NORM_EPS)
            y_ref[:, sl] = (oh * r * nwv * (z * _sigmoid(z))).astype(y_ref.dtype)

    return pl.pallas_call(
        body, grid=(t // tm,),
        in_specs=[pl.BlockSpec(memory_space=pl.ANY), pl.BlockSpec((tm, DQ), lambda i: (i, 0)),
                  pl.BlockSpec((tm, DQ), lambda i: (i, 5)), pl.BlockSpec((1, DH), lambda i: (0, 0))],
        out_specs=pl.BlockSpec((tm, DQ), lambda i: (i, 1)),
        out_shape=jax.ShapeDtypeStruct(ybuf.shape, ybuf.dtype), input_output_aliases={0: 0},
        compiler_params=_cp(("parallel",)), name=name)(ybuf, o, proj, nw)


def dn_post_bwd(dproj, dy, o, proj, nw, name):
    t = o.shape[0]
    tm = _row_tile(t)

    def body(dp_in, dy_ref, o_ref, z_ref, nw_ref, do_ref, dp_ref, dnw_ref):
        del dp_in
        i = pl.program_id(0)
        nwv = nw_ref[...]
        acc = jnp.zeros((1, DH), F32)
        for h in range(NH):
            sl = slice(h * DH, (h + 1) * DH)
            oh = o_ref[:, sl]
            z = z_ref[:, sl]
            dyh = dy_ref[:, sl]
            r = lax.rsqrt(jnp.mean(oh * oh, axis=-1, keepdims=True) + NORM_EPS)
            xh = oh * r
            sg = _sigmoid(z)
            sz = z * sg
            dxh = dyh * nwv * sz
            do_ref[:, sl] = r * (dxh - xh * jnp.mean(dxh * xh, axis=-1, keepdims=True))
            dp_ref[:, sl] = (dyh * xh * nwv * (sg * (1.0 + z * (1.0 - sg)))).astype(dp_ref.dtype)
            acc = acc + jnp.sum(dyh * xh * sz, axis=0, keepdims=True)

        @pl.when(i == 0)
        def _():
            dnw_ref[...] = jnp.zeros_like(dnw_ref)

        dnw_ref[...] += acc

    vec = pl.BlockSpec((1, DH), lambda i: (0, 0))
    return pl.pallas_call(
        body, grid=(t // tm,),
        in_specs=[pl.BlockSpec(memory_space=pl.ANY), pl.BlockSpec((tm, DQ), lambda i: (i, 1)),
                  pl.BlockSpec((tm, DQ), lambda i: (i, 0)), pl.BlockSpec((tm, DQ), lambda i: (i, 5)), vec],
        out_specs=[pl.BlockSpec((tm, DQ), lambda i: (i, 0)), pl.BlockSpec((tm, DQ), lambda i: (i, 5)), vec],
        out_shape=[jax.ShapeDtypeStruct((t, DQ), F32), jax.ShapeDtypeStruct(dproj.shape, dproj.dtype),
                   jax.ShapeDtypeStruct((1, DH), F32)],
        input_output_aliases={0: 1}, compiler_params=_cp(("arbitrary",)), name=name)(dproj, dy, o, proj, nw)


def _shifted(first, second, s, lane):
    if s == 0:
        return first
    return jnp.where(lane < LANES - s, pltpu.roll(first, LANES - s, 1), pltpu.roll(second, LANES - s, 1))


def unshard_cols(g8, w, widths, name):
    _, r, wp = g8.shape
    rb = _pick(r, (256, 128, 64, 32, 16))

    def body(g_ref, *o_refs):
        lane = lax.broadcasted_iota(jnp.int32, (rb, LANES), 1)
        zeros = jnp.zeros((rb, LANES), F32)

        def src(j, ta):
            if j >= NDEV or ta * LANES >= wp:
                return zeros
            return g_ref[j, :, ta * LANES:(ta + 1) * LANES].astype(F32)

        base = 0
        for o_ref, width in zip(o_refs, widths):
            for b in range(width // LANES):
                c0 = base + b * LANES
                if c0 >= NDEV * w:
                    tile = zeros
                else:
                    j0, o0 = divmod(c0, w)
                    n0 = min(w - o0, LANES)
                    ta, s = divmod(o0, LANES)
                    tile = _shifted(src(j0, ta), src(j0, ta + 1), s, lane)
                    if n0 < LANES:
                        nxt = pltpu.roll(src(j0 + 1, 0), n0, 1) if j0 + 1 < NDEV else zeros
                        tile = jnp.where(lane < n0, tile, nxt)
                o_ref[:, b * LANES:(b + 1) * LANES] = tile.astype(o_ref.dtype)
            base += width

    return pl.pallas_call(
        body, grid=(r // rb,), in_specs=[pl.BlockSpec((NDEV, rb, wp), lambda i: (0, i, 0))],
        out_specs=[pl.BlockSpec((rb, width), lambda i: (i, 0)) for width in widths],
        out_shape=[jax.ShapeDtypeStruct((r, width), g8.dtype) for width in widths],
        compiler_params=_cp(("parallel",)), name=name)(g8)


def shard_cols(parts, w, name):
    r = parts[0].shape[0]
    wp = _lane_pad(w)
    rb = _pick(r, (256, 128, 64, 32, 16))
    tiles_of = [p.shape[1] // LANES for p in parts]

    def body(*refs):
        p_refs, o_ref = refs[:-1], refs[-1]
        lane = lax.broadcasted_iota(jnp.int32, (rb, LANES), 1)
        zeros = jnp.zeros((rb, LANES), F32)

        def glob(tile_idx):
            for p_ref, n_tiles in zip(p_refs, tiles_of):
                if tile_idx < n_tiles:
                    return p_ref[:, tile_idx * LANES:(tile_idx + 1) * LANES].astype(F32)
                tile_idx -= n_tiles
            return zeros

        for j in range(NDEV):
            for a in range(wp // LANES):
                nv = min(w - a * LANES, LANES)
                tb, s = divmod(w * j + a * LANES, LANES)
                tile = _shifted(glob(tb), glob(tb + 1), s, lane)
                if nv < LANES:
                    tile = jnp.where(lane < nv, tile, 0.0)
                o_ref[j, :, a * LANES:(a + 1) * LANES] = tile.astype(o_ref.dtype)

    return pl.pallas_call(
        body, grid=(r // rb,), in_specs=[pl.BlockSpec((rb, p.shape[1]), lambda i: (i, 0)) for p in parts],
        out_specs=pl.BlockSpec((NDEV, rb, wp), lambda i: (0, i, 0)),
        out_shape=jax.ShapeDtypeStruct((NDEV, r, wp), GRAD_DTYPE),
        compiler_params=_cp(("parallel",)), name=name)(*parts)


def _me_and_peers():
    mx, my, mc = lax.axis_index("x"), lax.axis_index("y"), lax.axis_index("c")
    me = 4 * mx + 2 * my + mc
    peers = []
    for kk in range(1, NDEV):
        px = 1 - mx if kk & 4 else mx
        py = 1 - my if kk & 2 else my
        pc = 1 - mc if kk & 1 else mc
        peers.append(((px, py, pc), 4 * px + 2 * py + pc))
    return me, peers


def _push_copies(x_refs, o_refs, send_sems, recv_sems, local_sems, scatter):
    me, peers = _me_and_peers()
    npeer = NDEV - 1
    local, sends, recvs = [], [], []
    for a, (x_ref, o_ref) in enumerate(zip(x_refs, o_refs)):
        local.append(pltpu.make_async_copy(x_ref.at[me] if scatter else x_ref, o_ref.at[me], local_sems.at[a]))
        for kk, (peer, pidx) in enumerate(peers):
            src = x_ref.at[pidx] if scatter else x_ref
            sems = dict(send_sem=send_sems.at[a * npeer + kk], recv_sem=recv_sems.at[a * npeer + kk],
                        device_id=peer, device_id_type=pl.DeviceIdType.MESH)
            sends.append(pltpu.make_async_remote_copy(src_ref=src, dst_ref=o_ref.at[me], **sems))
            recvs.append(pltpu.make_async_remote_copy(src_ref=src, dst_ref=o_ref.at[pidx], **sems))
    return local, sends, recvs


def _push_start(copies):
    local, sends, _ = copies
    for cp in local + sends:
        cp.start()


def _push_finish(copies):
    local, sends, recvs = copies
    for cp in recvs:
        cp.wait_recv()
    for cp in sends:
        cp.wait_send()
    for cp in local:
        cp.wait()


def _push_out_shapes(xs, scatter):
    return [jax.ShapeDtypeStruct(x.shape if scatter else (NDEV,) + x.shape, x.dtype) for x in xs]


def _push_sems(n):
    return [pltpu.SemaphoreType.DMA((n * (NDEV - 1),)), pltpu.SemaphoreType.DMA((n * (NDEV - 1),)),
            pltpu.SemaphoreType.DMA((n,))]


def _push_to_all(xs, name, scatter):
    n = len(xs)

    def body(*refs):
        copies = _push_copies(refs[:n], refs[n:2 * n], *refs[2 * n:], scatter=scatter)
        _push_start(copies)
        _push_finish(copies)

    any_spec = pl.BlockSpec(memory_space=pl.ANY)
    return pl.pallas_call(
        body, in_specs=[any_spec] * n, out_specs=[any_spec] * n, out_shape=_push_out_shapes(xs, scatter),
        scratch_shapes=_push_sems(n), name=name)(*xs)


def adamw(recv, w, m, v, name):
    rows, cols = w.shape
    c1 = 1.0 - ADAM_B1 ** ADAM_STEP
    c2 = 1.0 - ADAM_B2 ** ADAM_STEP
    cap = ADAM_BLOCK_BYTES // (NDEV * cols * 4)
    rb = _pick(rows, [p for p in (2048, 1024, 512, 256, 128, 64, 32, 16, 8) if p <= cap])

    def body(r_ref, w_ref, m_ref, v_ref, g_ref, d_ref, m2_ref, v2_ref):
        g = r_ref[0].astype(F32)
        for j in range(1, NDEV):
            g = g + r_ref[j].astype(F32)
        m2 = ADAM_B1 * m_ref[...] + (1.0 - ADAM_B1) * g
        v2 = ADAM_B2 * v_ref[...] + (1.0 - ADAM_B2) * (g * g)
        g_ref[...] = g
        m2_ref[...] = m2
        v2_ref[...] = v2
        d_ref[...] = -ADAM_LR * ((m2 / c1) / (jnp.sqrt(v2 / c2) + ADAM_EPS) + ADAM_WD * w_ref[...])

    blk = pl.BlockSpec((rb, cols), lambda i: (i, 0))
    return pl.pallas_call(
        body, grid=(rows // rb,),
        in_specs=[pl.BlockSpec((NDEV, rb, cols), lambda i: (0, i, 0)), blk, blk, blk],
        out_specs=[blk, blk, blk, blk], out_shape=[jax.ShapeDtypeStruct((rows, cols), F32)] * 4,
        compiler_params=_cp(("parallel",)), name=name)(recv, w, m, v)


SMALL_SHARDED = ("meta_tokens", "conv_dw_w", "dn_conv_w")
SMALL_REPLICATED = ("norm_mix_w", "conv_dw_b", "conv_ln_w", "conv_ln_b", "dn_A_log", "dn_dt_bias", "dn_norm_w",
                    "norm_ffn_w", "final_norm_w")
PARAM_ORDER = ("meta_tokens", "norm_mix_w", "w_in", "conv_dw_w", "conv_dw_b", "conv_ln_w", "conv_ln_b", "dn_conv_w",
               "dn_A_log", "dn_dt_bias", "dn_norm_w", "w_out", "norm_ffn_w", "ffn_w_gu", "ffn_w_down", "final_norm_w")


def _pack_small(parts, axis):
    flat = jnp.concatenate(parts, axis=axis)
    n = flat.shape[axis]
    total = -(-n // (8 * LANES)) * (8 * LANES)
    pad = [(0, 0)] * flat.ndim
    pad[axis] = (0, total - n)
    flat = jnp.pad(flat, pad)
    return flat.reshape(flat.shape[:axis] + (total // LANES, LANES))


def _unshard_last(g8):
    moved = jnp.moveaxis(g8, 0, -2)
    return moved.reshape(moved.shape[:-2] + (-1,))


def _per_destination_last(full):
    split = full.reshape(full.shape[:-1] + (NDEV, full.shape[-1] // NDEV))
    return jnp.moveaxis(split, -2, 0).reshape(NDEV, -1)


def _lane_row(vec4, width):
    return jnp.pad(vec4, (NH, width - 2 * NH))[None]


def kernel(x, meta_tokens, norm_mix_w, w_in, conv_dw_w, conv_dw_b, conv_ln_w, conv_ln_b, dn_conv_w, dn_A_log, dn_dt_bias, dn_norm_w, w_out, norm_ffn_w, ffn_w_gu, ffn_w_down, final_norm_w, loss_target, m_meta_tokens, m_norm_mix_w, m_w_in, m_conv_dw_w, m_conv_dw_b, m_conv_ln_w, m_conv_ln_b, m_dn_conv_w, m_dn_A_log, m_dn_dt_bias, m_dn_norm_w, m_w_out, m_norm_ffn_w, m_ffn_w_gu, m_ffn_w_down, m_final_norm_w, v_meta_tokens, v_norm_mix_w, v_w_in, v_conv_dw_w, v_conv_dw_b, v_conv_ln_w, v_conv_ln_b, v_dn_conv_w, v_dn_A_log, v_dn_dt_bias, v_dn_norm_w, v_w_out, v_norm_ffn_w, v_ffn_w_gu, v_ffn_w_down, v_final_norm_w):
    weights = dict(meta_tokens=meta_tokens, norm_mix_w=norm_mix_w, w_in=w_in, conv_dw_w=conv_dw_w, conv_dw_b=conv_dw_b,
                   conv_ln_w=conv_ln_w, conv_ln_b=conv_ln_b, dn_conv_w=dn_conv_w, dn_A_log=dn_A_log,
                   dn_dt_bias=dn_dt_bias, dn_norm_w=dn_norm_w, w_out=w_out, norm_ffn_w=norm_ffn_w, ffn_w_gu=ffn_w_gu,
                   ffn_w_down=ffn_w_down, final_norm_w=final_norm_w)
    m_in = dict(meta_tokens=m_meta_tokens, norm_mix_w=m_norm_mix_w, w_in=m_w_in, conv_dw_w=m_conv_dw_w,
                conv_dw_b=m_conv_dw_b, conv_ln_w=m_conv_ln_w, conv_ln_b=m_conv_ln_b, dn_conv_w=m_dn_conv_w,
                dn_A_log=m_dn_A_log, dn_dt_bias=m_dn_dt_bias, dn_norm_w=m_dn_norm_w, w_out=m_w_out,
                norm_ffn_w=m_norm_ffn_w, ffn_w_gu=m_ffn_w_gu, ffn_w_down=m_ffn_w_down, final_norm_w=m_final_norm_w)
    v_in = dict(meta_tokens=v_meta_tokens, norm_mix_w=v_norm_mix_w, w_in=v_w_in, conv_dw_w=v_conv_dw_w,
                conv_dw_b=v_conv_dw_b, conv_ln_w=v_conv_ln_w, conv_ln_b=v_conv_ln_b, dn_conv_w=v_dn_conv_w,
                dn_A_log=v_dn_A_log, dn_dt_bias=v_dn_dt_bias, dn_norm_w=v_dn_norm_w, w_out=v_w_out,
                norm_ffn_w=v_norm_ffn_w, ffn_w_gu=v_ffn_w_gu, ffn_w_down=v_ffn_w_down, final_norm_w=v_final_norm_w)

    depth = w_in.shape[0]
    seq = x.shape[1]
    t = _padded_rows(seq)
    rows_d = depth * D
    win_w, gu_w = w_in.shape[2], ffn_w_gu.shape[2]
    win_wp, gu_wp = _lane_pad(win_w), _lane_pad(gu_w)

    def pad_cols(a, wp):
        return jnp.pad(a, ((0, 0), (0, 0), (0, wp - a.shape[2]))).reshape(rows_d, wp)

    def rows2d(a):
        return a.reshape(-1, a.shape[2])

    small_shards = [weights[n] for n in SMALL_SHARDED]
    win_p = pad_cols(w_in, win_wp).astype(MXU_DTYPE).reshape(depth, D, win_wp)
    gu_p = pad_cols(ffn_w_gu, gu_wp).astype(MXU_DTYPE).reshape(depth, D, gu_wp)
    wout_b, wdown_b = w_out.astype(MXU_DTYPE), ffn_w_down.astype(MXU_DTYPE)
    layer_shards = lambda l: [win_p[l], gu_p[l], wout_b[l], wdown_b[l]]

    def whole_weights(got):
        g_win, g_gu, g_wout, g_down = got
        main, gate_cols = unshard_cols(g_win, win_w, [PROJ_MAIN, GATE_W], "unshard_w_in")
        wg, wu = unshard_cols(g_gu, gu_w, [DFF, DFF], "unshard_w_gu")
        return dict(main=main, gate=gate_cols, wg=wg, wu=wu, out=g_wout.reshape(D, D), down=g_down.reshape(DFF, D))

    *got0, g_small = _push_to_all(layer_shards(0) + [_pack_small([s.reshape(-1) for s in small_shards], 0)],
                                  "gather_first", scatter=False)
    wts = [whole_weights(got0)]
    small_flat, off, small_full = g_small.reshape(NDEV, -1), 0, {}
    for n, s in zip(SMALL_SHARDED, small_shards):
        small_full[n] = _unshard_last(small_flat[:, off:off + s.size].reshape((NDEV,) + s.shape))
        off += s.size
    cdw32 = jnp.pad(small_full["conv_dw_w"], ((0, 0), (0, CONV_HALO - CONV_W), (0, 0)))
    dcw8 = jnp.pad(small_full["dn_conv_w"], ((0, 0), (0, DN_HALO - DN_W), (0, 0)))

    h = jnp.concatenate([jnp.zeros((FRONT, D), F32), small_full["meta_tokens"], x[0],
                         jnp.zeros((t - HEAD - seq, D), F32)], axis=0)
    tgt = jnp.pad(loss_target[0], ((HEAD, t - HEAD - seq), (0, 0)))

    saved = []
    for l in range(depth):
        nmw, nfw = norm_mix_w[l][None], norm_ffn_w[l][None]
        cdb, clw, clb = conv_dw_b[l][None], conv_ln_w[l][None], conv_ln_b[l][None]
        alog, dtb, dnw = _lane_row(dn_A_log[l], GATE_W), _lane_row(dn_dt_bias[l], GATE_W), dn_norm_w[l][None]
        wl = wts[l]
        proj = mm([h], wl["main"], a_fn=_rms_apply, a_vecs=[nmw], name="mm_proj")
        pg = mm([h], wl["gate"], a_fn=_rms_apply, a_vecs=[nmw], name="mm_proj_gate")
        ybuf, u1 = conv_fwd(proj, cdw32[l], cdb, clw, clb, seq, "conv_fwd")
        qkv = dn_pre_fwd(proj, dcw8[l], "dn_pre_fwd")
        bg = gate_fwd(pg, alog, dtb, seq, "gate_fwd")
        if l + 1 < depth:
            (o, s_hist, m_hist, u_all, w_all), got = delta_fwd(qkv, bg, "delta_fwd_gather", push=layer_shards(l + 1))
            wts.append(whole_weights(got))
        else:
            (o, s_hist, m_hist, u_all, w_all), _ = delta_fwd(qkv, bg, "delta_fwd")
        ybuf = dn_post_fwd(ybuf, o, proj, dnw, "dn_post_fwd")
        h_mid = mm(ybuf, wl["out"], extras=[h], out_fn=_add, name="mm_out")
        gate = mm([h_mid], wl["wg"], a_fn=_rms_apply, a_vecs=[nfw], name="mm_gate")
        up = mm([h_mid], wl["wu"], a_fn=_rms_apply, a_vecs=[nfw], name="mm_up")
        h_out = mm([gate, up], wl["down"], a_fn=_swiglu, extras=[h_mid], out_fn=_add, tile_cap=ROW_TILE, name="mm_down")
        saved.append(dict(h=h, proj=proj, pg=pg, ybuf=ybuf, u1=u1, qkv=qkv, bg=bg, o=o, s_hist=s_hist,
                          m_hist=m_hist, u_all=u_all, w_all=w_all, h_mid=h_mid, gate=gate, up=up,
                          nmw=nmw, nfw=nfw, clw=clw, clb=clb, alog=alog, dtb=dtb, dnw=dnw))
        h = h_out

    dh, loss_part, d_final = loss_bwd(h, tgt, final_norm_w[None], seq, "loss_bwd")
    loss = lax.psum(loss_part[0, 0], MESH_AXES)

    per_layer = ("norm_mix_w", "conv_dw_w", "conv_dw_b", "conv_ln_w", "conv_ln_b", "dn_conv_w", "dn_A_log", "dn_dt_bias",
                 "dn_norm_w", "norm_ffn_w")
    grads = {n: [None] * depth for n in per_layer}
    received = [None] * depth
    pending = None
    dw_mm = lambda a, b, name, **kw: mm(a, b, ta=True, out_dtypes=(GRAD_DTYPE,), name=name, **kw)
    for l in reversed(range(depth)):
        s, wl = saved[l], wts[l]
        dgate, dup = mm(dh, wl["down"], tb=True, extras=[s["gate"], s["up"]], out_fn=_swiglu_bwd,
                        out_dtypes=(MXU_DTYPE, MXU_DTYPE), tile_cap=ROW_TILE, name="mm_down_dx")
        d_down = dw_mm([s["gate"], s["up"]], dh, "mm_down_dw", a_fn=_swiglu, tile_cap=ROW_TILE)
        dh_mid, dnfw = mm(dup, wl["wu"], tb=True, extras=[mm(dgate, wl["wg"], tb=True, name="mm_gate_dx"), s["h_mid"], dh],
                          o_vecs=[s["nfw"]], out_fn=_rms_bwd_tail, sum_last=True, tile_cap=ROW_TILE, name="mm_up_dx")
        d_wg = dw_mm([s["h_mid"]], dgate, "mm_gate_dw", a_fn=_rms_apply, a_vecs=[s["nfw"]])
        d_wu = dw_mm([s["h_mid"]], dup, "mm_up_dw", a_fn=_rms_apply, a_vecs=[s["nfw"]])
        dy = mm(dh_mid, wl["out"], tb=True, name="mm_out_dx")
        d_out = dw_mm(s["ybuf"], dh_mid, "mm_out_dw")
        dproj, dcdw, dcdb, dclw, dclb = conv_bwd(dy, s["u1"], s["proj"], cdw32[l], s["clw"], s["clb"], seq, "conv_bwd")
        do, dproj, ddnw = dn_post_bwd(dproj, dy, s["o"], s["proj"], s["dnw"], "dn_post_bwd")
        delta_args = (s["qkv"], s["bg"], do, s["s_hist"], s["m_hist"], s["u_all"], s["w_all"])
        if pending is None:
            dqkv, dbg, _ = delta_bwd(*delta_args, "delta_bwd")
        else:
            dqkv, dbg, received[l + 1] = delta_bwd(*delta_args, "delta_bwd_exchange", push=pending)
        dproj, ddcw = dn_pre_bwd(dproj, dqkv, s["proj"], dcw8[l], "dn_pre_bwd")
        dpg, dalog, ddtb = gate_bwd(dbg, s["pg"], s["alog"], s["dtb"], seq, "gate_bwd")
        dhn_gate = mm(dpg, wl["gate"], tb=True, name="mm_proj_gate_dx")
        dh_in, dnmw = mm(dproj, wl["main"], tb=True, extras=[dhn_gate, s["h"], dh_mid], o_vecs=[s["nmw"]],
                         out_fn=_rms_bwd_tail, sum_last=True, tile_cap=ROW_TILE, name="mm_proj_dx")
        d_main = dw_mm([s["h"]], dproj, "mm_proj_dw", a_fn=_rms_apply, a_vecs=[s["nmw"]])
        d_gate_cols = dw_mm([s["h"]], dpg, "mm_proj_gate_dw", a_fn=_rms_apply, a_vecs=[s["nmw"]])
        pending = [shard_cols([d_main, d_gate_cols], win_w, "shard_w_in"), shard_cols([d_wg, d_wu], gu_w, "shard_w_gu"),
                   d_out.reshape(NDEV, D // NDEV, D), d_down.reshape(NDEV, DFF // NDEV, D)]
        dh = dh_in
        grads["norm_mix_w"][l] = dnmw[0]
        grads["norm_ffn_w"][l] = dnfw[0]
        grads["conv_dw_w"][l] = dcdw[:CONV_W]
        grads["conv_dw_b"][l] = dcdb[0]
        grads["conv_ln_w"][l] = dclw[0]
        grads["conv_ln_b"][l] = dclb[0]
        grads["dn_conv_w"][l] = ddcw[:DN_W]
        grads["dn_A_log"][l] = dalog[0, NH:2 * NH]
        grads["dn_dt_bias"][l] = ddtb[0, NH:2 * NH]
        grads["dn_norm_w"][l] = ddnw[0]

    grad_x = dh[HEAD:HEAD + seq][None]
    full = {n: jnp.stack(g) for n, g in grads.items()}
    full["meta_tokens"] = dh[FRONT:HEAD]
    full["final_norm_w"] = d_final[0]

    send_small = _pack_small(
        [_per_destination_last(full[n]) for n in SMALL_SHARDED]
        + [jnp.broadcast_to(full[n].reshape(1, -1), (NDEV, full[n].size)) for n in SMALL_REPLICATED], 1)
    *received[0], r_small = _push_to_all(pending + [send_small], "exchange_last", scatter=True)
    r_win, r_gu, r_wout, r_down = (jnp.concatenate([received[l][k] for l in range(depth)], axis=1) for k in range(4))

    small_names = SMALL_SHARDED + SMALL_REPLICATED
    pack_local = lambda tree: _pack_small([tree[n].reshape(-1) for n in small_names], 0)
    results = {}

    def run_adamw(name, recv, prep, finish):
        outs = adamw(recv, prep(weights[name]), prep(m_in[name]), prep(v_in[name]), "adamw_" + name)
        results[name] = [finish(o) for o in outs]

    run_adamw("w_in", r_win, lambda a: pad_cols(a, win_wp),
              lambda o: o[:, :win_w].reshape(depth, D, win_w))
    run_adamw("ffn_w_gu", r_gu, lambda a: pad_cols(a, gu_wp), lambda o: o[:, :gu_w].reshape(depth, D, gu_w))
    run_adamw("w_out", r_wout, rows2d, lambda o: o.reshape(w_out.shape))
    run_adamw("ffn_w_down", r_down, rows2d, lambda o: o.reshape(ffn_w_down.shape))
    small_outs = adamw(r_small, pack_local(weights), pack_local(m_in), pack_local(v_in), "adamw_small")
    for kind in range(4):
        flat, off = small_outs[kind].reshape(-1), 0
        for n in small_names:
            wgt = weights[n]
            results.setdefault(n, [None] * 4)[kind] = flat[off:off + wgt.size].reshape(wgt.shape)
            off += wgt.size

    return (loss, grad_x, *[results[n][0] for n in PARAM_ORDER], *[results[n][1] for n in PARAM_ORDER],
            *[results[n][2] for n in PARAM_ORDER], *[results[n][3] for n in PARAM_ORDER])
```

```python
import jax
import jax.numpy as jnp
from jax import lax
from jax.experimental import pallas as pl
from jax.experimental.pallas import tpu as pltpu

F32 = jnp.float32
MXU_DTYPE = jnp.bfloat16

D = 1024
N_META = 16
CHUNK = 64
CHUNK_LOG2 = 6
INV_BASE_LOG2 = 3
FRONT = CHUNK - N_META
HEAD = CHUNK
CONV_CH = 512
CONV_W = 31
CONV_HALO = 32
NH = 4
DH = 128
DQ = NH * DH
DN_W = 4
DN_HALO = 8
DFF = 2816
PROJ_MAIN = 3072
D_IN = 3080
GATE_W = 128
LANES = 128
NDEV = 8
NORM_EPS = 1e-6
LN_EPS = 1e-5
L2_EPS = 1e-6
VMEM_LIMIT_V7X = 48 * 1024 * 1024
ROW_TILE = 640
ROW_TILE_SMALL = 128
MM_TILES = (1408, 1280, 1024, 640, 512, 256, 128)
MM_SUB = 4
GRAD_DTYPE = jnp.bfloat16
DELTA_CHUNKS = (4, 2, 1)
ADAM_BLOCK_BYTES = 8 * 1024 * 1024

ADAM_LR = 0.001
ADAM_B1 = 0.9
ADAM_B2 = 0.999
ADAM_EPS = 1e-08
ADAM_WD = 0.01
ADAM_STEP = 10

MESH_AXES = ("x", "y", "c")
NN = ((1,), (0,))
NT = ((1,), (1,))
TN = ((0,), (0,))

assert 1 << CHUNK_LOG2 == CHUNK


def _row_tile(t):
    return ROW_TILE if t % ROW_TILE == 0 else ROW_TILE_SMALL


def _padded_rows(seq):
    n = HEAD + seq
    tm = ROW_TILE if n >= 4 * ROW_TILE else ROW_TILE_SMALL
    return -(-n // tm) * tm


def _pick(n, prefs):
    for p in prefs:
        if n % p == 0:
            return p
    return n


def _lane_pad(n):
    return -(-n // LANES) * LANES


def _cp(sem):
    return pltpu.CompilerParams(dimension_semantics=sem, vmem_limit_bytes=VMEM_LIMIT_V7X)


def _sigmoid(x):
    return 1.0 / (1.0 + jnp.exp(-x))


def _softplus(x):
    return jnp.maximum(x, 0.0) + jnp.log(1.0 + jnp.exp(-jnp.abs(x)))


def _valid_rows(i, tm, seq, width, first=FRONT):
    rows = i * tm + lax.broadcasted_iota(jnp.int32, (tm, width), 0)
    return jnp.logical_and(rows >= first, rows < HEAD + seq)


def _dot(a, b, dims):
    return lax.dot_general(a, b, (dims, ((), ())), preferred_element_type=F32)


def _split(x, n):
    out, r = [], x
    for _ in range(n):
        p = r.astype(MXU_DTYPE)
        out.append(p)
        r = r - p.astype(F32)
    return out


def _mm1(a, b, dims):
    return _dot(a.astype(MXU_DTYPE), b.astype(MXU_DTYPE), dims)


def _mm1_many(a_list, b_list, dims):
    return [_mm1(a, b, dims) for a, b in zip(a_list, b_list)]


def _mm3_many(a_list, b_list, dims):
    sa = [_split(a, 2) for a in a_list]
    sb = [_split(b, 2) for b in b_list]
    hh = [_dot(x[0], y[0], dims) for x, y in zip(sa, sb)]
    hl = [_dot(x[0], y[1], dims) for x, y in zip(sa, sb)]
    lh = [_dot(x[1], y[0], dims) for x, y in zip(sa, sb)]
    return [p + (q + r) for p, q, r in zip(hh, hl, lh)]


def _mmx(e, b, dims):
    e = e.astype(MXU_DTYPE)
    b1, b2, b3 = _split(b, 3)
    return _dot(e, b1, dims) + (_dot(e, b2, dims) + _dot(e, b3, dims))


def mm(a, b, *, ta=False, tb=False, a_fn=None, a_vecs=(), extras=(), out_fn=None, out_dtypes=(F32,), tile_cap=None,
       name):
    a_list = list(a) if isinstance(a, (list, tuple)) else [a]
    (k_dim, m_dim) = a_list[0].shape if ta else a_list[0].shape[::-1]
    n_dim = b.shape[0] if tb else b.shape[1]
    assert (b.shape[1] if tb else b.shape[0]) == k_dim
    capped = MM_TILES if tile_cap is None else tuple(p for p in MM_TILES if p <= tile_cap)
    tm = _pick(m_dim, MM_TILES if ta else capped)
    tk = _pick(k_dim, capped if ta else MM_TILES)
    tn = _pick(n_dim, MM_TILES)
    nk = k_dim // tk
    na, nv, ne, no = len(a_list), len(a_vecs), len(extras), len(out_dtypes)
    dims = ((0,) if ta else (1,), (1,) if tb else (0,))
    nsub = MM_SUB if a_fn is not None and not ta and tm % (MM_SUB * 16) == 0 else 1
    sub = tm // nsub

    def body(*refs):
        a_refs, v_refs, b_ref = refs[:na], refs[na:na + nv], refs[na + nv]
        e_refs = refs[na + nv + 1:na + nv + 1 + ne]
        o_refs = refs[na + nv + 1 + ne:na + nv + 1 + ne + no]
        acc_ref = refs[-1] if nk > 1 else None
        k = pl.program_id(2)

        def left(rows):
            tiles = [r[rows, :] for r in a_refs]
            return tiles[0] if a_fn is None else a_fn(*tiles, *[v[...] for v in v_refs])

        def finish(acc, rows):
            outs = (acc,) if out_fn is None else out_fn(acc, *[e[rows, :] for e in e_refs])
            for o_ref, out in zip(o_refs, outs):
                o_ref[rows, :] = out.astype(o_ref.dtype)

        if nk > 1:
            @pl.when(k == 0)
            def _():
                acc_ref[...] = jnp.zeros_like(acc_ref)

        for r in range(nsub):
            rows = slice(r * sub, (r + 1) * sub) if nsub > 1 else slice(None)
            prod = _mm1(left(rows), b_ref[...], dims)
            if nk == 1:
                finish(prod, rows)
            else:
                acc_ref[rows, :] += prod

        if nk > 1:
            @pl.when(k == nk - 1)
            def _():
                finish(acc_ref[...], slice(None))

    if ta:
        a_spec = pl.BlockSpec((tk, tm), lambda i, j, k: (k, i))
        v_spec = pl.BlockSpec((1, tm), lambda i, j, k: (0, i))
    else:
        a_spec = pl.BlockSpec((tm, tk), lambda i, j, k: (i, k))
        v_spec = pl.BlockSpec((1, tk), lambda i, j, k: (0, k))
    b_spec = pl.BlockSpec((tn, tk), lambda i, j, k: (j, k)) if tb else pl.BlockSpec((tk, tn), lambda i, j, k: (k, j))
    o_spec = pl.BlockSpec((tm, tn), lambda i, j, k: (i, j))
    outs = pl.pallas_call(
        body, grid=(m_dim // tm, n_dim // tn, nk),
        in_specs=[a_spec] * na + [v_spec] * nv + [b_spec] + [o_spec] * ne, out_specs=[o_spec] * no,
        out_shape=[jax.ShapeDtypeStruct((m_dim, n_dim), dt) for dt in out_dtypes],
        scratch_shapes=[pltpu.VMEM((tm, tn), F32)] if nk > 1 else [],
        compiler_params=_cp(("parallel", "parallel", "arbitrary")), name=name)(*a_list, *a_vecs, b, *extras)
    return outs[0] if no == 1 else outs


def _rms_apply(x, w):
    assert x.shape[-1] == D
    return x * lax.rsqrt(jnp.mean(x * x, axis=-1, keepdims=True) + NORM_EPS) * w


def _swiglu(g, u):
    return g * _sigmoid(g) * u


def _swiglu_bwd(dact, g, u):
    sg = _sigmoid(g)
    return dact * u * (sg * (1.0 + g * (1.0 - sg))), dact * (g * sg)


def _add(acc, r):
    return (acc + r,)


def rms_bwd(dy, h, w, dres, name):
    t = h.shape[0]
    tm = _row_tile(t)

    def body(dy_ref, h_ref, w_ref, dres_ref, dh_ref, dw_ref):
        i = pl.program_id(0)
        x = h_ref[...]
        r = lax.rsqrt(jnp.mean(x * x, axis=-1, keepdims=True) + NORM_EPS)
        xh = x * r
        g = dy_ref[...] * w_ref[...]
        dh_ref[...] = dres_ref[...] + r * (g - xh * jnp.mean(g * xh, axis=-1, keepdims=True))

        @pl.when(i == 0)
        def _():
            dw_ref[...] = jnp.zeros_like(dw_ref)

        dw_ref[...] += jnp.sum(dy_ref[...] * xh, axis=0, keepdims=True)

    row = pl.BlockSpec((tm, D), lambda i: (i, 0))
    vec = pl.BlockSpec((1, D), lambda i: (0, 0))
    return pl.pallas_call(
        body, grid=(t // tm,), in_specs=[row, row, vec, row], out_specs=[row, vec],
        out_shape=[jax.ShapeDtypeStruct((t, D), F32), jax.ShapeDtypeStruct((1, D), F32)],
        compiler_params=_cp(("arbitrary",)), name=name)(dy, h, w, dres)


def loss_bwd(h, tgt, w, seq, name):
    t = h.shape[0]
    tm = _row_tile(t)

    def body(h_ref, t_ref, w_ref, dh_ref, loss_ref, dw_ref):
        i = pl.program_id(0)
        x = h_ref[...]
        wv = w_ref[...]
        r = lax.rsqrt(jnp.mean(x * x, axis=-1, keepdims=True) + NORM_EPS)
        xh = x * r
        err = jnp.where(_valid_rows(i, tm, seq, D, HEAD), xh * wv - t_ref[...], 0.0)
        dy = err * (1.0 / D)
        g = dy * wv
        dh_ref[...] = r * (g - xh * jnp.mean(g * xh, axis=-1, keepdims=True))

        @pl.when(i == 0)
        def _():
            dw_ref[...] = jnp.zeros_like(dw_ref)
            loss_ref[...] = jnp.zeros_like(loss_ref)

        dw_ref[...] += jnp.sum(dy * xh, axis=0, keepdims=True)
        part = jnp.sum(jnp.sum(err * err, axis=1, keepdims=True), axis=0, keepdims=True) * (0.5 / D)
        loss_ref[...] += jnp.broadcast_to(part, loss_ref.shape)

    row = pl.BlockSpec((tm, D), lambda i: (i, 0))
    vec = pl.BlockSpec((1, D), lambda i: (0, 0))
    return pl.pallas_call(
        body, grid=(t // tm,), in_specs=[row, row, vec],
        out_specs=[row, pl.BlockSpec((1, LANES), lambda i: (0, 0)), vec],
        out_shape=[jax.ShapeDtypeStruct((t, D), F32), jax.ShapeDtypeStruct((1, LANES), F32),
                   jax.ShapeDtypeStruct((1, D), F32)],
        compiler_params=_cp(("arbitrary",)), name=name)(h, tgt, w)


def _layernorm_parts(u1):
    mu = jnp.mean(u1, axis=-1, keepdims=True)
    xc = u1 - mu
    rstd = lax.rsqrt(jnp.mean(xc * xc, axis=-1, keepdims=True) + LN_EPS)
    return xc * rstd, rstd


SUBLANES = 8
CONV_ROWS = 16


def _shift_copies(ext, sh, tm):
    for s in range(1, SUBLANES):
        sh[s - 1, :, :] = ext[pl.ds(s, tm + CONV_HALO - SUBLANES), :]


def _window(ext, sh, off, rows, start=0):
    s, m = off % SUBLANES, off // SUBLANES
    if s == 0:
        return ext[pl.ds(start + off, rows), :]
    return sh[s - 1, pl.ds(start + SUBLANES * m, rows), :]


def _shift_scratch(tm):
    return pltpu.VMEM((SUBLANES - 1, tm + CONV_HALO - SUBLANES, CONV_CH), F32)


def conv_fwd(proj, w32, b, lw, lb, seq, name):
    t = proj.shape[0]
    tm = _row_tile(t)

    def body(cv_ref, cg_ref, w_ref, b_ref, lw_ref, lb_ref, y_ref, u1_ref, ext, sh):
        i = pl.program_id(0)

        @pl.when(i == 0)
        def _():
            ext[0:CONV_HALO, :] = jnp.zeros((CONV_HALO, CONV_CH), F32)

        @pl.when(i > 0)
        def _():
            ext[0:CONV_HALO, :] = ext[tm:tm + CONV_HALO, :]

        ext[CONV_HALO:CONV_HALO + tm, :] = cv_ref[...] * _sigmoid(cg_ref[...])
        _shift_copies(ext, sh, tm)
        acc = jnp.broadcast_to(b_ref[...], (tm, CONV_CH))
        for j in range(CONV_W):
            acc = acc + w_ref[j:j + 1, :] * _window(ext, sh, CONV_HALO - (CONV_W - 1) + j, tm)
        u1_ref[...] = acc
        xh, _ = _layernorm_parts(acc)
        ln = xh * lw_ref[...] + lb_ref[...]
        y = ln * _sigmoid(ln)
        y_ref[...] = jnp.where(_valid_rows(i, tm, seq, CONV_CH), y, 0.0).astype(y_ref.dtype)

    half = lambda c: pl.BlockSpec((tm, CONV_CH), lambda i: (i, c))
    vec = pl.BlockSpec((1, CONV_CH), lambda i: (0, 0))
    return pl.pallas_call(
        body, grid=(t // tm,),
        in_specs=[half(0), half(1), pl.BlockSpec((CONV_HALO, CONV_CH), lambda i: (0, 0)), vec, vec, vec],
        out_specs=[half(0), half(0)],
        out_shape=[jax.ShapeDtypeStruct((t, D), MXU_DTYPE), jax.ShapeDtypeStruct((t, CONV_CH), F32)],
        scratch_shapes=[pltpu.VMEM((tm + CONV_HALO, CONV_CH), F32), _shift_scratch(tm)],
        compiler_params=_cp(("arbitrary",)), name=name)(proj, proj, w32, b, lw, lb)


def conv_bwd(dy, u1, proj, w32, lw, lb, seq, name):
    t = proj.shape[0]
    tm = _row_tile(t)
    nt = t // tm
    per = tm // CONV_HALO

    def body(dy_ref, u1_ref, cv_ref, cg_ref, cvp_ref, cgp_ref, w_ref, lw_ref, lb_ref,
             dp_ref, dw_ref, db_ref, dlw_ref, dlb_ref, ext_d, ext_u, sh_d, sh_u, du0_s, dw_acc):
        i = pl.program_id(0)
        tile = nt - 1 - i

        @pl.when(i == 0)
        def _():
            ext_d[tm:tm + CONV_HALO, :] = jnp.zeros((CONV_HALO, CONV_CH), F32)
            dw_acc[...] = jnp.zeros_like(dw_acc)
            db_ref[...] = jnp.zeros_like(db_ref)
            dlw_ref[...] = jnp.zeros_like(dlw_ref)
            dlb_ref[...] = jnp.zeros_like(dlb_ref)

        @pl.when(i > 0)
        def _():
            ext_d[tm:tm + CONV_HALO, :] = ext_d[0:CONV_HALO, :]

        xh, rstd = _layernorm_parts(u1_ref[...])
        lwv = lw_ref[...]
        ln = xh * lwv + lb_ref[...]
        sg = _sigmoid(ln)
        dln = jnp.where(_valid_rows(tile, tm, seq, CONV_CH), dy_ref[...], 0.0) * (sg * (1.0 + ln * (1.0 - sg)))
        dlw_ref[...] += jnp.sum(dln * xh, axis=0, keepdims=True)
        dlb_ref[...] += jnp.sum(dln, axis=0, keepdims=True)
        dxh = dln * lwv
        du1 = rstd * (dxh - jnp.mean(dxh, axis=-1, keepdims=True)
                      - xh * jnp.mean(dxh * xh, axis=-1, keepdims=True))
        db_ref[...] += jnp.sum(du1, axis=0, keepdims=True)
        ext_d[0:tm, :] = du1

        cv = cv_ref[...]
        sgc = _sigmoid(cg_ref[...])
        prev = cvp_ref[...] * _sigmoid(cgp_ref[...])
        ext_u[0:CONV_HALO, :] = jnp.where(tile > 0, prev, 0.0)
        ext_u[CONV_HALO:CONV_HALO + tm, :] = cv * sgc

        _shift_copies(ext_d, sh_d, tm)
        _shift_copies(ext_u, sh_u, tm)

        def row_block(rb, carry):
            r0 = pl.multiple_of(rb * CONV_ROWS, CONV_ROWS)
            du1_b = ext_d[pl.ds(r0, CONV_ROWS), :]
            acc = jnp.zeros((CONV_ROWS, CONV_CH), F32)
            for j in range(CONV_W):
                acc = acc + w_ref[j:j + 1, :] * _window(ext_d, sh_d, CONV_W - 1 - j, CONV_ROWS, r0)
                prod = du1_b * _window(ext_u, sh_u, CONV_HALO - (CONV_W - 1) + j, CONV_ROWS, r0)
                dw_acc[j] += prod[0:SUBLANES, :] + prod[SUBLANES:CONV_ROWS, :]
            du0_s[pl.ds(r0, CONV_ROWS), :] = acc
            return carry

        lax.fori_loop(0, tm // CONV_ROWS, row_block, 0)

        @pl.when(i == nt - 1)
        def _():
            dw_ref[...] = jnp.sum(dw_acc[...], axis=1)

        du0 = du0_s[...]
        dp_ref[:, 0:CONV_CH] = (du0 * sgc).astype(dp_ref.dtype)
        dp_ref[:, CONV_CH:2 * CONV_CH] = (du0 * cv * sgc * (1.0 - sgc)).astype(dp_ref.dtype)

    rev = lambda c: pl.BlockSpec((tm, CONV_CH), lambda i: (nt - 1 - i, c))
    prev = lambda c: pl.BlockSpec((CONV_HALO, CONV_CH), lambda i: (jnp.maximum((nt - 1 - i) * per - 1, 0), c))
    vec = pl.BlockSpec((1, CONV_CH), lambda i: (0, 0))
    wspec = pl.BlockSpec((CONV_HALO, CONV_CH), lambda i: (0, 0))
    return pl.pallas_call(
        body, grid=(nt,),
        in_specs=[rev(0), rev(0), rev(0), rev(1), prev(0), prev(1), wspec, vec, vec],
        out_specs=[pl.BlockSpec((tm, 2 * CONV_CH), lambda i: (nt - 1 - i, 0)), wspec, vec, vec, vec],
        out_shape=[jax.ShapeDtypeStruct((t, PROJ_MAIN), MXU_DTYPE), jax.ShapeDtypeStruct((CONV_HALO, CONV_CH), F32),
                   jax.ShapeDtypeStruct((1, CONV_CH), F32), jax.ShapeDtypeStruct((1, CONV_CH), F32),
                   jax.ShapeDtypeStruct((1, CONV_CH), F32)],
        scratch_shapes=[pltpu.VMEM((tm + CONV_HALO, CONV_CH), F32), pltpu.VMEM((tm + CONV_HALO, CONV_CH), F32),
                        _shift_scratch(tm), _shift_scratch(tm), pltpu.VMEM((tm, CONV_CH), F32),
                        pltpu.VMEM((CONV_HALO, SUBLANES, CONV_CH), F32)],
        compiler_params=_cp(("arbitrary",)), name=name)(dy, u1, proj, proj, proj, proj, w32, lw, lb)


def dn_pre_fwd(proj, w8, name):
    t = proj.shape[0]
    tm = _row_tile(t)

    def body(raw_ref, w_ref, o_ref, ext):
        g = pl.program_id(0)
        i = pl.program_id(1)

        @pl.when(i == 0)
        def _():
            ext[0:DN_HALO, :] = jnp.zeros((DN_HALO, DQ), F32)

        @pl.when(i > 0)
        def _():
            ext[0:DN_HALO, :] = ext[tm:tm + DN_HALO, :]

        ext[DN_HALO:DN_HALO + tm, :] = raw_ref[...]
        c = jnp.zeros((tm, DQ), F32)
        for j in range(DN_W):
            c = c + w_ref[j:j + 1, :] * ext[pl.ds(DN_HALO - (DN_W - 1) + j, tm), :]
        s = c * _sigmoid(c)
        scale = jnp.where(g == 0, DH ** -0.5, 1.0)
        for h in range(NH):
            sh = s[:, h * DH:(h + 1) * DH]
            r = lax.rsqrt(jnp.sum(sh * sh, axis=-1, keepdims=True) + L2_EPS)
            o_ref[:, h * DH:(h + 1) * DH] = jnp.where(g == 2, sh, sh * (r * scale))

    return pl.pallas_call(
        body, grid=(3, t // tm),
        in_specs=[pl.BlockSpec((tm, DQ), lambda g, i: (i, 2 + g)), pl.BlockSpec((DN_HALO, DQ), lambda g, i: (0, g))],
        out_specs=pl.BlockSpec((tm, DQ), lambda g, i: (i, g)),
        out_shape=jax.ShapeDtypeStruct((t, 3 * DQ), F32),
        scratch_shapes=[pltpu.VMEM((tm + DN_HALO, DQ), F32)],
        compiler_params=_cp(("arbitrary", "arbitrary")), name=name)(proj, w8)


def dn_pre_bwd(dproj, dqkv, proj, w8, name):
    t = proj.shape[0]
    tm = _row_tile(t)
    nt = t // tm
    per = tm // DN_HALO

    def body(dp_in, d_ref, raw_ref, rawp_ref, w_ref, dp_ref, dw_ref, ext_d, ext_r):
        del dp_in
        g = pl.program_id(0)
        i = pl.program_id(1)
        tile = nt - 1 - i

        @pl.when(i == 0)
        def _():
            ext_d[tm:tm + DN_HALO, :] = jnp.zeros((DN_HALO, DQ), F32)
            dw_ref[...] = jnp.zeros_like(dw_ref)

        @pl.when(i > 0)
        def _():
            ext_d[tm:tm + DN_HALO, :] = ext_d[0:DN_HALO, :]

        ext_r[0:DN_HALO, :] = jnp.where(tile > 0, rawp_ref[...], 0.0)
        ext_r[DN_HALO:DN_HALO + tm, :] = raw_ref[...]
        c = jnp.zeros((tm, DQ), F32)
        for j in range(DN_W):
            c = c + w_ref[j:j + 1, :] * ext_r[pl.ds(DN_HALO - (DN_W - 1) + j, tm), :]
        sg = _sigmoid(c)
        s = c * sg
        scale = jnp.where(g == 0, DH ** -0.5, 1.0)
        for h in range(NH):
            sl = slice(h * DH, (h + 1) * DH)
            sh = s[:, sl]
            dn = d_ref[:, sl]
            r = lax.rsqrt(jnp.sum(sh * sh, axis=-1, keepdims=True) + L2_EPS)
            unit = sh * r
            dsn = (r * scale) * (dn - unit * jnp.sum(dn * unit, axis=-1, keepdims=True))
            ds = jnp.where(g == 2, dn, dsn)
            ext_d[0:tm, sl] = ds * (sg[:, sl] * (1.0 + c[:, sl] * (1.0 - sg[:, sl])))
        dc = ext_d[0:tm, :]
        draw = jnp.zeros((tm, DQ), F32)
        for j in range(DN_W):
            draw = draw + w_ref[j:j + 1, :] * ext_d[pl.ds(DN_W - 1 - j, tm), :]
            dw_ref[j:j + 1, :] += jnp.sum(
                dc * ext_r[pl.ds(DN_HALO - (DN_W - 1) + j, tm), :], axis=0, keepdims=True)
        dp_ref[...] = draw.astype(dp_ref.dtype)

    return pl.pallas_call(
        body, grid=(3, nt),
        in_specs=[pl.BlockSpec(memory_space=pl.ANY),
                  pl.BlockSpec((tm, DQ), lambda g, i: (nt - 1 - i, g)),
                  pl.BlockSpec((tm, DQ), lambda g, i: (nt - 1 - i, 2 + g)),
                  pl.BlockSpec((DN_HALO, DQ), lambda g, i: (jnp.maximum((nt - 1 - i) * per - 1, 0), 2 + g)),
                  pl.BlockSpec((DN_HALO, DQ), lambda g, i: (0, g))],
        out_specs=[pl.BlockSpec((tm, DQ), lambda g, i: (nt - 1 - i, 2 + g)),
                   pl.BlockSpec((DN_HALO, DQ), lambda g, i: (0, g))],
        out_shape=[jax.ShapeDtypeStruct(dproj.shape, dproj.dtype), jax.ShapeDtypeStruct((DN_HALO, 3 * DQ), F32)],
        scratch_shapes=[pltpu.VMEM((tm + DN_HALO, DQ), F32), pltpu.VMEM((tm + DN_HALO, DQ), F32)],
        input_output_aliases={0: 0},
        compiler_params=_cp(("arbitrary", "arbitrary")), name=name)(dproj, dqkv, proj, proj, w8)


def gate_fwd(pg, alog, dtb, seq, name):
    t = pg.shape[0]
    tm = _row_tile(t)

    def body(x_ref, al_ref, dt_ref, o_ref):
        i = pl.program_id(0)
        x = x_ref[...]
        lane = lax.broadcasted_iota(jnp.int32, (tm, GATE_W), 1)
        gg = -jnp.exp(al_ref[...]) * _softplus(x + dt_ref[...])
        out = jnp.where(lane < NH, _sigmoid(x), jnp.where(lane < 2 * NH, gg, 0.0))
        o_ref[...] = jnp.where(_valid_rows(i, tm, seq, GATE_W), out, 0.0)

    row = pl.BlockSpec((tm, GATE_W), lambda i: (i, 0))
    vec = pl.BlockSpec((1, GATE_W), lambda i: (0, 0))
    return pl.pallas_call(
        body, grid=(t // tm,), in_specs=[row, vec, vec], out_specs=row,
        out_shape=jax.ShapeDtypeStruct((t, GATE_W), F32), compiler_params=_cp(("parallel",)), name=name)(pg, alog, dtb)


def gate_bwd(dbg, pg, alog, dtb, seq, name):
    t = pg.shape[0]
    tm = _row_tile(t)

    def body(d_ref, x_ref, al_ref, dt_ref, o_ref, dal_ref, ddt_ref):
        i = pl.program_id(0)
        x = x_ref[...]
        lane = lax.broadcasted_iota(jnp.int32, (tm, GATE_W), 1)
        d = jnp.where(_valid_rows(i, tm, seq, GATE_W), d_ref[...], 0.0)
        beta = _sigmoid(x)
        xs = x + dt_ref[...]
        e = -jnp.exp(al_ref[...])
        is_g = jnp.logical_and(lane >= NH, lane < 2 * NH)
        da = jnp.where(is_g, d * e * _sigmoid(xs), 0.0)
        dgg = jnp.where(is_g, d * e * _softplus(xs), 0.0)
        o_ref[...] = jnp.where(lane < NH, d * beta * (1.0 - beta), da).astype(o_ref.dtype)

        @pl.when(i == 0)
        def _():
            dal_ref[...] = jnp.zeros_like(dal_ref)
            ddt_ref[...] = jnp.zeros_like(ddt_ref)

        dal_ref[...] += jnp.sum(dgg, axis=0, keepdims=True)
        ddt_ref[...] += jnp.sum(da, axis=0, keepdims=True)

    row = pl.BlockSpec((tm, GATE_W), lambda i: (i, 0))
    vec = pl.BlockSpec((1, GATE_W), lambda i: (0, 0))
    return pl.pallas_call(
        body, grid=(t // tm,), in_specs=[row, row, vec, vec], out_specs=[row, vec, vec],
        out_shape=[jax.ShapeDtypeStruct((t, GATE_W), MXU_DTYPE), jax.ShapeDtypeStruct((1, GATE_W), F32),
                   jax.ShapeDtypeStruct((1, GATE_W), F32)],
        compiler_params=_cp(("arbitrary",)), name=name)(dbg, pg, alog, dtb)


def _chunk_masks():
    ii = lax.broadcasted_iota(jnp.int32, (CHUNK, CHUNK), 0)
    jj = lax.broadcasted_iota(jnp.int32, (CHUNK, CHUNK), 1)
    return ii, jj, ii >= jj, ii > jj


def _lane_col(x, lane, idx):
    return jnp.sum(jnp.where(lane == idx, x, 0.0), axis=1, keepdims=True)


def _delta_terms(q, k, v, bgs, nch, low, strict):
    idx = [(c, h) for c in range(nch) for h in range(NH)]
    lane = lax.broadcasted_iota(jnp.int32, (CHUNK, GATE_W), 1)
    rowi = lax.broadcasted_iota(jnp.int32, (CHUNK, 1), 0)
    r4 = lax.broadcasted_iota(jnp.int32, (NH * CHUNK, GATE_W), 0)
    l4 = lax.broadcasted_iota(jnp.int32, (NH * CHUNK, GATE_W), 1)
    sel = jnp.where(l4 == NH + jnp.right_shift(r4, CHUNK_LOG2), 1.0, 0.0)
    lowf = jnp.where(low, 1.0, 0.0)
    gam_all = [_mmx(lowf, b, NN) for b in bgs]
    gam_rows = [_mmx(sel, g, NT) for g in gam_all]
    beta = [_lane_col(bgs[c], lane, h) for c, h in idx]
    gam = [_lane_col(gam_all[c], lane, NH + h) for c, h in idx]
    dm = [jnp.exp(jnp.where(low, g - gam_rows[c][h * CHUNK:(h + 1) * CHUNK, :], -1e30))
          for g, (c, h) in zip(gam, idx)]
    glast = [jnp.sum(jnp.where(rowi == CHUNK - 1, g, 0.0), axis=0, keepdims=True) for g in gam]
    eg = [jnp.exp(g) for g in gam]
    ekl = [jnp.exp(gl - g) for gl, g in zip(glast, gam)]
    gl = [jnp.exp(x) for x in glast]
    kb = [x * b for x, b in zip(k, beta)]
    vb = [x * b for x, b in zip(v, beta)]
    kbg = [x * e for x, e in zip(kb, eg)]
    kk = _mm1_many(kb, k, NT)
    qk = _mm1_many(q, k, NT)
    a_mat = [jnp.where(strict, x * d, 0.0) for x, d in zip(kk, dm)]
    p_mat = [jnp.where(low, x * d, 0.0) for x, d in zip(qk, dm)]
    qd = [x * e for x, e in zip(q, eg)]
    kd = [x * e for x, e in zip(k, ekl)]
    return dict(idx=idx, beta=beta, dm=dm, eg=eg, ekl=ekl, gl=gl, kb=kb, vb=vb, kbg=kbg, a=a_mat, p=p_mat, qd=qd,
                kd=kd, lane=lane, rowi=rowi)


def _unit_lower_inverses(a_list, ii, jj, eye):
    def same(log2):
        return jnp.right_shift(ii, log2) == jnp.right_shift(jj, log2)

    n = [-jnp.where(same(INV_BASE_LOG2), a, 0.0) for a in a_list]
    x = [eye + v for v in n]
    p = n
    for _ in range(INV_BASE_LOG2 - 1):
        p = _mm1_many(p, p, NN)
        x = [xi + y for xi, y in zip(x, _mm1_many(x, p, NN))]
    for log2 in range(INV_BASE_LOG2, CHUNK_LOG2):
        off = jnp.logical_and(same(log2 + 1), jnp.logical_not(same(log2)))
        a_off = [jnp.where(off, a, 0.0) for a in a_list]
        x = [xi - y for xi, y in zip(x, _mm1_many(x, _mm1_many(a_off, x, NN), NN))]
    return x


def _transposes(xs, eye):
    e = eye.astype(MXU_DTYPE)
    parts = [_split(x, 2) for x in xs]
    return [_dot(p[0], e, TN) + _dot(p[1], e, TN) for p in parts]


def _load_heads(ref, nch):
    return [ref[c * CHUNK:(c + 1) * CHUNK, h * DH:(h + 1) * DH] for c in range(nch) for h in range(NH)]


def delta_fwd(qkv, bg, name, push=None):
    t = qkv.shape[0]
    nc = t // CHUNK
    nch = _pick(nc, DELTA_CHUNKS)
    rows = nch * CHUNK
    ng = nc // nch
    npush = 0 if push is None else len(push)

    def body(*refs):
        q_ref, k_ref, v_ref, bg_ref = refs[:4]
        x_refs = refs[4:4 + npush]
        o_ref, sh_ref, mi_ref, u_ref, w_ref = refs[4 + npush:9 + npush]
        got_refs = refs[9 + npush:9 + 2 * npush]
        s_ref = refs[9 + 2 * npush]
        n = pl.program_id(0)
        if npush:
            copies = _push_copies(x_refs, got_refs, *refs[10 + 2 * npush:], scatter=False)

            @pl.when(n == 0)
            def _():
                _push_start(copies)

        @pl.when(n == 0)
        def _():
            s_ref[...] = jnp.zeros_like(s_ref)

        ii, jj, low, strict = _chunk_masks()
        eye = jnp.where(ii == jj, 1.0, 0.0)
        q, k, v = _load_heads(q_ref, nch), _load_heads(k_ref, nch), _load_heads(v_ref, nch)
        bgs = [bg_ref[c * CHUNK:(c + 1) * CHUNK, :] for c in range(nch)]
        tm_ = _delta_terms(q, k, v, bgs, nch, low, strict)
        m_inv = _unit_lower_inverses(tm_["a"], ii, jj, eye)
        u = _mm3_many(m_inv, tm_["vb"], NN)
        w = _mm3_many(m_inv, tm_["kbg"], NN)
        for i, (c, h) in enumerate(tm_["idx"]):
            mi_ref[c, h] = m_inv[i]
            u_ref[c * CHUNK:(c + 1) * CHUNK, h * DH:(h + 1) * DH] = u[i]
            w_ref[c * CHUNK:(c + 1) * CHUNK, h * DH:(h + 1) * DH] = w[i]
        s = [s_ref[h] for h in range(NH)]
        for c in range(nch):
            pr = range(c * NH, (c + 1) * NH)
            ws = [_mm1(w[i], s[i - c * NH], NN) for i in pr]
            qs = [_mm1(tm_["qd"][i], s[i - c * NH], NN) for i in pr]
            vn = [u[i] - x for i, x in zip(pr, ws)]
            pv = [_mm1(tm_["p"][i], x, NN) for i, x in zip(pr, vn)]
            kv = [_mm1(tm_["kd"][i], x, TN) for i, x in zip(pr, vn)]
            for h in range(NH):
                o_ref[c * CHUNK:(c + 1) * CHUNK, h * DH:(h + 1) * DH] = qs[h] + pv[h]
                sh_ref[c, h] = s[h]
                s[h] = tm_["gl"][c * NH + h] * s[h] + kv[h]
        for h in range(NH):
            s_ref[h] = s[h]
        if npush:
            @pl.when(n == ng - 1)
            def _():
                _push_finish(copies)

    col = lambda c: pl.BlockSpec((rows, DQ), lambda n: (n, c))
    any_spec = pl.BlockSpec(memory_space=pl.ANY)
    pushed = [] if push is None else list(push)
    outs = pl.pallas_call(
        body, grid=(ng,),
        in_specs=[col(0), col(1), col(2), pl.BlockSpec((rows, GATE_W), lambda n: (n, 0))] + [any_spec] * npush,
        out_specs=[col(0), pl.BlockSpec((nch, NH, DH, DH), lambda n: (n, 0, 0, 0)),
                   pl.BlockSpec((nch, NH, CHUNK, CHUNK), lambda n: (n, 0, 0, 0)), col(0), col(0)]
        + [any_spec] * npush,
        out_shape=[jax.ShapeDtypeStruct((t, DQ), F32), jax.ShapeDtypeStruct((nc, NH, DH, DH), F32),
                   jax.ShapeDtypeStruct((nc, NH, CHUNK, CHUNK), F32), jax.ShapeDtypeStruct((t, DQ), F32),
                   jax.ShapeDtypeStruct((t, DQ), F32)] + _push_out_shapes(pushed, scatter=False),
        scratch_shapes=[pltpu.VMEM((NH, DH, DH), F32)] + (_push_sems(npush) if npush else []),
        compiler_params=_cp(("arbitrary",)), name=name)(qkv, qkv, qkv, bg, *pushed)
    return outs[:5], outs[5:]


def delta_bwd(qkv, bg, do, s_hist, m_hist, u_all, w_all, name, push=None):
    t = qkv.shape[0]
    nc = t // CHUNK
    nch = _pick(nc, DELTA_CHUNKS)
    rows = nch * CHUNK
    ng = nc // nch
    npush = 0 if push is None else len(push)

    def body(*refs):
        q_ref, k_ref, v_ref, bg_ref, do_ref, sh_ref, mi_ref, u_ref, w_ref = refs[:9]
        x_refs = refs[9:9 + npush]
        dqkv_ref, dbg_ref = refs[9 + npush:11 + npush]
        got_refs = refs[11 + npush:11 + 2 * npush]
        ds_ref = refs[11 + 2 * npush]
        n = pl.program_id(0)
        if npush:
            copies = _push_copies(x_refs, got_refs, *refs[12 + 2 * npush:], scatter=True)

            @pl.when(n == 0)
            def _():
                _push_start(copies)

        @pl.when(n == 0)
        def _():
            ds_ref[...] = jnp.zeros_like(ds_ref)

        ii, jj, low, strict = _chunk_masks()
        eye = jnp.where(ii == jj, 1.0, 0.0)
        q, k, v = _load_heads(q_ref, nch), _load_heads(k_ref, nch), _load_heads(v_ref, nch)
        d_o = _load_heads(do_ref, nch)
        bgs = [bg_ref[c * CHUNK:(c + 1) * CHUNK, :] for c in range(nch)]
        tm_ = _delta_terms(q, k, v, bgs, nch, low, strict)
        idx, lane, rowi = tm_["idx"], tm_["lane"], tm_["rowi"]
        beta, dm, eg, ekl, gl = tm_["beta"], tm_["dm"], tm_["eg"], tm_["ekl"], tm_["gl"]
        kb, kbg, qd, kd, a_mat, p_mat = tm_["kb"], tm_["kbg"], tm_["qd"], tm_["kd"], tm_["a"], tm_["p"]
        s = [sh_ref[c, h] for c, h in idx]
        m_inv = [mi_ref[c, h] for c, h in idx]
        u, w = _load_heads(u_ref, nch), _load_heads(w_ref, nch)
        ws = _mm1_many(w, s, NN)
        vn = [x - y for x, y in zip(u, ws)]
        pdo = _mm1_many(p_mat, d_o, TN)
        qdo = _mm1_many(qd, d_o, TN)
        dqd = _mm1_many(d_o, s, NT)
        dp = [jnp.where(low, x, 0.0) for x in _mm1_many(d_o, vn, NT)]

        nprob = len(idx)
        dvn, dkd, dgl = [None] * nprob, [None] * nprob, [None] * nprob
        ds = [ds_ref[h] for h in range(NH)]
        for c in reversed(range(nch)):
            pr = list(range(c * NH, (c + 1) * NH))
            kds = [_mm1(kd[i], ds[i - c * NH], NN) for i in pr]
            for i, x in zip(pr, kds):
                dvn[i] = pdo[i] + x
            wdv = [_mm1(w[i], dvn[i], TN) for i in pr]
            for i in pr:
                h = i - c * NH
                dkd[i] = _mm1(vn[i], ds[h], NT)
                dgl[i] = jnp.sum(jnp.sum(s[i] * ds[h], axis=1, keepdims=True), axis=0, keepdims=True)
                ds[h] = qdo[i] + gl[i] * ds[h] - wdv[h]
        for h in range(NH):
            ds_ref[h] = ds[h]

        dw = [-x for x in _mm1_many(dvn, s, NT)]
        dvb = _mm3_many(m_inv, dvn, TN)
        dkbg = _mm3_many(m_inv, dw, TN)
        da1 = _mm1_many(dvb, u, NT)
        da2 = _mm1_many(dkbg, w, NT)
        da = [-jnp.where(strict, x + y, 0.0) for x, y in zip(da1, da2)]
        gm = [x * d for x, d in zip(da, dm)]
        hm = [x * d for x, d in zip(dp, dm)]
        gk = _mm1_many(gm, k, NN)
        gkb = _mm1_many(gm, kb, TN)
        hq = _mm1_many(hm, q, TN)
        hk = _mm1_many(hm, k, NN)
        em = [x * a + y * p for x, a, y, p in zip(da, a_mat, dp, p_mat)]
        em_t = _transposes(em, eye)
        dbeta_all = [jnp.zeros((CHUNK, GATE_W), F32) for _ in range(nch)]
        dgam_all = [jnp.zeros((CHUNK, GATE_W), F32) for _ in range(nch)]
        for i, (c, h) in enumerate(idx):
            dkb = gk[i] + dkbg[i] * eg[i]
            dk = gkb[i] + hq[i] + dkd[i] * ekl[i] + beta[i] * dkb
            dq = hk[i] + dqd[i] * eg[i]
            dkd_kd = jnp.sum(dkd[i] * kd[i], axis=1, keepdims=True)
            dgam = (jnp.sum(em[i], axis=1, keepdims=True) - jnp.sum(em_t[i], axis=1, keepdims=True)
                    + jnp.sum(dqd[i] * qd[i], axis=1, keepdims=True) - dkd_kd
                    + jnp.sum(dkbg[i] * kbg[i], axis=1, keepdims=True))
            tail = jnp.sum(dkd_kd, axis=0, keepdims=True) + dgl[i] * gl[i]
            dgam = dgam + jnp.where(rowi == CHUNK - 1, tail, 0.0)
            dbeta = jnp.sum(dkb * k[i], axis=1, keepdims=True) + jnp.sum(dvb[i] * v[i], axis=1, keepdims=True)
            rs = slice(c * CHUNK, (c + 1) * CHUNK)
            dqkv_ref[rs, h * DH:(h + 1) * DH] = dq
            dqkv_ref[rs, DQ + h * DH:DQ + (h + 1) * DH] = dk
            dqkv_ref[rs, 2 * DQ + h * DH:2 * DQ + (h + 1) * DH] = beta[i] * dvb[i]
            dbeta_all[c] = dbeta_all[c] + jnp.where(lane == h, dbeta, 0.0)
            dgam_all[c] = dgam_all[c] + jnp.where(lane == NH + h, dgam, 0.0)
        upf = jnp.where(ii <= jj, 1.0, 0.0)
        for c in range(nch):
            dg_all = _mmx(upf, dgam_all[c], NN)
            dbg_ref[c * CHUNK:(c + 1) * CHUNK, :] = jnp.where(lane < NH, dbeta_all[c], dg_all)
        if npush:
            @pl.when(n == ng - 1)
            def _():
                _push_finish(copies)

    col = lambda c: pl.BlockSpec((rows, DQ), lambda n: (ng - 1 - n, c))
    gate = pl.BlockSpec((rows, GATE_W), lambda n: (ng - 1 - n, 0))
    any_spec = pl.BlockSpec(memory_space=pl.ANY)
    pushed = [] if push is None else list(push)
    outs = pl.pallas_call(
        body, grid=(ng,),
        in_specs=[col(0), col(1), col(2), gate, col(0),
                  pl.BlockSpec((nch, NH, DH, DH), lambda n: (ng - 1 - n, 0, 0, 0)),
                  pl.BlockSpec((nch, NH, CHUNK, CHUNK), lambda n: (ng - 1 - n, 0, 0, 0)), col(0), col(0)]
        + [any_spec] * npush,
        out_specs=[pl.BlockSpec((rows, 3 * DQ), lambda n: (ng - 1 - n, 0)), gate] + [any_spec] * npush,
        out_shape=[jax.ShapeDtypeStruct((t, 3 * DQ), F32), jax.ShapeDtypeStruct((t, GATE_W), F32)]
        + _push_out_shapes(pushed, scatter=True),
        scratch_shapes=[pltpu.VMEM((NH, DH, DH), F32)] + (_push_sems(npush) if npush else []),
        compiler_params=_cp(("arbitrary",)), name=name)(qkv, qkv, qkv, bg, do, s_hist, m_hist, u_all, w_all, *pushed)
    return outs[0], outs[1], outs[2:]


def dn_post_fwd(ybuf, o, proj, nw, name):
    t = o.shape[0]
    tm = _row_tile(t)

    def body(y_in, o_ref, z_ref, nw_ref, y_ref):
        del y_in
        nwv = nw_ref[...]
        for h in range(NH):
            sl = slice(h * DH, (h + 1) * DH)
            oh = o_ref[:, sl]
            z = z_ref[:, sl]
            r = lax.rsqrt(jnp.mean(oh * oh, axis=-1, keepdims=True) + NORM_EPS)
            y_ref[:, sl] = (oh * r * nwv * (z * _sigmoid(z))).astype(y_ref.dtype)

    return pl.pallas_call(
        body, grid=(t // tm,),
        in_specs=[pl.BlockSpec(memory_space=pl.ANY), pl.BlockSpec((tm, DQ), lambda i: (i, 0)),
                  pl.BlockSpec((tm, DQ), lambda i: (i, 5)), pl.BlockSpec((1, DH), lambda i: (0, 0))],
        out_specs=pl.BlockSpec((tm, DQ), lambda i: (i, 1)),
        out_shape=jax.ShapeDtypeStruct(ybuf.shape, ybuf.dtype), input_output_aliases={0: 0},
        compiler_params=_cp(("parallel",)), name=name)(ybuf, o, proj, nw)


def dn_post_bwd(dproj, dy, o, proj, nw, name):
    t = o.shape[0]
    tm = _row_tile(t)

    def body(dp_in, dy_ref, o_ref, z_ref, nw_ref, do_ref, dp_ref, dnw_ref):
        del dp_in
        i = pl.program_id(0)
        nwv = nw_ref[...]
        acc = jnp.zeros((1, DH), F32)
        for h in range(NH):
            sl = slice(h * DH, (h + 1) * DH)
            oh = o_ref[:, sl]
            z = z_ref[:, sl]
            dyh = dy_ref[:, sl]
            r = lax.rsqrt(jnp.mean(oh * oh, axis=-1, keepdims=True) + NORM_EPS)
            xh = oh * r
            sg = _sigmoid(z)
            sz = z * sg
            dxh = dyh * nwv * sz
            do_ref[:, sl] = r * (dxh - xh * jnp.mean(dxh * xh, axis=-1, keepdims=True))
            dp_ref[:, sl] = (dyh * xh * nwv * (sg * (1.0 + z * (1.0 - sg)))).astype(dp_ref.dtype)
            acc = acc + jnp.sum(dyh * xh * sz, axis=0, keepdims=True)

        @pl.when(i == 0)
        def _():
            dnw_ref[...] = jnp.zeros_like(dnw_ref)

        dnw_ref[...] += acc

    vec = pl.BlockSpec((1, DH), lambda i: (0, 0))
    return pl.pallas_call(
        body, grid=(t // tm,),
        in_specs=[pl.BlockSpec(memory_space=pl.ANY), pl.BlockSpec((tm, DQ), lambda i: (i, 1)),
                  pl.BlockSpec((tm, DQ), lambda i: (i, 0)), pl.BlockSpec((tm, DQ), lambda i: (i, 5)), vec],
        out_specs=[pl.BlockSpec((tm, DQ), lambda i: (i, 0)), pl.BlockSpec((tm, DQ), lambda i: (i, 5)), vec],
        out_shape=[jax.ShapeDtypeStruct((t, DQ), F32), jax.ShapeDtypeStruct(dproj.shape, dproj.dtype),
                   jax.ShapeDtypeStruct((1, DH), F32)],
        input_output_aliases={0: 1}, compiler_params=_cp(("arbitrary",)), name=name)(dproj, dy, o, proj, nw)


def _shifted(first, second, s, lane):
    if s == 0:
        return first
    return jnp.where(lane < LANES - s, pltpu.roll(first, LANES - s, 1), pltpu.roll(second, LANES - s, 1))


def unshard_cols(g8, w, widths, name):
    _, r, wp = g8.shape
    rb = _pick(r, (256, 128, 64, 32, 16))

    def body(g_ref, *o_refs):
        lane = lax.broadcasted_iota(jnp.int32, (rb, LANES), 1)
        zeros = jnp.zeros((rb, LANES), F32)

        def src(j, ta):
            if j >= NDEV or ta * LANES >= wp:
                return zeros
            return g_ref[j, :, ta * LANES:(ta + 1) * LANES].astype(F32)

        base = 0
        for o_ref, width in zip(o_refs, widths):
            for b in range(width // LANES):
                c0 = base + b * LANES
                if c0 >= NDEV * w:
                    tile = zeros
                else:
                    j0, o0 = divmod(c0, w)
                    n0 = min(w - o0, LANES)
                    ta, s = divmod(o0, LANES)
                    tile = _shifted(src(j0, ta), src(j0, ta + 1), s, lane)
                    if n0 < LANES:
                        nxt = pltpu.roll(src(j0 + 1, 0), n0, 1) if j0 + 1 < NDEV else zeros
                        tile = jnp.where(lane < n0, tile, nxt)
                o_ref[:, b * LANES:(b + 1) * LANES] = tile.astype(o_ref.dtype)
            base += width

    return pl.pallas_call(
        body, grid=(r // rb,), in_specs=[pl.BlockSpec((NDEV, rb, wp), lambda i: (0, i, 0))],
        out_specs=[pl.BlockSpec((rb, width), lambda i: (i, 0)) for width in widths],
        out_shape=[jax.ShapeDtypeStruct((r, width), g8.dtype) for width in widths],
        compiler_params=_cp(("parallel",)), name=name)(g8)


def shard_cols(parts, w, name):
    r = parts[0].shape[0]
    wp = _lane_pad(w)
    rb = _pick(r, (256, 128, 64, 32, 16))
    tiles_of = [p.shape[1] // LANES for p in parts]

    def body(*refs):
        p_refs, o_ref = refs[:-1], refs[-1]
        lane = lax.broadcasted_iota(jnp.int32, (rb, LANES), 1)
        zeros = jnp.zeros((rb, LANES), F32)

        def glob(tile_idx):
            for p_ref, n_tiles in zip(p_refs, tiles_of):
                if tile_idx < n_tiles:
                    return p_ref[:, tile_idx * LANES:(tile_idx + 1) * LANES].astype(F32)
                tile_idx -= n_tiles
            return zeros

        for j in range(NDEV):
            for a in range(wp // LANES):
                nv = min(w - a * LANES, LANES)
                tb, s = divmod(w * j + a * LANES, LANES)
                tile = _shifted(glob(tb), glob(tb + 1), s, lane)
                if nv < LANES:
                    tile = jnp.where(lane < nv, tile, 0.0)
                o_ref[j, :, a * LANES:(a + 1) * LANES] = tile.astype(o_ref.dtype)

    return pl.pallas_call(
        body, grid=(r // rb,), in_specs=[pl.BlockSpec((rb, p.shape[1]), lambda i: (i, 0)) for p in parts],
        out_specs=pl.BlockSpec((NDEV, rb, wp), lambda i: (0, i, 0)),
        out_shape=jax.ShapeDtypeStruct((NDEV, r, wp), GRAD_DTYPE),
        compiler_params=_cp(("parallel",)), name=name)(*parts)


def _me_and_peers():
    mx, my, mc = lax.axis_index("x"), lax.axis_index("y"), lax.axis_index("c")
    me = 4 * mx + 2 * my + mc
    peers = []
    for kk in range(1, NDEV):
        px = 1 - mx if kk & 4 else mx
        py = 1 - my if kk & 2 else my
        pc = 1 - mc if kk & 1 else mc
        peers.append(((px, py, pc), 4 * px + 2 * py + pc))
    return me, peers


def _push_copies(x_refs, o_refs, send_sems, recv_sems, local_sems, scatter):
    me, peers = _me_and_peers()
    npeer = NDEV - 1
    local, sends, recvs = [], [], []
    for a, (x_ref, o_ref) in enumerate(zip(x_refs, o_refs)):
        local.append(pltpu.make_async_copy(x_ref.at[me] if scatter else x_ref, o_ref.at[me], local_sems.at[a]))
        for kk, (peer, pidx) in enumerate(peers):
            src = x_ref.at[pidx] if scatter else x_ref
            sems = dict(send_sem=send_sems.at[a * npeer + kk], recv_sem=recv_sems.at[a * npeer + kk],
                        device_id=peer, device_id_type=pl.DeviceIdType.MESH)
            sends.append(pltpu.make_async_remote_copy(src_ref=src, dst_ref=o_ref.at[me], **sems))
            recvs.append(pltpu.make_async_remote_copy(src_ref=src, dst_ref=o_ref.at[pidx], **sems))
    return local, sends, recvs


def _push_start(copies):
    local, sends, _ = copies
    for cp in local + sends:
        cp.start()


def _push_finish(copies):
    local, sends, recvs = copies
    for cp in recvs:
        cp.wait_recv()
    for cp in sends:
        cp.wait_send()
    for cp in local:
        cp.wait()


def _push_out_shapes(xs, scatter):
    return [jax.ShapeDtypeStruct(x.shape if scatter else (NDEV,) + x.shape, x.dtype) for x in xs]


def _push_sems(n):
    return [pltpu.SemaphoreType.DMA((n * (NDEV - 1),)), pltpu.SemaphoreType.DMA((n * (NDEV - 1),)),
            pltpu.SemaphoreType.DMA((n,))]


def _push_to_all(xs, name, scatter):
    n = len(xs)

    def body(*refs):
        copies = _push_copies(refs[:n], refs[n:2 * n], *refs[2 * n:], scatter=scatter)
        _push_start(copies)
        _push_finish(copies)

    any_spec = pl.BlockSpec(memory_space=pl.ANY)
    return pl.pallas_call(
        body, in_specs=[any_spec] * n, out_specs=[any_spec] * n, out_shape=_push_out_shapes(xs, scatter),
        scratch_shapes=_push_sems(n), name=name)(*xs)


def adamw(recv, w, m, v, name):
    rows, cols = w.shape
    c1 = 1.0 - ADAM_B1 ** ADAM_STEP
    c2 = 1.0 - ADAM_B2 ** ADAM_STEP
    cap = ADAM_BLOCK_BYTES // (NDEV * cols * 4)
    rb = _pick(rows, [p for p in (2048, 1024, 512, 256, 128, 64, 32, 16, 8) if p <= cap])

    def body(r_ref, w_ref, m_ref, v_ref, g_ref, d_ref, m2_ref, v2_ref):
        g = r_ref[0].astype(F32)
        for j in range(1, NDEV):
            g = g + r_ref[j].astype(F32)
        m2 = ADAM_B1 * m_ref[...] + (1.0 - ADAM_B1) * g
        v2 = ADAM_B2 * v_ref[...] + (1.0 - ADAM_B2) * (g * g)
        g_ref[...] = g
        m2_ref[...] = m2
        v2_ref[...] = v2
        d_ref[...] = -ADAM_LR * ((m2 / c1) / (jnp.sqrt(v2 / c2) + ADAM_EPS) + ADAM_WD * w_ref[...])

    blk = pl.BlockSpec((rb, cols), lambda i: (i, 0))
    return pl.pallas_call(
        body, grid=(rows // rb,),
        in_specs=[pl.BlockSpec((NDEV, rb, cols), lambda i: (0, i, 0)), blk, blk, blk],
        out_specs=[blk, blk, blk, blk], out_shape=[jax.ShapeDtypeStruct((rows, cols), F32)] * 4,
        compiler_params=_cp(("parallel",)), name=name)(recv, w, m, v)


SMALL_SHARDED = ("meta_tokens", "conv_dw_w", "dn_conv_w")
SMALL_REPLICATED = ("norm_mix_w", "conv_dw_b", "conv_ln_w", "conv_ln_b", "dn_A_log", "dn_dt_bias", "dn_norm_w",
                    "norm_ffn_w", "final_norm_w")
PARAM_ORDER = ("meta_tokens", "norm_mix_w", "w_in", "conv_dw_w", "conv_dw_b", "conv_ln_w", "conv_ln_b", "dn_conv_w",
               "dn_A_log", "dn_dt_bias", "dn_norm_w", "w_out", "norm_ffn_w", "ffn_w_gu", "ffn_w_down", "final_norm_w")


def _pack_small(parts, axis):
    flat = jnp.concatenate(parts, axis=axis)
    n = flat.shape[axis]
    total = -(-n // (8 * LANES)) * (8 * LANES)
    pad = [(0, 0)] * flat.ndim
    pad[axis] = (0, total - n)
    flat = jnp.pad(flat, pad)
    return flat.reshape(flat.shape[:axis] + (total // LANES, LANES))


def _unshard_last(g8):
    moved = jnp.moveaxis(g8, 0, -2)
    return moved.reshape(moved.shape[:-2] + (-1,))


def _per_destination_last(full):
    split = full.reshape(full.shape[:-1] + (NDEV, full.shape[-1] // NDEV))
    return jnp.moveaxis(split, -2, 0).reshape(NDEV, -1)


def _lane_row(vec4, width):
    return jnp.pad(vec4, (NH, width - 2 * NH))[None]


def kernel(x, meta_tokens, norm_mix_w, w_in, conv_dw_w, conv_dw_b, conv_ln_w, conv_ln_b, dn_conv_w, dn_A_log, dn_dt_bias, dn_norm_w, w_out, norm_ffn_w, ffn_w_gu, ffn_w_down, final_norm_w, loss_target, m_meta_tokens, m_norm_mix_w, m_w_in, m_conv_dw_w, m_conv_dw_b, m_conv_ln_w, m_conv_ln_b, m_dn_conv_w, m_dn_A_log, m_dn_dt_bias, m_dn_norm_w, m_w_out, m_norm_ffn_w, m_ffn_w_gu, m_ffn_w_down, m_final_norm_w, v_meta_tokens, v_norm_mix_w, v_w_in, v_conv_dw_w, v_conv_dw_b, v_conv_ln_w, v_conv_ln_b, v_dn_conv_w, v_dn_A_log, v_dn_dt_bias, v_dn_norm_w, v_w_out, v_norm_ffn_w, v_ffn_w_gu, v_ffn_w_down, v_final_norm_w):
    weights = dict(meta_tokens=meta_tokens, norm_mix_w=norm_mix_w, w_in=w_in, conv_dw_w=conv_dw_w, conv_dw_b=conv_dw_b,
                   conv_ln_w=conv_ln_w, conv_ln_b=conv_ln_b, dn_conv_w=dn_conv_w, dn_A_log=dn_A_log,
                   dn_dt_bias=dn_dt_bias, dn_norm_w=dn_norm_w, w_out=w_out, norm_ffn_w=norm_ffn_w, ffn_w_gu=ffn_w_gu,
                   ffn_w_down=ffn_w_down, final_norm_w=final_norm_w)
    m_in = dict(meta_tokens=m_meta_tokens, norm_mix_w=m_norm_mix_w, w_in=m_w_in, conv_dw_w=m_conv_dw_w,
                conv_dw_b=m_conv_dw_b, conv_ln_w=m_conv_ln_w, conv_ln_b=m_conv_ln_b, dn_conv_w=m_dn_conv_w,
                dn_A_log=m_dn_A_log, dn_dt_bias=m_dn_dt_bias, dn_norm_w=m_dn_norm_w, w_out=m_w_out,
                norm_ffn_w=m_norm_ffn_w, ffn_w_gu=m_ffn_w_gu, ffn_w_down=m_ffn_w_down, final_norm_w=m_final_norm_w)
    v_in = dict(meta_tokens=v_meta_tokens, norm_mix_w=v_norm_mix_w, w_in=v_w_in, conv_dw_w=v_conv_dw_w,
                conv_dw_b=v_conv_dw_b, conv_ln_w=v_conv_ln_w, conv_ln_b=v_conv_ln_b, dn_conv_w=v_dn_conv_w,
                dn_A_log=v_dn_A_log, dn_dt_bias=v_dn_dt_bias, dn_norm_w=v_dn_norm_w, w_out=v_w_out,
                norm_ffn_w=v_norm_ffn_w, ffn_w_gu=v_ffn_w_gu, ffn_w_down=v_ffn_w_down, final_norm_w=v_final_norm_w)

    depth = w_in.shape[0]
    seq = x.shape[1]
    t = _padded_rows(seq)
    rows_d = depth * D
    win_w, gu_w = w_in.shape[2], ffn_w_gu.shape[2]
    win_wp, gu_wp = _lane_pad(win_w), _lane_pad(gu_w)

    def pad_cols(a, wp):
        return jnp.pad(a, ((0, 0), (0, 0), (0, wp - a.shape[2]))).reshape(rows_d, wp)

    def rows2d(a):
        return a.reshape(-1, a.shape[2])

    small_shards = [weights[n] for n in SMALL_SHARDED]
    win_p = pad_cols(w_in, win_wp).astype(MXU_DTYPE).reshape(depth, D, win_wp)
    gu_p = pad_cols(ffn_w_gu, gu_wp).astype(MXU_DTYPE).reshape(depth, D, gu_wp)
    wout_b, wdown_b = w_out.astype(MXU_DTYPE), ffn_w_down.astype(MXU_DTYPE)
    layer_shards = lambda l: [win_p[l], gu_p[l], wout_b[l], wdown_b[l]]

    def whole_weights(got):
        g_win, g_gu, g_wout, g_down = got
        main, gate_cols = unshard_cols(g_win, win_w, [PROJ_MAIN, GATE_W], "unshard_w_in")
        wg, wu = unshard_cols(g_gu, gu_w, [DFF, DFF], "unshard_w_gu")
        return dict(main=main, gate=gate_cols, wg=wg, wu=wu, out=g_wout.reshape(D, D), down=g_down.reshape(DFF, D))

    *got0, g_small = _push_to_all(layer_shards(0) + [_pack_small([s.reshape(-1) for s in small_shards], 0)],
                                  "gather_first", scatter=False)
    wts = [whole_weights(got0)]
    small_flat, off, small_full = g_small.reshape(NDEV, -1), 0, {}
    for n, s in zip(SMALL_SHARDED, small_shards):
        small_full[n] = _unshard_last(small_flat[:, off:off + s.size].reshape((NDEV,) + s.shape))
        off += s.size
    cdw32 = jnp.pad(small_full["conv_dw_w"], ((0, 0), (0, CONV_HALO - CONV_W), (0, 0)))
    dcw8 = jnp.pad(small_full["dn_conv_w"], ((0, 0), (0, DN_HALO - DN_W), (0, 0)))

    h = jnp.concatenate([jnp.zeros((FRONT, D), F32), small_full["meta_tokens"], x[0],
                         jnp.zeros((t - HEAD - seq, D), F32)], axis=0)
    tgt = jnp.pad(loss_target[0], ((HEAD, t - HEAD - seq), (0, 0)))

    saved = []
    for l in range(depth):
        nmw, nfw = norm_mix_w[l][None], norm_ffn_w[l][None]
        cdb, clw, clb = conv_dw_b[l][None], conv_ln_w[l][None], conv_ln_b[l][None]
        alog, dtb, dnw = _lane_row(dn_A_log[l], GATE_W), _lane_row(dn_dt_bias[l], GATE_W), dn_norm_w[l][None]
        wl = wts[l]
        proj = mm([h], wl["main"], a_fn=_rms_apply, a_vecs=[nmw], name="mm_proj")
        pg = mm([h], wl["gate"], a_fn=_rms_apply, a_vecs=[nmw], name="mm_proj_gate")
        ybuf, u1 = conv_fwd(proj, cdw32[l], cdb, clw, clb, seq, "conv_fwd")
        qkv = dn_pre_fwd(proj, dcw8[l], "dn_pre_fwd")
        bg = gate_fwd(pg, alog, dtb, seq, "gate_fwd")
        if l + 1 < depth:
            (o, s_hist, m_hist, u_all, w_all), got = delta_fwd(qkv, bg, "delta_fwd_gather", push=layer_shards(l + 1))
            wts.append(whole_weights(got))
        else:
            (o, s_hist, m_hist, u_all, w_all), _ = delta_fwd(qkv, bg, "delta_fwd")
        ybuf = dn_post_fwd(ybuf, o, proj, dnw, "dn_post_fwd")
        h_mid = mm(ybuf, wl["out"], extras=[h], out_fn=_add, name="mm_out")
        gate = mm([h_mid], wl["wg"], a_fn=_rms_apply, a_vecs=[nfw], name="mm_gate")
        up = mm([h_mid], wl["wu"], a_fn=_rms_apply, a_vecs=[nfw], name="mm_up")
        h_out = mm([gate, up], wl["down"], a_fn=_swiglu, extras=[h_mid], out_fn=_add, tile_cap=ROW_TILE, name="mm_down")
        saved.append(dict(h=h, proj=proj, pg=pg, ybuf=ybuf, u1=u1, qkv=qkv, bg=bg, o=o, s_hist=s_hist,
                          m_hist=m_hist, u_all=u_all, w_all=w_all, h_mid=h_mid, gate=gate, up=up,
                          nmw=nmw, nfw=nfw, clw=clw, clb=clb, alog=alog, dtb=dtb, dnw=dnw))
        h = h_out

    dh, loss_part, d_final = loss_bwd(h, tgt, final_norm_w[None], seq, "loss_bwd")
    loss = lax.psum(loss_part[0, 0], MESH_AXES)

    per_layer = ("norm_mix_w", "conv_dw_w", "conv_dw_b", "conv_ln_w", "conv_ln_b", "dn_conv_w", "dn_A_log", "dn_dt_bias",
                 "dn_norm_w", "norm_ffn_w")
    grads = {n: [None] * depth for n in per_layer}
    received = [None] * depth
    pending = None
    dw_mm = lambda a, b, name, **kw: mm(a, b, ta=True, out_dtypes=(GRAD_DTYPE,), name=name, **kw)
    for l in reversed(range(depth)):
        s, wl = saved[l], wts[l]
        dgate, dup = mm(dh, wl["down"], tb=True, extras=[s["gate"], s["up"]], out_fn=_swiglu_bwd,
                        out_dtypes=(MXU_DTYPE, MXU_DTYPE), tile_cap=ROW_TILE, name="mm_down_dx")
        d_down = dw_mm([s["gate"], s["up"]], dh, "mm_down_dw", a_fn=_swiglu, tile_cap=ROW_TILE)
        dhn2 = mm(dup, wl["wu"], tb=True, extras=[mm(dgate, wl["wg"], tb=True, name="mm_gate_dx")], out_fn=_add,
                  name="mm_up_dx")
        dh_mid, dnfw = rms_bwd(dhn2, s["h_mid"], s["nfw"], dh, "rms_ffn_bwd")
        d_wg = dw_mm([s["h_mid"]], dgate, "mm_gate_dw", a_fn=_rms_apply, a_vecs=[s["nfw"]])
        d_wu = dw_mm([s["h_mid"]], dup, "mm_up_dw", a_fn=_rms_apply, a_vecs=[s["nfw"]])
        dy = mm(dh_mid, wl["out"], tb=True, name="mm_out_dx")
        d_out = dw_mm(s["ybuf"], dh_mid, "mm_out_dw")
        dproj, dcdw, dcdb, dclw, dclb = conv_bwd(dy, s["u1"], s["proj"], cdw32[l], s["clw"], s["clb"], seq, "conv_bwd")
        do, dproj, ddnw = dn_post_bwd(dproj, dy, s["o"], s["proj"], s["dnw"], "dn_post_bwd")
        delta_args = (s["qkv"], s["bg"], do, s["s_hist"], s["m_hist"], s["u_all"], s["w_all"])
        if pending is None:
            dqkv, dbg, _ = delta_bwd(*delta_args, "delta_bwd")
        else:
            dqkv, dbg, received[l + 1] = delta_bwd(*delta_args, "delta_bwd_exchange", push=pending)
        dproj, ddcw = dn_pre_bwd(dproj, dqkv, s["proj"], dcw8[l], "dn_pre_bwd")
        dpg, dalog, ddtb = gate_bwd(dbg, s["pg"], s["alog"], s["dtb"], seq, "gate_bwd")
        dhn_gate = mm(dpg, wl["gate"], tb=True, name="mm_proj_gate_dx")
        dhn = mm(dproj, wl["main"], tb=True, extras=[dhn_gate], out_fn=_add, name="mm_proj_dx")
        d_main = dw_mm([s["h"]], dproj, "mm_proj_dw", a_fn=_rms_apply, a_vecs=[s["nmw"]])
        d_gate_cols = dw_mm([s["h"]], dpg, "mm_proj_gate_dw", a_fn=_rms_apply, a_vecs=[s["nmw"]])
        pending = [shard_cols([d_main, d_gate_cols], win_w, "shard_w_in"), shard_cols([d_wg, d_wu], gu_w, "shard_w_gu"),
                   d_out.reshape(NDEV, D // NDEV, D), d_down.reshape(NDEV, DFF // NDEV, D)]
        dh, dnmw = rms_bwd(dhn, s["h"], s["nmw"], dh_mid, "rms_mix_bwd")
        grads["norm_mix_w"][l] = dnmw[0]
        grads["norm_ffn_w"][l] = dnfw[0]
        grads["conv_dw_w"][l] = dcdw[:CONV_W]
        grads["conv_dw_b"][l] = dcdb[0]
        grads["conv_ln_w"][l] = dclw[0]
        grads["conv_ln_b"][l] = dclb[0]
        grads["dn_conv_w"][l] = ddcw[:DN_W]
        grads["dn_A_log"][l] = dalog[0, NH:2 * NH]
        grads["dn_dt_bias"][l] = ddtb[0, NH:2 * NH]
        grads["dn_norm_w"][l] = ddnw[0]

    grad_x = dh[HEAD:HEAD + seq][None]
    full = {n: jnp.stack(g) for n, g in grads.items()}
    full["meta_tokens"] = dh[FRONT:HEAD]
    full["final_norm_w"] = d_final[0]

    send_small = _pack_small(
        [_per_destination_last(full[n]) for n in SMALL_SHARDED]
        + [jnp.broadcast_to(full[n].reshape(1, -1), (NDEV, full[n].size)) for n in SMALL_REPLICATED], 1)
    *received[0], r_small = _push_to_all(pending + [send_small], "exchange_last", scatter=True)
    r_win, r_gu, r_wout, r_down = (jnp.concatenate([received[l][k] for l in range(depth)], axis=1) for k in range(4))

    small_names = SMALL_SHARDED + SMALL_REPLICATED
    pack_local = lambda tree: _pack_small([tree[n].reshape(-1) for n in small_names], 0)
    results = {}

    def run_adamw(name, recv, prep, finish):
        outs = adamw(recv, prep(weights[name]), prep(m_in[name]), prep(v_in[name]), "adamw_" + name)
        results[name] = [finish(o) for o in outs]

    run_adamw("w_in", r_win, lambda a: pad_cols(a, win_wp),
              lambda o: o[:, :win_w].reshape(depth, D, win_w))
    run_adamw("ffn_w_gu", r_gu, lambda a: pad_cols(a, gu_wp), lambda o: o[:, :gu_w].reshape(depth, D, gu_w))
    run_adamw("w_out", r_wout, rows2d, lambda o: o.reshape(w_out.shape))
    run_adamw("ffn_w_down", r_down, rows2d, lambda o: o.reshape(ffn_w_down.shape))
    small_outs = adamw(r_small, pack_local(weights), pack_local(m_in), pack_local(v_in), "adamw_small")
    for kind in range(4):
        flat, off = small_outs[kind].reshape(-1), 0
        for n in small_names:
            wgt = weights[n]
            results.setdefault(n, [None] * 4)[kind] = flat[off:off + wgt.size].reshape(wgt.shape)
            off += wgt.size

    return (loss, grad_x, *[results[n][0] for n in PARAM_ORDER], *[results[n][1] for n in PARAM_ORDER],
            *[results[n][2] for n in PARAM_ORDER], *[results[n][3] for n in PARAM_ORDER])
```

```python
import jax
import jax.numpy as jnp
from jax import lax
from jax.experimental import pallas as pl
from jax.experimental.pallas import tpu as pltpu

F32 = jnp.float32
MXU_DTYPE = jnp.bfloat16

D = 1024
N_META = 16
CHUNK = 64
CHUNK_LOG2 = 6
INV_BASE_LOG2 = 3
FRONT = CHUNK - N_META
HEAD = CHUNK
CONV_CH = 512
CONV_W = 31
CONV_HALO = 32
NH = 4
DH = 128
DQ = NH * DH
DN_W = 4
DN_HALO = 8
DFF = 2816
PROJ_MAIN = 3072
D_IN = 3080
GATE_W = 128
LANES = 128
NDEV = 8
NORM_EPS = 1e-6
LN_EPS = 1e-5
L2_EPS = 1e-6
VMEM_LIMIT_V7X = 48 * 1024 * 1024
ROW_TILE = 640
ROW_TILE_SMALL = 128
MM_TILES = (1408, 1280, 1024, 640, 512, 256, 128)
MM_SUB = 4
GRAD_DTYPE = jnp.bfloat16
DELTA_CHUNKS = (4, 2, 1)
ADAM_BLOCK_BYTES = 8 * 1024 * 1024

ADAM_LR = 0.001
ADAM_B1 = 0.9
ADAM_B2 = 0.999
ADAM_EPS = 1e-08
ADAM_WD = 0.01
ADAM_STEP = 10

MESH_AXES = ("x", "y", "c")
NN = ((1,), (0,))
NT = ((1,), (1,))
TN = ((0,), (0,))

assert 1 << CHUNK_LOG2 == CHUNK


def _row_tile(t):
    return ROW_TILE if t % ROW_TILE == 0 else ROW_TILE_SMALL


def _padded_rows(seq):
    n = HEAD + seq
    tm = ROW_TILE if n >= 4 * ROW_TILE else ROW_TILE_SMALL
    return -(-n // tm) * tm


def _pick(n, prefs):
    for p in prefs:
        if n % p == 0:
            return p
    return n


def _lane_pad(n):
    return -(-n // LANES) * LANES


def _cp(sem):
    return pltpu.CompilerParams(dimension_semantics=sem, vmem_limit_bytes=VMEM_LIMIT_V7X)


def _sigmoid(x):
    return 1.0 / (1.0 + jnp.exp(-x))


def _softplus(x):
    return jnp.maximum(x, 0.0) + jnp.log(1.0 + jnp.exp(-jnp.abs(x)))


def _valid_rows(i, tm, seq, width, first=FRONT):
    rows = i * tm + lax.broadcasted_iota(jnp.int32, (tm, width), 0)
    return jnp.logical_and(rows >= first, rows < HEAD + seq)


def _dot(a, b, dims):
    return lax.dot_general(a, b, (dims, ((), ())), preferred_element_type=F32)


def _split(x, n):
    out, r = [], x
    for _ in range(n):
        p = r.astype(MXU_DTYPE)
        out.append(p)
        r = r - p.astype(F32)
    return out


def _mm1(a, b, dims):
    return _dot(a.astype(MXU_DTYPE), b.astype(MXU_DTYPE), dims)


def _mm1_many(a_list, b_list, dims):
    return [_mm1(a, b, dims) for a, b in zip(a_list, b_list)]


def _mm3_many(a_list, b_list, dims):
    sa = [_split(a, 2) for a in a_list]
    sb = [_split(b, 2) for b in b_list]
    hh = [_dot(x[0], y[0], dims) for x, y in zip(sa, sb)]
    hl = [_dot(x[0], y[1], dims) for x, y in zip(sa, sb)]
    lh = [_dot(x[1], y[0], dims) for x, y in zip(sa, sb)]
    return [p + (q + r) for p, q, r in zip(hh, hl, lh)]


def _mmx(e, b, dims):
    e = e.astype(MXU_DTYPE)
    b1, b2, b3 = _split(b, 3)
    return _dot(e, b1, dims) + (_dot(e, b2, dims) + _dot(e, b3, dims))


def mm(a, b, *, ta=False, tb=False, a_fn=None, a_vecs=(), extras=(), out_fn=None, out_dtypes=(F32,), tile_cap=None,
       name):
    a_list = list(a) if isinstance(a, (list, tuple)) else [a]
    (k_dim, m_dim) = a_list[0].shape if ta else a_list[0].shape[::-1]
    n_dim = b.shape[0] if tb else b.shape[1]
    assert (b.shape[1] if tb else b.shape[0]) == k_dim
    capped = MM_TILES if tile_cap is None else tuple(p for p in MM_TILES if p <= tile_cap)
    tm = _pick(m_dim, MM_TILES if ta else capped)
    tk = _pick(k_dim, capped if ta else MM_TILES)
    tn = _pick(n_dim, MM_TILES)
    nk = k_dim // tk
    na, nv, ne, no = len(a_list), len(a_vecs), len(extras), len(out_dtypes)
    dims = ((0,) if ta else (1,), (1,) if tb else (0,))
    nsub = MM_SUB if a_fn is not None and not ta and tm % (MM_SUB * 16) == 0 else 1
    sub = tm // nsub

    def body(*refs):
        a_refs, v_refs, b_ref = refs[:na], refs[na:na + nv], refs[na + nv]
        e_refs = refs[na + nv + 1:na + nv + 1 + ne]
        o_refs = refs[na + nv + 1 + ne:na + nv + 1 + ne + no]
        acc_ref = refs[-1] if nk > 1 else None
        k = pl.program_id(2)

        def left(rows):
            tiles = [r[rows, :] for r in a_refs]
            return tiles[0] if a_fn is None else a_fn(*tiles, *[v[...] for v in v_refs])

        def finish(acc, rows):
            outs = (acc,) if out_fn is None else out_fn(acc, *[e[rows, :] for e in e_refs])
            for o_ref, out in zip(o_refs, outs):
                o_ref[rows, :] = out.astype(o_ref.dtype)

        if nk > 1:
            @pl.when(k == 0)
            def _():
                acc_ref[...] = jnp.zeros_like(acc_ref)

        for r in range(nsub):
            rows = slice(r * sub, (r + 1) * sub) if nsub > 1 else slice(None)
            prod = _mm1(left(rows), b_ref[...], dims)
            if nk == 1:
                finish(prod, rows)
            else:
                acc_ref[rows, :] += prod

        if nk > 1:
            @pl.when(k == nk - 1)
            def _():
                finish(acc_ref[...], slice(None))

    if ta:
        a_spec = pl.BlockSpec((tk, tm), lambda i, j, k: (k, i))
        v_spec = pl.BlockSpec((1, tm), lambda i, j, k: (0, i))
    else:
        a_spec = pl.BlockSpec((tm, tk), lambda i, j, k: (i, k))
        v_spec = pl.BlockSpec((1, tk), lambda i, j, k: (0, k))
    b_spec = pl.BlockSpec((tn, tk), lambda i, j, k: (j, k)) if tb else pl.BlockSpec((tk, tn), lambda i, j, k: (k, j))
    o_spec = pl.BlockSpec((tm, tn), lambda i, j, k: (i, j))
    outs = pl.pallas_call(
        body, grid=(m_dim // tm, n_dim // tn, nk),
        in_specs=[a_spec] * na + [v_spec] * nv + [b_spec] + [o_spec] * ne, out_specs=[o_spec] * no,
        out_shape=[jax.ShapeDtypeStruct((m_dim, n_dim), dt) for dt in out_dtypes],
        scratch_shapes=[pltpu.VMEM((tm, tn), F32)] if nk > 1 else [],
        compiler_params=_cp(("parallel", "parallel", "arbitrary")), name=name)(*a_list, *a_vecs, b, *extras)
    return outs[0] if no == 1 else outs


def _rms_apply(x, w):
    assert x.shape[-1] == D
    return x * lax.rsqrt(jnp.mean(x * x, axis=-1, keepdims=True) + NORM_EPS) * w


def _swiglu(g, u):
    return g * _sigmoid(g) * u


def _swiglu_bwd(dact, g, u):
    sg = _sigmoid(g)
    return dact * u * (sg * (1.0 + g * (1.0 - sg))), dact * (g * sg)


def _add(acc, r):
    return (acc + r,)


def rms_bwd(dy, h, w, dres, name):
    t = h.shape[0]
    tm = _row_tile(t)

    def body(dy_ref, h_ref, w_ref, dres_ref, dh_ref, dw_ref):
        i = pl.program_id(0)
        x = h_ref[...]
        r = lax.rsqrt(jnp.mean(x * x, axis=-1, keepdims=True) + NORM_EPS)
        xh = x * r
        g = dy_ref[...] * w_ref[...]
        dh_ref[...] = dres_ref[...] + r * (g - xh * jnp.mean(g * xh, axis=-1, keepdims=True))

        @pl.when(i == 0)
        def _():
            dw_ref[...] = jnp.zeros_like(dw_ref)

        dw_ref[...] += jnp.sum(dy_ref[...] * xh, axis=0, keepdims=True)

    row = pl.BlockSpec((tm, D), lambda i: (i, 0))
    vec = pl.BlockSpec((1, D), lambda i: (0, 0))
    return pl.pallas_call(
        body, grid=(t // tm,), in_specs=[row, row, vec, row], out_specs=[row, vec],
        out_shape=[jax.ShapeDtypeStruct((t, D), F32), jax.ShapeDtypeStruct((1, D), F32)],
        compiler_params=_cp(("arbitrary",)), name=name)(dy, h, w, dres)


def loss_bwd(h, tgt, w, seq, name):
    t = h.shape[0]
    tm = _row_tile(t)

    def body(h_ref, t_ref, w_ref, dh_ref, loss_ref, dw_ref):
        i = pl.program_id(0)
        x = h_ref[...]
        wv = w_ref[...]
        r = lax.rsqrt(jnp.mean(x * x, axis=-1, keepdims=True) + NORM_EPS)
        xh = x * r
        err = jnp.where(_valid_rows(i, tm, seq, D, HEAD), xh * wv - t_ref[...], 0.0)
        dy = err * (1.0 / D)
        g = dy * wv
        dh_ref[...] = r * (g - xh * jnp.mean(g * xh, axis=-1, keepdims=True))

        @pl.when(i == 0)
        def _():
            dw_ref[...] = jnp.zeros_like(dw_ref)
            loss_ref[...] = jnp.zeros_like(loss_ref)

        dw_ref[...] += jnp.sum(dy * xh, axis=0, keepdims=True)
        part = jnp.sum(jnp.sum(err * err, axis=1, keepdims=True), axis=0, keepdims=True) * (0.5 / D)
        loss_ref[...] += jnp.broadcast_to(part, loss_ref.shape)

    row = pl.BlockSpec((tm, D), lambda i: (i, 0))
    vec = pl.BlockSpec((1, D), lambda i: (0, 0))
    return pl.pallas_call(
        body, grid=(t // tm,), in_specs=[row, row, vec],
        out_specs=[row, pl.BlockSpec((1, LANES), lambda i: (0, 0)), vec],
        out_shape=[jax.ShapeDtypeStruct((t, D), F32), jax.ShapeDtypeStruct((1, LANES), F32),
                   jax.ShapeDtypeStruct((1, D), F32)],
        compiler_params=_cp(("arbitrary",)), name=name)(h, tgt, w)


def _layernorm_parts(u1):
    mu = jnp.mean(u1, axis=-1, keepdims=True)
    xc = u1 - mu
    rstd = lax.rsqrt(jnp.mean(xc * xc, axis=-1, keepdims=True) + LN_EPS)
    return xc * rstd, rstd


SUBLANES = 8
CONV_ROWS = 16


def _shift_copies(ext, sh, tm):
    for s in range(1, SUBLANES):
        sh[s - 1, :, :] = ext[pl.ds(s, tm + CONV_HALO - SUBLANES), :]


def _window(ext, sh, off, rows, start=0):
    s, m = off % SUBLANES, off // SUBLANES
    if s == 0:
        return ext[pl.ds(start + off, rows), :]
    return sh[s - 1, pl.ds(start + SUBLANES * m, rows), :]


def _shift_scratch(tm):
    return pltpu.VMEM((SUBLANES - 1, tm + CONV_HALO - SUBLANES, CONV_CH), F32)


def conv_fwd(proj, w32, b, lw, lb, seq, name):
    t = proj.shape[0]
    tm = _row_tile(t)

    def body(cv_ref, cg_ref, w_ref, b_ref, lw_ref, lb_ref, y_ref, u1_ref, ext, sh):
        i = pl.program_id(0)

        @pl.when(i == 0)
        def _():
            ext[0:CONV_HALO, :] = jnp.zeros((CONV_HALO, CONV_CH), F32)

        @pl.when(i > 0)
        def _():
            ext[0:CONV_HALO, :] = ext[tm:tm + CONV_HALO, :]

        ext[CONV_HALO:CONV_HALO + tm, :] = cv_ref[...] * _sigmoid(cg_ref[...])
        _shift_copies(ext, sh, tm)
        acc = jnp.broadcast_to(b_ref[...], (tm, CONV_CH))
        for j in range(CONV_W):
            acc = acc + w_ref[j:j + 1, :] * _window(ext, sh, CONV_HALO - (CONV_W - 1) + j, tm)
        u1_ref[...] = acc
        xh, _ = _layernorm_parts(acc)
        ln = xh * lw_ref[...] + lb_ref[...]
        y = ln * _sigmoid(ln)
        y_ref[...] = jnp.where(_valid_rows(i, tm, seq, CONV_CH), y, 0.0).astype(y_ref.dtype)

    half = lambda c: pl.BlockSpec((tm, CONV_CH), lambda i: (i, c))
    vec = pl.BlockSpec((1, CONV_CH), lambda i: (0, 0))
    return pl.pallas_call(
        body, grid=(t // tm,),
        in_specs=[half(0), half(1), pl.BlockSpec((CONV_HALO, CONV_CH), lambda i: (0, 0)), vec, vec, vec],
        out_specs=[half(0), half(0)],
        out_shape=[jax.ShapeDtypeStruct((t, D), MXU_DTYPE), jax.ShapeDtypeStruct((t, CONV_CH), F32)],
        scratch_shapes=[pltpu.VMEM((tm + CONV_HALO, CONV_CH), F32), _shift_scratch(tm)],
        compiler_params=_cp(("arbitrary",)), name=name)(proj, proj, w32, b, lw, lb)


def conv_bwd(dy, u1, proj, w32, lw, lb, seq, name):
    t = proj.shape[0]
    tm = _row_tile(t)
    nt = t // tm
    per = tm // CONV_HALO

    def body(dy_ref, u1_ref, cv_ref, cg_ref, cvp_ref, cgp_ref, w_ref, lw_ref, lb_ref,
             dp_ref, dw_ref, db_ref, dlw_ref, dlb_ref, ext_d, ext_u, sh_d, sh_u, du0_s, dw_acc):
        i = pl.program_id(0)
        tile = nt - 1 - i

        @pl.when(i == 0)
        def _():
            ext_d[tm:tm + CONV_HALO, :] = jnp.zeros((CONV_HALO, CONV_CH), F32)
            dw_acc[...] = jnp.zeros_like(dw_acc)
            db_ref[...] = jnp.zeros_like(db_ref)
            dlw_ref[...] = jnp.zeros_like(dlw_ref)
            dlb_ref[...] = jnp.zeros_like(dlb_ref)

        @pl.when(i > 0)
        def _():
            ext_d[tm:tm + CONV_HALO, :] = ext_d[0:CONV_HALO, :]

        xh, rstd = _layernorm_parts(u1_ref[...])
        lwv = lw_ref[...]
        ln = xh * lwv + lb_ref[...]
        sg = _sigmoid(ln)
        dln = jnp.where(_valid_rows(tile, tm, seq, CONV_CH), dy_ref[...], 0.0) * (sg * (1.0 + ln * (1.0 - sg)))
        dlw_ref[...] += jnp.sum(dln * xh, axis=0, keepdims=True)
        dlb_ref[...] += jnp.sum(dln, axis=0, keepdims=True)
        dxh = dln * lwv
        du1 = rstd * (dxh - jnp.mean(dxh, axis=-1, keepdims=True)
                      - xh * jnp.mean(dxh * xh, axis=-1, keepdims=True))
        db_ref[...] += jnp.sum(du1, axis=0, keepdims=True)
        ext_d[0:tm, :] = du1

        cv = cv_ref[...]
        sgc = _sigmoid(cg_ref[...])
        prev = cvp_ref[...] * _sigmoid(cgp_ref[...])
        ext_u[0:CONV_HALO, :] = jnp.where(tile > 0, prev, 0.0)
        ext_u[CONV_HALO:CONV_HALO + tm, :] = cv * sgc

        _shift_copies(ext_d, sh_d, tm)
        _shift_copies(ext_u, sh_u, tm)

        def row_block(rb, carry):
            r0 = pl.multiple_of(rb * CONV_ROWS, CONV_ROWS)
            du1_b = ext_d[pl.ds(r0, CONV_ROWS), :]
            acc = jnp.zeros((CONV_ROWS, CONV_CH), F32)
            for j in range(CONV_W):
                acc = acc + w_ref[j:j + 1, :] * _window(ext_d, sh_d, CONV_W - 1 - j, CONV_ROWS, r0)
                prod = du1_b * _window(ext_u, sh_u, CONV_HALO - (CONV_W - 1) + j, CONV_ROWS, r0)
                dw_acc[j] += prod[0:SUBLANES, :] + prod[SUBLANES:CONV_ROWS, :]
            du0_s[pl.ds(r0, CONV_ROWS), :] = acc
            return carry

        lax.fori_loop(0, tm // CONV_ROWS, row_block, 0)

        @pl.when(i == nt - 1)
        def _():
            dw_ref[...] = jnp.sum(dw_acc[...], axis=1)

        du0 = du0_s[...]
        dp_ref[:, 0:CONV_CH] = (du0 * sgc).astype(dp_ref.dtype)
        dp_ref[:, CONV_CH:2 * CONV_CH] = (du0 * cv * sgc * (1.0 - sgc)).astype(dp_ref.dtype)

    rev = lambda c: pl.BlockSpec((tm, CONV_CH), lambda i: (nt - 1 - i, c))
    prev = lambda c: pl.BlockSpec((CONV_HALO, CONV_CH), lambda i: (jnp.maximum((nt - 1 - i) * per - 1, 0), c))
    vec = pl.BlockSpec((1, CONV_CH), lambda i: (0, 0))
    wspec = pl.BlockSpec((CONV_HALO, CONV_CH), lambda i: (0, 0))
    return pl.pallas_call(
        body, grid=(nt,),
        in_specs=[rev(0), rev(0), rev(0), rev(1), prev(0), prev(1), wspec, vec, vec],
        out_specs=[pl.BlockSpec((tm, 2 * CONV_CH), lambda i: (nt - 1 - i, 0)), wspec, vec, vec, vec],
        out_shape=[jax.ShapeDtypeStruct((t, PROJ_MAIN), MXU_DTYPE), jax.ShapeDtypeStruct((CONV_HALO, CONV_CH), F32),
                   jax.ShapeDtypeStruct((1, CONV_CH), F32), jax.ShapeDtypeStruct((1, CONV_CH), F32),
                   jax.ShapeDtypeStruct((1, CONV_CH), F32)],
        scratch_shapes=[pltpu.VMEM((tm + CONV_HALO, CONV_CH), F32), pltpu.VMEM((tm + CONV_HALO, CONV_CH), F32),
                        _shift_scratch(tm), _shift_scratch(tm), pltpu.VMEM((tm, CONV_CH), F32),
                        pltpu.VMEM((CONV_HALO, SUBLANES, CONV_CH), F32)],
        compiler_params=_cp(("arbitrary",)), name=name)(dy, u1, proj, proj, proj, proj, w32, lw, lb)


def dn_pre_fwd(proj, w8, name):
    t = proj.shape[0]
    tm = _row_tile(t)

    def body(raw_ref, w_ref, o_ref, ext):
        g = pl.program_id(0)
        i = pl.program_id(1)

        @pl.when(i == 0)
        def _():
            ext[0:DN_HALO, :] = jnp.zeros((DN_HALO, DQ), F32)

        @pl.when(i > 0)
        def _():
            ext[0:DN_HALO, :] = ext[tm:tm + DN_HALO, :]

        ext[DN_HALO:DN_HALO + tm, :] = raw_ref[...]
        c = jnp.zeros((tm, DQ), F32)
        for j in range(DN_W):
            c = c + w_ref[j:j + 1, :] * ext[pl.ds(DN_HALO - (DN_W - 1) + j, tm), :]
        s = c * _sigmoid(c)
        scale = jnp.where(g == 0, DH ** -0.5, 1.0)
        for h in range(NH):
            sh = s[:, h * DH:(h + 1) * DH]
            r = lax.rsqrt(jnp.sum(sh * sh, axis=-1, keepdims=True) + L2_EPS)
            o_ref[:, h * DH:(h + 1) * DH] = jnp.where(g == 2, sh, sh * (r * scale))

    return pl.pallas_call(
        body, grid=(3, t // tm),
        in_specs=[pl.BlockSpec((tm, DQ), lambda g, i: (i, 2 + g)), pl.BlockSpec((DN_HALO, DQ), lambda g, i: (0, g))],
        out_specs=pl.BlockSpec((tm, DQ), lambda g, i: (i, g)),
        out_shape=jax.ShapeDtypeStruct((t, 3 * DQ), F32),
        scratch_shapes=[pltpu.VMEM((tm + DN_HALO, DQ), F32)],
        compiler_params=_cp(("arbitrary", "arbitrary")), name=name)(proj, w8)


def dn_pre_bwd(dproj, dqkv, proj, w8, name):
    t = proj.shape[0]
    tm = _row_tile(t)
    nt = t // tm
    per = tm // DN_HALO

    def body(dp_in, d_ref, raw_ref, rawp_ref, w_ref, dp_ref, dw_ref, ext_d, ext_r):
        del dp_in
        g = pl.program_id(0)
        i = pl.program_id(1)
        tile = nt - 1 - i

        @pl.when(i == 0)
        def _():
            ext_d[tm:tm + DN_HALO, :] = jnp.zeros((DN_HALO, DQ), F32)
            dw_ref[...] = jnp.zeros_like(dw_ref)

        @pl.when(i > 0)
        def _():
            ext_d[tm:tm + DN_HALO, :] = ext_d[0:DN_HALO, :]

        ext_r[0:DN_HALO, :] = jnp.where(tile > 0, rawp_ref[...], 0.0)
        ext_r[DN_HALO:DN_HALO + tm, :] = raw_ref[...]
        taps = [ext_r[pl.ds(DN_HALO - (DN_W - 1) + j, tm), :] for j in range(DN_W)]
        c = jnp.zeros((tm, DQ), F32)
        for j in range(DN_W):
            c = c + w_ref[j:j + 1, :] * taps[j]
        sg = _sigmoid(c)
        s = c * sg
        scale = jnp.where(g == 0, DH ** -0.5, 1.0)
        for h in range(NH):
            sl = slice(h * DH, (h + 1) * DH)
            sh = s[:, sl]
            dn = d_ref[:, sl]
            r = lax.rsqrt(jnp.sum(sh * sh, axis=-1, keepdims=True) + L2_EPS)
            unit = sh * r
            dsn = (r * scale) * (dn - unit * jnp.sum(dn * unit, axis=-1, keepdims=True))
            ds = jnp.where(g == 2, dn, dsn)
            ext_d[0:tm, sl] = ds * (sg[:, sl] * (1.0 + c[:, sl] * (1.0 - sg[:, sl])))
        dc = ext_d[0:tm, :]
        draw = jnp.zeros((tm, DQ), F32)
        for j in range(DN_W):
            draw = draw + w_ref[j:j + 1, :] * ext_d[pl.ds(DN_W - 1 - j, tm), :]
            dw_ref[j:j + 1, :] += jnp.sum(dc * taps[j], axis=0, keepdims=True)
        dp_ref[...] = draw.astype(dp_ref.dtype)

    return pl.pallas_call(
        body, grid=(3, nt),
        in_specs=[pl.BlockSpec(memory_space=pl.ANY),
                  pl.BlockSpec((tm, DQ), lambda g, i: (nt - 1 - i, g)),
                  pl.BlockSpec((tm, DQ), lambda g, i: (nt - 1 - i, 2 + g)),
                  pl.BlockSpec((DN_HALO, DQ), lambda g, i: (jnp.maximum((nt - 1 - i) * per - 1, 0), 2 + g)),
                  pl.BlockSpec((DN_HALO, DQ), lambda g, i: (0, g))],
        out_specs=[pl.BlockSpec((tm, DQ), lambda g, i: (nt - 1 - i, 2 + g)),
                   pl.BlockSpec((DN_HALO, DQ), lambda g, i: (0, g))],
        out_shape=[jax.ShapeDtypeStruct(dproj.shape, dproj.dtype), jax.ShapeDtypeStruct((DN_HALO, 3 * DQ), F32)],
        scratch_shapes=[pltpu.VMEM((tm + DN_HALO, DQ), F32), pltpu.VMEM((tm + DN_HALO, DQ), F32)],
        input_output_aliases={0: 0},
        compiler_params=_cp(("arbitrary", "arbitrary")), name=name)(dproj, dqkv, proj, proj, w8)


def gate_fwd(pg, alog, dtb, seq, name):
    t = pg.shape[0]
    tm = _row_tile(t)

    def body(x_ref, al_ref, dt_ref, o_ref):
        i = pl.program_id(0)
        x = x_ref[...]
        lane = lax.broadcasted_iota(jnp.int32, (tm, GATE_W), 1)
        gg = -jnp.exp(al_ref[...]) * _softplus(x + dt_ref[...])
        out = jnp.where(lane < NH, _sigmoid(x), jnp.where(lane < 2 * NH, gg, 0.0))
        o_ref[...] = jnp.where(_valid_rows(i, tm, seq, GATE_W), out, 0.0)

    row = pl.BlockSpec((tm, GATE_W), lambda i: (i, 0))
    vec = pl.BlockSpec((1, GATE_W), lambda i: (0, 0))
    return pl.pallas_call(
        body, grid=(t // tm,), in_specs=[row, vec, vec], out_specs=row,
        out_shape=jax.ShapeDtypeStruct((t, GATE_W), F32), compiler_params=_cp(("parallel",)), name=name)(pg, alog, dtb)


def gate_bwd(dbg, pg, alog, dtb, seq, name):
    t = pg.shape[0]
    tm = _row_tile(t)

    def body(d_ref, x_ref, al_ref, dt_ref, o_ref, dal_ref, ddt_ref):
        i = pl.program_id(0)
        x = x_ref[...]
        lane = lax.broadcasted_iota(jnp.int32, (tm, GATE_W), 1)
        d = jnp.where(_valid_rows(i, tm, seq, GATE_W), d_ref[...], 0.0)
        beta = _sigmoid(x)
        xs = x + dt_ref[...]
        e = -jnp.exp(al_ref[...])
        is_g = jnp.logical_and(lane >= NH, lane < 2 * NH)
        da = jnp.where(is_g, d * e * _sigmoid(xs), 0.0)
        dgg = jnp.where(is_g, d * e * _softplus(xs), 0.0)
        o_ref[...] = jnp.where(lane < NH, d * beta * (1.0 - beta), da).astype(o_ref.dtype)

        @pl.when(i == 0)
        def _():
            dal_ref[...] = jnp.zeros_like(dal_ref)
            ddt_ref[...] = jnp.zeros_like(ddt_ref)

        dal_ref[...] += jnp.sum(dgg, axis=0, keepdims=True)
        ddt_ref[...] += jnp.sum(da, axis=0, keepdims=True)

    row = pl.BlockSpec((tm, GATE_W), lambda i: (i, 0))
    vec = pl.BlockSpec((1, GATE_W), lambda i: (0, 0))
    return pl.pallas_call(
        body, grid=(t // tm,), in_specs=[row, row, vec, vec], out_specs=[row, vec, vec],
        out_shape=[jax.ShapeDtypeStruct((t, GATE_W), MXU_DTYPE), jax.ShapeDtypeStruct((1, GATE_W), F32),
                   jax.ShapeDtypeStruct((1, GATE_W), F32)],
        compiler_params=_cp(("arbitrary",)), name=name)(dbg, pg, alog, dtb)


def _chunk_masks():
    ii = lax.broadcasted_iota(jnp.int32, (CHUNK, CHUNK), 0)
    jj = lax.broadcasted_iota(jnp.int32, (CHUNK, CHUNK), 1)
    return ii, jj, ii >= jj, ii > jj


def _lane_col(x, lane, idx):
    return jnp.sum(jnp.where(lane == idx, x, 0.0), axis=1, keepdims=True)


def _stack_rows(xs, ys):
    return [jnp.concatenate([x, y], axis=0) for x, y in zip(xs, ys)]


def _side_by_side(xs, ys):
    return [jnp.concatenate([x, y], axis=1) for x, y in zip(xs, ys)]


def _delta_terms(q, k, v, bgs, nch, low, strict):
    idx = [(c, h) for c in range(nch) for h in range(NH)]
    lane = lax.broadcasted_iota(jnp.int32, (CHUNK, GATE_W), 1)
    rowi = lax.broadcasted_iota(jnp.int32, (CHUNK, 1), 0)
    r4 = lax.broadcasted_iota(jnp.int32, (NH * CHUNK, GATE_W), 0)
    l4 = lax.broadcasted_iota(jnp.int32, (NH * CHUNK, GATE_W), 1)
    sel = jnp.where(l4 == NH + jnp.right_shift(r4, CHUNK_LOG2), 1.0, 0.0)
    lowf = jnp.where(low, 1.0, 0.0)
    gam_all = [_mmx(lowf, b, NN) for b in bgs]
    gam_rows = [_mmx(sel, g, NT) for g in gam_all]
    beta = [_lane_col(bgs[c], lane, h) for c, h in idx]
    gam = [_lane_col(gam_all[c], lane, NH + h) for c, h in idx]
    dm = [jnp.exp(jnp.where(low, g - gam_rows[c][h * CHUNK:(h + 1) * CHUNK, :], -1e30))
          for g, (c, h) in zip(gam, idx)]
    glast = [jnp.sum(jnp.where(rowi == CHUNK - 1, g, 0.0), axis=0, keepdims=True) for g in gam]
    eg = [jnp.exp(g) for g in gam]
    ekl = [jnp.exp(gl - g) for gl, g in zip(glast, gam)]
    gl = [jnp.exp(x) for x in glast]
    kb = [x * b for x, b in zip(k, beta)]
    vb = [x * b for x, b in zip(v, beta)]
    kbg = [x * e for x, e in zip(kb, eg)]
    kq = _mm1_many(_stack_rows(kb, q), k, NT)
    a_mat = [jnp.where(strict, x[:CHUNK] * d, 0.0) for x, d in zip(kq, dm)]
    p_mat = [jnp.where(low, x[CHUNK:] * d, 0.0) for x, d in zip(kq, dm)]
    qd = [x * e for x, e in zip(q, eg)]
    kd = [x * e for x, e in zip(k, ekl)]
    return dict(idx=idx, beta=beta, dm=dm, eg=eg, ekl=ekl, gl=gl, kb=kb, vb=vb, kbg=kbg, a=a_mat, p=p_mat, qd=qd,
                kd=kd, lane=lane, rowi=rowi)


def _unit_lower_inverses(a_list, ii, jj, eye):
    def same(log2):
        return jnp.right_shift(ii, log2) == jnp.right_shift(jj, log2)

    n = [-jnp.where(same(INV_BASE_LOG2), a, 0.0) for a in a_list]
    x = [eye + v for v in n]
    p = n
    for _ in range(INV_BASE_LOG2 - 1):
        p = _mm1_many(p, p, NN)
        x = [xi + y for xi, y in zip(x, _mm1_many(x, p, NN))]
    for log2 in range(INV_BASE_LOG2, CHUNK_LOG2):
        off = jnp.logical_and(same(log2 + 1), jnp.logical_not(same(log2)))
        a_off = [jnp.where(off, a, 0.0) for a in a_list]
        x = [xi - y for xi, y in zip(x, _mm1_many(x, _mm1_many(a_off, x, NN), NN))]
    return x


def _transposes(xs, eye):
    e = eye.astype(MXU_DTYPE)
    parts = [_split(x, 2) for x in xs]
    return [_dot(p[0], e, TN) + _dot(p[1], e, TN) for p in parts]


def _load_heads(ref, nch):
    return [ref[c * CHUNK:(c + 1) * CHUNK, h * DH:(h + 1) * DH] for c in range(nch) for h in range(NH)]


def delta_fwd(qkv, bg, name, push=None):
    t = qkv.shape[0]
    nc = t // CHUNK
    nch = _pick(nc, DELTA_CHUNKS)
    rows = nch * CHUNK
    ng = nc // nch
    npush = 0 if push is None else len(push)

    def body(*refs):
        q_ref, k_ref, v_ref, bg_ref = refs[:4]
        x_refs = refs[4:4 + npush]
        o_ref, sh_ref, mi_ref, u_ref, w_ref = refs[4 + npush:9 + npush]
        got_refs = refs[9 + npush:9 + 2 * npush]
        s_ref = refs[9 + 2 * npush]
        n = pl.program_id(0)
        if npush:
            copies = _push_copies(x_refs, got_refs, *refs[10 + 2 * npush:], scatter=False)

            @pl.when(n == 0)
            def _():
                _push_start(copies)

        @pl.when(n == 0)
        def _():
            s_ref[...] = jnp.zeros_like(s_ref)

        ii, jj, low, strict = _chunk_masks()
        eye = jnp.where(ii == jj, 1.0, 0.0)
        q, k, v = _load_heads(q_ref, nch), _load_heads(k_ref, nch), _load_heads(v_ref, nch)
        bgs = [bg_ref[c * CHUNK:(c + 1) * CHUNK, :] for c in range(nch)]
        tm_ = _delta_terms(q, k, v, bgs, nch, low, strict)
        m_inv = _unit_lower_inverses(tm_["a"], ii, jj, eye)
        uw = _mm3_many(m_inv, _side_by_side(tm_["vb"], tm_["kbg"]), NN)
        u = [x[:, :DH] for x in uw]
        w = [x[:, DH:] for x in uw]
        for i, (c, h) in enumerate(tm_["idx"]):
            mi_ref[c, h] = m_inv[i]
            u_ref[c * CHUNK:(c + 1) * CHUNK, h * DH:(h + 1) * DH] = u[i]
            w_ref[c * CHUNK:(c + 1) * CHUNK, h * DH:(h + 1) * DH] = w[i]
        wq = _stack_rows(w, tm_["qd"])
        s = [s_ref[h] for h in range(NH)]
        for c in range(nch):
            pr = range(c * NH, (c + 1) * NH)
            wqs = [_mm1(wq[i], s[i - c * NH], NN) for i in pr]
            ws = [x[:CHUNK] for x in wqs]
            qs = [x[CHUNK:] for x in wqs]
            vn = [u[i] - x for i, x in zip(pr, ws)]
            pv = [_mm1(tm_["p"][i], x, NN) for i, x in zip(pr, vn)]
            kv = [_mm1(tm_["kd"][i], x, TN) for i, x in zip(pr, vn)]
            for h in range(NH):
                o_ref[c * CHUNK:(c + 1) * CHUNK, h * DH:(h + 1) * DH] = qs[h] + pv[h]
                sh_ref[c, h] = s[h]
                s[h] = tm_["gl"][c * NH + h] * s[h] + kv[h]
        for h in range(NH):
            s_ref[h] = s[h]
        if npush:
            @pl.when(n == ng - 1)
            def _():
                _push_finish(copies)

    col = lambda c: pl.BlockSpec((rows, DQ), lambda n: (n, c))
    any_spec = pl.BlockSpec(memory_space=pl.ANY)
    pushed = [] if push is None else list(push)
    outs = pl.pallas_call(
        body, grid=(ng,),
        in_specs=[col(0), col(1), col(2), pl.BlockSpec((rows, GATE_W), lambda n: (n, 0))] + [any_spec] * npush,
        out_specs=[col(0), pl.BlockSpec((nch, NH, DH, DH), lambda n: (n, 0, 0, 0)),
                   pl.BlockSpec((nch, NH, CHUNK, CHUNK), lambda n: (n, 0, 0, 0)), col(0), col(0)]
        + [any_spec] * npush,
        out_shape=[jax.ShapeDtypeStruct((t, DQ), F32), jax.ShapeDtypeStruct((nc, NH, DH, DH), F32),
                   jax.ShapeDtypeStruct((nc, NH, CHUNK, CHUNK), F32), jax.ShapeDtypeStruct((t, DQ), F32),
                   jax.ShapeDtypeStruct((t, DQ), F32)] + _push_out_shapes(pushed, scatter=False),
        scratch_shapes=[pltpu.VMEM((NH, DH, DH), F32)] + (_push_sems(npush) if npush else []),
        compiler_params=_cp(("arbitrary",)), name=name)(qkv, qkv, qkv, bg, *pushed)
    return outs[:5], outs[5:]


def delta_bwd(qkv, bg, do, s_hist, m_hist, u_all, w_all, name, push=None):
    t = qkv.shape[0]
    nc = t // CHUNK
    nch = _pick(nc, DELTA_CHUNKS)
    rows = nch * CHUNK
    ng = nc // nch
    npush = 0 if push is None else len(push)

    def body(*refs):
        q_ref, k_ref, v_ref, bg_ref, do_ref, sh_ref, mi_ref, u_ref, w_ref = refs[:9]
        x_refs = refs[9:9 + npush]
        dqkv_ref, dbg_ref = refs[9 + npush:11 + npush]
        got_refs = refs[11 + npush:11 + 2 * npush]
        ds_ref = refs[11 + 2 * npush]
        n = pl.program_id(0)
        if npush:
            copies = _push_copies(x_refs, got_refs, *refs[12 + 2 * npush:], scatter=True)

            @pl.when(n == 0)
            def _():
                _push_start(copies)

        @pl.when(n == 0)
        def _():
            ds_ref[...] = jnp.zeros_like(ds_ref)

        ii, jj, low, strict = _chunk_masks()
        eye = jnp.where(ii == jj, 1.0, 0.0)
        q, k, v = _load_heads(q_ref, nch), _load_heads(k_ref, nch), _load_heads(v_ref, nch)
        d_o = _load_heads(do_ref, nch)
        bgs = [bg_ref[c * CHUNK:(c + 1) * CHUNK, :] for c in range(nch)]
        tm_ = _delta_terms(q, k, v, bgs, nch, low, strict)
        idx, lane, rowi = tm_["idx"], tm_["lane"], tm_["rowi"]
        beta, dm, eg, ekl, gl = tm_["beta"], tm_["dm"], tm_["eg"], tm_["ekl"], tm_["gl"]
        kb, kbg, qd, kd, a_mat, p_mat = tm_["kb"], tm_["kbg"], tm_["qd"], tm_["kd"], tm_["a"], tm_["p"]
        s = [sh_ref[c, h] for c, h in idx]
        m_inv = [mi_ref[c, h] for c, h in idx]
        u, w = _load_heads(u_ref, nch), _load_heads(w_ref, nch)
        ws = _mm1_many(w, s, NN)
        vn = [x - y for x, y in zip(u, ws)]
        qp_do = _mm1_many(_side_by_side(qd, p_mat), d_o, TN)
        qdo = [x[:DH] for x in qp_do]
        pdo = [x[DH:] for x in qp_do]
        dqd = _mm1_many(d_o, s, NT)
        dp = [jnp.where(low, x, 0.0) for x in _mm1_many(d_o, vn, NT)]

        nprob = len(idx)
        dvn, dkd, dgl = [None] * nprob, [None] * nprob, [None] * nprob
        ds = [ds_ref[h] for h in range(NH)]
        for c in reversed(range(nch)):
            pr = list(range(c * NH, (c + 1) * NH))
            kds = [_mm1(kd[i], ds[i - c * NH], NN) for i in pr]
            for i, x in zip(pr, kds):
                dvn[i] = pdo[i] + x
            wdv = [_mm1(w[i], dvn[i], TN) for i in pr]
            for i in pr:
                h = i - c * NH
                dkd[i] = _mm1(vn[i], ds[h], NT)
                dgl[i] = jnp.sum(jnp.sum(s[i] * ds[h], axis=1, keepdims=True), axis=0, keepdims=True)
                ds[h] = qdo[i] + gl[i] * ds[h] - wdv[h]
        for h in range(NH):
            ds_ref[h] = ds[h]

        dw = [-x for x in _mm1_many(dvn, s, NT)]
        dvb_dkbg = _mm3_many(m_inv, _side_by_side(dvn, dw), TN)
        dvb = [x[:, :DH] for x in dvb_dkbg]
        dkbg = [x[:, DH:] for x in dvb_dkbg]
        da = [-jnp.where(strict, x, 0.0)
              for x in _mm1_many(dvb_dkbg, _side_by_side(u, w), NT)]
        gm = [x * d for x, d in zip(da, dm)]
        hm = [x * d for x, d in zip(dp, dm)]
        gh = _stack_rows(gm, hm)
        gh_k = _mm1_many(gh, k, NN)
        gk = [x[:CHUNK] for x in gh_k]
        hk = [x[CHUNK:] for x in gh_k]
        gkb_hq = _mm1_many(gh, _stack_rows(kb, q), TN)
        em = [x * a + y * p for x, a, y, p in zip(da, a_mat, dp, p_mat)]
        em_t = _transposes(em, eye)
        dbeta_all = [jnp.zeros((CHUNK, GATE_W), F32) for _ in range(nch)]
        dgam_all = [jnp.zeros((CHUNK, GATE_W), F32) for _ in range(nch)]
        for i, (c, h) in enumerate(idx):
            dkb = gk[i] + dkbg[i] * eg[i]
            dk = gkb_hq[i] + dkd[i] * ekl[i] + beta[i] * dkb
            dq = hk[i] + dqd[i] * eg[i]
            dkd_kd = jnp.sum(dkd[i] * kd[i], axis=1, keepdims=True)
            dgam = (jnp.sum(em[i], axis=1, keepdims=True) - jnp.sum(em_t[i], axis=1, keepdims=True)
                    + jnp.sum(dqd[i] * qd[i], axis=1, keepdims=True) - dkd_kd
                    + jnp.sum(dkbg[i] * kbg[i], axis=1, keepdims=True))
            tail = jnp.sum(dkd_kd, axis=0, keepdims=True) + dgl[i] * gl[i]
            dgam = dgam + jnp.where(rowi == CHUNK - 1, tail, 0.0)
            dbeta = jnp.sum(dkb * k[i], axis=1, keepdims=True) + jnp.sum(dvb[i] * v[i], axis=1, keepdims=True)
            rs = slice(c * CHUNK, (c + 1) * CHUNK)
            dqkv_ref[rs, h * DH:(h + 1) * DH] = dq
            dqkv_ref[rs, DQ + h * DH:DQ + (h + 1) * DH] = dk
            dqkv_ref[rs, 2 * DQ + h * DH:2 * DQ + (h + 1) * DH] = beta[i] * dvb[i]
            dbeta_all[c] = dbeta_all[c] + jnp.where(lane == h, dbeta, 0.0)
            dgam_all[c] = dgam_all[c] + jnp.where(lane == NH + h, dgam, 0.0)
        upf = jnp.where(ii <= jj, 1.0, 0.0)
        for c in range(nch):
            dg_all = _mmx(upf, dgam_all[c], NN)
            dbg_ref[c * CHUNK:(c + 1) * CHUNK, :] = jnp.where(lane < NH, dbeta_all[c], dg_all)
        if npush:
            @pl.when(n == ng - 1)
            def _():
                _push_finish(copies)

    col = lambda c: pl.BlockSpec((rows, DQ), lambda n: (ng - 1 - n, c))
    gate = pl.BlockSpec((rows, GATE_W), lambda n: (ng - 1 - n, 0))
    any_spec = pl.BlockSpec(memory_space=pl.ANY)
    pushed = [] if push is None else list(push)
    outs = pl.pallas_call(
        body, grid=(ng,),
        in_specs=[col(0), col(1), col(2), gate, col(0),
                  pl.BlockSpec((nch, NH, DH, DH), lambda n: (ng - 1 - n, 0, 0, 0)),
                  pl.BlockSpec((nch, NH, CHUNK, CHUNK), lambda n: (ng - 1 - n, 0, 0, 0)), col(0), col(0)]
        + [any_spec] * npush,
        out_specs=[pl.BlockSpec((rows, 3 * DQ), lambda n: (ng - 1 - n, 0)), gate] + [any_spec] * npush,
        out_shape=[jax.ShapeDtypeStruct((t, 3 * DQ), F32), jax.ShapeDtypeStruct((t, GATE_W), F32)]
        + _push_out_shapes(pushed, scatter=True),
        scratch_shapes=[pltpu.VMEM((NH, DH, DH), F32)] + (_push_sems(npush) if npush else []),
        compiler_params=_cp(("arbitrary",)), name=name)(qkv, qkv, qkv, bg, do, s_hist, m_hist, u_all, w_all, *pushed)
    return outs[0], outs[1], outs[2:]


def dn_post_fwd(ybuf, o, proj, nw, name):
    t = o.shape[0]
    tm = _row_tile(t)

    def body(y_in, o_ref, z_ref, nw_ref, y_ref):
        del y_in
        nwv = nw_ref[...]
        for h in range(NH):
            sl = slice(h * DH, (h + 1) * DH)
            oh = o_ref[:, sl]
            z = z_ref[:, sl]
            r = lax.rsqrt(jnp.mean(oh * oh, axis=-1, keepdims=True) + NORM_EPS)
            y_ref[:, sl] = (oh * r * nwv * (z * _sigmoid(z))).astype(y_ref.dtype)

    return pl.pallas_call(
        body, grid=(t // tm,),
        in_specs=[pl.BlockSpec(memory_space=pl.ANY), pl.BlockSpec((tm, DQ), lambda i: (i, 0)),
                  pl.BlockSpec((tm, DQ), lambda i: (i, 5)), pl.BlockSpec((1, DH), lambda i: (0, 0))],
        out_specs=pl.BlockSpec((tm, DQ), lambda i: (i, 1)),
        out_shape=jax.ShapeDtypeStruct(ybuf.shape, ybuf.dtype), input_output_aliases={0: 0},
        compiler_params=_cp(("parallel",)), name=name)(ybuf, o, proj, nw)


def dn_post_bwd(dproj, dy, o, proj, nw, name):
    t = o.shape[0]
    tm = _row_tile(t)

    def body(dp_in, dy_ref, o_ref, z_ref, nw_ref, do_ref, dp_ref, dnw_ref):
        del dp_in
        i = pl.program_id(0)
        nwv = nw_ref[...]
        acc = jnp.zeros((1, DH), F32)
        for h in range(NH):
            sl = slice(h * DH, (h + 1) * DH)
            oh = o_ref[:, sl]
            z = z_ref[:, sl]
            dyh = dy_ref[:, sl]
            r = lax.rsqrt(jnp.mean(oh * oh, axis=-1, keepdims=True) + NORM_EPS)
            xh = oh * r
            sg = _sigmoid(z)
            sz = z * sg
            dxh = dyh * nwv * sz
            do_ref[:, sl] = r * (dxh - xh * jnp.mean(dxh * xh, axis=-1, keepdims=True))
            dp_ref[:, sl] = (dyh * xh * nwv * (sg * (1.0 + z * (1.0 - sg)))).astype(dp_ref.dtype)
            acc = acc + jnp.sum(dyh * xh * sz, axis=0, keepdims=True)

        @pl.when(i == 0)
        def _():
            dnw_ref[...] = jnp.zeros_like(dnw_ref)

        dnw_ref[...] += acc

    vec = pl.BlockSpec((1, DH), lambda i: (0, 0))
    return pl.pallas_call(
        body, grid=(t // tm,),
        in_specs=[pl.BlockSpec(memory_space=pl.ANY), pl.BlockSpec((tm, DQ), lambda i: (i, 1)),
                  pl.BlockSpec((tm, DQ), lambda i: (i, 0)), pl.BlockSpec((tm, DQ), lambda i: (i, 5)), vec],
        out_specs=[pl.BlockSpec((tm, DQ), lambda i: (i, 0)), pl.BlockSpec((tm, DQ), lambda i: (i, 5)), vec],
        out_shape=[jax.ShapeDtypeStruct((t, DQ), F32), jax.ShapeDtypeStruct(dproj.shape, dproj.dtype),
                   jax.ShapeDtypeStruct((1, DH), F32)],
        input_output_aliases={0: 1}, compiler_params=_cp(("arbitrary",)), name=name)(dproj, dy, o, proj, nw)


def _shifted(first, second, s, lane):
    if s == 0:
        return first
    return jnp.where(lane < LANES - s, pltpu.roll(first, LANES - s, 1), pltpu.roll(second, LANES - s, 1))


def unshard_cols(g8, w, widths, name):
    _, r, wp = g8.shape
    rb = _pick(r, (256, 128, 64, 32, 16))

    def body(g_ref, *o_refs):
        lane = lax.broadcasted_iota(jnp.int32, (rb, LANES), 1)
        zeros = jnp.zeros((rb, LANES), F32)

        def src(j, ta):
            if j >= NDEV or ta * LANES >= wp:
                return zeros
            return g_ref[j, :, ta * LANES:(ta + 1) * LANES].astype(F32)

        base = 0
        for o_ref, width in zip(o_refs, widths):
            for b in range(width // LANES):
                c0 = base + b * LANES
                if c0 >= NDEV * w:
                    tile = zeros
                else:
                    j0, o0 = divmod(c0, w)
                    n0 = min(w - o0, LANES)
                    ta, s = divmod(o0, LANES)
                    tile = _shifted(src(j0, ta), src(j0, ta + 1), s, lane)
                    if n0 < LANES:
                        nxt = pltpu.roll(src(j0 + 1, 0), n0, 1) if j0 + 1 < NDEV else zeros
                        tile = jnp.where(lane < n0, tile, nxt)
                o_ref[:, b * LANES:(b + 1) * LANES] = tile.astype(o_ref.dtype)
            base += width

    return pl.pallas_call(
        body, grid=(r // rb,), in_specs=[pl.BlockSpec((NDEV, rb, wp), lambda i: (0, i, 0))],
        out_specs=[pl.BlockSpec((rb, width), lambda i: (i, 0)) for width in widths],
        out_shape=[jax.ShapeDtypeStruct((r, width), g8.dtype) for width in widths],
        compiler_params=_cp(("parallel",)), name=name)(g8)


def shard_cols(parts, w, name):
    r = parts[0].shape[0]
    wp = _lane_pad(w)
    rb = _pick(r, (256, 128, 64, 32, 16))
    tiles_of = [p.shape[1] // LANES for p in parts]

    def body(*refs):
        p_refs, o_ref = refs[:-1], refs[-1]
        lane = lax.broadcasted_iota(jnp.int32, (rb, LANES), 1)
        zeros = jnp.zeros((rb, LANES), F32)

        def glob(tile_idx):
            for p_ref, n_tiles in zip(p_refs, tiles_of):
                if tile_idx < n_tiles:
                    return p_ref[:, tile_idx * LANES:(tile_idx + 1) * LANES].astype(F32)
                tile_idx -= n_tiles
            return zeros

        for j in range(NDEV):
            for a in range(wp // LANES):
                nv = min(w - a * LANES, LANES)
                tb, s = divmod(w * j + a * LANES, LANES)
                tile = _shifted(glob(tb), glob(tb + 1), s, lane)
                if nv < LANES:
                    tile = jnp.where(lane < nv, tile, 0.0)
                o_ref[j, :, a * LANES:(a + 1) * LANES] = tile.astype(o_ref.dtype)

    return pl.pallas_call(
        body, grid=(r // rb,), in_specs=[pl.BlockSpec((rb, p.shape[1]), lambda i: (i, 0)) for p in parts],
        out_specs=pl.BlockSpec((NDEV, rb, wp), lambda i: (0, i, 0)),
        out_shape=jax.ShapeDtypeStruct((NDEV, r, wp), GRAD_DTYPE),
        compiler_params=_cp(("parallel",)), name=name)(*parts)


def _me_and_peers():
    mx, my, mc = lax.axis_index("x"), lax.axis_index("y"), lax.axis_index("c")
    me = 4 * mx + 2 * my + mc
    peers = []
    for kk in range(1, NDEV):
        px = 1 - mx if kk & 4 else mx
        py = 1 - my if kk & 2 else my
        pc = 1 - mc if kk & 1 else mc
        peers.append(((px, py, pc), 4 * px + 2 * py + pc))
    return me, peers


def _push_copies(x_refs, o_refs, send_sems, recv_sems, local_sems, scatter):
    me, peers = _me_and_peers()
    npeer = NDEV - 1
    local, sends, recvs = [], [], []
    for a, (x_ref, o_ref) in enumerate(zip(x_refs, o_refs)):
        local.append(pltpu.make_async_copy(x_ref.at[me] if scatter else x_ref, o_ref.at[me], local_sems.at[a]))
        for kk, (peer, pidx) in enumerate(peers):
            src = x_ref.at[pidx] if scatter else x_ref
            sems = dict(send_sem=send_sems.at[a * npeer + kk], recv_sem=recv_sems.at[a * npeer + kk],
                        device_id=peer, device_id_type=pl.DeviceIdType.MESH)
            sends.append(pltpu.make_async_remote_copy(src_ref=src, dst_ref=o_ref.at[me], **sems))
            recvs.append(pltpu.make_async_remote_copy(src_ref=src, dst_ref=o_ref.at[pidx], **sems))
    return local, sends, recvs


def _push_start(copies):
    local, sends, _ = copies
    for cp in local + sends:
        cp.start()


def _push_finish(copies):
    local, sends, recvs = copies
    for cp in recvs:
        cp.wait_recv()
    for cp in sends:
        cp.wait_send()
    for cp in local:
        cp.wait()


def _push_out_shapes(xs, scatter):
    return [jax.ShapeDtypeStruct(x.shape if scatter else (NDEV,) + x.shape, x.dtype) for x in xs]


def _push_sems(n):
    return [pltpu.SemaphoreType.DMA((n * (NDEV - 1),)), pltpu.SemaphoreType.DMA((n * (NDEV - 1),)),
            pltpu.SemaphoreType.DMA((n,))]


def _push_to_all(xs, name, scatter):
    n = len(xs)

    def body(*refs):
        copies = _push_copies(refs[:n], refs[n:2 * n], *refs[2 * n:], scatter=scatter)
        _push_start(copies)
        _push_finish(copies)

    any_spec = pl.BlockSpec(memory_space=pl.ANY)
    return pl.pallas_call(
        body, in_specs=[any_spec] * n, out_specs=[any_spec] * n, out_shape=_push_out_shapes(xs, scatter),
        scratch_shapes=_push_sems(n), name=name)(*xs)


def adamw(recv, w, m, v, name):
    rows, cols = w.shape
    c1 = 1.0 - ADAM_B1 ** ADAM_STEP
    c2 = 1.0 - ADAM_B2 ** ADAM_STEP
    cap = ADAM_BLOCK_BYTES // (NDEV * cols * 4)
    rb = _pick(rows, [p for p in (2048, 1024, 512, 256, 128, 64, 32, 16, 8) if p <= cap])

    def body(r_ref, w_ref, m_ref, v_ref, g_ref, d_ref, m2_ref, v2_ref):
        g = r_ref[0].astype(F32)
        for j in range(1, NDEV):
            g = g + r_ref[j].astype(F32)
        m2 = ADAM_B1 * m_ref[...] + (1.0 - ADAM_B1) * g
        v2 = ADAM_B2 * v_ref[...] + (1.0 - ADAM_B2) * (g * g)
        g_ref[...] = g
        m2_ref[...] = m2
        v2_ref[...] = v2
        d_ref[...] = -ADAM_LR * ((m2 / c1) / (jnp.sqrt(v2 / c2) + ADAM_EPS) + ADAM_WD * w_ref[...])

    blk = pl.BlockSpec((rb, cols), lambda i: (i, 0))
    return pl.pallas_call(
        body, grid=(rows // rb,),
        in_specs=[pl.BlockSpec((NDEV, rb, cols), lambda i: (0, i, 0)), blk, blk, blk],
        out_specs=[blk, blk, blk, blk], out_shape=[jax.ShapeDtypeStruct((rows, cols), F32)] * 4,
        compiler_params=_cp(("parallel",)), name=name)(recv, w, m, v)


SMALL_SHARDED = ("meta_tokens", "conv_dw_w", "dn_conv_w")
SMALL_REPLICATED = ("norm_mix_w", "conv_dw_b", "conv_ln_w", "conv_ln_b", "dn_A_log", "dn_dt_bias", "dn_norm_w",
                    "norm_ffn_w", "final_norm_w")
PARAM_ORDER = ("meta_tokens", "norm_mix_w", "w_in", "conv_dw_w", "conv_dw_b", "conv_ln_w", "conv_ln_b", "dn_conv_w",
               "dn_A_log", "dn_dt_bias", "dn_norm_w", "w_out", "norm_ffn_w", "ffn_w_gu", "ffn_w_down", "final_norm_w")


def _pack_small(parts, axis):
    flat = jnp.concatenate(parts, axis=axis)
    n = flat.shape[axis]
    total = -(-n // (8 * LANES)) * (8 * LANES)
    pad = [(0, 0)] * flat.ndim
    pad[axis] = (0, total - n)
    flat = jnp.pad(flat, pad)
    return flat.reshape(flat.shape[:axis] + (total // LANES, LANES))


def _unshard_last(g8):
    moved = jnp.moveaxis(g8, 0, -2)
    return moved.reshape(moved.shape[:-2] + (-1,))


def _per_destination_last(full):
    split = full.reshape(full.shape[:-1] + (NDEV, full.shape[-1] // NDEV))
    return jnp.moveaxis(split, -2, 0).reshape(NDEV, -1)


def _lane_row(vec4, width):
    return jnp.pad(vec4, (NH, width - 2 * NH))[None]


def kernel(x, meta_tokens, norm_mix_w, w_in, conv_dw_w, conv_dw_b, conv_ln_w, conv_ln_b, dn_conv_w, dn_A_log, dn_dt_bias, dn_norm_w, w_out, norm_ffn_w, ffn_w_gu, ffn_w_down, final_norm_w, loss_target, m_meta_tokens, m_norm_mix_w, m_w_in, m_conv_dw_w, m_conv_dw_b, m_conv_ln_w, m_conv_ln_b, m_dn_conv_w, m_dn_A_log, m_dn_dt_bias, m_dn_norm_w, m_w_out, m_norm_ffn_w, m_ffn_w_gu, m_ffn_w_down, m_final_norm_w, v_meta_tokens, v_norm_mix_w, v_w_in, v_conv_dw_w, v_conv_dw_b, v_conv_ln_w, v_conv_ln_b, v_dn_conv_w, v_dn_A_log, v_dn_dt_bias, v_dn_norm_w, v_w_out, v_norm_ffn_w, v_ffn_w_gu, v_ffn_w_down, v_final_norm_w):
    weights = dict(meta_tokens=meta_tokens, norm_mix_w=norm_mix_w, w_in=w_in, conv_dw_w=conv_dw_w, conv_dw_b=conv_dw_b,
                   conv_ln_w=conv_ln_w, conv_ln_b=conv_ln_b, dn_conv_w=dn_conv_w, dn_A_log=dn_A_log,
                   dn_dt_bias=dn_dt_bias, dn_norm_w=dn_norm_w, w_out=w_out, norm_ffn_w=norm_ffn_w, ffn_w_gu=ffn_w_gu,
                   ffn_w_down=ffn_w_down, final_norm_w=final_norm_w)
    m_in = dict(meta_tokens=m_meta_tokens, norm_mix_w=m_norm_mix_w, w_in=m_w_in, conv_dw_w=m_conv_dw_w,
                conv_dw_b=m_conv_dw_b, conv_ln_w=m_conv_ln_w, conv_ln_b=m_conv_ln_b, dn_conv_w=m_dn_conv_w,
                dn_A_log=m_dn_A_log, dn_dt_bias=m_dn_dt_bias, dn_norm_w=m_dn_norm_w, w_out=m_w_out,
                norm_ffn_w=m_norm_ffn_w, ffn_w_gu=m_ffn_w_gu, ffn_w_down=m_ffn_w_down, final_norm_w=m_final_norm_w)
    v_in = dict(meta_tokens=v_meta_tokens, norm_mix_w=v_norm_mix_w, w_in=v_w_in, conv_dw_w=v_conv_dw_w,
                conv_dw_b=v_conv_dw_b, conv_ln_w=v_conv_ln_w, conv_ln_b=v_conv_ln_b, dn_conv_w=v_dn_conv_w,
                dn_A_log=v_dn_A_log, dn_dt_bias=v_dn_dt_bias, dn_norm_w=v_dn_norm_w, w_out=v_w_out,
                norm_ffn_w=v_norm_ffn_w, ffn_w_gu=v_ffn_w_gu, ffn_w_down=v_ffn_w_down, final_norm_w=v_final_norm_w)

    depth = w_in.shape[0]
    seq = x.shape[1]
    t = _padded_rows(seq)
    rows_d = depth * D
    win_w, gu_w = w_in.shape[2], ffn_w_gu.shape[2]
    win_wp, gu_wp = _lane_pad(win_w), _lane_pad(gu_w)

    def pad_cols(a, wp):
        return jnp.pad(a, ((0, 0), (0, 0), (0, wp - a.shape[2]))).reshape(rows_d, wp)

    def rows2d(a):
        return a.reshape(-1, a.shape[2])

    small_shards = [weights[n] for n in SMALL_SHARDED]
    win_p = pad_cols(w_in, win_wp).astype(MXU_DTYPE).reshape(depth, D, win_wp)
    gu_p = pad_cols(ffn_w_gu, gu_wp).astype(MXU_DTYPE).reshape(depth, D, gu_wp)
    wout_b, wdown_b = w_out.astype(MXU_DTYPE), ffn_w_down.astype(MXU_DTYPE)
    layer_shards = lambda l: [win_p[l], gu_p[l], wout_b[l], wdown_b[l]]

    def whole_weights(got):
        g_win, g_gu, g_wout, g_down = got
        main, gate_cols = unshard_cols(g_win, win_w, [PROJ_MAIN, GATE_W], "unshard_w_in")
        wg, wu = unshard_cols(g_gu, gu_w, [DFF, DFF], "unshard_w_gu")
        return dict(main=main, gate=gate_cols, wg=wg, wu=wu, out=g_wout.reshape(D, D), down=g_down.reshape(DFF, D))

    *got0, g_small = _push_to_all(layer_shards(0) + [_pack_small([s.reshape(-1) for s in small_shards], 0)],
                                  "gather_first", scatter=False)
    wts = [whole_weights(got0)]
    small_flat, off, small_full = g_small.reshape(NDEV, -1), 0, {}
    for n, s in zip(SMALL_SHARDED, small_shards):
        small_full[n] = _unshard_last(small_flat[:, off:off + s.size].reshape((NDEV,) + s.shape))
        off += s.size
    cdw32 = jnp.pad(small_full["conv_dw_w"], ((0, 0), (0, CONV_HALO - CONV_W), (0, 0)))
    dcw8 = jnp.pad(small_full["dn_conv_w"], ((0, 0), (0, DN_HALO - DN_W), (0, 0)))

    h = jnp.concatenate([jnp.zeros((FRONT, D), F32), small_full["meta_tokens"], x[0],
                         jnp.zeros((t - HEAD - seq, D), F32)], axis=0)
    tgt = jnp.pad(loss_target[0], ((HEAD, t - HEAD - seq), (0, 0)))

    saved = []
    for l in range(depth):
        nmw, nfw = norm_mix_w[l][None], norm_ffn_w[l][None]
        cdb, clw, clb = conv_dw_b[l][None], conv_ln_w[l][None], conv_ln_b[l][None]
        alog, dtb, dnw = _lane_row(dn_A_log[l], GATE_W), _lane_row(dn_dt_bias[l], GATE_W), dn_norm_w[l][None]
        wl = wts[l]
        proj = mm([h], wl["main"], a_fn=_rms_apply, a_vecs=[nmw], name="mm_proj")
        pg = mm([h], wl["gate"], a_fn=_rms_apply, a_vecs=[nmw], name="mm_proj_gate")
        ybuf, u1 = conv_fwd(proj, cdw32[l], cdb, clw, clb, seq, "conv_fwd")
        qkv = dn_pre_fwd(proj, dcw8[l], "dn_pre_fwd")
        bg = gate_fwd(pg, alog, dtb, seq, "gate_fwd")
        if l + 1 < depth:
            (o, s_hist, m_hist, u_all, w_all), got = delta_fwd(qkv, bg, "delta_fwd_gather", push=layer_shards(l + 1))
            wts.append(whole_weights(got))
        else:
            (o, s_hist, m_hist, u_all, w_all), _ = delta_fwd(qkv, bg, "delta_fwd")
        ybuf = dn_post_fwd(ybuf, o, proj, dnw, "dn_post_fwd")
        h_mid = mm(ybuf, wl["out"], extras=[h], out_fn=_add, name="mm_out")
        gate = mm([h_mid], wl["wg"], a_fn=_rms_apply, a_vecs=[nfw], name="mm_gate")
        up = mm([h_mid], wl["wu"], a_fn=_rms_apply, a_vecs=[nfw], name="mm_up")
        h_out = mm([gate, up], wl["down"], a_fn=_swiglu, extras=[h_mid], out_fn=_add, tile_cap=ROW_TILE, name="mm_down")
        saved.append(dict(h=h, proj=proj, pg=pg, ybuf=ybuf, u1=u1, qkv=qkv, bg=bg, o=o, s_hist=s_hist,
                          m_hist=m_hist, u_all=u_all, w_all=w_all, h_mid=h_mid, gate=gate, up=up,
                          nmw=nmw, nfw=nfw, clw=clw, clb=clb, alog=alog, dtb=dtb, dnw=dnw))
        h = h_out

    dh, loss_part, d_final = loss_bwd(h, tgt, final_norm_w[None], seq, "loss_bwd")
    loss = lax.psum(loss_part[0, 0], MESH_AXES)

    per_layer = ("norm_mix_w", "conv_dw_w", "conv_dw_b", "conv_ln_w", "conv_ln_b", "dn_conv_w", "dn_A_log", "dn_dt_bias",
                 "dn_norm_w", "norm_ffn_w")
    grads = {n: [None] * depth for n in per_layer}
    received = [None] * depth
    pending = None
    dw_mm = lambda a, b, name, **kw: mm(a, b, ta=True, out_dtypes=(GRAD_DTYPE,), name=name, **kw)
    for l in reversed(range(depth)):
        s, wl = saved[l], wts[l]
        dgate, dup = mm(dh, wl["down"], tb=True, extras=[s["gate"], s["up"]], out_fn=_swiglu_bwd,
                        out_dtypes=(MXU_DTYPE, MXU_DTYPE), tile_cap=ROW_TILE, name="mm_down_dx")
        d_down = dw_mm([s["gate"], s["up"]], dh, "mm_down_dw", a_fn=_swiglu, tile_cap=ROW_TILE)
        dhn2 = mm(dup, wl["wu"], tb=True, extras=[mm(dgate, wl["wg"], tb=True, name="mm_gate_dx")], out_fn=_add,
                  name="mm_up_dx")
        dh_mid, dnfw = rms_bwd(dhn2, s["h_mid"], s["nfw"], dh, "rms_ffn_bwd")
        d_wg = dw_mm([s["h_mid"]], dgate, "mm_gate_dw", a_fn=_rms_apply, a_vecs=[s["nfw"]])
        d_wu = dw_mm([s["h_mid"]], dup, "mm_up_dw", a_fn=_rms_apply, a_vecs=[s["nfw"]])
        dy = mm(dh_mid, wl["out"], tb=True, name="mm_out_dx")
        d_out = dw_mm(s["ybuf"], dh_mid, "mm_out_dw")
        dproj, dcdw, dcdb, dclw, dclb = conv_bwd(dy, s["u1"], s["proj"], cdw32[l], s["clw"], s["clb"], seq, "conv_bwd")
        do, dproj, ddnw = dn_post_bwd(dproj, dy, s["o"], s["proj"], s["dnw"], "dn_post_bwd")
        delta_args = (s["qkv"], s["bg"], do, s["s_hist"], s["m_hist"], s["u_all"], s["w_all"])
        if pending is None:
            dqkv, dbg, _ = delta_bwd(*delta_args, "delta_bwd")
        else:
            dqkv, dbg, received[l + 1] = delta_bwd(*delta_args, "delta_bwd_exchange", push=pending)
        dproj, ddcw = dn_pre_bwd(dproj, dqkv, s["proj"], dcw8[l], "dn_pre_bwd")
        dpg, dalog, ddtb = gate_bwd(dbg, s["pg"], s["alog"], s["dtb"], seq, "gate_bwd")
        dhn_gate = mm(dpg, wl["gate"], tb=True, name="mm_proj_gate_dx")
        dhn = mm(dproj, wl["main"], tb=True, extras=[dhn_gate], out_fn=_add, name="mm_proj_dx")
        d_main = dw_mm([s["h"]], dproj, "mm_proj_dw", a_fn=_rms_apply, a_vecs=[s["nmw"]])
        d_gate_cols = dw_mm([s["h"]], dpg, "mm_proj_gate_dw", a_fn=_rms_apply, a_vecs=[s["nmw"]])
        pending = [shard_cols([d_main, d_gate_cols], win_w, "shard_w_in"), shard_cols([d_wg, d_wu], gu_w, "shard_w_gu"),
                   d_out.reshape(NDEV, D // NDEV, D), d_down.reshape(NDEV, DFF // NDEV, D)]
        dh, dnmw = rms_bwd(dhn, s["h"], s["nmw"], dh_mid, "rms_mix_bwd")
        grads["norm_mix_w"][l] = dnmw[0]
        grads["norm_ffn_w"][l] = dnfw[0]
        grads["conv_dw_w"][l] = dcdw[:CONV_W]
        grads["conv_dw_b"][l] = dcdb[0]
        grads["conv_ln_w"][l] = dclw[0]
        grads["conv_ln_b"][l] = dclb[0]
        grads["dn_conv_w"][l] = ddcw[:DN_W]
        grads["dn_A_log"][l] = dalog[0, NH:2 * NH]
        grads["dn_dt_bias"][l] = ddtb[0, NH:2 * NH]
        grads["dn_norm_w"][l] = ddnw[0]

    grad_x = dh[HEAD:HEAD + seq][None]
    full = {n: jnp.stack(g) for n, g in grads.items()}
    full["meta_tokens"] = dh[FRONT:HEAD]
    full["final_norm_w"] = d_final[0]

    send_small = _pack_small(
        [_per_destination_last(full[n]) for n in SMALL_SHARDED]
        + [jnp.broadcast_to(full[n].reshape(1, -1), (NDEV, full[n].size)) for n in SMALL_REPLICATED], 1)
    *received[0], r_small = _push_to_all(pending + [send_small], "exchange_last", scatter=True)
    r_win, r_gu, r_wout, r_down = (jnp.concatenate([received[l][k] for l in range(depth)], axis=1) for k in range(4))

    small_names = SMALL_SHARDED + SMALL_REPLICATED
    pack_local = lambda tree: _pack_small([tree[n].reshape(-1) for n in small_names], 0)
    results = {}

    def run_adamw(name, recv, prep, finish):
        outs = adamw(recv, prep(weights[name]), prep(m_in[name]), prep(v_in[name]), "adamw_" + name)
        results[name] = [finish(o) for o in outs]

    run_adamw("w_in", r_win, lambda a: pad_cols(a, win_wp),
              lambda o: o[:, :win_w].reshape(depth, D, win_w))
    run_adamw("ffn_w_gu", r_gu, lambda a: pad_cols(a, gu_wp), lambda o: o[:, :gu_w].reshape(depth, D, gu_w))
    run_adamw("w_out", r_wout, rows2d, lambda o: o.reshape(w_out.shape))
    run_adamw("ffn_w_down", r_down, rows2d, lambda o: o.reshape(ffn_w_down.shape))
    small_outs = adamw(r_small, pack_local(weights), pack_local(m_in), pack_local(v_in), "adamw_small")
    for kind in range(4):
        flat, off = small_outs[kind].reshape(-1), 0
        for n in small_names:
            wgt = weights[n]
            results.setdefault(n, [None] * 4)[kind] = flat[off:off + wgt.size].reshape(wgt.shape)
            off += wgt.size

    return (loss, grad_x, *[results[n][0] for n in PARAM_ORDER], *[results[n][1] for n in PARAM_ORDER],
            *[results[n][2] for n in PARAM_ORDER], *[results[n][3] for n in PARAM_ORDER])
```

```python
import jax
import jax.numpy as jnp
from jax import lax
from jax.experimental import pallas as pl
from jax.experimental.pallas import tpu as pltpu

F32 = jnp.float32
MXU_DTYPE = jnp.bfloat16

D = 1024
N_META = 16
CHUNK = 64
CHUNK_LOG2 = 6
INV_BASE_LOG2 = 3
FRONT = CHUNK - N_META
HEAD = CHUNK
CONV_CH = 512
CONV_W = 31
CONV_HALO = 32
NH = 4
DH = 128
DQ = NH * DH
DN_W = 4
DN_HALO = 8
DFF = 2816
PROJ_MAIN = 3072
D_IN = 3080
GATE_W = 128
LANES = 128
NDEV = 8
NORM_EPS = 1e-6
LN_EPS = 1e-5
L2_EPS = 1e-6
VMEM_LIMIT_V7X = 48 * 1024 * 1024
ROW_TILE = 640
ROW_TILE_SMALL = 128
MM_TILES = (1408, 1280, 1024, 640, 512, 256, 128)
MM_SUB = 4
GRAD_DTYPE = jnp.bfloat16
DELTA_CHUNKS = (4, 2, 1)
ADAM_BLOCK_BYTES = 8 * 1024 * 1024

ADAM_LR = 0.001
ADAM_B1 = 0.9
ADAM_B2 = 0.999
ADAM_EPS = 1e-08
ADAM_WD = 0.01
ADAM_STEP = 10

MESH_AXES = ("x", "y", "c")
NN = ((1,), (0,))
NT = ((1,), (1,))
TN = ((0,), (0,))

assert 1 << CHUNK_LOG2 == CHUNK


def _row_tile(t):
    return ROW_TILE if t % ROW_TILE == 0 else ROW_TILE_SMALL


def _padded_rows(seq):
    n = HEAD + seq
    tm = ROW_TILE if n >= 4 * ROW_TILE else ROW_TILE_SMALL
    return -(-n // tm) * tm


def _pick(n, prefs):
    for p in prefs:
        if n % p == 0:
            return p
    return n


def _lane_pad(n):
    return -(-n // LANES) * LANES


def _cp(sem):
    return pltpu.CompilerParams(dimension_semantics=sem, vmem_limit_bytes=VMEM_LIMIT_V7X)


def _sigmoid(x):
    return 1.0 / (1.0 + jnp.exp(-x))


def _softplus(x):
    return jnp.maximum(x, 0.0) + jnp.log(1.0 + jnp.exp(-jnp.abs(x)))


def _valid_rows(i, tm, seq, width, first=FRONT):
    rows = i * tm + lax.broadcasted_iota(jnp.int32, (tm, width), 0)
    return jnp.logical_and(rows >= first, rows < HEAD + seq)


def _dot(a, b, dims):
    return lax.dot_general(a, b, (dims, ((), ())), preferred_element_type=F32)


def _split(x, n):
    out, r = [], x
    for _ in range(n):
        p = r.astype(MXU_DTYPE)
        out.append(p)
        r = r - p.astype(F32)
    return out


def _mm1(a, b, dims):
    return _dot(a.astype(MXU_DTYPE), b.astype(MXU_DTYPE), dims)


def _mm1_many(a_list, b_list, dims):
    return [_mm1(a, b, dims) for a, b in zip(a_list, b_list)]


def _mm3_many(a_list, b_list, dims):
    sa = [_split(a, 2) for a in a_list]
    sb = [_split(b, 2) for b in b_list]
    hh = [_dot(x[0], y[0], dims) for x, y in zip(sa, sb)]
    hl = [_dot(x[0], y[1], dims) for x, y in zip(sa, sb)]
    lh = [_dot(x[1], y[0], dims) for x, y in zip(sa, sb)]
    return [p + (q + r) for p, q, r in zip(hh, hl, lh)]


def _mmx(e, b, dims):
    e = e.astype(MXU_DTYPE)
    b1, b2, b3 = _split(b, 3)
    return _dot(e, b1, dims) + (_dot(e, b2, dims) + _dot(e, b3, dims))


def mm(a, b, *, ta=False, tb=False, a_fn=None, a_vecs=(), extras=(), out_fn=None, out_dtypes=(F32,), tile_cap=None,
       name):
    a_list = list(a) if isinstance(a, (list, tuple)) else [a]
    (k_dim, m_dim) = a_list[0].shape if ta else a_list[0].shape[::-1]
    n_dim = b.shape[0] if tb else b.shape[1]
    assert (b.shape[1] if tb else b.shape[0]) == k_dim
    capped = MM_TILES if tile_cap is None else tuple(p for p in MM_TILES if p <= tile_cap)
    tm = _pick(m_dim, MM_TILES if ta else capped)
    tk = _pick(k_dim, capped if ta else MM_TILES)
    tn = _pick(n_dim, MM_TILES)
    nk = k_dim // tk
    na, nv, ne, no = len(a_list), len(a_vecs), len(extras), len(out_dtypes)
    dims = ((0,) if ta else (1,), (1,) if tb else (0,))
    nsub = MM_SUB if a_fn is not None and not ta and tm % (MM_SUB * 16) == 0 else 1
    sub = tm // nsub

    def body(*refs):
        a_refs, v_refs, b_ref = refs[:na], refs[na:na + nv], refs[na + nv]
        e_refs = refs[na + nv + 1:na + nv + 1 + ne]
        o_refs = refs[na + nv + 1 + ne:na + nv + 1 + ne + no]
        acc_ref = refs[-1] if nk > 1 else None
        k = pl.program_id(2)

        def left(rows):
            tiles = [r[rows, :] for r in a_refs]
            return tiles[0] if a_fn is None else a_fn(*tiles, *[v[...] for v in v_refs])

        def finish(acc, rows):
            outs = (acc,) if out_fn is None else out_fn(acc, *[e[rows, :] for e in e_refs])
            for o_ref, out in zip(o_refs, outs):
                o_ref[rows, :] = out.astype(o_ref.dtype)

        if nk > 1:
            @pl.when(k == 0)
            def _():
                acc_ref[...] = jnp.zeros_like(acc_ref)

        for r in range(nsub):
            rows = slice(r * sub, (r + 1) * sub) if nsub > 1 else slice(None)
            prod = _mm1(left(rows), b_ref[...], dims)
            if nk == 1:
                finish(prod, rows)
            else:
                acc_ref[rows, :] += prod

        if nk > 1:
            @pl.when(k == nk - 1)
            def _():
                finish(acc_ref[...], slice(None))

    if ta:
        a_spec = pl.BlockSpec((tk, tm), lambda i, j, k: (k, i))
        v_spec = pl.BlockSpec((1, tm), lambda i, j, k: (0, i))
    else:
        a_spec = pl.BlockSpec((tm, tk), lambda i, j, k: (i, k))
        v_spec = pl.BlockSpec((1, tk), lambda i, j, k: (0, k))
    b_spec = pl.BlockSpec((tn, tk), lambda i, j, k: (j, k)) if tb else pl.BlockSpec((tk, tn), lambda i, j, k: (k, j))
    o_spec = pl.BlockSpec((tm, tn), lambda i, j, k: (i, j))
    outs = pl.pallas_call(
        body, grid=(m_dim // tm, n_dim // tn, nk),
        in_specs=[a_spec] * na + [v_spec] * nv + [b_spec] + [o_spec] * ne, out_specs=[o_spec] * no,
        out_shape=[jax.ShapeDtypeStruct((m_dim, n_dim), dt) for dt in out_dtypes],
        scratch_shapes=[pltpu.VMEM((tm, tn), F32)] if nk > 1 else [],
        compiler_params=_cp(("parallel", "parallel", "arbitrary")), name=name)(*a_list, *a_vecs, b, *extras)
    return outs[0] if no == 1 else outs


def _rms_apply(x, w):
    assert x.shape[-1] == D
    return x * lax.rsqrt(jnp.mean(x * x, axis=-1, keepdims=True) + NORM_EPS) * w


def _swiglu(g, u):
    return g * _sigmoid(g) * u


def _swiglu_bwd(dact, g, u):
    sg = _sigmoid(g)
    return dact * u * (sg * (1.0 + g * (1.0 - sg))), dact * (g * sg)


def _add(acc, r):
    return (acc + r,)


def rms_bwd(dy, h, w, dres, name):
    t = h.shape[0]
    tm = _row_tile(t)

    def body(dy_ref, h_ref, w_ref, dres_ref, dh_ref, dw_ref):
        i = pl.program_id(0)
        x = h_ref[...]
        r = lax.rsqrt(jnp.mean(x * x, axis=-1, keepdims=True) + NORM_EPS)
        xh = x * r
        g = dy_ref[...] * w_ref[...]
        dh_ref[...] = dres_ref[...] + r * (g - xh * jnp.mean(g * xh, axis=-1, keepdims=True))

        @pl.when(i == 0)
        def _():
            dw_ref[...] = jnp.zeros_like(dw_ref)

        dw_ref[...] += jnp.sum(dy_ref[...] * xh, axis=0, keepdims=True)

    row = pl.BlockSpec((tm, D), lambda i: (i, 0))
    vec = pl.BlockSpec((1, D), lambda i: (0, 0))
    return pl.pallas_call(
        body, grid=(t // tm,), in_specs=[row, row, vec, row], out_specs=[row, vec],
        out_shape=[jax.ShapeDtypeStruct((t, D), F32), jax.ShapeDtypeStruct((1, D), F32)],
        compiler_params=_cp(("arbitrary",)), name=name)(dy, h, w, dres)


def loss_bwd(h, tgt, w, seq, name):
    t = h.shape[0]
    tm = _row_tile(t)

    def body(h_ref, t_ref, w_ref, dh_ref, loss_ref, dw_ref):
        i = pl.program_id(0)
        x = h_ref[...]
        wv = w_ref[...]
        r = lax.rsqrt(jnp.mean(x * x, axis=-1, keepdims=True) + NORM_EPS)
        xh = x * r
        err = jnp.where(_valid_rows(i, tm, seq, D, HEAD), xh * wv - t_ref[...], 0.0)
        dy = err * (1.0 / D)
        g = dy * wv
        dh_ref[...] = r * (g - xh * jnp.mean(g * xh, axis=-1, keepdims=True))

        @pl.when(i == 0)
        def _():
            dw_ref[...] = jnp.zeros_like(dw_ref)
            loss_ref[...] = jnp.zeros_like(loss_ref)

        dw_ref[...] += jnp.sum(dy * xh, axis=0, keepdims=True)
        part = jnp.sum(jnp.sum(err * err, axis=1, keepdims=True), axis=0, keepdims=True) * (0.5 / D)
        loss_ref[...] += jnp.broadcast_to(part, loss_ref.shape)

    row = pl.BlockSpec((tm, D), lambda i: (i, 0))
    vec = pl.BlockSpec((1, D), lambda i: (0, 0))
    return pl.pallas_call(
        body, grid=(t // tm,), in_specs=[row, row, vec],
        out_specs=[row, pl.BlockSpec((1, LANES), lambda i: (0, 0)), vec],
        out_shape=[jax.ShapeDtypeStruct((t, D), F32), jax.ShapeDtypeStruct((1, LANES), F32),
                   jax.ShapeDtypeStruct((1, D), F32)],
        compiler_params=_cp(("arbitrary",)), name=name)(h, tgt, w)


def _layernorm_parts(u1):
    mu = jnp.mean(u1, axis=-1, keepdims=True)
    xc = u1 - mu
    rstd = lax.rsqrt(jnp.mean(xc * xc, axis=-1, keepdims=True) + LN_EPS)
    return xc * rstd, rstd


SUBLANES = 8
CONV_ROWS = 16


def _shift_copies(ext, sh, tm):
    for s in range(1, SUBLANES):
        sh[s - 1, :, :] = ext[pl.ds(s, tm + CONV_HALO - SUBLANES), :]


def _window(ext, sh, off, rows, start=0):
    s, m = off % SUBLANES, off // SUBLANES
    if s == 0:
        return ext[pl.ds(start + off, rows), :]
    return sh[s - 1, pl.ds(start + SUBLANES * m, rows), :]


def _shift_scratch(tm):
    return pltpu.VMEM((SUBLANES - 1, tm + CONV_HALO - SUBLANES, CONV_CH), F32)


def conv_fwd(proj, w32, b, lw, lb, seq, name, push=()):
    t = proj.shape[0]
    tm = _row_tile(t)
    nt = t // tm
    npush = len(push)

    def body(*refs):
        cv_ref, cg_ref, w_ref, b_ref, lw_ref, lb_ref = refs[:6]
        x_refs = refs[6:6 + npush]
        y_ref, u1_ref = refs[6 + npush:8 + npush]
        got_refs = refs[8 + npush:8 + 2 * npush]
        ext, sh = refs[8 + 2 * npush:10 + 2 * npush]
        i = pl.program_id(0)
        if npush:
            copies = _push_copies(x_refs, got_refs, *refs[10 + 2 * npush:], scatter=False)

            @pl.when(i == 0)
            def _():
                _push_start(copies)

        @pl.when(i == 0)
        def _():
            ext[0:CONV_HALO, :] = jnp.zeros((CONV_HALO, CONV_CH), F32)

        @pl.when(i > 0)
        def _():
            ext[0:CONV_HALO, :] = ext[tm:tm + CONV_HALO, :]

        ext[CONV_HALO:CONV_HALO + tm, :] = cv_ref[...] * _sigmoid(cg_ref[...])
        _shift_copies(ext, sh, tm)
        acc = jnp.broadcast_to(b_ref[...], (tm, CONV_CH))
        for j in range(CONV_W):
            acc = acc + w_ref[j:j + 1, :] * _window(ext, sh, CONV_HALO - (CONV_W - 1) + j, tm)
        u1_ref[...] = acc
        xh, _ = _layernorm_parts(acc)
        ln = xh * lw_ref[...] + lb_ref[...]
        y = ln * _sigmoid(ln)
        y_ref[...] = jnp.where(_valid_rows(i, tm, seq, CONV_CH), y, 0.0).astype(y_ref.dtype)
        if npush:
            @pl.when(i == nt - 1)
            def _():
                _push_finish(copies)

    half = lambda c: pl.BlockSpec((tm, CONV_CH), lambda i: (i, c))
    vec = pl.BlockSpec((1, CONV_CH), lambda i: (0, 0))
    any_spec = pl.BlockSpec(memory_space=pl.ANY)
    outs = pl.pallas_call(
        body, grid=(nt,),
        in_specs=[half(0), half(1), pl.BlockSpec((CONV_HALO, CONV_CH), lambda i: (0, 0)), vec, vec, vec]
        + [any_spec] * npush,
        out_specs=[half(0), half(0)] + [any_spec] * npush,
        out_shape=[jax.ShapeDtypeStruct((t, D), MXU_DTYPE), jax.ShapeDtypeStruct((t, CONV_CH), F32)]
        + _push_out_shapes(push, scatter=False),
        scratch_shapes=[pltpu.VMEM((tm + CONV_HALO, CONV_CH), F32), _shift_scratch(tm)]
        + (_push_sems(npush) if npush else []),
        compiler_params=_cp(("arbitrary",)), name=name)(proj, proj, w32, b, lw, lb, *push)
    return outs[0], outs[1], outs[2:]


def conv_bwd(dy, u1, proj, w32, lw, lb, seq, name, push=()):
    t = proj.shape[0]
    tm = _row_tile(t)
    nt = t // tm
    per = tm // CONV_HALO
    npush = len(push)

    def body(*refs):
        dy_ref, u1_ref, cv_ref, cg_ref, cvp_ref, cgp_ref, w_ref, lw_ref, lb_ref = refs[:9]
        x_refs = refs[9:9 + npush]
        dp_ref, dw_ref, db_ref, dlw_ref, dlb_ref = refs[9 + npush:14 + npush]
        got_refs = refs[14 + npush:14 + 2 * npush]
        ext_d, ext_u, sh_d, sh_u, du0_s, dw_acc = refs[14 + 2 * npush:20 + 2 * npush]
        i = pl.program_id(0)
        tile = nt - 1 - i
        if npush:
            copies = _push_copies(x_refs, got_refs, *refs[20 + 2 * npush:], scatter=True)

            @pl.when(i == 0)
            def _():
                _push_start(copies)

        @pl.when(i == 0)
        def _():
            ext_d[tm:tm + CONV_HALO, :] = jnp.zeros((CONV_HALO, CONV_CH), F32)
            dw_acc[...] = jnp.zeros_like(dw_acc)
            db_ref[...] = jnp.zeros_like(db_ref)
            dlw_ref[...] = jnp.zeros_like(dlw_ref)
            dlb_ref[...] = jnp.zeros_like(dlb_ref)

        @pl.when(i > 0)
        def _():
            ext_d[tm:tm + CONV_HALO, :] = ext_d[0:CONV_HALO, :]

        xh, rstd = _layernorm_parts(u1_ref[...])
        lwv = lw_ref[...]
        ln = xh * lwv + lb_ref[...]
        sg = _sigmoid(ln)
        dln = jnp.where(_valid_rows(tile, tm, seq, CONV_CH), dy_ref[...], 0.0) * (sg * (1.0 + ln * (1.0 - sg)))
        dlw_ref[...] += jnp.sum(dln * xh, axis=0, keepdims=True)
        dlb_ref[...] += jnp.sum(dln, axis=0, keepdims=True)
        dxh = dln * lwv
        du1 = rstd * (dxh - jnp.mean(dxh, axis=-1, keepdims=True)
                      - xh * jnp.mean(dxh * xh, axis=-1, keepdims=True))
        db_ref[...] += jnp.sum(du1, axis=0, keepdims=True)
        ext_d[0:tm, :] = du1

        cv = cv_ref[...]
        sgc = _sigmoid(cg_ref[...])
        prev = cvp_ref[...] * _sigmoid(cgp_ref[...])
        ext_u[0:CONV_HALO, :] = jnp.where(tile > 0, prev, 0.0)
        ext_u[CONV_HALO:CONV_HALO + tm, :] = cv * sgc

        _shift_copies(ext_d, sh_d, tm)
        _shift_copies(ext_u, sh_u, tm)

        def row_block(rb, carry):
            r0 = pl.multiple_of(rb * CONV_ROWS, CONV_ROWS)
            du1_b = ext_d[pl.ds(r0, CONV_ROWS), :]
            acc = jnp.zeros((CONV_ROWS, CONV_CH), F32)
            for j in range(CONV_W):
                acc = acc + w_ref[j:j + 1, :] * _window(ext_d, sh_d, CONV_W - 1 - j, CONV_ROWS, r0)
                prod = du1_b * _window(ext_u, sh_u, CONV_HALO - (CONV_W - 1) + j, CONV_ROWS, r0)
                dw_acc[j] += prod[0:SUBLANES, :] + prod[SUBLANES:CONV_ROWS, :]
            du0_s[pl.ds(r0, CONV_ROWS), :] = acc
            return carry

        lax.fori_loop(0, tm // CONV_ROWS, row_block, 0)

        @pl.when(i == nt - 1)
        def _():
            dw_ref[...] = jnp.sum(dw_acc[...], axis=1)

        du0 = du0_s[...]
        dp_ref[:, 0:CONV_CH] = (du0 * sgc).astype(dp_ref.dtype)
        dp_ref[:, CONV_CH:2 * CONV_CH] = (du0 * cv * sgc * (1.0 - sgc)).astype(dp_ref.dtype)
        if npush:
            @pl.when(i == nt - 1)
            def _():
                _push_finish(copies)

    rev = lambda c: pl.BlockSpec((tm, CONV_CH), lambda i: (nt - 1 - i, c))
    prev = lambda c: pl.BlockSpec((CONV_HALO, CONV_CH), lambda i: (jnp.maximum((nt - 1 - i) * per - 1, 0), c))
    vec = pl.BlockSpec((1, CONV_CH), lambda i: (0, 0))
    wspec = pl.BlockSpec((CONV_HALO, CONV_CH), lambda i: (0, 0))
    any_spec = pl.BlockSpec(memory_space=pl.ANY)
    outs = pl.pallas_call(
        body, grid=(nt,),
        in_specs=[rev(0), rev(0), rev(0), rev(1), prev(0), prev(1), wspec, vec, vec] + [any_spec] * npush,
        out_specs=[pl.BlockSpec((tm, 2 * CONV_CH), lambda i: (nt - 1 - i, 0)), wspec, vec, vec, vec]
        + [any_spec] * npush,
        out_shape=[jax.ShapeDtypeStruct((t, PROJ_MAIN), MXU_DTYPE), jax.ShapeDtypeStruct((CONV_HALO, CONV_CH), F32),
                   jax.ShapeDtypeStruct((1, CONV_CH), F32), jax.ShapeDtypeStruct((1, CONV_CH), F32),
                   jax.ShapeDtypeStruct((1, CONV_CH), F32)] + _push_out_shapes(push, scatter=True),
        scratch_shapes=[pltpu.VMEM((tm + CONV_HALO, CONV_CH), F32), pltpu.VMEM((tm + CONV_HALO, CONV_CH), F32),
                        _shift_scratch(tm), _shift_scratch(tm), pltpu.VMEM((tm, CONV_CH), F32),
                        pltpu.VMEM((CONV_HALO, SUBLANES, CONV_CH), F32)] + (_push_sems(npush) if npush else []),
        compiler_params=_cp(("arbitrary",)), name=name)(dy, u1, proj, proj, proj, proj, w32, lw, lb, *push)
    return outs[:5], outs[5:]


def dn_pre_fwd(proj, w8, name):
    t = proj.shape[0]
    tm = _row_tile(t)

    def body(raw_ref, w_ref, o_ref, ext):
        g = pl.program_id(0)
        i = pl.program_id(1)

        @pl.when(i == 0)
        def _():
            ext[0:DN_HALO, :] = jnp.zeros((DN_HALO, DQ), F32)

        @pl.when(i > 0)
        def _():
            ext[0:DN_HALO, :] = ext[tm:tm + DN_HALO, :]

        ext[DN_HALO:DN_HALO + tm, :] = raw_ref[...]
        c = jnp.zeros((tm, DQ), F32)
        for j in range(DN_W):
            c = c + w_ref[j:j + 1, :] * ext[pl.ds(DN_HALO - (DN_W - 1) + j, tm), :]
        s = c * _sigmoid(c)
        scale = jnp.where(g == 0, DH ** -0.5, 1.0)
        for h in range(NH):
            sh = s[:, h * DH:(h + 1) * DH]
            r = lax.rsqrt(jnp.sum(sh * sh, axis=-1, keepdims=True) + L2_EPS)
            o_ref[:, h * DH:(h + 1) * DH] = jnp.where(g == 2, sh, sh * (r * scale))

    return pl.pallas_call(
        body, grid=(3, t // tm),
        in_specs=[pl.BlockSpec((tm, DQ), lambda g, i: (i, 2 + g)), pl.BlockSpec((DN_HALO, DQ), lambda g, i: (0, g))],
        out_specs=pl.BlockSpec((tm, DQ), lambda g, i: (i, g)),
        out_shape=jax.ShapeDtypeStruct((t, 3 * DQ), F32),
        scratch_shapes=[pltpu.VMEM((tm + DN_HALO, DQ), F32)],
        compiler_params=_cp(("arbitrary", "arbitrary")), name=name)(proj, w8)


def dn_pre_bwd(dproj, dqkv, proj, w8, name):
    t = proj.shape[0]
    tm = _row_tile(t)
    nt = t // tm
    per = tm // DN_HALO

    def body(dp_in, d_ref, raw_ref, rawp_ref, w_ref, dp_ref, dw_ref, ext_d, ext_r):
        del dp_in
        g = pl.program_id(0)
        i = pl.program_id(1)
        tile = nt - 1 - i

        @pl.when(i == 0)
        def _():
            ext_d[tm:tm + DN_HALO, :] = jnp.zeros((DN_HALO, DQ), F32)
            dw_ref[...] = jnp.zeros_like(dw_ref)

        @pl.when(i > 0)
        def _():
            ext_d[tm:tm + DN_HALO, :] = ext_d[0:DN_HALO, :]

        ext_r[0:DN_HALO, :] = jnp.where(tile > 0, rawp_ref[...], 0.0)
        ext_r[DN_HALO:DN_HALO + tm, :] = raw_ref[...]
        taps = [ext_r[pl.ds(DN_HALO - (DN_W - 1) + j, tm), :] for j in range(DN_W)]
        c = jnp.zeros((tm, DQ), F32)
        for j in range(DN_W):
            c = c + w_ref[j:j + 1, :] * taps[j]
        sg = _sigmoid(c)
        s = c * sg
        scale = jnp.where(g == 0, DH ** -0.5, 1.0)
        for h in range(NH):
            sl = slice(h * DH, (h + 1) * DH)
            sh = s[:, sl]
            dn = d_ref[:, sl]
            r = lax.rsqrt(jnp.sum(sh * sh, axis=-1, keepdims=True) + L2_EPS)
            unit = sh * r
            dsn = (r * scale) * (dn - unit * jnp.sum(dn * unit, axis=-1, keepdims=True))
            ds = jnp.where(g == 2, dn, dsn)
            ext_d[0:tm, sl] = ds * (sg[:, sl] * (1.0 + c[:, sl] * (1.0 - sg[:, sl])))
        dc = ext_d[0:tm, :]
        draw = jnp.zeros((tm, DQ), F32)
        for j in range(DN_W):
            draw = draw + w_ref[j:j + 1, :] * ext_d[pl.ds(DN_W - 1 - j, tm), :]
            dw_ref[j:j + 1, :] += jnp.sum(dc * taps[j], axis=0, keepdims=True)
        dp_ref[...] = draw.astype(dp_ref.dtype)

    return pl.pallas_call(
        body, grid=(3, nt),
        in_specs=[pl.BlockSpec(memory_space=pl.ANY),
                  pl.BlockSpec((tm, DQ), lambda g, i: (nt - 1 - i, g)),
                  pl.BlockSpec((tm, DQ), lambda g, i: (nt - 1 - i, 2 + g)),
                  pl.BlockSpec((DN_HALO, DQ), lambda g, i: (jnp.maximum((nt - 1 - i) * per - 1, 0), 2 + g)),
                  pl.BlockSpec((DN_HALO, DQ), lambda g, i: (0, g))],
        out_specs=[pl.BlockSpec((tm, DQ), lambda g, i: (nt - 1 - i, 2 + g)),
                   pl.BlockSpec((DN_HALO, DQ), lambda g, i: (0, g))],
        out_shape=[jax.ShapeDtypeStruct(dproj.shape, dproj.dtype), jax.ShapeDtypeStruct((DN_HALO, 3 * DQ), F32)],
        scratch_shapes=[pltpu.VMEM((tm + DN_HALO, DQ), F32), pltpu.VMEM((tm + DN_HALO, DQ), F32)],
        input_output_aliases={0: 0},
        compiler_params=_cp(("arbitrary", "arbitrary")), name=name)(dproj, dqkv, proj, proj, w8)


def gate_fwd(pg, alog, dtb, seq, name):
    t = pg.shape[0]
    tm = _row_tile(t)

    def body(x_ref, al_ref, dt_ref, o_ref):
        i = pl.program_id(0)
        x = x_ref[...]
        lane = lax.broadcasted_iota(jnp.int32, (tm, GATE_W), 1)
        gg = -jnp.exp(al_ref[...]) * _softplus(x + dt_ref[...])
        out = jnp.where(lane < NH, _sigmoid(x), jnp.where(lane < 2 * NH, gg, 0.0))
        o_ref[...] = jnp.where(_valid_rows(i, tm, seq, GATE_W), out, 0.0)

    row = pl.BlockSpec((tm, GATE_W), lambda i: (i, 0))
    vec = pl.BlockSpec((1, GATE_W), lambda i: (0, 0))
    return pl.pallas_call(
        body, grid=(t // tm,), in_specs=[row, vec, vec], out_specs=row,
        out_shape=jax.ShapeDtypeStruct((t, GATE_W), F32), compiler_params=_cp(("parallel",)), name=name)(pg, alog, dtb)


def gate_bwd(dbg, pg, alog, dtb, seq, name):
    t = pg.shape[0]
    tm = _row_tile(t)

    def body(d_ref, x_ref, al_ref, dt_ref, o_ref, dal_ref, ddt_ref):
        i = pl.program_id(0)
        x = x_ref[...]
        lane = lax.broadcasted_iota(jnp.int32, (tm, GATE_W), 1)
        d = jnp.where(_valid_rows(i, tm, seq, GATE_W), d_ref[...], 0.0)
        beta = _sigmoid(x)
        xs = x + dt_ref[...]
        e = -jnp.exp(al_ref[...])
        is_g = jnp.logical_and(lane >= NH, lane < 2 * NH)
        da = jnp.where(is_g, d * e * _sigmoid(xs), 0.0)
        dgg = jnp.where(is_g, d * e * _softplus(xs), 0.0)
        o_ref[...] = jnp.where(lane < NH, d * beta * (1.0 - beta), da).astype(o_ref.dtype)

        @pl.when(i == 0)
        def _():
            dal_ref[...] = jnp.zeros_like(dal_ref)
            ddt_ref[...] = jnp.zeros_like(ddt_ref)

        dal_ref[...] += jnp.sum(dgg, axis=0, keepdims=True)
        ddt_ref[...] += jnp.sum(da, axis=0, keepdims=True)

    row = pl.BlockSpec((tm, GATE_W), lambda i: (i, 0))
    vec = pl.BlockSpec((1, GATE_W), lambda i: (0, 0))
    return pl.pallas_call(
        body, grid=(t // tm,), in_specs=[row, row, vec, vec], out_specs=[row, vec, vec],
        out_shape=[jax.ShapeDtypeStruct((t, GATE_W), MXU_DTYPE), jax.ShapeDtypeStruct((1, GATE_W), F32),
                   jax.ShapeDtypeStruct((1, GATE_W), F32)],
        compiler_params=_cp(("arbitrary",)), name=name)(dbg, pg, alog, dtb)


def _chunk_masks():
    ii = lax.broadcasted_iota(jnp.int32, (CHUNK, CHUNK), 0)
    jj = lax.broadcasted_iota(jnp.int32, (CHUNK, CHUNK), 1)
    return ii, jj, ii >= jj, ii > jj


def _lane_col(x, lane, idx):
    return jnp.sum(jnp.where(lane == idx, x, 0.0), axis=1, keepdims=True)


def _stack_rows(xs, ys):
    return [jnp.concatenate([x, y], axis=0) for x, y in zip(xs, ys)]


def _side_by_side(xs, ys):
    return [jnp.concatenate([x, y], axis=1) for x, y in zip(xs, ys)]


def _delta_terms(q, k, v, bgs, nch, low, strict):
    idx = [(c, h) for c in range(nch) for h in range(NH)]
    lane = lax.broadcasted_iota(jnp.int32, (CHUNK, GATE_W), 1)
    rowi = lax.broadcasted_iota(jnp.int32, (CHUNK, 1), 0)
    r4 = lax.broadcasted_iota(jnp.int32, (NH * CHUNK, GATE_W), 0)
    l4 = lax.broadcasted_iota(jnp.int32, (NH * CHUNK, GATE_W), 1)
    sel = jnp.where(l4 == NH + jnp.right_shift(r4, CHUNK_LOG2), 1.0, 0.0)
    lowf = jnp.where(low, 1.0, 0.0)
    gam_all = [_mmx(lowf, b, NN) for b in bgs]
    gam_rows = [_mmx(sel, g, NT) for g in gam_all]
    beta = [_lane_col(bgs[c], lane, h) for c, h in idx]
    gam = [_lane_col(gam_all[c], lane, NH + h) for c, h in idx]
    dm = [jnp.exp(jnp.where(low, g - gam_rows[c][h * CHUNK:(h + 1) * CHUNK, :], -1e30))
          for g, (c, h) in zip(gam, idx)]
    glast = [jnp.sum(jnp.where(rowi == CHUNK - 1, g, 0.0), axis=0, keepdims=True) for g in gam]
    eg = [jnp.exp(g) for g in gam]
    ekl = [jnp.exp(gl - g) for gl, g in zip(glast, gam)]
    gl = [jnp.exp(x) for x in glast]
    kb = [x * b for x, b in zip(k, beta)]
    vb = [x * b for x, b in zip(v, beta)]
    kbg = [x * e for x, e in zip(kb, eg)]
    kq = _mm1_many(_stack_rows(kb, q), k, NT)
    a_mat = [jnp.where(strict, x[:CHUNK] * d, 0.0) for x, d in zip(kq, dm)]
    p_mat = [jnp.where(low, x[CHUNK:] * d, 0.0) for x, d in zip(kq, dm)]
    qd = [x * e for x, e in zip(q, eg)]
    kd = [x * e for x, e in zip(k, ekl)]
    return dict(idx=idx, beta=beta, dm=dm, eg=eg, ekl=ekl, gl=gl, kb=kb, vb=vb, kbg=kbg, a=a_mat, p=p_mat, qd=qd,
                kd=kd, lane=lane, rowi=rowi)


def _unit_lower_inverses(a_list, ii, jj, eye):
    def same(log2):
        return jnp.right_shift(ii, log2) == jnp.right_shift(jj, log2)

    n = [-jnp.where(same(INV_BASE_LOG2), a, 0.0) for a in a_list]
    x = [eye + v for v in n]
    p = n
    for _ in range(INV_BASE_LOG2 - 1):
        p = _mm1_many(p, p, NN)
        x = [xi + y for xi, y in zip(x, _mm1_many(x, p, NN))]
    for log2 in range(INV_BASE_LOG2, CHUNK_LOG2):
        off = jnp.logical_and(same(log2 + 1), jnp.logical_not(same(log2)))
        a_off = [jnp.where(off, a, 0.0) for a in a_list]
        x = [xi - y for xi, y in zip(x, _mm1_many(x, _mm1_many(a_off, x, NN), NN))]
    return x


def _transposes(xs, eye):
    e = eye.astype(MXU_DTYPE)
    parts = [_split(x, 2) for x in xs]
    return [_dot(p[0], e, TN) + _dot(p[1], e, TN) for p in parts]


def _load_heads(ref, nch):
    return [ref[c * CHUNK:(c + 1) * CHUNK, h * DH:(h + 1) * DH] for c in range(nch) for h in range(NH)]


def delta_fwd(qkv, bg, name, push=None):
    t = qkv.shape[0]
    nc = t // CHUNK
    nch = _pick(nc, DELTA_CHUNKS)
    rows = nch * CHUNK
    ng = nc // nch
    npush = 0 if push is None else len(push)

    def body(*refs):
        q_ref, k_ref, v_ref, bg_ref = refs[:4]
        x_refs = refs[4:4 + npush]
        o_ref, sh_ref, mi_ref, u_ref, w_ref = refs[4 + npush:9 + npush]
        got_refs = refs[9 + npush:9 + 2 * npush]
        s_ref = refs[9 + 2 * npush]
        n = pl.program_id(0)
        if npush:
            copies = _push_copies(x_refs, got_refs, *refs[10 + 2 * npush:], scatter=False)

            @pl.when(n == 0)
            def _():
                _push_start(copies)

        @pl.when(n == 0)
        def _():
            s_ref[...] = jnp.zeros_like(s_ref)

        ii, jj, low, strict = _chunk_masks()
        eye = jnp.where(ii == jj, 1.0, 0.0)
        q, k, v = _load_heads(q_ref, nch), _load_heads(k_ref, nch), _load_heads(v_ref, nch)
        bgs = [bg_ref[c * CHUNK:(c + 1) * CHUNK, :] for c in range(nch)]
        tm_ = _delta_terms(q, k, v, bgs, nch, low, strict)
        m_inv = _unit_lower_inverses(tm_["a"], ii, jj, eye)
        uw = _mm3_many(m_inv, _side_by_side(tm_["vb"], tm_["kbg"]), NN)
        u = [x[:, :DH] for x in uw]
        w = [x[:, DH:] for x in uw]
        for i, (c, h) in enumerate(tm_["idx"]):
            mi_ref[c, h] = m_inv[i]
            u_ref[c * CHUNK:(c + 1) * CHUNK, h * DH:(h + 1) * DH] = u[i]
            w_ref[c * CHUNK:(c + 1) * CHUNK, h * DH:(h + 1) * DH] = w[i]
        wq = _stack_rows(w, tm_["qd"])
        s = [s_ref[h] for h in range(NH)]
        for c in range(nch):
            pr = range(c * NH, (c + 1) * NH)
            wqs = [_mm1(wq[i], s[i - c * NH], NN) for i in pr]
            ws = [x[:CHUNK] for x in wqs]
            qs = [x[CHUNK:] for x in wqs]
            vn = [u[i] - x for i, x in zip(pr, ws)]
            pv = [_mm1(tm_["p"][i], x, NN) for i, x in zip(pr, vn)]
            kv = [_mm1(tm_["kd"][i], x, TN) for i, x in zip(pr, vn)]
            for h in range(NH):
                o_ref[c * CHUNK:(c + 1) * CHUNK, h * DH:(h + 1) * DH] = qs[h] + pv[h]
                sh_ref[c, h] = s[h]
                s[h] = tm_["gl"][c * NH + h] * s[h] + kv[h]
        for h in range(NH):
            s_ref[h] = s[h]
        if npush:
            @pl.when(n == ng - 1)
            def _():
                _push_finish(copies)

    col = lambda c: pl.BlockSpec((rows, DQ), lambda n: (n, c))
    any_spec = pl.BlockSpec(memory_space=pl.ANY)
    pushed = [] if push is None else list(push)
    outs = pl.pallas_call(
        body, grid=(ng,),
        in_specs=[col(0), col(1), col(2), pl.BlockSpec((rows, GATE_W), lambda n: (n, 0))] + [any_spec] * npush,
        out_specs=[col(0), pl.BlockSpec((nch, NH, DH, DH), lambda n: (n, 0, 0, 0)),
                   pl.BlockSpec((nch, NH, CHUNK, CHUNK), lambda n: (n, 0, 0, 0)), col(0), col(0)]
        + [any_spec] * npush,
        out_shape=[jax.ShapeDtypeStruct((t, DQ), F32), jax.ShapeDtypeStruct((nc, NH, DH, DH), F32),
                   jax.ShapeDtypeStruct((nc, NH, CHUNK, CHUNK), F32), jax.ShapeDtypeStruct((t, DQ), F32),
                   jax.ShapeDtypeStruct((t, DQ), F32)] + _push_out_shapes(pushed, scatter=False),
        scratch_shapes=[pltpu.VMEM((NH, DH, DH), F32)] + (_push_sems(npush) if npush else []),
        compiler_params=_cp(("arbitrary",)), name=name)(qkv, qkv, qkv, bg, *pushed)
    return outs[:5], outs[5:]


def delta_bwd(qkv, bg, do, s_hist, m_hist, u_all, w_all, name, push=None):
    t = qkv.shape[0]
    nc = t // CHUNK
    nch = _pick(nc, DELTA_CHUNKS)
    rows = nch * CHUNK
    ng = nc // nch
    npush = 0 if push is None else len(push)

    def body(*refs):
        q_ref, k_ref, v_ref, bg_ref, do_ref, sh_ref, mi_ref, u_ref, w_ref = refs[:9]
        x_refs = refs[9:9 + npush]
        dqkv_ref, dbg_ref = refs[9 + npush:11 + npush]
        got_refs = refs[11 + npush:11 + 2 * npush]
        ds_ref = refs[11 + 2 * npush]
        n = pl.program_id(0)
        if npush:
            copies = _push_copies(x_refs, got_refs, *refs[12 + 2 * npush:], scatter=True)

            @pl.when(n == 0)
            def _():
                _push_start(copies)

        @pl.when(n == 0)
        def _():
            ds_ref[...] = jnp.zeros_like(ds_ref)

        ii, jj, low, strict = _chunk_masks()
        eye = jnp.where(ii == jj, 1.0, 0.0)
        q, k, v = _load_heads(q_ref, nch), _load_heads(k_ref, nch), _load_heads(v_ref, nch)
        d_o = _load_heads(do_ref, nch)
        bgs = [bg_ref[c * CHUNK:(c + 1) * CHUNK, :] for c in range(nch)]
        tm_ = _delta_terms(q, k, v, bgs, nch, low, strict)
        idx, lane, rowi = tm_["idx"], tm_["lane"], tm_["rowi"]
        beta, dm, eg, ekl, gl = tm_["beta"], tm_["dm"], tm_["eg"], tm_["ekl"], tm_["gl"]
        kb, kbg, qd, kd, a_mat, p_mat = tm_["kb"], tm_["kbg"], tm_["qd"], tm_["kd"], tm_["a"], tm_["p"]
        s = [sh_ref[c, h] for c, h in idx]
        m_inv = [mi_ref[c, h] for c, h in idx]
        u, w = _load_heads(u_ref, nch), _load_heads(w_ref, nch)
        ws = _mm1_many(w, s, NN)
        vn = [x - y for x, y in zip(u, ws)]
        qp_do = _mm1_many(_side_by_side(qd, p_mat), d_o, TN)
        qdo = [x[:DH] for x in qp_do]
        pdo = [x[DH:] for x in qp_do]
        dqd = _mm1_many(d_o, s, NT)
        dp = [jnp.where(low, x, 0.0) for x in _mm1_many(d_o, vn, NT)]

        nprob = len(idx)
        dvn, dkd, dgl = [None] * nprob, [None] * nprob, [None] * nprob
        ds = [ds_ref[h] for h in range(NH)]
        for c in reversed(range(nch)):
            pr = list(range(c * NH, (c + 1) * NH))
            kds = [_mm1(kd[i], ds[i - c * NH], NN) for i in pr]
            for i, x in zip(pr, kds):
                dvn[i] = pdo[i] + x
            wdv = [_mm1(w[i], dvn[i], TN) for i in pr]
            for i in pr:
                h = i - c * NH
                dkd[i] = _mm1(vn[i], ds[h], NT)
                dgl[i] = jnp.sum(jnp.sum(s[i] * ds[h], axis=1, keepdims=True), axis=0, keepdims=True)
                ds[h] = qdo[i] + gl[i] * ds[h] - wdv[h]
        for h in range(NH):
            ds_ref[h] = ds[h]

        dw = [-x for x in _mm1_many(dvn, s, NT)]
        dvb_dkbg = _mm3_many(m_inv, _side_by_side(dvn, dw), TN)
        dvb = [x[:, :DH] for x in dvb_dkbg]
        dkbg = [x[:, DH:] for x in dvb_dkbg]
        da = [-jnp.where(strict, x, 0.0)
              for x in _mm1_many(dvb_dkbg, _side_by_side(u, w), NT)]
        gm = [x * d for x, d in zip(da, dm)]
        hm = [x * d for x, d in zip(dp, dm)]
        gh = _stack_rows(gm, hm)
        gh_k = _mm1_many(gh, k, NN)
        gk = [x[:CHUNK] for x in gh_k]
        hk = [x[CHUNK:] for x in gh_k]
        gkb_hq = _mm1_many(gh, _stack_rows(kb, q), TN)
        em = [x * a + y * p for x, a, y, p in zip(da, a_mat, dp, p_mat)]
        em_t = _transposes(em, eye)
        dbeta_all = [jnp.zeros((CHUNK, GATE_W), F32) for _ in range(nch)]
        dgam_all = [jnp.zeros((CHUNK, GATE_W), F32) for _ in range(nch)]
        for i, (c, h) in enumerate(idx):
            dkb = gk[i] + dkbg[i] * eg[i]
            dk = gkb_hq[i] + dkd[i] * ekl[i] + beta[i] * dkb
            dq = hk[i] + dqd[i] * eg[i]
            dkd_kd = jnp.sum(dkd[i] * kd[i], axis=1, keepdims=True)
            dgam = (jnp.sum(em[i], axis=1, keepdims=True) - jnp.sum(em_t[i], axis=1, keepdims=True)
                    + jnp.sum(dqd[i] * qd[i], axis=1, keepdims=True) - dkd_kd
                    + jnp.sum(dkbg[i] * kbg[i], axis=1, keepdims=True))
            tail = jnp.sum(dkd_kd, axis=0, keepdims=True) + dgl[i] * gl[i]
            dgam = dgam + jnp.where(rowi == CHUNK - 1, tail, 0.0)
            dbeta = jnp.sum(dkb * k[i], axis=1, keepdims=True) + jnp.sum(dvb[i] * v[i], axis=1, keepdims=True)
            rs = slice(c * CHUNK, (c + 1) * CHUNK)
            dqkv_ref[rs, h * DH:(h + 1) * DH] = dq
            dqkv_ref[rs, DQ + h * DH:DQ + (h + 1) * DH] = dk
            dqkv_ref[rs, 2 * DQ + h * DH:2 * DQ + (h + 1) * DH] = beta[i] * dvb[i]
            dbeta_all[c] = dbeta_all[c] + jnp.where(lane == h, dbeta, 0.0)
            dgam_all[c] = dgam_all[c] + jnp.where(lane == NH + h, dgam, 0.0)
        upf = jnp.where(ii <= jj, 1.0, 0.0)
        for c in range(nch):
            dg_all = _mmx(upf, dgam_all[c], NN)
            dbg_ref[c * CHUNK:(c + 1) * CHUNK, :] = jnp.where(lane < NH, dbeta_all[c], dg_all)
        if npush:
            @pl.when(n == ng - 1)
            def _():
                _push_finish(copies)

    col = lambda c: pl.BlockSpec((rows, DQ), lambda n: (ng - 1 - n, c))
    gate = pl.BlockSpec((rows, GATE_W), lambda n: (ng - 1 - n, 0))
    any_spec = pl.BlockSpec(memory_space=pl.ANY)
    pushed = [] if push is None else list(push)
    outs = pl.pallas_call(
        body, grid=(ng,),
        in_specs=[col(0), col(1), col(2), gate, col(0),
                  pl.BlockSpec((nch, NH, DH, DH), lambda n: (ng - 1 - n, 0, 0, 0)),
                  pl.BlockSpec((nch, NH, CHUNK, CHUNK), lambda n: (ng - 1 - n, 0, 0, 0)), col(0), col(0)]
        + [any_spec] * npush,
        out_specs=[pl.BlockSpec((rows, 3 * DQ), lambda n: (ng - 1 - n, 0)), gate] + [any_spec] * npush,
        out_shape=[jax.ShapeDtypeStruct((t, 3 * DQ), F32), jax.ShapeDtypeStruct((t, GATE_W), F32)]
        + _push_out_shapes(pushed, scatter=True),
        scratch_shapes=[pltpu.VMEM((NH, DH, DH), F32)] + (_push_sems(npush) if npush else []),
        compiler_params=_cp(("arbitrary",)), name=name)(qkv, qkv, qkv, bg, do, s_hist, m_hist, u_all, w_all, *pushed)
    return outs[0], outs[1], outs[2:]


def dn_post_fwd(ybuf, o, proj, nw, name):
    t = o.shape[0]
    tm = _row_tile(t)

    def body(y_in, o_ref, z_ref, nw_ref, y_ref):
        del y_in
        nwv = nw_ref[...]
        for h in range(NH):
            sl = slice(h * DH, (h + 1) * DH)
            oh = o_ref[:, sl]
            z = z_ref[:, sl]
            r = lax.rsqrt(jnp.mean(oh * oh, axis=-1, keepdims=True) + NORM_EPS)
            y_ref[:, sl] = (oh * r * nwv * (z * _sigmoid(z))).astype(y_ref.dtype)

    return pl.pallas_call(
        body, grid=(t // tm,),
        in_specs=[pl.BlockSpec(memory_space=pl.ANY), pl.BlockSpec((tm, DQ), lambda i: (i, 0)),
                  pl.BlockSpec((tm, DQ), lambda i: (i, 5)), pl.BlockSpec((1, DH), lambda i: (0, 0))],
        out_specs=pl.BlockSpec((tm, DQ), lambda i: (i, 1)),
        out_shape=jax.ShapeDtypeStruct(ybuf.shape, ybuf.dtype), input_output_aliases={0: 0},
        compiler_params=_cp(("parallel",)), name=name)(ybuf, o, proj, nw)


def dn_post_bwd(dproj, dy, o, proj, nw, name):
    t = o.shape[0]
    tm = _row_tile(t)

    def body(dp_in, dy_ref, o_ref, z_ref, nw_ref, do_ref, dp_ref, dnw_ref):
        del dp_in
        i = pl.program_id(0)
        nwv = nw_ref[...]
        acc = jnp.zeros((1, DH), F32)
        for h in range(NH):
            sl = slice(h * DH, (h + 1) * DH)
            oh = o_ref[:, sl]
            z = z_ref[:, sl]
            dyh = dy_ref[:, sl]
            r = lax.rsqrt(jnp.mean(oh * oh, axis=-1, keepdims=True) + NORM_EPS)
            xh = oh * r
            sg = _sigmoid(z)
            sz = z * sg
            dxh = dyh * nwv * sz
            do_ref[:, sl] = r * (dxh - xh * jnp.mean(dxh * xh, axis=-1, keepdims=True))
            dp_ref[:, sl] = (dyh * xh * nwv * (sg * (1.0 + z * (1.0 - sg)))).astype(dp_ref.dtype)
            acc = acc + jnp.sum(dyh * xh * sz, axis=0, keepdims=True)

        @pl.when(i == 0)
        def _():
            dnw_ref[...] = jnp.zeros_like(dnw_ref)

        dnw_ref[...] += acc

    vec = pl.BlockSpec((1, DH), lambda i: (0, 0))
    return pl.pallas_call(
        body, grid=(t // tm,),
        in_specs=[pl.BlockSpec(memory_space=pl.ANY), pl.BlockSpec((tm, DQ), lambda i: (i, 1)),
                  pl.BlockSpec((tm, DQ), lambda i: (i, 0)), pl.BlockSpec((tm, DQ), lambda i: (i, 5)), vec],
        out_specs=[pl.BlockSpec((tm, DQ), lambda i: (i, 0)), pl.BlockSpec((tm, DQ), lambda i: (i, 5)), vec],
        out_shape=[jax.ShapeDtypeStruct((t, DQ), F32), jax.ShapeDtypeStruct(dproj.shape, dproj.dtype),
                   jax.ShapeDtypeStruct((1, DH), F32)],
        input_output_aliases={0: 1}, compiler_params=_cp(("arbitrary",)), name=name)(dproj, dy, o, proj, nw)


def _shifted(first, second, s, lane):
    if s == 0:
        return first
    return jnp.where(lane < LANES - s, pltpu.roll(first, LANES - s, 1), pltpu.roll(second, LANES - s, 1))


def unshard_cols(g8, w, widths, name):
    _, r, wp = g8.shape
    rb = _pick(r, (256, 128, 64, 32, 16))

    def body(g_ref, *o_refs):
        lane = lax.broadcasted_iota(jnp.int32, (rb, LANES), 1)
        zeros = jnp.zeros((rb, LANES), F32)

        def src(j, ta):
            if j >= NDEV or ta * LANES >= wp:
                return zeros
            return g_ref[j, :, ta * LANES:(ta + 1) * LANES].astype(F32)

        base = 0
        for o_ref, width in zip(o_refs, widths):
            for b in range(width // LANES):
                c0 = base + b * LANES
                if c0 >= NDEV * w:
                    tile = zeros
                else:
                    j0, o0 = divmod(c0, w)
                    n0 = min(w - o0, LANES)
                    ta, s = divmod(o0, LANES)
                    tile = _shifted(src(j0, ta), src(j0, ta + 1), s, lane)
                    if n0 < LANES:
                        nxt = pltpu.roll(src(j0 + 1, 0), n0, 1) if j0 + 1 < NDEV else zeros
                        tile = jnp.where(lane < n0, tile, nxt)
                o_ref[:, b * LANES:(b + 1) * LANES] = tile.astype(o_ref.dtype)
            base += width

    return pl.pallas_call(
        body, grid=(r // rb,), in_specs=[pl.BlockSpec((NDEV, rb, wp), lambda i: (0, i, 0))],
        out_specs=[pl.BlockSpec((rb, width), lambda i: (i, 0)) for width in widths],
        out_shape=[jax.ShapeDtypeStruct((r, width), g8.dtype) for width in widths],
        compiler_params=_cp(("parallel",)), name=name)(g8)


def shard_cols(parts, w, name):
    r = parts[0].shape[0]
    wp = _lane_pad(w)
    rb = _pick(r, (256, 128, 64, 32, 16))
    tiles_of = [p.shape[1] // LANES for p in parts]

    def body(*refs):
        p_refs, o_ref = refs[:-1], refs[-1]
        lane = lax.broadcasted_iota(jnp.int32, (rb, LANES), 1)
        zeros = jnp.zeros((rb, LANES), F32)

        def glob(tile_idx):
            for p_ref, n_tiles in zip(p_refs, tiles_of):
                if tile_idx < n_tiles:
                    return p_ref[:, tile_idx * LANES:(tile_idx + 1) * LANES].astype(F32)
                tile_idx -= n_tiles
            return zeros

        for j in range(NDEV):
            for a in range(wp // LANES):
                nv = min(w - a * LANES, LANES)
                tb, s = divmod(w * j + a * LANES, LANES)
                tile = _shifted(glob(tb), glob(tb + 1), s, lane)
                if nv < LANES:
                    tile = jnp.where(lane < nv, tile, 0.0)
                o_ref[j, :, a * LANES:(a + 1) * LANES] = tile.astype(o_ref.dtype)

    return pl.pallas_call(
        body, grid=(r // rb,), in_specs=[pl.BlockSpec((rb, p.shape[1]), lambda i: (i, 0)) for p in parts],
        out_specs=pl.BlockSpec((NDEV, rb, wp), lambda i: (0, i, 0)),
        out_shape=jax.ShapeDtypeStruct((NDEV, r, wp), GRAD_DTYPE),
        compiler_params=_cp(("parallel",)), name=name)(*parts)


def _me_and_peers():
    mx, my, mc = lax.axis_index("x"), lax.axis_index("y"), lax.axis_index("c")
    me = 4 * mx + 2 * my + mc
    peers = []
    for kk in range(1, NDEV):
        px = 1 - mx if kk & 4 else mx
        py = 1 - my if kk & 2 else my
        pc = 1 - mc if kk & 1 else mc
        peers.append(((px, py, pc), 4 * px + 2 * py + pc))
    return me, peers


def _push_copies(x_refs, o_refs, send_sems, recv_sems, local_sems, scatter):
    me, peers = _me_and_peers()
    npeer = NDEV - 1
    local, sends, recvs = [], [], []
    for a, (x_ref, o_ref) in enumerate(zip(x_refs, o_refs)):
        local.append(pltpu.make_async_copy(x_ref.at[me] if scatter else x_ref, o_ref.at[me], local_sems.at[a]))
        for kk, (peer, pidx) in enumerate(peers):
            src = x_ref.at[pidx] if scatter else x_ref
            sems = dict(send_sem=send_sems.at[a * npeer + kk], recv_sem=recv_sems.at[a * npeer + kk],
                        device_id=peer, device_id_type=pl.DeviceIdType.MESH)
            sends.append(pltpu.make_async_remote_copy(src_ref=src, dst_ref=o_ref.at[me], **sems))
            recvs.append(pltpu.make_async_remote_copy(src_ref=src, dst_ref=o_ref.at[pidx], **sems))
    return local, sends, recvs


def _push_start(copies):
    local, sends, _ = copies
    for cp in local + sends:
        cp.start()


def _push_finish(copies):
    local, sends, recvs = copies
    for cp in recvs:
        cp.wait_recv()
    for cp in sends:
        cp.wait_send()
    for cp in local:
        cp.wait()


def _push_out_shapes(xs, scatter):
    return [jax.ShapeDtypeStruct(x.shape if scatter else (NDEV,) + x.shape, x.dtype) for x in xs]


def _push_sems(n):
    return [pltpu.SemaphoreType.DMA((n * (NDEV - 1),)), pltpu.SemaphoreType.DMA((n * (NDEV - 1),)),
            pltpu.SemaphoreType.DMA((n,))]


def _push_to_all(xs, name, scatter):
    n = len(xs)

    def body(*refs):
        copies = _push_copies(refs[:n], refs[n:2 * n], *refs[2 * n:], scatter=scatter)
        _push_start(copies)
        _push_finish(copies)

    any_spec = pl.BlockSpec(memory_space=pl.ANY)
    return pl.pallas_call(
        body, in_specs=[any_spec] * n, out_specs=[any_spec] * n, out_shape=_push_out_shapes(xs, scatter),
        scratch_shapes=_push_sems(n), name=name)(*xs)


def adamw(recv, w, m, v, name):
    rows, cols = w.shape
    c1 = 1.0 - ADAM_B1 ** ADAM_STEP
    c2 = 1.0 - ADAM_B2 ** ADAM_STEP
    cap = ADAM_BLOCK_BYTES // (NDEV * cols * 4)
    rb = _pick(rows, [p for p in (2048, 1024, 512, 256, 128, 64, 32, 16, 8) if p <= cap])

    def body(r_ref, w_ref, m_ref, v_ref, g_ref, d_ref, m2_ref, v2_ref):
        g = r_ref[0].astype(F32)
        for j in range(1, NDEV):
            g = g + r_ref[j].astype(F32)
        m2 = ADAM_B1 * m_ref[...] + (1.0 - ADAM_B1) * g
        v2 = ADAM_B2 * v_ref[...] + (1.0 - ADAM_B2) * (g * g)
        g_ref[...] = g
        m2_ref[...] = m2
        v2_ref[...] = v2
        d_ref[...] = -ADAM_LR * ((m2 / c1) / (jnp.sqrt(v2 / c2) + ADAM_EPS) + ADAM_WD * w_ref[...])

    blk = pl.BlockSpec((rb, cols), lambda i: (i, 0))
    return pl.pallas_call(
        body, grid=(rows // rb,),
        in_specs=[pl.BlockSpec((NDEV, rb, cols), lambda i: (0, i, 0)), blk, blk, blk],
        out_specs=[blk, blk, blk, blk], out_shape=[jax.ShapeDtypeStruct((rows, cols), F32)] * 4,
        compiler_params=_cp(("parallel",)), name=name)(recv, w, m, v)


SMALL_SHARDED = ("meta_tokens", "conv_dw_w", "dn_conv_w")
SMALL_REPLICATED = ("norm_mix_w", "conv_dw_b", "conv_ln_w", "conv_ln_b", "dn_A_log", "dn_dt_bias", "dn_norm_w",
                    "norm_ffn_w", "final_norm_w")
PARAM_ORDER = ("meta_tokens", "norm_mix_w", "w_in", "conv_dw_w", "conv_dw_b", "conv_ln_w", "conv_ln_b", "dn_conv_w",
               "dn_A_log", "dn_dt_bias", "dn_norm_w", "w_out", "norm_ffn_w", "ffn_w_gu", "ffn_w_down", "final_norm_w")


def _pack_small(parts, axis):
    flat = jnp.concatenate(parts, axis=axis)
    n = flat.shape[axis]
    total = -(-n // (8 * LANES)) * (8 * LANES)
    pad = [(0, 0)] * flat.ndim
    pad[axis] = (0, total - n)
    flat = jnp.pad(flat, pad)
    return flat.reshape(flat.shape[:axis] + (total // LANES, LANES))


def _unshard_last(g8):
    moved = jnp.moveaxis(g8, 0, -2)
    return moved.reshape(moved.shape[:-2] + (-1,))


def _per_destination_last(full):
    split = full.reshape(full.shape[:-1] + (NDEV, full.shape[-1] // NDEV))
    return jnp.moveaxis(split, -2, 0).reshape(NDEV, -1)


def _lane_row(vec4, width):
    return jnp.pad(vec4, (NH, width - 2 * NH))[None]


def kernel(x, meta_tokens, norm_mix_w, w_in, conv_dw_w, conv_dw_b, conv_ln_w, conv_ln_b, dn_conv_w, dn_A_log, dn_dt_bias, dn_norm_w, w_out, norm_ffn_w, ffn_w_gu, ffn_w_down, final_norm_w, loss_target, m_meta_tokens, m_norm_mix_w, m_w_in, m_conv_dw_w, m_conv_dw_b, m_conv_ln_w, m_conv_ln_b, m_dn_conv_w, m_dn_A_log, m_dn_dt_bias, m_dn_norm_w, m_w_out, m_norm_ffn_w, m_ffn_w_gu, m_ffn_w_down, m_final_norm_w, v_meta_tokens, v_norm_mix_w, v_w_in, v_conv_dw_w, v_conv_dw_b, v_conv_ln_w, v_conv_ln_b, v_dn_conv_w, v_dn_A_log, v_dn_dt_bias, v_dn_norm_w, v_w_out, v_norm_ffn_w, v_ffn_w_gu, v_ffn_w_down, v_final_norm_w):
    weights = dict(meta_tokens=meta_tokens, norm_mix_w=norm_mix_w, w_in=w_in, conv_dw_w=conv_dw_w, conv_dw_b=conv_dw_b,
                   conv_ln_w=conv_ln_w, conv_ln_b=conv_ln_b, dn_conv_w=dn_conv_w, dn_A_log=dn_A_log,
                   dn_dt_bias=dn_dt_bias, dn_norm_w=dn_norm_w, w_out=w_out, norm_ffn_w=norm_ffn_w, ffn_w_gu=ffn_w_gu,
                   ffn_w_down=ffn_w_down, final_norm_w=final_norm_w)
    m_in = dict(meta_tokens=m_meta_tokens, norm_mix_w=m_norm_mix_w, w_in=m_w_in, conv_dw_w=m_conv_dw_w,
                conv_dw_b=m_conv_dw_b, conv_ln_w=m_conv_ln_w, conv_ln_b=m_conv_ln_b, dn_conv_w=m_dn_conv_w,
                dn_A_log=m_dn_A_log, dn_dt_bias=m_dn_dt_bias, dn_norm_w=m_dn_norm_w, w_out=m_w_out,
                norm_ffn_w=m_norm_ffn_w, ffn_w_gu=m_ffn_w_gu, ffn_w_down=m_ffn_w_down, final_norm_w=m_final_norm_w)
    v_in = dict(meta_tokens=v_meta_tokens, norm_mix_w=v_norm_mix_w, w_in=v_w_in, conv_dw_w=v_conv_dw_w,
                conv_dw_b=v_conv_dw_b, conv_ln_w=v_conv_ln_w, conv_ln_b=v_conv_ln_b, dn_conv_w=v_dn_conv_w,
                dn_A_log=v_dn_A_log, dn_dt_bias=v_dn_dt_bias, dn_norm_w=v_dn_norm_w, w_out=v_w_out,
                norm_ffn_w=v_norm_ffn_w, ffn_w_gu=v_ffn_w_gu, ffn_w_down=v_ffn_w_down, final_norm_w=v_final_norm_w)

    depth = w_in.shape[0]
    seq = x.shape[1]
    t = _padded_rows(seq)
    rows_d = depth * D
    win_w, gu_w = w_in.shape[2], ffn_w_gu.shape[2]
    win_wp, gu_wp = _lane_pad(win_w), _lane_pad(gu_w)

    def pad_cols(a, wp):
        return jnp.pad(a, ((0, 0), (0, 0), (0, wp - a.shape[2]))).reshape(rows_d, wp)

    def rows2d(a):
        return a.reshape(-1, a.shape[2])

    small_shards = [weights[n] for n in SMALL_SHARDED]
    win_p = pad_cols(w_in, win_wp).astype(MXU_DTYPE).reshape(depth, D, win_wp)
    gu_p = pad_cols(ffn_w_gu, gu_wp).astype(MXU_DTYPE).reshape(depth, D, gu_wp)
    wout_b, wdown_b = w_out.astype(MXU_DTYPE), ffn_w_down.astype(MXU_DTYPE)

    got = [dict() for _ in range(depth)]
    got[0]["win"], g_small = _push_to_all([win_p[0], _pack_small([s.reshape(-1) for s in small_shards], 0)],
                                          "gather_first", scatter=False)
    wts = []
    small_flat, off, small_full = g_small.reshape(NDEV, -1), 0, {}
    for n, s in zip(SMALL_SHARDED, small_shards):
        small_full[n] = _unshard_last(small_flat[:, off:off + s.size].reshape((NDEV,) + s.shape))
        off += s.size
    cdw32 = jnp.pad(small_full["conv_dw_w"], ((0, 0), (0, CONV_HALO - CONV_W), (0, 0)))
    dcw8 = jnp.pad(small_full["dn_conv_w"], ((0, 0), (0, DN_HALO - DN_W), (0, 0)))

    h = jnp.concatenate([jnp.zeros((FRONT, D), F32), small_full["meta_tokens"], x[0],
                         jnp.zeros((t - HEAD - seq, D), F32)], axis=0)
    tgt = jnp.pad(loss_target[0], ((HEAD, t - HEAD - seq), (0, 0)))

    saved = []
    for l in range(depth):
        nmw, nfw = norm_mix_w[l][None], norm_ffn_w[l][None]
        cdb, clw, clb = conv_dw_b[l][None], conv_ln_w[l][None], conv_ln_b[l][None]
        alog, dtb, dnw = _lane_row(dn_A_log[l], GATE_W), _lane_row(dn_dt_bias[l], GATE_W), dn_norm_w[l][None]
        w_main, w_gate_cols = unshard_cols(got[l]["win"], win_w, [PROJ_MAIN, GATE_W], "unshard_w_in")
        proj = mm([h], w_main, a_fn=_rms_apply, a_vecs=[nmw], name="mm_proj")
        pg = mm([h], w_gate_cols, a_fn=_rms_apply, a_vecs=[nmw], name="mm_proj_gate")
        own = [gu_p[0], wout_b[0], wdown_b[0]] if l == 0 else []
        ahead = [wout_b[l + 1], wdown_b[l + 1]] if l + 1 < depth else []
        ybuf, u1, came = conv_fwd(proj, cdw32[l], cdb, clw, clb, seq, "conv_fwd_gather" if own + ahead else "conv_fwd",
                                  push=own + ahead)
        if own:
            got[0]["gu"], got[0]["out"], got[0]["down"] = came[:3]
        if ahead:
            got[l + 1]["out"], got[l + 1]["down"] = came[len(own):]
        qkv = dn_pre_fwd(proj, dcw8[l], "dn_pre_fwd")
        bg = gate_fwd(pg, alog, dtb, seq, "gate_fwd")
        if l + 1 < depth:
            (o, s_hist, m_hist, u_all, w_all), (got[l + 1]["win"], got[l + 1]["gu"]) = delta_fwd(
                qkv, bg, "delta_fwd_gather", push=[win_p[l + 1], gu_p[l + 1]])
        else:
            (o, s_hist, m_hist, u_all, w_all), _ = delta_fwd(qkv, bg, "delta_fwd")
        w_g, w_u = unshard_cols(got[l]["gu"], gu_w, [DFF, DFF], "unshard_w_gu")
        wl = dict(main=w_main, gate=w_gate_cols, wg=w_g, wu=w_u, out=got[l]["out"].reshape(D, D),
                  down=got[l]["down"].reshape(DFF, D))
        wts.append(wl)
        ybuf = dn_post_fwd(ybuf, o, proj, dnw, "dn_post_fwd")
        h_mid = mm(ybuf, wl["out"], extras=[h], out_fn=_add, name="mm_out")
        gate = mm([h_mid], wl["wg"], a_fn=_rms_apply, a_vecs=[nfw], name="mm_gate")
        up = mm([h_mid], wl["wu"], a_fn=_rms_apply, a_vecs=[nfw], name="mm_up")
        h_out = mm([gate, up], wl["down"], a_fn=_swiglu, extras=[h_mid], out_fn=_add, tile_cap=ROW_TILE, name="mm_down")
        saved.append(dict(h=h, proj=proj, pg=pg, ybuf=ybuf, u1=u1, qkv=qkv, bg=bg, o=o, s_hist=s_hist,
                          m_hist=m_hist, u_all=u_all, w_all=w_all, h_mid=h_mid, gate=gate, up=up,
                          nmw=nmw, nfw=nfw, clw=clw, clb=clb, alog=alog, dtb=dtb, dnw=dnw))
        h = h_out

    dh, loss_part, d_final = loss_bwd(h, tgt, final_norm_w[None], seq, "loss_bwd")
    loss = lax.psum(loss_part[0, 0], MESH_AXES)

    per_layer = ("norm_mix_w", "conv_dw_w", "conv_dw_b", "conv_ln_w", "conv_ln_b", "dn_conv_w", "dn_A_log", "dn_dt_bias",
                 "dn_norm_w", "norm_ffn_w")
    grads = {n: [None] * depth for n in per_layer}
    received = [None] * depth
    pending = None
    dw_mm = lambda a, b, name, **kw: mm(a, b, ta=True, out_dtypes=(GRAD_DTYPE,), name=name, **kw)
    for l in reversed(range(depth)):
        s, wl = saved[l], wts[l]
        dgate, dup = mm(dh, wl["down"], tb=True, extras=[s["gate"], s["up"]], out_fn=_swiglu_bwd,
                        out_dtypes=(MXU_DTYPE, MXU_DTYPE), tile_cap=ROW_TILE, name="mm_down_dx")
        d_down = dw_mm([s["gate"], s["up"]], dh, "mm_down_dw", a_fn=_swiglu, tile_cap=ROW_TILE)
        dhn2 = mm(dup, wl["wu"], tb=True, extras=[mm(dgate, wl["wg"], tb=True, name="mm_gate_dx")], out_fn=_add,
                  name="mm_up_dx")
        dh_mid, dnfw = rms_bwd(dhn2, s["h_mid"], s["nfw"], dh, "rms_ffn_bwd")
        d_wg = dw_mm([s["h_mid"]], dgate, "mm_gate_dw", a_fn=_rms_apply, a_vecs=[s["nfw"]])
        d_wu = dw_mm([s["h_mid"]], dup, "mm_up_dw", a_fn=_rms_apply, a_vecs=[s["nfw"]])
        dy = mm(dh_mid, wl["out"], tb=True, name="mm_out_dx")
        d_out = dw_mm(s["ybuf"], dh_mid, "mm_out_dw")
        early = [shard_cols([d_wg, d_wu], gu_w, "shard_w_gu"), d_out.reshape(NDEV, D // NDEV, D),
                 d_down.reshape(NDEV, DFF // NDEV, D)]
        (dproj, dcdw, dcdb, dclw, dclb), early_came = conv_bwd(
            dy, s["u1"], s["proj"], cdw32[l], s["clw"], s["clb"], seq, "conv_bwd" if l else "conv_bwd_exchange",
            push=() if l else early)
        do, dproj, ddnw = dn_post_bwd(dproj, dy, s["o"], s["proj"], s["dnw"], "dn_post_bwd")
        delta_args = (s["qkv"], s["bg"], do, s["s_hist"], s["m_hist"], s["u_all"], s["w_all"])
        if pending is None:
            dqkv, dbg, _ = delta_bwd(*delta_args, "delta_bwd")
        else:
            dqkv, dbg, received[l + 1] = delta_bwd(*delta_args, "delta_bwd_exchange", push=pending)
        dproj, ddcw = dn_pre_bwd(dproj, dqkv, s["proj"], dcw8[l], "dn_pre_bwd")
        dpg, dalog, ddtb = gate_bwd(dbg, s["pg"], s["alog"], s["dtb"], seq, "gate_bwd")
        dhn_gate = mm(dpg, wl["gate"], tb=True, name="mm_proj_gate_dx")
        dhn = mm(dproj, wl["main"], tb=True, extras=[dhn_gate], out_fn=_add, name="mm_proj_dx")
        d_main = dw_mm([s["h"]], dproj, "mm_proj_dw", a_fn=_rms_apply, a_vecs=[s["nmw"]])
        d_gate_cols = dw_mm([s["h"]], dpg, "mm_proj_gate_dw", a_fn=_rms_apply, a_vecs=[s["nmw"]])
        pending = [shard_cols([d_main, d_gate_cols], win_w, "shard_w_in")] + (early if l else [])
        dh, dnmw = rms_bwd(dhn, s["h"], s["nmw"], dh_mid, "rms_mix_bwd")
        grads["norm_mix_w"][l] = dnmw[0]
        grads["norm_ffn_w"][l] = dnfw[0]
        grads["conv_dw_w"][l] = dcdw[:CONV_W]
        grads["conv_dw_b"][l] = dcdb[0]
        grads["conv_ln_w"][l] = dclw[0]
        grads["conv_ln_b"][l] = dclb[0]
        grads["dn_conv_w"][l] = ddcw[:DN_W]
        grads["dn_A_log"][l] = dalog[0, NH:2 * NH]
        grads["dn_dt_bias"][l] = ddtb[0, NH:2 * NH]
        grads["dn_norm_w"][l] = ddnw[0]

    grad_x = dh[HEAD:HEAD + seq][None]
    full = {n: jnp.stack(g) for n, g in grads.items()}
    full["meta_tokens"] = dh[FRONT:HEAD]
    full["final_norm_w"] = d_final[0]

    send_small = _pack_small(
        [_per_destination_last(full[n]) for n in SMALL_SHARDED]
        + [jnp.broadcast_to(full[n].reshape(1, -1), (NDEV, full[n].size)) for n in SMALL_REPLICATED], 1)
    r_win0, r_small = _push_to_all(pending + [send_small], "exchange_last", scatter=True)
    received[0] = [r_win0, *early_came]
    r_win, r_gu, r_wout, r_down = (jnp.concatenate([received[l][k] for l in range(depth)], axis=1) for k in range(4))

    small_names = SMALL_SHARDED + SMALL_REPLICATED
    pack_local = lambda tree: _pack_small([tree[n].reshape(-1) for n in small_names], 0)
    results = {}

    def run_adamw(name, recv, prep, finish):
        outs = adamw(recv, prep(weights[name]), prep(m_in[name]), prep(v_in[name]), "adamw_" + name)
        results[name] = [finish(o) for o in outs]

    run_adamw("w_in", r_win, lambda a: pad_cols(a, win_wp),
              lambda o: o[:, :win_w].reshape(depth, D, win_w))
    run_adamw("ffn_w_gu", r_gu, lambda a: pad_cols(a, gu_wp), lambda o: o[:, :gu_w].reshape(depth, D, gu_w))
    run_adamw("w_out", r_wout, rows2d, lambda o: o.reshape(w_out.shape))
    run_adamw("ffn_w_down", r_down, rows2d, lambda o: o.reshape(ffn_w_down.shape))
    small_outs = adamw(r_small, pack_local(weights), pack_local(m_in), pack_local(v_in), "adamw_small")
    for kind in range(4):
        flat, off = small_outs[kind].reshape(-1), 0
        for n in small_names:
            wgt = weights[n]
            results.setdefault(n, [None] * 4)[kind] = flat[off:off + wgt.size].reshape(wgt.shape)
            off += wgt.size

    return (loss, grad_x, *[results[n][0] for n in PARAM_ORDER], *[results[n][1] for n in PARAM_ORDER],
            *[results[n][2] for n in PARAM_ORDER], *[results[n][3] for n in PARAM_ORDER])
```

```python
import jax
import jax.numpy as jnp
from jax import lax
from jax.experimental import pallas as pl
from jax.experimental.pallas import tpu as pltpu

F32 = jnp.float32
MXU_DTYPE = jnp.bfloat16

D = 1024
N_META = 16
CHUNK = 64
CHUNK_LOG2 = 6
INV_BASE_LOG2 = 3
FRONT = CHUNK - N_META
HEAD = CHUNK
CONV_CH = 512
CONV_W = 31
CONV_HALO = 32
NH = 4
DH = 128
DQ = NH * DH
DN_W = 4
DN_HALO = 8
DFF = 2816
PROJ_MAIN = 3072
D_IN = 3080
GATE_W = 128
LANES = 128
NDEV = 8
NORM_EPS = 1e-6
LN_EPS = 1e-5
L2_EPS = 1e-6
VMEM_LIMIT_V7X = 48 * 1024 * 1024
ROW_TILE = 640
ROW_TILE_SMALL = 128
MM_TILES = (1408, 1280, 1024, 640, 512, 256, 128)
MM_SUB = 4
GRAD_DTYPE = jnp.bfloat16
DELTA_CHUNKS = (4, 2, 1)
ADAM_BLOCK_BYTES = 8 * 1024 * 1024

ADAM_LR = 0.001
ADAM_B1 = 0.9
ADAM_B2 = 0.999
ADAM_EPS = 1e-08
ADAM_WD = 0.01
ADAM_STEP = 10

MESH_AXES = ("x", "y", "c")
NN = ((1,), (0,))
NT = ((1,), (1,))
TN = ((0,), (0,))

assert 1 << CHUNK_LOG2 == CHUNK


def _row_tile(t):
    return ROW_TILE if t % ROW_TILE == 0 else ROW_TILE_SMALL


def _padded_rows(seq):
    n = HEAD + seq
    tm = ROW_TILE if n >= 4 * ROW_TILE else ROW_TILE_SMALL
    return -(-n // tm) * tm


def _pick(n, prefs):
    for p in prefs:
        if n % p == 0:
            return p
    return n


def _lane_pad(n):
    return -(-n // LANES) * LANES


def _cp(sem):
    return pltpu.CompilerParams(dimension_semantics=sem, vmem_limit_bytes=VMEM_LIMIT_V7X)


def _sigmoid(x):
    return 1.0 / (1.0 + jnp.exp(-x))


def _softplus(x):
    return jnp.maximum(x, 0.0) + jnp.log(1.0 + jnp.exp(-jnp.abs(x)))


def _valid_rows(i, tm, seq, width, first=FRONT):
    rows = i * tm + lax.broadcasted_iota(jnp.int32, (tm, width), 0)
    return jnp.logical_and(rows >= first, rows < HEAD + seq)


def _dot(a, b, dims):
    return lax.dot_general(a, b, (dims, ((), ())), preferred_element_type=F32)


def _split(x, n):
    out, r = [], x
    for _ in range(n):
        p = r.astype(MXU_DTYPE)
        out.append(p)
        r = r - p.astype(F32)
    return out


def _mm1(a, b, dims):
    return _dot(a.astype(MXU_DTYPE), b.astype(MXU_DTYPE), dims)


def _mm1_many(a_list, b_list, dims):
    return [_mm1(a, b, dims) for a, b in zip(a_list, b_list)]


def _mm3_many(a_list, b_list, dims):
    sa = [_split(a, 2) for a in a_list]
    sb = [_split(b, 2) for b in b_list]
    hh = [_dot(x[0], y[0], dims) for x, y in zip(sa, sb)]
    hl = [_dot(x[0], y[1], dims) for x, y in zip(sa, sb)]
    lh = [_dot(x[1], y[0], dims) for x, y in zip(sa, sb)]
    return [p + (q + r) for p, q, r in zip(hh, hl, lh)]


def _mmx(e, b, dims):
    e = e.astype(MXU_DTYPE)
    b1, b2, b3 = _split(b, 3)
    return _dot(e, b1, dims) + (_dot(e, b2, dims) + _dot(e, b3, dims))


def mm(a, b, *, ta=False, tb=False, a_fn=None, a_vecs=(), extras=(), out_fn=None, out_dtypes=(F32,), tile_cap=None,
       plus=(), name):
    a_list = list(a) if isinstance(a, (list, tuple)) else [a]
    (k_dim, m_dim) = a_list[0].shape if ta else a_list[0].shape[::-1]
    n_dim = b.shape[0] if tb else b.shape[1]
    assert (b.shape[1] if tb else b.shape[0]) == k_dim
    capped = MM_TILES if tile_cap is None else tuple(p for p in MM_TILES if p <= tile_cap)
    tm = _pick(m_dim, MM_TILES if ta else capped)
    tk = _pick(k_dim, capped if ta else MM_TILES)
    tn = _pick(n_dim, MM_TILES)
    nk = k_dim // tk
    na, nv, ne, no = len(a_list), len(a_vecs), len(extras), len(out_dtypes)
    dims = ((0,) if ta else (1,), (1,) if tb else (0,))
    nsub = MM_SUB if a_fn is not None and not ta and tm % (MM_SUB * 16) == 0 else 1
    sub = tm // nsub
    npl = len(plus)

    def body(*refs):
        a_refs, v_refs, b_ref = refs[:na], refs[na:na + nv], refs[na + nv]
        e_refs = refs[na + nv + 1:na + nv + 1 + ne]
        base = na + nv + 1 + ne
        pa_refs, pb_refs = refs[base:base + npl], refs[base + npl:base + 2 * npl]
        base += 2 * npl
        o_refs = refs[base:base + no]
        acc_ref = refs[-1] if nk > 1 else None
        k = pl.program_id(2)

        def left(rows):
            tiles = [r[rows, :] for r in a_refs]
            return tiles[0] if a_fn is None else a_fn(*tiles, *[v[...] for v in v_refs])

        def finish(acc, rows):
            outs = (acc,) if out_fn is None else out_fn(acc, *[e[rows, :] for e in e_refs])
            for o_ref, out in zip(o_refs, outs):
                o_ref[rows, :] = out.astype(o_ref.dtype)

        if nk > 1:
            @pl.when(k == 0)
            def _():
                acc_ref[...] = jnp.zeros_like(acc_ref)

        for r in range(nsub):
            rows = slice(r * sub, (r + 1) * sub) if nsub > 1 else slice(None)
            lhs = left(rows)
            prod = _mm1(lhs, b_ref[...], dims)
            for pa_ref, pb_ref in zip(pa_refs, pb_refs):
                prod = prod + _mm1(pa_ref[...], pb_ref[...], dims)
            if nk == 1:
                finish(prod, rows)
            else:
                acc_ref[rows, :] += prod

        if nk > 1:
            @pl.when(k == nk - 1)
            def _():
                finish(acc_ref[...], slice(None))

    if ta:
        a_spec = pl.BlockSpec((tk, tm), lambda i, j, k: (k, i))
        v_spec = pl.BlockSpec((1, tm), lambda i, j, k: (0, i))
    else:
        a_spec = pl.BlockSpec((tm, tk), lambda i, j, k: (i, k))
        v_spec = pl.BlockSpec((1, tk), lambda i, j, k: (0, k))
    b_spec = pl.BlockSpec((tn, tk), lambda i, j, k: (j, k)) if tb else pl.BlockSpec((tk, tn), lambda i, j, k: (k, j))
    o_spec = pl.BlockSpec((tm, tn), lambda i, j, k: (i, j))
    outs = pl.pallas_call(
        body, grid=(m_dim // tm, n_dim // tn, nk),
        in_specs=[a_spec] * na + [v_spec] * nv + [b_spec] + [o_spec] * ne + [a_spec] * npl + [b_spec] * npl,
        out_specs=[o_spec] * no,
        out_shape=[jax.ShapeDtypeStruct((m_dim, n_dim), dt) for dt in out_dtypes],
        scratch_shapes=[pltpu.VMEM((tm, tn), F32)] if nk > 1 else [],
        compiler_params=_cp(("parallel", "parallel", "arbitrary")),
        name=name)(*a_list, *a_vecs, b, *extras, *[p[0] for p in plus], *[p[1] for p in plus])
    return outs[0] if no == 1 else outs


def _rms_apply(x, w):
    assert x.shape[-1] == D
    return x * lax.rsqrt(jnp.mean(x * x, axis=-1, keepdims=True) + NORM_EPS) * w


def _swiglu(g, u):
    return g * _sigmoid(g) * u


def _swiglu_bwd(dact, g, u):
    sg = _sigmoid(g)
    return dact * u * (sg * (1.0 + g * (1.0 - sg))), dact * (g * sg)


def _add(acc, r):
    return (acc + r,)


def rms_bwd(dy, h, w, dres, name):
    t = h.shape[0]
    tm = _row_tile(t)

    def body(dy_ref, h_ref, w_ref, dres_ref, dh_ref, dw_ref):
        i = pl.program_id(0)
        x = h_ref[...]
        r = lax.rsqrt(jnp.mean(x * x, axis=-1, keepdims=True) + NORM_EPS)
        xh = x * r
        g = dy_ref[...] * w_ref[...]
        dh_ref[...] = dres_ref[...] + r * (g - xh * jnp.mean(g * xh, axis=-1, keepdims=True))

        @pl.when(i == 0)
        def _():
            dw_ref[...] = jnp.zeros_like(dw_ref)

        dw_ref[...] += jnp.sum(dy_ref[...] * xh, axis=0, keepdims=True)

    row = pl.BlockSpec((tm, D), lambda i: (i, 0))
    vec = pl.BlockSpec((1, D), lambda i: (0, 0))
    return pl.pallas_call(
        body, grid=(t // tm,), in_specs=[row, row, vec, row], out_specs=[row, vec],
        out_shape=[jax.ShapeDtypeStruct((t, D), F32), jax.ShapeDtypeStruct((1, D), F32)],
        compiler_params=_cp(("arbitrary",)), name=name)(dy, h, w, dres)


def loss_bwd(h, tgt, w, seq, name):
    t = h.shape[0]
    tm = _row_tile(t)

    def body(h_ref, t_ref, w_ref, dh_ref, loss_ref, dw_ref):
        i = pl.program_id(0)
        x = h_ref[...]
        wv = w_ref[...]
        r = lax.rsqrt(jnp.mean(x * x, axis=-1, keepdims=True) + NORM_EPS)
        xh = x * r
        err = jnp.where(_valid_rows(i, tm, seq, D, HEAD), xh * wv - t_ref[...], 0.0)
        dy = err * (1.0 / D)
        g = dy * wv
        dh_ref[...] = r * (g - xh * jnp.mean(g * xh, axis=-1, keepdims=True))

        @pl.when(i == 0)
        def _():
            dw_ref[...] = jnp.zeros_like(dw_ref)
            loss_ref[...] = jnp.zeros_like(loss_ref)

        dw_ref[...] += jnp.sum(dy * xh, axis=0, keepdims=True)
        part = jnp.sum(jnp.sum(err * err, axis=1, keepdims=True), axis=0, keepdims=True) * (0.5 / D)
        loss_ref[...] += jnp.broadcast_to(part, loss_ref.shape)

    row = pl.BlockSpec((tm, D), lambda i: (i, 0))
    vec = pl.BlockSpec((1, D), lambda i: (0, 0))
    return pl.pallas_call(
        body, grid=(t // tm,), in_specs=[row, row, vec],
        out_specs=[row, pl.BlockSpec((1, LANES), lambda i: (0, 0)), vec],
        out_shape=[jax.ShapeDtypeStruct((t, D), F32), jax.ShapeDtypeStruct((1, LANES), F32),
                   jax.ShapeDtypeStruct((1, D), F32)],
        compiler_params=_cp(("arbitrary",)), name=name)(h, tgt, w)


def _layernorm_parts(u1):
    mu = jnp.mean(u1, axis=-1, keepdims=True)
    xc = u1 - mu
    rstd = lax.rsqrt(jnp.mean(xc * xc, axis=-1, keepdims=True) + LN_EPS)
    return xc * rstd, rstd


SUBLANES = 8
CONV_ROWS = 16


def _shift_copies(ext, sh, tm):
    for s in range(1, SUBLANES):
        sh[s - 1, :, :] = ext[pl.ds(s, tm + CONV_HALO - SUBLANES), :]


def _window(ext, sh, off, rows, start=0):
    s, m = off % SUBLANES, off // SUBLANES
    if s == 0:
        return ext[pl.ds(start + off, rows), :]
    return sh[s - 1, pl.ds(start + SUBLANES * m, rows), :]


def _shift_scratch(tm):
    return pltpu.VMEM((SUBLANES - 1, tm + CONV_HALO - SUBLANES, CONV_CH), F32)


def conv_fwd(proj, w32, b, lw, lb, seq, name, push=()):
    t = proj.shape[0]
    tm = _row_tile(t)
    nt = t // tm
    npush = len(push)

    def body(*refs):
        cv_ref, cg_ref, w_ref, b_ref, lw_ref, lb_ref = refs[:6]
        x_refs = refs[6:6 + npush]
        y_ref, u1_ref = refs[6 + npush:8 + npush]
        got_refs = refs[8 + npush:8 + 2 * npush]
        ext, sh = refs[8 + 2 * npush:10 + 2 * npush]
        i = pl.program_id(0)
        if npush:
            copies = _push_copies(x_refs, got_refs, *refs[10 + 2 * npush:], scatter=False)

            @pl.when(i == 0)
            def _():
                _push_start(copies)

        @pl.when(i == 0)
        def _():
            ext[0:CONV_HALO, :] = jnp.zeros((CONV_HALO, CONV_CH), F32)

        @pl.when(i > 0)
        def _():
            ext[0:CONV_HALO, :] = ext[tm:tm + CONV_HALO, :]

        ext[CONV_HALO:CONV_HALO + tm, :] = cv_ref[...] * _sigmoid(cg_ref[...])
        _shift_copies(ext, sh, tm)
        acc = jnp.broadcast_to(b_ref[...], (tm, CONV_CH))
        for j in range(CONV_W):
            acc = acc + w_ref[j:j + 1, :] * _window(ext, sh, CONV_HALO - (CONV_W - 1) + j, tm)
        u1_ref[...] = acc
        xh, _ = _layernorm_parts(acc)
        ln = xh * lw_ref[...] + lb_ref[...]
        y = ln * _sigmoid(ln)
        y_ref[...] = jnp.where(_valid_rows(i, tm, seq, CONV_CH), y, 0.0).astype(y_ref.dtype)
        if npush:
            @pl.when(i == nt - 1)
            def _():
                _push_finish(copies)

    half = lambda c: pl.BlockSpec((tm, CONV_CH), lambda i: (i, c))
    vec = pl.BlockSpec((1, CONV_CH), lambda i: (0, 0))
    any_spec = pl.BlockSpec(memory_space=pl.ANY)
    outs = pl.pallas_call(
        body, grid=(nt,),
        in_specs=[half(0), half(1), pl.BlockSpec((CONV_HALO, CONV_CH), lambda i: (0, 0)), vec, vec, vec]
        + [any_spec] * npush,
        out_specs=[half(0), half(0)] + [any_spec] * npush,
        out_shape=[jax.ShapeDtypeStruct((t, D), MXU_DTYPE), jax.ShapeDtypeStruct((t, CONV_CH), F32)]
        + _push_out_shapes(push, scatter=False),
        scratch_shapes=[pltpu.VMEM((tm + CONV_HALO, CONV_CH), F32), _shift_scratch(tm)]
        + (_push_sems(npush) if npush else []),
        compiler_params=_cp(("arbitrary",)), name=name)(proj, proj, w32, b, lw, lb, *push)
    return outs[0], outs[1], outs[2:]


def conv_bwd(dy, u1, proj, w32, lw, lb, seq, name, push=()):
    t = proj.shape[0]
    tm = _row_tile(t)
    nt = t // tm
    per = tm // CONV_HALO
    npush = len(push)

    def body(*refs):
        dy_ref, u1_ref, cv_ref, cg_ref, cvp_ref, cgp_ref, w_ref, lw_ref, lb_ref = refs[:9]
        x_refs = refs[9:9 + npush]
        dp_ref, dw_ref, db_ref, dlw_ref, dlb_ref = refs[9 + npush:14 + npush]
        got_refs = refs[14 + npush:14 + 2 * npush]
        ext_d, ext_u, sh_d, sh_u, du0_s, dw_acc = refs[14 + 2 * npush:20 + 2 * npush]
        i = pl.program_id(0)
        tile = nt - 1 - i
        if npush:
            copies = _push_copies(x_refs, got_refs, *refs[20 + 2 * npush:], scatter=True)

            @pl.when(i == 0)
            def _():
                _push_start(copies)

        @pl.when(i == 0)
        def _():
            ext_d[tm:tm + CONV_HALO, :] = jnp.zeros((CONV_HALO, CONV_CH), F32)
            dw_acc[...] = jnp.zeros_like(dw_acc)
            db_ref[...] = jnp.zeros_like(db_ref)
            dlw_ref[...] = jnp.zeros_like(dlw_ref)
            dlb_ref[...] = jnp.zeros_like(dlb_ref)

        @pl.when(i > 0)
        def _():
            ext_d[tm:tm + CONV_HALO, :] = ext_d[0:CONV_HALO, :]

        xh, rstd = _layernorm_parts(u1_ref[...])
        lwv = lw_ref[...]
        ln = xh * lwv + lb_ref[...]
        sg = _sigmoid(ln)
        dln = jnp.where(_valid_rows(tile, tm, seq, CONV_CH), dy_ref[...], 0.0) * (sg * (1.0 + ln * (1.0 - sg)))
        dlw_ref[...] += jnp.sum(dln * xh, axis=0, keepdims=True)
        dlb_ref[...] += jnp.sum(dln, axis=0, keepdims=True)
        dxh = dln * lwv
        du1 = rstd * (dxh - jnp.mean(dxh, axis=-1, keepdims=True)
                      - xh * jnp.mean(dxh * xh, axis=-1, keepdims=True))
        db_ref[...] += jnp.sum(du1, axis=0, keepdims=True)
        ext_d[0:tm, :] = du1

        cv = cv_ref[...]
        sgc = _sigmoid(cg_ref[...])
        prev = cvp_ref[...] * _sigmoid(cgp_ref[...])
        ext_u[0:CONV_HALO, :] = jnp.where(tile > 0, prev, 0.0)
        ext_u[CONV_HALO:CONV_HALO + tm, :] = cv * sgc

        _shift_copies(ext_d, sh_d, tm)
        _shift_copies(ext_u, sh_u, tm)

        def row_block(rb, carry):
            r0 = pl.multiple_of(rb * CONV_ROWS, CONV_ROWS)
            du1_b = ext_d[pl.ds(r0, CONV_ROWS), :]
            acc = jnp.zeros((CONV_ROWS, CONV_CH), F32)
            for j in range(CONV_W):
                acc = acc + w_ref[j:j + 1, :] * _window(ext_d, sh_d, CONV_W - 1 - j, CONV_ROWS, r0)
                prod = du1_b * _window(ext_u, sh_u, CONV_HALO - (CONV_W - 1) + j, CONV_ROWS, r0)
                dw_acc[j] += prod[0:SUBLANES, :] + prod[SUBLANES:CONV_ROWS, :]
            du0_s[pl.ds(r0, CONV_ROWS), :] = acc
            return carry

        lax.fori_loop(0, tm // CONV_ROWS, row_block, 0)

        @pl.when(i == nt - 1)
        def _():
            dw_ref[...] = jnp.sum(dw_acc[...], axis=1)

        du0 = du0_s[...]
        dp_ref[:, 0:CONV_CH] = (du0 * sgc).astype(dp_ref.dtype)
        dp_ref[:, CONV_CH:2 * CONV_CH] = (du0 * cv * sgc * (1.0 - sgc)).astype(dp_ref.dtype)
        if npush:
            @pl.when(i == nt - 1)
            def _():
                _push_finish(copies)

    rev = lambda c: pl.BlockSpec((tm, CONV_CH), lambda i: (nt - 1 - i, c))
    prev = lambda c: pl.BlockSpec((CONV_HALO, CONV_CH), lambda i: (jnp.maximum((nt - 1 - i) * per - 1, 0), c))
    vec = pl.BlockSpec((1, CONV_CH), lambda i: (0, 0))
    wspec = pl.BlockSpec((CONV_HALO, CONV_CH), lambda i: (0, 0))
    any_spec = pl.BlockSpec(memory_space=pl.ANY)
    outs = pl.pallas_call(
        body, grid=(nt,),
        in_specs=[rev(0), rev(0), rev(0), rev(1), prev(0), prev(1), wspec, vec, vec] + [any_spec] * npush,
        out_specs=[pl.BlockSpec((tm, 2 * CONV_CH), lambda i: (nt - 1 - i, 0)), wspec, vec, vec, vec]
        + [any_spec] * npush,
        out_shape=[jax.ShapeDtypeStruct((t, PROJ_MAIN), MXU_DTYPE), jax.ShapeDtypeStruct((CONV_HALO, CONV_CH), F32),
                   jax.ShapeDtypeStruct((1, CONV_CH), F32), jax.ShapeDtypeStruct((1, CONV_CH), F32),
                   jax.ShapeDtypeStruct((1, CONV_CH), F32)] + _push_out_shapes(push, scatter=True),
        scratch_shapes=[pltpu.VMEM((tm + CONV_HALO, CONV_CH), F32), pltpu.VMEM((tm + CONV_HALO, CONV_CH), F32),
                        _shift_scratch(tm), _shift_scratch(tm), pltpu.VMEM((tm, CONV_CH), F32),
                        pltpu.VMEM((CONV_HALO, SUBLANES, CONV_CH), F32)] + (_push_sems(npush) if npush else []),
        compiler_params=_cp(("arbitrary",)), name=name)(dy, u1, proj, proj, proj, proj, w32, lw, lb, *push)
    return outs[:5], outs[5:]


def dn_pre_fwd(proj, w8, name):
    t = proj.shape[0]
    tm = _row_tile(t)

    def body(raw_ref, w_ref, o_ref, ext):
        g = pl.program_id(0)
        i = pl.program_id(1)

        @pl.when(i == 0)
        def _():
            ext[0:DN_HALO, :] = jnp.zeros((DN_HALO, DQ), F32)

        @pl.when(i > 0)
        def _():
            ext[0:DN_HALO, :] = ext[tm:tm + DN_HALO, :]

        ext[DN_HALO:DN_HALO + tm, :] = raw_ref[...]
        c = jnp.zeros((tm, DQ), F32)
        for j in range(DN_W):
            c = c + w_ref[j:j + 1, :] * ext[pl.ds(DN_HALO - (DN_W - 1) + j, tm), :]
        s = c * _sigmoid(c)
        scale = jnp.where(g == 0, DH ** -0.5, 1.0)
        for h in range(NH):
            sh = s[:, h * DH:(h + 1) * DH]
            r = lax.rsqrt(jnp.sum(sh * sh, axis=-1, keepdims=True) + L2_EPS)
            o_ref[:, h * DH:(h + 1) * DH] = jnp.where(g == 2, sh, sh * (r * scale))

    return pl.pallas_call(
        body, grid=(3, t // tm),
        in_specs=[pl.BlockSpec((tm, DQ), lambda g, i: (i, 2 + g)), pl.BlockSpec((DN_HALO, DQ), lambda g, i: (0, g))],
        out_specs=pl.BlockSpec((tm, DQ), lambda g, i: (i, g)),
        out_shape=jax.ShapeDtypeStruct((t, 3 * DQ), F32),
        scratch_shapes=[pltpu.VMEM((tm + DN_HALO, DQ), F32)],
        compiler_params=_cp(("arbitrary", "arbitrary")), name=name)(proj, w8)


def dn_pre_bwd(dproj, dqkv, proj, w8, name):
    t = proj.shape[0]
    tm = _row_tile(t)
    nt = t // tm
    per = tm // DN_HALO

    def body(dp_in, d_ref, raw_ref, rawp_ref, w_ref, dp_ref, dw_ref, ext_d, ext_r):
        del dp_in
        g = pl.program_id(0)
        i = pl.program_id(1)
        tile = nt - 1 - i

        @pl.when(i == 0)
        def _():
            ext_d[tm:tm + DN_HALO, :] = jnp.zeros((DN_HALO, DQ), F32)
            dw_ref[...] = jnp.zeros_like(dw_ref)

        @pl.when(i > 0)
        def _():
            ext_d[tm:tm + DN_HALO, :] = ext_d[0:DN_HALO, :]

        ext_r[0:DN_HALO, :] = jnp.where(tile > 0, rawp_ref[...], 0.0)
        ext_r[DN_HALO:DN_HALO + tm, :] = raw_ref[...]
        taps = [ext_r[pl.ds(DN_HALO - (DN_W - 1) + j, tm), :] for j in range(DN_W)]
        c = jnp.zeros((tm, DQ), F32)
        for j in range(DN_W):
            c = c + w_ref[j:j + 1, :] * taps[j]
        sg = _sigmoid(c)
        s = c * sg
        scale = jnp.where(g == 0, DH ** -0.5, 1.0)
        for h in range(NH):
            sl = slice(h * DH, (h + 1) * DH)
            sh = s[:, sl]
            dn = d_ref[:, sl]
            r = lax.rsqrt(jnp.sum(sh * sh, axis=-1, keepdims=True) + L2_EPS)
            unit = sh * r
            dsn = (r * scale) * (dn - unit * jnp.sum(dn * unit, axis=-1, keepdims=True))
            ds = jnp.where(g == 2, dn, dsn)
            ext_d[0:tm, sl] = ds * (sg[:, sl] * (1.0 + c[:, sl] * (1.0 - sg[:, sl])))
        dc = ext_d[0:tm, :]
        draw = jnp.zeros((tm, DQ), F32)
        for j in range(DN_W):
            draw = draw + w_ref[j:j + 1, :] * ext_d[pl.ds(DN_W - 1 - j, tm), :]
            dw_ref[j:j + 1, :] += jnp.sum(dc * taps[j], axis=0, keepdims=True)
        dp_ref[...] = draw.astype(dp_ref.dtype)

    return pl.pallas_call(
        body, grid=(3, nt),
        in_specs=[pl.BlockSpec(memory_space=pl.ANY),
                  pl.BlockSpec((tm, DQ), lambda g, i: (nt - 1 - i, g)),
                  pl.BlockSpec((tm, DQ), lambda g, i: (nt - 1 - i, 2 + g)),
                  pl.BlockSpec((DN_HALO, DQ), lambda g, i: (jnp.maximum((nt - 1 - i) * per - 1, 0), 2 + g)),
                  pl.BlockSpec((DN_HALO, DQ), lambda g, i: (0, g))],
        out_specs=[pl.BlockSpec((tm, DQ), lambda g, i: (nt - 1 - i, 2 + g)),
                   pl.BlockSpec((DN_HALO, DQ), lambda g, i: (0, g))],
        out_shape=[jax.ShapeDtypeStruct(dproj.shape, dproj.dtype), jax.ShapeDtypeStruct((DN_HALO, 3 * DQ), F32)],
        scratch_shapes=[pltpu.VMEM((tm + DN_HALO, DQ), F32), pltpu.VMEM((tm + DN_HALO, DQ), F32)],
        input_output_aliases={0: 0},
        compiler_params=_cp(("arbitrary", "arbitrary")), name=name)(dproj, dqkv, proj, proj, w8)


def gate_fwd(pg, alog, dtb, seq, name):
    t = pg.shape[0]
    tm = _row_tile(t)

    def body(x_ref, al_ref, dt_ref, o_ref):
        i = pl.program_id(0)
        x = x_ref[...]
        lane = lax.broadcasted_iota(jnp.int32, (tm, GATE_W), 1)
        gg = -jnp.exp(al_ref[...]) * _softplus(x + dt_ref[...])
        out = jnp.where(lane < NH, _sigmoid(x), jnp.where(lane < 2 * NH, gg, 0.0))
        o_ref[...] = jnp.where(_valid_rows(i, tm, seq, GATE_W), out, 0.0)

    row = pl.BlockSpec((tm, GATE_W), lambda i: (i, 0))
    vec = pl.BlockSpec((1, GATE_W), lambda i: (0, 0))
    return pl.pallas_call(
        body, grid=(t // tm,), in_specs=[row, vec, vec], out_specs=row,
        out_shape=jax.ShapeDtypeStruct((t, GATE_W), F32), compiler_params=_cp(("parallel",)), name=name)(pg, alog, dtb)


def gate_bwd(dbg, pg, alog, dtb, seq, name):
    t = pg.shape[0]
    tm = _row_tile(t)

    def body(d_ref, x_ref, al_ref, dt_ref, o_ref, dal_ref, ddt_ref):
        i = pl.program_id(0)
        x = x_ref[...]
        lane = lax.broadcasted_iota(jnp.int32, (tm, GATE_W), 1)
        d = jnp.where(_valid_rows(i, tm, seq, GATE_W), d_ref[...], 0.0)
        beta = _sigmoid(x)
        xs = x + dt_ref[...]
        e = -jnp.exp(al_ref[...])
        is_g = jnp.logical_and(lane >= NH, lane < 2 * NH)
        da = jnp.where(is_g, d * e * _sigmoid(xs), 0.0)
        dgg = jnp.where(is_g, d * e * _softplus(xs), 0.0)
        o_ref[...] = jnp.where(lane < NH, d * beta * (1.0 - beta), da).astype(o_ref.dtype)

        @pl.when(i == 0)
        def _():
            dal_ref[...] = jnp.zeros_like(dal_ref)
            ddt_ref[...] = jnp.zeros_like(ddt_ref)

        dal_ref[...] += jnp.sum(dgg, axis=0, keepdims=True)
        ddt_ref[...] += jnp.sum(da, axis=0, keepdims=True)

    row = pl.BlockSpec((tm, GATE_W), lambda i: (i, 0))
    vec = pl.BlockSpec((1, GATE_W), lambda i: (0, 0))
    return pl.pallas_call(
        body, grid=(t // tm,), in_specs=[row, row, vec, vec], out_specs=[row, vec, vec],
        out_shape=[jax.ShapeDtypeStruct((t, GATE_W), MXU_DTYPE), jax.ShapeDtypeStruct((1, GATE_W), F32),
                   jax.ShapeDtypeStruct((1, GATE_W), F32)],
        compiler_params=_cp(("arbitrary",)), name=name)(dbg, pg, alog, dtb)


def _chunk_masks():
    ii = lax.broadcasted_iota(jnp.int32, (CHUNK, CHUNK), 0)
    jj = lax.broadcasted_iota(jnp.int32, (CHUNK, CHUNK), 1)
    return ii, jj, ii >= jj, ii > jj


def _lane_col(x, lane, idx):
    return jnp.sum(jnp.where(lane == idx, x, 0.0), axis=1, keepdims=True)


def _stack_rows(xs, ys):
    return [jnp.concatenate([x, y], axis=0) for x, y in zip(xs, ys)]


def _side_by_side(xs, ys):
    return [jnp.concatenate([x, y], axis=1) for x, y in zip(xs, ys)]


def _delta_terms(q, k, v, bgs, nch, low, strict):
    idx = [(c, h) for c in range(nch) for h in range(NH)]
    lane = lax.broadcasted_iota(jnp.int32, (CHUNK, GATE_W), 1)
    rowi = lax.broadcasted_iota(jnp.int32, (CHUNK, 1), 0)
    r4 = lax.broadcasted_iota(jnp.int32, (NH * CHUNK, GATE_W), 0)
    l4 = lax.broadcasted_iota(jnp.int32, (NH * CHUNK, GATE_W), 1)
    sel = jnp.where(l4 == NH + jnp.right_shift(r4, CHUNK_LOG2), 1.0, 0.0)
    lowf = jnp.where(low, 1.0, 0.0)
    gam_all = [_mmx(lowf, b, NN) for b in bgs]
    gam_rows = [_mmx(sel, g, NT) for g in gam_all]
    beta = [_lane_col(bgs[c], lane, h) for c, h in idx]
    gam = [_lane_col(gam_all[c], lane, NH + h) for c, h in idx]
    dm = [jnp.exp(jnp.where(low, g - gam_rows[c][h * CHUNK:(h + 1) * CHUNK, :], -1e30))
          for g, (c, h) in zip(gam, idx)]
    glast = [jnp.sum(jnp.where(rowi == CHUNK - 1, g, 0.0), axis=0, keepdims=True) for g in gam]
    eg = [jnp.exp(g) for g in gam]
    ekl = [jnp.exp(gl - g) for gl, g in zip(glast, gam)]
    gl = [jnp.exp(x) for x in glast]
    kb = [x * b for x, b in zip(k, beta)]
    vb = [x * b for x, b in zip(v, beta)]
    kbg = [x * e for x, e in zip(kb, eg)]
    kq = _mm1_many(_stack_rows(kb, q), k, NT)
    a_mat = [jnp.where(strict, x[:CHUNK] * d, 0.0) for x, d in zip(kq, dm)]
    p_mat = [jnp.where(low, x[CHUNK:] * d, 0.0) for x, d in zip(kq, dm)]
    qd = [x * e for x, e in zip(q, eg)]
    kd = [x * e for x, e in zip(k, ekl)]
    return dict(idx=idx, beta=beta, dm=dm, eg=eg, ekl=ekl, gl=gl, kb=kb, vb=vb, kbg=kbg, a=a_mat, p=p_mat, qd=qd,
                kd=kd, lane=lane, rowi=rowi)


def _unit_lower_inverses(a_list, ii, jj, eye):
    def same(log2):
        return jnp.right_shift(ii, log2) == jnp.right_shift(jj, log2)

    n = [-jnp.where(same(INV_BASE_LOG2), a, 0.0) for a in a_list]
    x = [eye + v for v in n]
    p = n
    for _ in range(INV_BASE_LOG2 - 1):
        p = _mm1_many(p, p, NN)
        x = [xi + y for xi, y in zip(x, _mm1_many(x, p, NN))]
    for log2 in range(INV_BASE_LOG2, CHUNK_LOG2):
        off = jnp.logical_and(same(log2 + 1), jnp.logical_not(same(log2)))
        a_off = [jnp.where(off, a, 0.0) for a in a_list]
        x = [xi - y for xi, y in zip(x, _mm1_many(x, _mm1_many(a_off, x, NN), NN))]
    return x


def _transposes(xs, eye):
    e = eye.astype(MXU_DTYPE)
    parts = [_split(x, 2) for x in xs]
    return [_dot(p[0], e, TN) + _dot(p[1], e, TN) for p in parts]


def _load_heads(ref, nch):
    return [ref[c * CHUNK:(c + 1) * CHUNK, h * DH:(h + 1) * DH] for c in range(nch) for h in range(NH)]


def delta_fwd(qkv, bg, name, push=None):
    t = qkv.shape[0]
    nc = t // CHUNK
    nch = _pick(nc, DELTA_CHUNKS)
    rows = nch * CHUNK
    ng = nc // nch
    npush = 0 if push is None else len(push)

    def body(*refs):
        q_ref, k_ref, v_ref, bg_ref = refs[:4]
        x_refs = refs[4:4 + npush]
        o_ref, sh_ref, mi_ref, u_ref, w_ref = refs[4 + npush:9 + npush]
        got_refs = refs[9 + npush:9 + 2 * npush]
        s_ref = refs[9 + 2 * npush]
        n = pl.program_id(0)
        if npush:
            copies = _push_copies(x_refs, got_refs, *refs[10 + 2 * npush:], scatter=False)

            @pl.when(n == 0)
            def _():
                _push_start(copies)

        @pl.when(n == 0)
        def _():
            s_ref[...] = jnp.zeros_like(s_ref)

        ii, jj, low, strict = _chunk_masks()
        eye = jnp.where(ii == jj, 1.0, 0.0)
        q, k, v = _load_heads(q_ref, nch), _load_heads(k_ref, nch), _load_heads(v_ref, nch)
        bgs = [bg_ref[c * CHUNK:(c + 1) * CHUNK, :] for c in range(nch)]
        tm_ = _delta_terms(q, k, v, bgs, nch, low, strict)
        m_inv = _unit_lower_inverses(tm_["a"], ii, jj, eye)
        uw = _mm3_many(m_inv, _side_by_side(tm_["vb"], tm_["kbg"]), NN)
        u = [x[:, :DH] for x in uw]
        w = [x[:, DH:] for x in uw]
        for i, (c, h) in enumerate(tm_["idx"]):
            mi_ref[c, h] = m_inv[i]
            u_ref[c * CHUNK:(c + 1) * CHUNK, h * DH:(h + 1) * DH] = u[i]
            w_ref[c * CHUNK:(c + 1) * CHUNK, h * DH:(h + 1) * DH] = w[i]
        wq = _stack_rows(w, tm_["qd"])
        s = [s_ref[h] for h in range(NH)]
        for c in range(nch):
            pr = range(c * NH, (c + 1) * NH)
            wqs = [_mm1(wq[i], s[i - c * NH], NN) for i in pr]
            ws = [x[:CHUNK] for x in wqs]
            qs = [x[CHUNK:] for x in wqs]
            vn = [u[i] - x for i, x in zip(pr, ws)]
            pv = [_mm1(tm_["p"][i], x, NN) for i, x in zip(pr, vn)]
            kv = [_mm1(tm_["kd"][i], x, TN) for i, x in zip(pr, vn)]
            for h in range(NH):
                o_ref[c * CHUNK:(c + 1) * CHUNK, h * DH:(h + 1) * DH] = qs[h] + pv[h]
                sh_ref[c, h] = s[h]
                s[h] = tm_["gl"][c * NH + h] * s[h] + kv[h]
        for h in range(NH):
            s_ref[h] = s[h]
        if npush:
            @pl.when(n == ng - 1)
            def _():
                _push_finish(copies)

    col = lambda c: pl.BlockSpec((rows, DQ), lambda n: (n, c))
    any_spec = pl.BlockSpec(memory_space=pl.ANY)
    pushed = [] if push is None else list(push)
    outs = pl.pallas_call(
        body, grid=(ng,),
        in_specs=[col(0), col(1), col(2), pl.BlockSpec((rows, GATE_W), lambda n: (n, 0))] + [any_spec] * npush,
        out_specs=[col(0), pl.BlockSpec((nch, NH, DH, DH), lambda n: (n, 0, 0, 0)),
                   pl.BlockSpec((nch, NH, CHUNK, CHUNK), lambda n: (n, 0, 0, 0)), col(0), col(0)]
        + [any_spec] * npush,
        out_shape=[jax.ShapeDtypeStruct((t, DQ), F32), jax.ShapeDtypeStruct((nc, NH, DH, DH), F32),
                   jax.ShapeDtypeStruct((nc, NH, CHUNK, CHUNK), F32), jax.ShapeDtypeStruct((t, DQ), F32),
                   jax.ShapeDtypeStruct((t, DQ), F32)] + _push_out_shapes(pushed, scatter=False),
        scratch_shapes=[pltpu.VMEM((NH, DH, DH), F32)] + (_push_sems(npush) if npush else []),
        compiler_params=_cp(("arbitrary",)), name=name)(qkv, qkv, qkv, bg, *pushed)
    return outs[:5], outs[5:]


def delta_bwd(qkv, bg, do, s_hist, m_hist, u_all, w_all, name, push=None):
    t = qkv.shape[0]
    nc = t // CHUNK
    nch = _pick(nc, DELTA_CHUNKS)
    rows = nch * CHUNK
    ng = nc // nch
    npush = 0 if push is None else len(push)

    def body(*refs):
        q_ref, k_ref, v_ref, bg_ref, do_ref, sh_ref, mi_ref, u_ref, w_ref = refs[:9]
        x_refs = refs[9:9 + npush]
        dqkv_ref, dbg_ref = refs[9 + npush:11 + npush]
        got_refs = refs[11 + npush:11 + 2 * npush]
        ds_ref = refs[11 + 2 * npush]
        n = pl.program_id(0)
        if npush:
            copies = _push_copies(x_refs, got_refs, *refs[12 + 2 * npush:], scatter=True)

            @pl.when(n == 0)
            def _():
                _push_start(copies)

        @pl.when(n == 0)
        def _():
            ds_ref[...] = jnp.zeros_like(ds_ref)

        ii, jj, low, strict = _chunk_masks()
        eye = jnp.where(ii == jj, 1.0, 0.0)
        q, k, v = _load_heads(q_ref, nch), _load_heads(k_ref, nch), _load_heads(v_ref, nch)
        d_o = _load_heads(do_ref, nch)
        bgs = [bg_ref[c * CHUNK:(c + 1) * CHUNK, :] for c in range(nch)]
        tm_ = _delta_terms(q, k, v, bgs, nch, low, strict)
        idx, lane, rowi = tm_["idx"], tm_["lane"], tm_["rowi"]
        beta, dm, eg, ekl, gl = tm_["beta"], tm_["dm"], tm_["eg"], tm_["ekl"], tm_["gl"]
        kb, kbg, qd, kd, a_mat, p_mat = tm_["kb"], tm_["kbg"], tm_["qd"], tm_["kd"], tm_["a"], tm_["p"]
        s = [sh_ref[c, h] for c, h in idx]
        m_inv = [mi_ref[c, h] for c, h in idx]
        u, w = _load_heads(u_ref, nch), _load_heads(w_ref, nch)
        ws = _mm1_many(w, s, NN)
        vn = [x - y for x, y in zip(u, ws)]
        qp_do = _mm1_many(_side_by_side(qd, p_mat), d_o, TN)
        qdo = [x[:DH] for x in qp_do]
        pdo = [x[DH:] for x in qp_do]
        dqd = _mm1_many(d_o, s, NT)
        dp = [jnp.where(low, x, 0.0) for x in _mm1_many(d_o, vn, NT)]

        nprob = len(idx)
        dvn, dkd, dgl = [None] * nprob, [None] * nprob, [None] * nprob
        ds = [ds_ref[h] for h in range(NH)]
        for c in reversed(range(nch)):
            pr = list(range(c * NH, (c + 1) * NH))
            kds = [_mm1(kd[i], ds[i - c * NH], NN) for i in pr]
            for i, x in zip(pr, kds):
                dvn[i] = pdo[i] + x
            wdv = [_mm1(w[i], dvn[i], TN) for i in pr]
            for i in pr:
                h = i - c * NH
                dkd[i] = _mm1(vn[i], ds[h], NT)
                dgl[i] = jnp.sum(jnp.sum(s[i] * ds[h], axis=1, keepdims=True), axis=0, keepdims=True)
                ds[h] = qdo[i] + gl[i] * ds[h] - wdv[h]
        for h in range(NH):
            ds_ref[h] = ds[h]

        dw = [-x for x in _mm1_many(dvn, s, NT)]
        dvb_dkbg = _mm3_many(m_inv, _side_by_side(dvn, dw), TN)
        dvb = [x[:, :DH] for x in dvb_dkbg]
        dkbg = [x[:, DH:] for x in dvb_dkbg]
        da = [-jnp.where(strict, x, 0.0)
              for x in _mm1_many(dvb_dkbg, _side_by_side(u, w), NT)]
        gm = [x * d for x, d in zip(da, dm)]
        hm = [x * d for x, d in zip(dp, dm)]
        gh = _stack_rows(gm, hm)
        gh_k = _mm1_many(gh, k, NN)
        gk = [x[:CHUNK] for x in gh_k]
        hk = [x[CHUNK:] for x in gh_k]
        gkb_hq = _mm1_many(gh, _stack_rows(kb, q), TN)
        em = [x * a + y * p for x, a, y, p in zip(da, a_mat, dp, p_mat)]
        em_t = _transposes(em, eye)
        dbeta_all = [jnp.zeros((CHUNK, GATE_W), F32) for _ in range(nch)]
        dgam_all = [jnp.zeros((CHUNK, GATE_W), F32) for _ in range(nch)]
        for i, (c, h) in enumerate(idx):
            dkb = gk[i] + dkbg[i] * eg[i]
            dk = gkb_hq[i] + dkd[i] * ekl[i] + beta[i] * dkb
            dq = hk[i] + dqd[i] * eg[i]
            dkd_kd = jnp.sum(dkd[i] * kd[i], axis=1, keepdims=True)
            dgam = (jnp.sum(em[i], axis=1, keepdims=True) - jnp.sum(em_t[i], axis=1, keepdims=True)
                    + jnp.sum(dqd[i] * qd[i], axis=1, keepdims=True) - dkd_kd
                    + jnp.sum(dkbg[i] * kbg[i], axis=1, keepdims=True))
            tail = jnp.sum(dkd_kd, axis=0, keepdims=True) + dgl[i] * gl[i]
            dgam = dgam + jnp.where(rowi == CHUNK - 1, tail, 0.0)
            dbeta = jnp.sum(dkb * k[i], axis=1, keepdims=True) + jnp.sum(dvb[i] * v[i], axis=1, keepdims=True)
            rs = slice(c * CHUNK, (c + 1) * CHUNK)
            dqkv_ref[rs, h * DH:(h + 1) * DH] = dq
            dqkv_ref[rs, DQ + h * DH:DQ + (h + 1) * DH] = dk
            dqkv_ref[rs, 2 * DQ + h * DH:2 * DQ + (h + 1) * DH] = beta[i] * dvb[i]
            dbeta_all[c] = dbeta_all[c] + jnp.where(lane == h, dbeta, 0.0)
            dgam_all[c] = dgam_all[c] + jnp.where(lane == NH + h, dgam, 0.0)
        upf = jnp.where(ii <= jj, 1.0, 0.0)
        for c in range(nch):
            dg_all = _mmx(upf, dgam_all[c], NN)
            dbg_ref[c * CHUNK:(c + 1) * CHUNK, :] = jnp.where(lane < NH, dbeta_all[c], dg_all)
        if npush:
            @pl.when(n == ng - 1)
            def _():
                _push_finish(copies)

    col = lambda c: pl.BlockSpec((rows, DQ), lambda n: (ng - 1 - n, c))
    gate = pl.BlockSpec((rows, GATE_W), lambda n: (ng - 1 - n, 0))
    any_spec = pl.BlockSpec(memory_space=pl.ANY)
    pushed = [] if push is None else list(push)
    outs = pl.pallas_call(
        body, grid=(ng,),
        in_specs=[col(0), col(1), col(2), gate, col(0),
                  pl.BlockSpec((nch, NH, DH, DH), lambda n: (ng - 1 - n, 0, 0, 0)),
                  pl.BlockSpec((nch, NH, CHUNK, CHUNK), lambda n: (ng - 1 - n, 0, 0, 0)), col(0), col(0)]
        + [any_spec] * npush,
        out_specs=[pl.BlockSpec((rows, 3 * DQ), lambda n: (ng - 1 - n, 0)), gate] + [any_spec] * npush,
        out_shape=[jax.ShapeDtypeStruct((t, 3 * DQ), F32), jax.ShapeDtypeStruct((t, GATE_W), F32)]
        + _push_out_shapes(pushed, scatter=True),
        scratch_shapes=[pltpu.VMEM((NH, DH, DH), F32)] + (_push_sems(npush) if npush else []),
        compiler_params=_cp(("arbitrary",)), name=name)(qkv, qkv, qkv, bg, do, s_hist, m_hist, u_all, w_all, *pushed)
    return outs[0], outs[1], outs[2:]


def dn_post_fwd(ybuf, o, proj, nw, name):
    t = o.shape[0]
    tm = _row_tile(t)

    def body(y_in, o_ref, z_ref, nw_ref, y_ref):
        del y_in
        nwv = nw_ref[...]
        for h in range(NH):
            sl = slice(h * DH, (h + 1) * DH)
            oh = o_ref[:, sl]
            z = z_ref[:, sl]
            r = lax.rsqrt(jnp.mean(oh * oh, axis=-1, keepdims=True) + NORM_EPS)
            y_ref[:, sl] = (oh * r * nwv * (z * _sigmoid(z))).astype(y_ref.dtype)

    return pl.pallas_call(
        body, grid=(t // tm,),
        in_specs=[pl.BlockSpec(memory_space=pl.ANY), pl.BlockSpec((tm, DQ), lambda i: (i, 0)),
                  pl.BlockSpec((tm, DQ), lambda i: (i, 5)), pl.BlockSpec((1, DH), lambda i: (0, 0))],
        out_specs=pl.BlockSpec((tm, DQ), lambda i: (i, 1)),
        out_shape=jax.ShapeDtypeStruct(ybuf.shape, ybuf.dtype), input_output_aliases={0: 0},
        compiler_params=_cp(("parallel",)), name=name)(ybuf, o, proj, nw)


def dn_post_bwd(dproj, dy, o, proj, nw, name):
    t = o.shape[0]
    tm = _row_tile(t)

    def body(dp_in, dy_ref, o_ref, z_ref, nw_ref, do_ref, dp_ref, dnw_ref):
        del dp_in
        i = pl.program_id(0)
        nwv = nw_ref[...]
        acc = jnp.zeros((1, DH), F32)
        for h in range(NH):
            sl = slice(h * DH, (h + 1) * DH)
            oh = o_ref[:, sl]
            z = z_ref[:, sl]
            dyh = dy_ref[:, sl]
            r = lax.rsqrt(jnp.mean(oh * oh, axis=-1, keepdims=True) + NORM_EPS)
            xh = oh * r
            sg = _sigmoid(z)
            sz = z * sg
            dxh = dyh * nwv * sz
            do_ref[:, sl] = r * (dxh - xh * jnp.mean(dxh * xh, axis=-1, keepdims=True))
            dp_ref[:, sl] = (dyh * xh * nwv * (sg * (1.0 + z * (1.0 - sg)))).astype(dp_ref.dtype)
            acc = acc + jnp.sum(dyh * xh * sz, axis=0, keepdims=True)

        @pl.when(i == 0)
        def _():
            dnw_ref[...] = jnp.zeros_like(dnw_ref)

        dnw_ref[...] += acc

    vec = pl.BlockSpec((1, DH), lambda i: (0, 0))
    return pl.pallas_call(
        body, grid=(t // tm,),
        in_specs=[pl.BlockSpec(memory_space=pl.ANY), pl.BlockSpec((tm, DQ), lambda i: (i, 1)),
                  pl.BlockSpec((tm, DQ), lambda i: (i, 0)), pl.BlockSpec((tm, DQ), lambda i: (i, 5)), vec],
        out_specs=[pl.BlockSpec((tm, DQ), lambda i: (i, 0)), pl.BlockSpec((tm, DQ), lambda i: (i, 5)), vec],
        out_shape=[jax.ShapeDtypeStruct((t, DQ), F32), jax.ShapeDtypeStruct(dproj.shape, dproj.dtype),
                   jax.ShapeDtypeStruct((1, DH), F32)],
        input_output_aliases={0: 1}, compiler_params=_cp(("arbitrary",)), name=name)(dproj, dy, o, proj, nw)


def _shifted(first, second, s, lane):
    if s == 0:
        return first
    return jnp.where(lane < LANES - s, pltpu.roll(first, LANES - s, 1), pltpu.roll(second, LANES - s, 1))


def unshard_cols(g8, w, widths, name):
    _, r, wp = g8.shape
    rb = _pick(r, (256, 128, 64, 32, 16))

    def body(g_ref, *o_refs):
        lane = lax.broadcasted_iota(jnp.int32, (rb, LANES), 1)
        zeros = jnp.zeros((rb, LANES), F32)

        def src(j, ta):
            if j >= NDEV or ta * LANES >= wp:
                return zeros
            return g_ref[j, :, ta * LANES:(ta + 1) * LANES].astype(F32)

        base = 0
        for o_ref, width in zip(o_refs, widths):
            for b in range(width // LANES):
                c0 = base + b * LANES
                if c0 >= NDEV * w:
                    tile = zeros
                else:
                    j0, o0 = divmod(c0, w)
                    n0 = min(w - o0, LANES)
                    ta, s = divmod(o0, LANES)
                    tile = _shifted(src(j0, ta), src(j0, ta + 1), s, lane)
                    if n0 < LANES:
                        nxt = pltpu.roll(src(j0 + 1, 0), n0, 1) if j0 + 1 < NDEV else zeros
                        tile = jnp.where(lane < n0, tile, nxt)
                o_ref[:, b * LANES:(b + 1) * LANES] = tile.astype(o_ref.dtype)
            base += width

    return pl.pallas_call(
        body, grid=(r // rb,), in_specs=[pl.BlockSpec((NDEV, rb, wp), lambda i: (0, i, 0))],
        out_specs=[pl.BlockSpec((rb, width), lambda i: (i, 0)) for width in widths],
        out_shape=[jax.ShapeDtypeStruct((r, width), g8.dtype) for width in widths],
        compiler_params=_cp(("parallel",)), name=name)(g8)


def shard_cols(parts, w, name):
    r = parts[0].shape[0]
    wp = _lane_pad(w)
    rb = _pick(r, (256, 128, 64, 32, 16))
    tiles_of = [p.shape[1] // LANES for p in parts]

    def body(*refs):
        p_refs, o_ref = refs[:-1], refs[-1]
        lane = lax.broadcasted_iota(jnp.int32, (rb, LANES), 1)
        zeros = jnp.zeros((rb, LANES), F32)

        def glob(tile_idx):
            for p_ref, n_tiles in zip(p_refs, tiles_of):
                if tile_idx < n_tiles:
                    return p_ref[:, tile_idx * LANES:(tile_idx + 1) * LANES].astype(F32)
                tile_idx -= n_tiles
            return zeros

        for j in range(NDEV):
            for a in range(wp // LANES):
                nv = min(w - a * LANES, LANES)
                tb, s = divmod(w * j + a * LANES, LANES)
                tile = _shifted(glob(tb), glob(tb + 1), s, lane)
                if nv < LANES:
                    tile = jnp.where(lane < nv, tile, 0.0)
                o_ref[j, :, a * LANES:(a + 1) * LANES] = tile.astype(o_ref.dtype)

    return pl.pallas_call(
        body, grid=(r // rb,), in_specs=[pl.BlockSpec((rb, p.shape[1]), lambda i: (i, 0)) for p in parts],
        out_specs=pl.BlockSpec((NDEV, rb, wp), lambda i: (0, i, 0)),
        out_shape=jax.ShapeDtypeStruct((NDEV, r, wp), GRAD_DTYPE),
        compiler_params=_cp(("parallel",)), name=name)(*parts)


def _me_and_peers():
    mx, my, mc = lax.axis_index("x"), lax.axis_index("y"), lax.axis_index("c")
    me = 4 * mx + 2 * my + mc
    peers = []
    for kk in range(1, NDEV):
        px = 1 - mx if kk & 4 else mx
        py = 1 - my if kk & 2 else my
        pc = 1 - mc if kk & 1 else mc
        peers.append(((px, py, pc), 4 * px + 2 * py + pc))
    return me, peers


def _push_copies(x_refs, o_refs, send_sems, recv_sems, local_sems, scatter):
    me, peers = _me_and_peers()
    npeer = NDEV - 1
    local, sends, recvs = [], [], []
    for a, (x_ref, o_ref) in enumerate(zip(x_refs, o_refs)):
        local.append(pltpu.make_async_copy(x_ref.at[me] if scatter else x_ref, o_ref.at[me], local_sems.at[a]))
        for kk, (peer, pidx) in enumerate(peers):
            src = x_ref.at[pidx] if scatter else x_ref
            sems = dict(send_sem=send_sems.at[a * npeer + kk], recv_sem=recv_sems.at[a * npeer + kk],
                        device_id=peer, device_id_type=pl.DeviceIdType.MESH)
            sends.append(pltpu.make_async_remote_copy(src_ref=src, dst_ref=o_ref.at[me], **sems))
            recvs.append(pltpu.make_async_remote_copy(src_ref=src, dst_ref=o_ref.at[pidx], **sems))
    return local, sends, recvs


def _push_start(copies):
    local, sends, _ = copies
    for cp in local + sends:
        cp.start()


def _push_finish(copies):
    local, sends, recvs = copies
    for cp in recvs:
        cp.wait_recv()
    for cp in sends:
        cp.wait_send()
    for cp in local:
        cp.wait()


def _push_out_shapes(xs, scatter):
    return [jax.ShapeDtypeStruct(x.shape if scatter else (NDEV,) + x.shape, x.dtype) for x in xs]


def _push_sems(n):
    return [pltpu.SemaphoreType.DMA((n * (NDEV - 1),)), pltpu.SemaphoreType.DMA((n * (NDEV - 1),)),
            pltpu.SemaphoreType.DMA((n,))]


def _push_to_all(xs, name, scatter):
    n = len(xs)

    def body(*refs):
        copies = _push_copies(refs[:n], refs[n:2 * n], *refs[2 * n:], scatter=scatter)
        _push_start(copies)
        _push_finish(copies)

    any_spec = pl.BlockSpec(memory_space=pl.ANY)
    return pl.pallas_call(
        body, in_specs=[any_spec] * n, out_specs=[any_spec] * n, out_shape=_push_out_shapes(xs, scatter),
        scratch_shapes=_push_sems(n), name=name)(*xs)


def adamw(recv, w, m, v, name):
    rows, cols = w.shape
    c1 = 1.0 - ADAM_B1 ** ADAM_STEP
    c2 = 1.0 - ADAM_B2 ** ADAM_STEP
    cap = ADAM_BLOCK_BYTES // (NDEV * cols * 4)
    rb = _pick(rows, [p for p in (2048, 1024, 512, 256, 128, 64, 32, 16, 8) if p <= cap])

    def body(r_ref, w_ref, m_ref, v_ref, g_ref, d_ref, m2_ref, v2_ref):
        g = r_ref[0].astype(F32)
        for j in range(1, NDEV):
            g = g + r_ref[j].astype(F32)
        m2 = ADAM_B1 * m_ref[...] + (1.0 - ADAM_B1) * g
        v2 = ADAM_B2 * v_ref[...] + (1.0 - ADAM_B2) * (g * g)
        g_ref[...] = g
        m2_ref[...] = m2
        v2_ref[...] = v2
        d_ref[...] = -ADAM_LR * ((m2 / c1) / (jnp.sqrt(v2 / c2) + ADAM_EPS) + ADAM_WD * w_ref[...])

    blk = pl.BlockSpec((rb, cols), lambda i: (i, 0))
    return pl.pallas_call(
        body, grid=(rows // rb,),
        in_specs=[pl.BlockSpec((NDEV, rb, cols), lambda i: (0, i, 0)), blk, blk, blk],
        out_specs=[blk, blk, blk, blk], out_shape=[jax.ShapeDtypeStruct((rows, cols), F32)] * 4,
        compiler_params=_cp(("parallel",)), name=name)(recv, w, m, v)


SMALL_SHARDED = ("meta_tokens", "conv_dw_w", "dn_conv_w")
SMALL_REPLICATED = ("norm_mix_w", "conv_dw_b", "conv_ln_w", "conv_ln_b", "dn_A_log", "dn_dt_bias", "dn_norm_w",
                    "norm_ffn_w", "final_norm_w")
PARAM_ORDER = ("meta_tokens", "norm_mix_w", "w_in", "conv_dw_w", "conv_dw_b", "conv_ln_w", "conv_ln_b", "dn_conv_w",
               "dn_A_log", "dn_dt_bias", "dn_norm_w", "w_out", "norm_ffn_w", "ffn_w_gu", "ffn_w_down", "final_norm_w")


def _pack_small(parts, axis):
    flat = jnp.concatenate(parts, axis=axis)
    n = flat.shape[axis]
    total = -(-n // (8 * LANES)) * (8 * LANES)
    pad = [(0, 0)] * flat.ndim
    pad[axis] = (0, total - n)
    flat = jnp.pad(flat, pad)
    return flat.reshape(flat.shape[:axis] + (total // LANES, LANES))


def _unshard_last(g8):
    moved = jnp.moveaxis(g8, 0, -2)
    return moved.reshape(moved.shape[:-2] + (-1,))


def _per_destination_last(full):
    split = full.reshape(full.shape[:-1] + (NDEV, full.shape[-1] // NDEV))
    return jnp.moveaxis(split, -2, 0).reshape(NDEV, -1)


def _lane_row(vec4, width):
    return jnp.pad(vec4, (NH, width - 2 * NH))[None]


def kernel(x, meta_tokens, norm_mix_w, w_in, conv_dw_w, conv_dw_b, conv_ln_w, conv_ln_b, dn_conv_w, dn_A_log, dn_dt_bias, dn_norm_w, w_out, norm_ffn_w, ffn_w_gu, ffn_w_down, final_norm_w, loss_target, m_meta_tokens, m_norm_mix_w, m_w_in, m_conv_dw_w, m_conv_dw_b, m_conv_ln_w, m_conv_ln_b, m_dn_conv_w, m_dn_A_log, m_dn_dt_bias, m_dn_norm_w, m_w_out, m_norm_ffn_w, m_ffn_w_gu, m_ffn_w_down, m_final_norm_w, v_meta_tokens, v_norm_mix_w, v_w_in, v_conv_dw_w, v_conv_dw_b, v_conv_ln_w, v_conv_ln_b, v_dn_conv_w, v_dn_A_log, v_dn_dt_bias, v_dn_norm_w, v_w_out, v_norm_ffn_w, v_ffn_w_gu, v_ffn_w_down, v_final_norm_w):
    weights = dict(meta_tokens=meta_tokens, norm_mix_w=norm_mix_w, w_in=w_in, conv_dw_w=conv_dw_w, conv_dw_b=conv_dw_b,
                   conv_ln_w=conv_ln_w, conv_ln_b=conv_ln_b, dn_conv_w=dn_conv_w, dn_A_log=dn_A_log,
                   dn_dt_bias=dn_dt_bias, dn_norm_w=dn_norm_w, w_out=w_out, norm_ffn_w=norm_ffn_w, ffn_w_gu=ffn_w_gu,
                   ffn_w_down=ffn_w_down, final_norm_w=final_norm_w)
    m_in = dict(meta_tokens=m_meta_tokens, norm_mix_w=m_norm_mix_w, w_in=m_w_in, conv_dw_w=m_conv_dw_w,
                conv_dw_b=m_conv_dw_b, conv_ln_w=m_conv_ln_w, conv_ln_b=m_conv_ln_b, dn_conv_w=m_dn_conv_w,
                dn_A_log=m_dn_A_log, dn_dt_bias=m_dn_dt_bias, dn_norm_w=m_dn_norm_w, w_out=m_w_out,
                norm_ffn_w=m_norm_ffn_w, ffn_w_gu=m_ffn_w_gu, ffn_w_down=m_ffn_w_down, final_norm_w=m_final_norm_w)
    v_in = dict(meta_tokens=v_meta_tokens, norm_mix_w=v_norm_mix_w, w_in=v_w_in, conv_dw_w=v_conv_dw_w,
                conv_dw_b=v_conv_dw_b, conv_ln_w=v_conv_ln_w, conv_ln_b=v_conv_ln_b, dn_conv_w=v_dn_conv_w,
                dn_A_log=v_dn_A_log, dn_dt_bias=v_dn_dt_bias, dn_norm_w=v_dn_norm_w, w_out=v_w_out,
                norm_ffn_w=v_norm_ffn_w, ffn_w_gu=v_ffn_w_gu, ffn_w_down=v_ffn_w_down, final_norm_w=v_final_norm_w)

    depth = w_in.shape[0]
    seq = x.shape[1]
    t = _padded_rows(seq)
    rows_d = depth * D
    win_w, gu_w = w_in.shape[2], ffn_w_gu.shape[2]
    win_wp, gu_wp = _lane_pad(win_w), _lane_pad(gu_w)

    def pad_cols(a, wp):
        return jnp.pad(a, ((0, 0), (0, 0), (0, wp - a.shape[2]))).reshape(rows_d, wp)

    def rows2d(a):
        return a.reshape(-1, a.shape[2])

    small_shards = [weights[n] for n in SMALL_SHARDED]
    win_p = pad_cols(w_in, win_wp).astype(MXU_DTYPE).reshape(depth, D, win_wp)
    gu_p = pad_cols(ffn_w_gu, gu_wp).astype(MXU_DTYPE).reshape(depth, D, gu_wp)
    wout_b, wdown_b = w_out.astype(MXU_DTYPE), ffn_w_down.astype(MXU_DTYPE)

    got = [dict() for _ in range(depth)]
    got[0]["win"], g_small = _push_to_all([win_p[0], _pack_small([s.reshape(-1) for s in small_shards], 0)],
                                          "gather_first", scatter=False)
    wts = []
    small_flat, off, small_full = g_small.reshape(NDEV, -1), 0, {}
    for n, s in zip(SMALL_SHARDED, small_shards):
        small_full[n] = _unshard_last(small_flat[:, off:off + s.size].reshape((NDEV,) + s.shape))
        off += s.size
    cdw32 = jnp.pad(small_full["conv_dw_w"], ((0, 0), (0, CONV_HALO - CONV_W), (0, 0)))
    dcw8 = jnp.pad(small_full["dn_conv_w"], ((0, 0), (0, DN_HALO - DN_W), (0, 0)))

    h = jnp.concatenate([jnp.zeros((FRONT, D), F32), small_full["meta_tokens"], x[0],
                         jnp.zeros((t - HEAD - seq, D), F32)], axis=0)
    tgt = jnp.pad(loss_target[0], ((HEAD, t - HEAD - seq), (0, 0)))

    saved = []
    for l in range(depth):
        nmw, nfw = norm_mix_w[l][None], norm_ffn_w[l][None]
        cdb, clw, clb = conv_dw_b[l][None], conv_ln_w[l][None], conv_ln_b[l][None]
        alog, dtb, dnw = _lane_row(dn_A_log[l], GATE_W), _lane_row(dn_dt_bias[l], GATE_W), dn_norm_w[l][None]
        w_main, w_gate_cols = unshard_cols(got[l]["win"], win_w, [PROJ_MAIN, GATE_W], "unshard_w_in")
        proj = mm([h], w_main, a_fn=_rms_apply, a_vecs=[nmw], name="mm_proj")
        pg = mm([h], w_gate_cols, a_fn=_rms_apply, a_vecs=[nmw], name="mm_proj_gate")
        own = [gu_p[0], wout_b[0], wdown_b[0]] if l == 0 else []
        ahead = [wout_b[l + 1], wdown_b[l + 1]] if l + 1 < depth else []
        ybuf, u1, came = conv_fwd(proj, cdw32[l], cdb, clw, clb, seq, "conv_fwd_gather" if own + ahead else "conv_fwd",
                                  push=own + ahead)
        if own:
            got[0]["gu"], got[0]["out"], got[0]["down"] = came[:3]
        if ahead:
            got[l + 1]["out"], got[l + 1]["down"] = came[len(own):]
        qkv = dn_pre_fwd(proj, dcw8[l], "dn_pre_fwd")
        bg = gate_fwd(pg, alog, dtb, seq, "gate_fwd")
        if l + 1 < depth:
            (o, s_hist, m_hist, u_all, w_all), (got[l + 1]["win"], got[l + 1]["gu"]) = delta_fwd(
                qkv, bg, "delta_fwd_gather", push=[win_p[l + 1], gu_p[l + 1]])
        else:
            (o, s_hist, m_hist, u_all, w_all), _ = delta_fwd(qkv, bg, "delta_fwd")
        w_g, w_u = unshard_cols(got[l]["gu"], gu_w, [DFF, DFF], "unshard_w_gu")
        wl = dict(main=w_main, gate=w_gate_cols, wg=w_g, wu=w_u, out=got[l]["out"].reshape(D, D),
                  down=got[l]["down"].reshape(DFF, D))
        wts.append(wl)
        ybuf = dn_post_fwd(ybuf, o, proj, dnw, "dn_post_fwd")
        h_mid = mm(ybuf, wl["out"], extras=[h], out_fn=_add, name="mm_out")
        gate = mm([h_mid], wl["wg"], a_fn=_rms_apply, a_vecs=[nfw], name="mm_gate")
        up = mm([h_mid], wl["wu"], a_fn=_rms_apply, a_vecs=[nfw], name="mm_up")
        h_out = mm([gate, up], wl["down"], a_fn=_swiglu, extras=[h_mid], out_fn=_add, tile_cap=ROW_TILE, name="mm_down")
        saved.append(dict(h=h, proj=proj, pg=pg, ybuf=ybuf, u1=u1, qkv=qkv, bg=bg, o=o, s_hist=s_hist,
                          m_hist=m_hist, u_all=u_all, w_all=w_all, h_mid=h_mid, gate=gate, up=up,
                          nmw=nmw, nfw=nfw, clw=clw, clb=clb, alog=alog, dtb=dtb, dnw=dnw))
        h = h_out

    dh, loss_part, d_final = loss_bwd(h, tgt, final_norm_w[None], seq, "loss_bwd")
    loss = lax.psum(loss_part[0, 0], MESH_AXES)

    per_layer = ("norm_mix_w", "conv_dw_w", "conv_dw_b", "conv_ln_w", "conv_ln_b", "dn_conv_w", "dn_A_log", "dn_dt_bias",
                 "dn_norm_w", "norm_ffn_w")
    grads = {n: [None] * depth for n in per_layer}
    received = [None] * depth
    pending = None
    dw_mm = lambda a, b, name, **kw: mm(a, b, ta=True, out_dtypes=(GRAD_DTYPE,), name=name, **kw)
    for l in reversed(range(depth)):
        s, wl = saved[l], wts[l]
        dgate, dup = mm(dh, wl["down"], tb=True, extras=[s["gate"], s["up"]], out_fn=_swiglu_bwd,
                        out_dtypes=(MXU_DTYPE, MXU_DTYPE), tile_cap=ROW_TILE, name="mm_down_dx")
        d_down = dw_mm([s["gate"], s["up"]], dh, "mm_down_dw", a_fn=_swiglu, tile_cap=ROW_TILE)
        dhn2 = mm(dup, wl["wu"], tb=True, plus=[(dgate, wl["wg"])], name="mm_gate_up_dx")
        dh_mid, dnfw = rms_bwd(dhn2, s["h_mid"], s["nfw"], dh, "rms_ffn_bwd")
        d_wg = dw_mm([s["h_mid"]], dgate, "mm_gate_dw", a_fn=_rms_apply, a_vecs=[s["nfw"]])
        d_wu = dw_mm([s["h_mid"]], dup, "mm_up_dw", a_fn=_rms_apply, a_vecs=[s["nfw"]])
        dy = mm(dh_mid, wl["out"], tb=True, name="mm_out_dx")
        d_out = dw_mm(s["ybuf"], dh_mid, "mm_out_dw")
        early = [shard_cols([d_wg, d_wu], gu_w, "shard_w_gu"), d_out.reshape(NDEV, D // NDEV, D),
                 d_down.reshape(NDEV, DFF // NDEV, D)]
        (dproj, dcdw, dcdb, dclw, dclb), early_came = conv_bwd(
            dy, s["u1"], s["proj"], cdw32[l], s["clw"], s["clb"], seq, "conv_bwd" if l else "conv_bwd_exchange",
            push=() if l else early)
        do, dproj, ddnw = dn_post_bwd(dproj, dy, s["o"], s["proj"], s["dnw"], "dn_post_bwd")
        delta_args = (s["qkv"], s["bg"], do, s["s_hist"], s["m_hist"], s["u_all"], s["w_all"])
        if pending is None:
            dqkv, dbg, _ = delta_bwd(*delta_args, "delta_bwd")
        else:
            dqkv, dbg, received[l + 1] = delta_bwd(*delta_args, "delta_bwd_exchange", push=pending)
        dproj, ddcw = dn_pre_bwd(dproj, dqkv, s["proj"], dcw8[l], "dn_pre_bwd")
        dpg, dalog, ddtb = gate_bwd(dbg, s["pg"], s["alog"], s["dtb"], seq, "gate_bwd")
        dhn_gate = mm(dpg, wl["gate"], tb=True, name="mm_proj_gate_dx")
        dhn = mm(dproj, wl["main"], tb=True, extras=[dhn_gate], out_fn=_add, name="mm_proj_dx")
        d_main = dw_mm([s["h"]], dproj, "mm_proj_dw", a_fn=_rms_apply, a_vecs=[s["nmw"]])
        d_gate_cols = dw_mm([s["h"]], dpg, "mm_proj_gate_dw", a_fn=_rms_apply, a_vecs=[s["nmw"]])
        pending = [shard_cols([d_main, d_gate_cols], win_w, "shard_w_in")] + (early if l else [])
        dh, dnmw = rms_bwd(dhn, s["h"], s["nmw"], dh_mid, "rms_mix_bwd")
        grads["norm_mix_w"][l] = dnmw[0]
        grads["norm_ffn_w"][l] = dnfw[0]
        grads["conv_dw_w"][l] = dcdw[:CONV_W]
        grads["conv_dw_b"][l] = dcdb[0]
        grads["conv_ln_w"][l] = dclw[0]
        grads["conv_ln_b"][l] = dclb[0]
        grads["dn_conv_w"][l] = ddcw[:DN_W]
        grads["dn_A_log"][l] = dalog[0, NH:2 * NH]
        grads["dn_dt_bias"][l] = ddtb[0, NH:2 * NH]
        grads["dn_norm_w"][l] = ddnw[0]

    grad_x = dh[HEAD:HEAD + seq][None]
    full = {n: jnp.stack(g) for n, g in grads.items()}
    full["meta_tokens"] = dh[FRONT:HEAD]
    full["final_norm_w"] = d_final[0]

    send_small = _pack_small(
        [_per_destination_last(full[n]) for n in SMALL_SHARDED]
        + [jnp.broadcast_to(full[n].reshape(1, -1), (NDEV, full[n].size)) for n in SMALL_REPLICATED], 1)
    r_win0, r_small = _push_to_all(pending + [send_small], "exchange_last", scatter=True)
    received[0] = [r_win0, *early_came]
    r_win, r_gu, r_wout, r_down = (jnp.concatenate([received[l][k] for l in range(depth)], axis=1) for k in range(4))

    small_names = SMALL_SHARDED + SMALL_REPLICATED
    pack_local = lambda tree: _pack_small([tree[n].reshape(-1) for n in small_names], 0)
    results = {}

    def run_adamw(name, recv, prep, finish):
        outs = adamw(recv, prep(weights[name]), prep(m_in[name]), prep(v_in[name]), "adamw_" + name)
        results[name] = [finish(o) for o in outs]

    run_adamw("w_in", r_win, lambda a: pad_cols(a, win_wp),
              lambda o: o[:, :win_w].reshape(depth, D, win_w))
    run_adamw("ffn_w_gu", r_gu, lambda a: pad_cols(a, gu_wp), lambda o: o[:, :gu_w].reshape(depth, D, gu_w))
    run_adamw("w_out", r_wout, rows2d, lambda o: o.reshape(w_out.shape))
    run_adamw("ffn_w_down", r_down, rows2d, lambda o: o.reshape(ffn_w_down.shape))
    small_outs = adamw(r_small, pack_local(weights), pack_local(m_in), pack_local(v_in), "adamw_small")
    for kind in range(4):
        flat, off = small_outs[kind].reshape(-1), 0
        for n in small_names:
            wgt = weights[n]
            results.setdefault(n, [None] * 4)[kind] = flat[off:off + wgt.size].reshape(wgt.shape)
            off += wgt.size

    return (loss, grad_x, *[results[n][0] for n in PARAM_ORDER], *[results[n][1] for n in PARAM_ORDER],
            *[results[n][2] for n in PARAM_ORDER], *[results[n][3] for n in PARAM_ORDER])
```

```python
import jax
import jax.numpy as jnp
from jax import lax
from jax.experimental import pallas as pl
from jax.experimental.pallas import tpu as pltpu

F32 = jnp.float32
MXU_DTYPE = jnp.bfloat16

D = 1024
N_META = 16
CHUNK = 64
CHUNK_LOG2 = 6
INV_BASE_LOG2 = 3
FRONT = CHUNK - N_META
HEAD = CHUNK
CONV_CH = 512
CONV_W = 31
CONV_HALO = 32
NH = 4
DH = 128
DQ = NH * DH
DN_W = 4
DN_HALO = 8
DFF = 2816
PROJ_MAIN = 3072
D_IN = 3080
GATE_W = 128
LANES = 128
NDEV = 8
NORM_EPS = 1e-6
LN_EPS = 1e-5
L2_EPS = 1e-6
VMEM_LIMIT_V7X = 48 * 1024 * 1024
ROW_TILE = 640
ROW_TILE_SMALL = 128
MM_TILES = (1408, 1280, 1024, 640, 512, 256, 128)
MM_SUB = 4
GRAD_DTYPE = jnp.bfloat16
DELTA_CHUNKS = (4, 2, 1)
ADAM_BLOCK_BYTES = 8 * 1024 * 1024

ADAM_LR = 0.001
ADAM_B1 = 0.9
ADAM_B2 = 0.999
ADAM_EPS = 1e-08
ADAM_WD = 0.01
ADAM_STEP = 10

MESH_AXES = ("x", "y", "c")
NN = ((1,), (0,))
NT = ((1,), (1,))
TN = ((0,), (0,))

assert 1 << CHUNK_LOG2 == CHUNK


def _row_tile(t):
    return ROW_TILE if t % ROW_TILE == 0 else ROW_TILE_SMALL


def _padded_rows(seq):
    n = HEAD + seq
    tm = ROW_TILE if n >= 4 * ROW_TILE else ROW_TILE_SMALL
    return -(-n // tm) * tm


def _pick(n, prefs):
    for p in prefs:
        if n % p == 0:
            return p
    return n


def _lane_pad(n):
    return -(-n // LANES) * LANES


def _cp(sem):
    return pltpu.CompilerParams(dimension_semantics=sem, vmem_limit_bytes=VMEM_LIMIT_V7X)


def _sigmoid(x):
    return 1.0 / (1.0 + jnp.exp(-x))


def _softplus(x):
    return jnp.maximum(x, 0.0) + jnp.log(1.0 + jnp.exp(-jnp.abs(x)))


def _valid_rows(i, tm, seq, width, first=FRONT):
    rows = i * tm + lax.broadcasted_iota(jnp.int32, (tm, width), 0)
    return jnp.logical_and(rows >= first, rows < HEAD + seq)


def _dot(a, b, dims):
    return lax.dot_general(a, b, (dims, ((), ())), preferred_element_type=F32)


def _split(x, n):
    out, r = [], x
    for _ in range(n):
        p = r.astype(MXU_DTYPE)
        out.append(p)
        r = r - p.astype(F32)
    return out


def _mm1(a, b, dims):
    return _dot(a.astype(MXU_DTYPE), b.astype(MXU_DTYPE), dims)


def _mm1_many(a_list, b_list, dims):
    return [_mm1(a, b, dims) for a, b in zip(a_list, b_list)]


def _mm3_many(a_list, b_list, dims):
    sa = [_split(a, 2) for a in a_list]
    sb = [_split(b, 2) for b in b_list]
    hh = [_dot(x[0], y[0], dims) for x, y in zip(sa, sb)]
    hl = [_dot(x[0], y[1], dims) for x, y in zip(sa, sb)]
    lh = [_dot(x[1], y[0], dims) for x, y in zip(sa, sb)]
    return [p + (q + r) for p, q, r in zip(hh, hl, lh)]


def _mmx(e, b, dims):
    e = e.astype(MXU_DTYPE)
    b1, b2, b3 = _split(b, 3)
    return _dot(e, b1, dims) + (_dot(e, b2, dims) + _dot(e, b3, dims))


def mm(a, b, *, ta=False, tb=False, a_fn=None, a_vecs=(), extras=(), out_fn=None, out_dtypes=(F32,), tile_cap=None,
       plus=(), name):
    a_list = list(a) if isinstance(a, (list, tuple)) else [a]
    (k_dim, m_dim) = a_list[0].shape if ta else a_list[0].shape[::-1]
    n_dim = b.shape[0] if tb else b.shape[1]
    assert (b.shape[1] if tb else b.shape[0]) == k_dim
    capped = MM_TILES if tile_cap is None else tuple(p for p in MM_TILES if p <= tile_cap)
    tm = _pick(m_dim, MM_TILES if ta else capped)
    tk = _pick(k_dim, capped if ta else MM_TILES)
    tn = _pick(n_dim, MM_TILES)
    nk = k_dim // tk
    na, nv, ne, no = len(a_list), len(a_vecs), len(extras), len(out_dtypes)
    dims = ((0,) if ta else (1,), (1,) if tb else (0,))
    nsub = MM_SUB if a_fn is not None and not ta and tm % (MM_SUB * 16) == 0 else 1
    sub = tm // nsub
    npl = len(plus)

    def body(*refs):
        a_refs, v_refs, b_ref = refs[:na], refs[na:na + nv], refs[na + nv]
        e_refs = refs[na + nv + 1:na + nv + 1 + ne]
        base = na + nv + 1 + ne
        pa_refs, pb_refs = refs[base:base + npl], refs[base + npl:base + 2 * npl]
        base += 2 * npl
        o_refs = refs[base:base + no]
        acc_ref = refs[-1] if nk > 1 else None
        k = pl.program_id(2)

        def left(rows):
            tiles = [r[rows, :] for r in a_refs]
            return tiles[0] if a_fn is None else a_fn(*tiles, *[v[...] for v in v_refs])

        def finish(acc, rows):
            outs = (acc,) if out_fn is None else out_fn(acc, *[e[rows, :] for e in e_refs])
            for o_ref, out in zip(o_refs, outs):
                o_ref[rows, :] = out.astype(o_ref.dtype)

        if nk > 1:
            @pl.when(k == 0)
            def _():
                acc_ref[...] = jnp.zeros_like(acc_ref)

        for r in range(nsub):
            rows = slice(r * sub, (r + 1) * sub) if nsub > 1 else slice(None)
            lhs = left(rows)
            prod = _mm1(lhs, b_ref[...], dims)
            for pa_ref, pb_ref in zip(pa_refs, pb_refs):
                prod = prod + _mm1(pa_ref[...], pb_ref[...], dims)
            if nk == 1:
                finish(prod, rows)
            else:
                acc_ref[rows, :] += prod

        if nk > 1:
            @pl.when(k == nk - 1)
            def _():
                finish(acc_ref[...], slice(None))

    if ta:
        a_spec = pl.BlockSpec((tk, tm), lambda i, j, k: (k, i))
        v_spec = pl.BlockSpec((1, tm), lambda i, j, k: (0, i))
    else:
        a_spec = pl.BlockSpec((tm, tk), lambda i, j, k: (i, k))
        v_spec = pl.BlockSpec((1, tk), lambda i, j, k: (0, k))
    b_spec = pl.BlockSpec((tn, tk), lambda i, j, k: (j, k)) if tb else pl.BlockSpec((tk, tn), lambda i, j, k: (k, j))
    o_spec = pl.BlockSpec((tm, tn), lambda i, j, k: (i, j))
    outs = pl.pallas_call(
        body, grid=(m_dim // tm, n_dim // tn, nk),
        in_specs=[a_spec] * na + [v_spec] * nv + [b_spec] + [o_spec] * ne + [a_spec] * npl + [b_spec] * npl,
        out_specs=[o_spec] * no,
        out_shape=[jax.ShapeDtypeStruct((m_dim, n_dim), dt) for dt in out_dtypes],
        scratch_shapes=[pltpu.VMEM((tm, tn), F32)] if nk > 1 else [],
        compiler_params=_cp(("parallel", "parallel", "arbitrary")),
        name=name)(*a_list, *a_vecs, b, *extras, *[p[0] for p in plus], *[p[1] for p in plus])
    return outs[0] if no == 1 else outs


def _rms_apply(x, w):
    assert x.shape[-1] == D
    return x * lax.rsqrt(jnp.mean(x * x, axis=-1, keepdims=True) + NORM_EPS) * w


def _swiglu(g, u):
    return g * _sigmoid(g) * u


def _swiglu_bwd(dact, g, u):
    sg = _sigmoid(g)
    return dact * u * (sg * (1.0 + g * (1.0 - sg))), dact * (g * sg)


def _add(acc, r):
    return (acc + r,)


def rms_bwd(dy, h, w, dres, name):
    t = h.shape[0]
    tm = _row_tile(t)

    def body(dy_ref, h_ref, w_ref, dres_ref, dh_ref, dw_ref):
        i = pl.program_id(0)
        x = h_ref[...]
        r = lax.rsqrt(jnp.mean(x * x, axis=-1, keepdims=True) + NORM_EPS)
        xh = x * r
        g = dy_ref[...] * w_ref[...]
        dh_ref[...] = dres_ref[...] + r * (g - xh * jnp.mean(g * xh, axis=-1, keepdims=True))

        @pl.when(i == 0)
        def _():
            dw_ref[...] = jnp.zeros_like(dw_ref)

        dw_ref[...] += jnp.sum(dy_ref[...] * xh, axis=0, keepdims=True)

    row = pl.BlockSpec((tm, D), lambda i: (i, 0))
    vec = pl.BlockSpec((1, D), lambda i: (0, 0))
    return pl.pallas_call(
        body, grid=(t // tm,), in_specs=[row, row, vec, row], out_specs=[row, vec],
        out_shape=[jax.ShapeDtypeStruct((t, D), F32), jax.ShapeDtypeStruct((1, D), F32)],
        compiler_params=_cp(("arbitrary",)), name=name)(dy, h, w, dres)


def loss_bwd(h, tgt, w, seq, name):
    t = h.shape[0]
    tm = _row_tile(t)

    def body(h_ref, t_ref, w_ref, dh_ref, loss_ref, dw_ref):
        i = pl.program_id(0)
        x = h_ref[...]
        wv = w_ref[...]
        r = lax.rsqrt(jnp.mean(x * x, axis=-1, keepdims=True) + NORM_EPS)
        xh = x * r
        err = jnp.where(_valid_rows(i, tm, seq, D, HEAD), xh * wv - t_ref[...], 0.0)
        dy = err * (1.0 / D)
        g = dy * wv
        dh_ref[...] = r * (g - xh * jnp.mean(g * xh, axis=-1, keepdims=True))

        @pl.when(i == 0)
        def _():
            dw_ref[...] = jnp.zeros_like(dw_ref)
            loss_ref[...] = jnp.zeros_like(loss_ref)

        dw_ref[...] += jnp.sum(dy * xh, axis=0, keepdims=True)
        part = jnp.sum(jnp.sum(err * err, axis=1, keepdims=True), axis=0, keepdims=True) * (0.5 / D)
        loss_ref[...] += jnp.broadcast_to(part, loss_ref.shape)

    row = pl.BlockSpec((tm, D), lambda i: (i, 0))
    vec = pl.BlockSpec((1, D), lambda i: (0, 0))
    return pl.pallas_call(
        body, grid=(t // tm,), in_specs=[row, row, vec],
        out_specs=[row, pl.BlockSpec((1, LANES), lambda i: (0, 0)), vec],
        out_shape=[jax.ShapeDtypeStruct((t, D), F32), jax.ShapeDtypeStruct((1, LANES), F32),
                   jax.ShapeDtypeStruct((1, D), F32)],
        compiler_params=_cp(("arbitrary",)), name=name)(h, tgt, w)


def _layernorm_parts(u1):
    mu = jnp.mean(u1, axis=-1, keepdims=True)
    xc = u1 - mu
    rstd = lax.rsqrt(jnp.mean(xc * xc, axis=-1, keepdims=True) + LN_EPS)
    return xc * rstd, rstd


SUBLANES = 8
CONV_ROWS = 16


def _shift_copies(ext, sh, tm):
    for s in range(1, SUBLANES):
        sh[s - 1, :, :] = ext[pl.ds(s, tm + CONV_HALO - SUBLANES), :]


def _window(ext, sh, off, rows, start=0):
    s, m = off % SUBLANES, off // SUBLANES
    if s == 0:
        return ext[pl.ds(start + off, rows), :]
    return sh[s - 1, pl.ds(start + SUBLANES * m, rows), :]


def _shift_scratch(tm):
    return pltpu.VMEM((SUBLANES - 1, tm + CONV_HALO - SUBLANES, CONV_CH), F32)


def conv_fwd(proj, w32, b, lw, lb, seq, name, push=()):
    t = proj.shape[0]
    tm = _row_tile(t)
    nt = t // tm
    npush = len(push)

    def body(*refs):
        cv_ref, cg_ref, w_ref, b_ref, lw_ref, lb_ref = refs[:6]
        x_refs = refs[6:6 + npush]
        y_ref, u1_ref = refs[6 + npush:8 + npush]
        got_refs = refs[8 + npush:8 + 2 * npush]
        ext, sh = refs[8 + 2 * npush:10 + 2 * npush]
        i = pl.program_id(0)
        if npush:
            copies = _push_copies(x_refs, got_refs, *refs[10 + 2 * npush:], scatter=False)

            @pl.when(i == 0)
            def _():
                _push_start(copies)

        @pl.when(i == 0)
        def _():
            ext[0:CONV_HALO, :] = jnp.zeros((CONV_HALO, CONV_CH), F32)

        @pl.when(i > 0)
        def _():
            ext[0:CONV_HALO, :] = ext[tm:tm + CONV_HALO, :]

        ext[CONV_HALO:CONV_HALO + tm, :] = cv_ref[...] * _sigmoid(cg_ref[...])
        _shift_copies(ext, sh, tm)
        acc = jnp.broadcast_to(b_ref[...], (tm, CONV_CH))
        for j in range(CONV_W):
            acc = acc + w_ref[j:j + 1, :] * _window(ext, sh, CONV_HALO - (CONV_W - 1) + j, tm)
        u1_ref[...] = acc
        xh, _ = _layernorm_parts(acc)
        ln = xh * lw_ref[...] + lb_ref[...]
        y = ln * _sigmoid(ln)
        y_ref[...] = jnp.where(_valid_rows(i, tm, seq, CONV_CH), y, 0.0).astype(y_ref.dtype)
        if npush:
            @pl.when(i == nt - 1)
            def _():
                _push_finish(copies)

    half = lambda c: pl.BlockSpec((tm, CONV_CH), lambda i: (i, c))
    vec = pl.BlockSpec((1, CONV_CH), lambda i: (0, 0))
    any_spec = pl.BlockSpec(memory_space=pl.ANY)
    outs = pl.pallas_call(
        body, grid=(nt,),
        in_specs=[half(0), half(1), pl.BlockSpec((CONV_HALO, CONV_CH), lambda i: (0, 0)), vec, vec, vec]
        + [any_spec] * npush,
        out_specs=[half(0), half(0)] + [any_spec] * npush,
        out_shape=[jax.ShapeDtypeStruct((t, D), MXU_DTYPE), jax.ShapeDtypeStruct((t, CONV_CH), F32)]
        + _push_out_shapes(push, scatter=False),
        scratch_shapes=[pltpu.VMEM((tm + CONV_HALO, CONV_CH), F32), _shift_scratch(tm)]
        + (_push_sems(npush) if npush else []),
        compiler_params=_cp(("arbitrary",)), name=name)(proj, proj, w32, b, lw, lb, *push)
    return outs[0], outs[1], outs[2:]


def conv_bwd(dy, u1, proj, w32, lw, lb, seq, name, push=()):
    t = proj.shape[0]
    tm = _row_tile(t)
    nt = t // tm
    per = tm // CONV_HALO
    npush = len(push)

    def body(*refs):
        dy_ref, u1_ref, cv_ref, cg_ref, cvp_ref, cgp_ref, w_ref, lw_ref, lb_ref = refs[:9]
        x_refs = refs[9:9 + npush]
        dp_ref, dw_ref, db_ref, dlw_ref, dlb_ref = refs[9 + npush:14 + npush]
        got_refs = refs[14 + npush:14 + 2 * npush]
        ext_d, ext_u, sh_d, sh_u, du0_s, dw_acc = refs[14 + 2 * npush:20 + 2 * npush]
        i = pl.program_id(0)
        tile = nt - 1 - i
        if npush:
            copies = _push_copies(x_refs, got_refs, *refs[20 + 2 * npush:], scatter=True)

            @pl.when(i == 0)
            def _():
                _push_start(copies)

        @pl.when(i == 0)
        def _():
            ext_d[tm:tm + CONV_HALO, :] = jnp.zeros((CONV_HALO, CONV_CH), F32)
            dw_acc[...] = jnp.zeros_like(dw_acc)
            db_ref[...] = jnp.zeros_like(db_ref)
            dlw_ref[...] = jnp.zeros_like(dlw_ref)
            dlb_ref[...] = jnp.zeros_like(dlb_ref)

        @pl.when(i > 0)
        def _():
            ext_d[tm:tm + CONV_HALO, :] = ext_d[0:CONV_HALO, :]

        xh, rstd = _layernorm_parts(u1_ref[...])
        lwv = lw_ref[...]
        ln = xh * lwv + lb_ref[...]
        sg = _sigmoid(ln)
        dln = jnp.where(_valid_rows(tile, tm, seq, CONV_CH), dy_ref[...], 0.0) * (sg * (1.0 + ln * (1.0 - sg)))
        dlw_ref[...] += jnp.sum(dln * xh, axis=0, keepdims=True)
        dlb_ref[...] += jnp.sum(dln, axis=0, keepdims=True)
        dxh = dln * lwv
        du1 = rstd * (dxh - jnp.mean(dxh, axis=-1, keepdims=True)
                      - xh * jnp.mean(dxh * xh, axis=-1, keepdims=True))
        db_ref[...] += jnp.sum(du1, axis=0, keepdims=True)
        ext_d[0:tm, :] = du1

        cv = cv_ref[...]
        sgc = _sigmoid(cg_ref[...])
        prev = cvp_ref[...] * _sigmoid(cgp_ref[...])
        ext_u[0:CONV_HALO, :] = jnp.where(tile > 0, prev, 0.0)
        ext_u[CONV_HALO:CONV_HALO + tm, :] = cv * sgc

        _shift_copies(ext_d, sh_d, tm)
        _shift_copies(ext_u, sh_u, tm)

        def row_block(rb, carry):
            r0 = pl.multiple_of(rb * CONV_ROWS, CONV_ROWS)
            du1_b = ext_d[pl.ds(r0, CONV_ROWS), :]
            acc = jnp.zeros((CONV_ROWS, CONV_CH), F32)
            for j in range(CONV_W):
                acc = acc + w_ref[j:j + 1, :] * _window(ext_d, sh_d, CONV_W - 1 - j, CONV_ROWS, r0)
                prod = du1_b * _window(ext_u, sh_u, CONV_HALO - (CONV_W - 1) + j, CONV_ROWS, r0)
                dw_acc[j] += prod[0:SUBLANES, :] + prod[SUBLANES:CONV_ROWS, :]
            du0_s[pl.ds(r0, CONV_ROWS), :] = acc
            return carry

        lax.fori_loop(0, tm // CONV_ROWS, row_block, 0)

        @pl.when(i == nt - 1)
        def _():
            dw_ref[...] = jnp.sum(dw_acc[...], axis=1)

        du0 = du0_s[...]
        dp_ref[:, 0:CONV_CH] = (du0 * sgc).astype(dp_ref.dtype)
        dp_ref[:, CONV_CH:2 * CONV_CH] = (du0 * cv * sgc * (1.0 - sgc)).astype(dp_ref.dtype)
        if npush:
            @pl.when(i == nt - 1)
            def _():
                _push_finish(copies)

    rev = lambda c: pl.BlockSpec((tm, CONV_CH), lambda i: (nt - 1 - i, c))
    prev = lambda c: pl.BlockSpec((CONV_HALO, CONV_CH), lambda i: (jnp.maximum((nt - 1 - i) * per - 1, 0), c))
    vec = pl.BlockSpec((1, CONV_CH), lambda i: (0, 0))
    wspec = pl.BlockSpec((CONV_HALO, CONV_CH), lambda i: (0, 0))
    any_spec = pl.BlockSpec(memory_space=pl.ANY)
    outs = pl.pallas_call(
        body, grid=(nt,),
        in_specs=[rev(0), rev(0), rev(0), rev(1), prev(0), prev(1), wspec, vec, vec] + [any_spec] * npush,
        out_specs=[pl.BlockSpec((tm, 2 * CONV_CH), lambda i: (nt - 1 - i, 0)), wspec, vec, vec, vec]
        + [any_spec] * npush,
        out_shape=[jax.ShapeDtypeStruct((t, PROJ_MAIN), MXU_DTYPE), jax.ShapeDtypeStruct((CONV_HALO, CONV_CH), F32),
                   jax.ShapeDtypeStruct((1, CONV_CH), F32), jax.ShapeDtypeStruct((1, CONV_CH), F32),
                   jax.ShapeDtypeStruct((1, CONV_CH), F32)] + _push_out_shapes(push, scatter=True),
        scratch_shapes=[pltpu.VMEM((tm + CONV_HALO, CONV_CH), F32), pltpu.VMEM((tm + CONV_HALO, CONV_CH), F32),
                        _shift_scratch(tm), _shift_scratch(tm), pltpu.VMEM((tm, CONV_CH), F32),
                        pltpu.VMEM((CONV_HALO, SUBLANES, CONV_CH), F32)] + (_push_sems(npush) if npush else []),
        compiler_params=_cp(("arbitrary",)), name=name)(dy, u1, proj, proj, proj, proj, w32, lw, lb, *push)
    return outs[:5], outs[5:]


def dn_pre_fwd(proj, w8, name):
    t = proj.shape[0]
    tm = _row_tile(t)

    def body(raw_ref, w_ref, o_ref, ext):
        g = pl.program_id(0)
        i = pl.program_id(1)

        @pl.when(i == 0)
        def _():
            ext[0:DN_HALO, :] = jnp.zeros((DN_HALO, DQ), F32)

        @pl.when(i > 0)
        def _():
            ext[0:DN_HALO, :] = ext[tm:tm + DN_HALO, :]

        ext[DN_HALO:DN_HALO + tm, :] = raw_ref[...]
        c = jnp.zeros((tm, DQ), F32)
        for j in range(DN_W):
            c = c + w_ref[j:j + 1, :] * ext[pl.ds(DN_HALO - (DN_W - 1) + j, tm), :]
        s = c * _sigmoid(c)
        scale = jnp.where(g == 0, DH ** -0.5, 1.0)
        for h in range(NH):
            sh = s[:, h * DH:(h + 1) * DH]
            r = lax.rsqrt(jnp.sum(sh * sh, axis=-1, keepdims=True) + L2_EPS)
            o_ref[:, h * DH:(h + 1) * DH] = jnp.where(g == 2, sh, sh * (r * scale))

    return pl.pallas_call(
        body, grid=(3, t // tm),
        in_specs=[pl.BlockSpec((tm, DQ), lambda g, i: (i, 2 + g)), pl.BlockSpec((DN_HALO, DQ), lambda g, i: (0, g))],
        out_specs=pl.BlockSpec((tm, DQ), lambda g, i: (i, g)),
        out_shape=jax.ShapeDtypeStruct((t, 3 * DQ), F32),
        scratch_shapes=[pltpu.VMEM((tm + DN_HALO, DQ), F32)],
        compiler_params=_cp(("arbitrary", "arbitrary")), name=name)(proj, w8)


def dn_pre_bwd(dproj, dqkv, proj, w8, name):
    t = proj.shape[0]
    tm = _row_tile(t)
    nt = t // tm
    per = tm // DN_HALO

    def body(dp_in, d_ref, raw_ref, rawp_ref, w_ref, dp_ref, dw_ref, ext_d, ext_r):
        del dp_in
        g = pl.program_id(0)
        i = pl.program_id(1)
        tile = nt - 1 - i

        @pl.when(i == 0)
        def _():
            ext_d[tm:tm + DN_HALO, :] = jnp.zeros((DN_HALO, DQ), F32)
            dw_ref[...] = jnp.zeros_like(dw_ref)

        @pl.when(i > 0)
        def _():
            ext_d[tm:tm + DN_HALO, :] = ext_d[0:DN_HALO, :]

        ext_r[0:DN_HALO, :] = jnp.where(tile > 0, rawp_ref[...], 0.0)
        ext_r[DN_HALO:DN_HALO + tm, :] = raw_ref[...]
        taps = [ext_r[pl.ds(DN_HALO - (DN_W - 1) + j, tm), :] for j in range(DN_W)]
        c = jnp.zeros((tm, DQ), F32)
        for j in range(DN_W):
            c = c + w_ref[j:j + 1, :] * taps[j]
        sg = _sigmoid(c)
        s = c * sg
        scale = jnp.where(g == 0, DH ** -0.5, 1.0)
        for h in range(NH):
            sl = slice(h * DH, (h + 1) * DH)
            sh = s[:, sl]
            dn = d_ref[:, sl]
            r = lax.rsqrt(jnp.sum(sh * sh, axis=-1, keepdims=True) + L2_EPS)
            unit = sh * r
            dsn = (r * scale) * (dn - unit * jnp.sum(dn * unit, axis=-1, keepdims=True))
            ds = jnp.where(g == 2, dn, dsn)
            ext_d[0:tm, sl] = ds * (sg[:, sl] * (1.0 + c[:, sl] * (1.0 - sg[:, sl])))
        dc = ext_d[0:tm, :]
        draw = jnp.zeros((tm, DQ), F32)
        for j in range(DN_W):
            draw = draw + w_ref[j:j + 1, :] * ext_d[pl.ds(DN_W - 1 - j, tm), :]
            dw_ref[j:j + 1, :] += jnp.sum(dc * taps[j], axis=0, keepdims=True)
        dp_ref[...] = draw.astype(dp_ref.dtype)

    return pl.pallas_call(
        body, grid=(3, nt),
        in_specs=[pl.BlockSpec(memory_space=pl.ANY),
                  pl.BlockSpec((tm, DQ), lambda g, i: (nt - 1 - i, g)),
                  pl.BlockSpec((tm, DQ), lambda g, i: (nt - 1 - i, 2 + g)),
                  pl.BlockSpec((DN_HALO, DQ), lambda g, i: (jnp.maximum((nt - 1 - i) * per - 1, 0), 2 + g)),
                  pl.BlockSpec((DN_HALO, DQ), lambda g, i: (0, g))],
        out_specs=[pl.BlockSpec((tm, DQ), lambda g, i: (nt - 1 - i, 2 + g)),
                   pl.BlockSpec((DN_HALO, DQ), lambda g, i: (0, g))],
        out_shape=[jax.ShapeDtypeStruct(dproj.shape, dproj.dtype), jax.ShapeDtypeStruct((DN_HALO, 3 * DQ), F32)],
        scratch_shapes=[pltpu.VMEM((tm + DN_HALO, DQ), F32), pltpu.VMEM((tm + DN_HALO, DQ), F32)],
        input_output_aliases={0: 0},
        compiler_params=_cp(("arbitrary", "arbitrary")), name=name)(dproj, dqkv, proj, proj, w8)


def gate_fwd(pg, alog, dtb, seq, name):
    t = pg.shape[0]
    tm = _row_tile(t)

    def body(x_ref, al_ref, dt_ref, o_ref):
        i = pl.program_id(0)
        x = x_ref[...]
        lane = lax.broadcasted_iota(jnp.int32, (tm, GATE_W), 1)
        gg = -jnp.exp(al_ref[...]) * _softplus(x + dt_ref[...])
        out = jnp.where(lane < NH, _sigmoid(x), jnp.where(lane < 2 * NH, gg, 0.0))
        o_ref[...] = jnp.where(_valid_rows(i, tm, seq, GATE_W), out, 0.0)

    row = pl.BlockSpec((tm, GATE_W), lambda i: (i, 0))
    vec = pl.BlockSpec((1, GATE_W), lambda i: (0, 0))
    return pl.pallas_call(
        body, grid=(t // tm,), in_specs=[row, vec, vec], out_specs=row,
        out_shape=jax.ShapeDtypeStruct((t, GATE_W), F32), compiler_params=_cp(("parallel",)), name=name)(pg, alog, dtb)


def gate_bwd(dbg, pg, alog, dtb, seq, name):
    t = pg.shape[0]
    tm = _row_tile(t)

    def body(d_ref, x_ref, al_ref, dt_ref, o_ref, dal_ref, ddt_ref):
        i = pl.program_id(0)
        x = x_ref[...]
        lane = lax.broadcasted_iota(jnp.int32, (tm, GATE_W), 1)
        d = jnp.where(_valid_rows(i, tm, seq, GATE_W), d_ref[...], 0.0)
        beta = _sigmoid(x)
        xs = x + dt_ref[...]
        e = -jnp.exp(al_ref[...])
        is_g = jnp.logical_and(lane >= NH, lane < 2 * NH)
        da = jnp.where(is_g, d * e * _sigmoid(xs), 0.0)
        dgg = jnp.where(is_g, d * e * _softplus(xs), 0.0)
        o_ref[...] = jnp.where(lane < NH, d * beta * (1.0 - beta), da).astype(o_ref.dtype)

        @pl.when(i == 0)
        def _():
            dal_ref[...] = jnp.zeros_like(dal_ref)
            ddt_ref[...] = jnp.zeros_like(ddt_ref)

        dal_ref[...] += jnp.sum(dgg, axis=0, keepdims=True)
        ddt_ref[...] += jnp.sum(da, axis=0, keepdims=True)

    row = pl.BlockSpec((tm, GATE_W), lambda i: (i, 0))
    vec = pl.BlockSpec((1, GATE_W), lambda i: (0, 0))
    return pl.pallas_call(
        body, grid=(t // tm,), in_specs=[row, row, vec, vec], out_specs=[row, vec, vec],
        out_shape=[jax.ShapeDtypeStruct((t, GATE_W), MXU_DTYPE), jax.ShapeDtypeStruct((1, GATE_W), F32),
                   jax.ShapeDtypeStruct((1, GATE_W), F32)],
        compiler_params=_cp(("arbitrary",)), name=name)(dbg, pg, alog, dtb)


def _chunk_masks():
    ii = lax.broadcasted_iota(jnp.int32, (CHUNK, CHUNK), 0)
    jj = lax.broadcasted_iota(jnp.int32, (CHUNK, CHUNK), 1)
    return ii, jj, ii >= jj, ii > jj


def _lane_col(x, lane, idx):
    return jnp.sum(jnp.where(lane == idx, x, 0.0), axis=1, keepdims=True)


def _stack_rows(xs, ys):
    return [jnp.concatenate([x, y], axis=0) for x, y in zip(xs, ys)]


def _side_by_side(xs, ys):
    return [jnp.concatenate([x, y], axis=1) for x, y in zip(xs, ys)]


def _delta_terms(q, k, v, bgs, nch, low, strict):
    idx = [(c, h) for c in range(nch) for h in range(NH)]
    lane = lax.broadcasted_iota(jnp.int32, (CHUNK, GATE_W), 1)
    rowi = lax.broadcasted_iota(jnp.int32, (CHUNK, 1), 0)
    r4 = lax.broadcasted_iota(jnp.int32, (NH * CHUNK, GATE_W), 0)
    l4 = lax.broadcasted_iota(jnp.int32, (NH * CHUNK, GATE_W), 1)
    sel = jnp.where(l4 == NH + jnp.right_shift(r4, CHUNK_LOG2), 1.0, 0.0)
    lowf = jnp.where(low, 1.0, 0.0)
    gam_all = [_mmx(lowf, b, NN) for b in bgs]
    gam_rows = [_mmx(sel, g, NT) for g in gam_all]
    beta = [_lane_col(bgs[c], lane, h) for c, h in idx]
    gam = [_lane_col(gam_all[c], lane, NH + h) for c, h in idx]
    dm = [jnp.exp(jnp.where(low, g - gam_rows[c][h * CHUNK:(h + 1) * CHUNK, :], -1e30))
          for g, (c, h) in zip(gam, idx)]
    glast = [jnp.sum(jnp.where(rowi == CHUNK - 1, g, 0.0), axis=0, keepdims=True) for g in gam]
    eg = [jnp.exp(g) for g in gam]
    ekl = [jnp.exp(gl - g) for gl, g in zip(glast, gam)]
    gl = [jnp.exp(x) for x in glast]
    kb = [x * b for x, b in zip(k, beta)]
    vb = [x * b for x, b in zip(v, beta)]
    kbg = [x * e for x, e in zip(kb, eg)]
    kq = _mm1_many(_stack_rows(kb, q), k, NT)
    a_mat = [jnp.where(strict, x[:CHUNK] * d, 0.0) for x, d in zip(kq, dm)]
    p_mat = [jnp.where(low, x[CHUNK:] * d, 0.0) for x, d in zip(kq, dm)]
    qd = [x * e for x, e in zip(q, eg)]
    kd = [x * e for x, e in zip(k, ekl)]
    return dict(idx=idx, beta=beta, dm=dm, eg=eg, ekl=ekl, gl=gl, kb=kb, vb=vb, kbg=kbg, a=a_mat, p=p_mat, qd=qd,
                kd=kd, lane=lane, rowi=rowi)


def _unit_lower_inverses(a_list, ii, jj, eye):
    def same(log2):
        return jnp.right_shift(ii, log2) == jnp.right_shift(jj, log2)

    n = [-jnp.where(same(INV_BASE_LOG2), a, 0.0) for a in a_list]
    x = [eye + v for v in n]
    p = n
    for _ in range(INV_BASE_LOG2 - 1):
        p = _mm1_many(p, p, NN)
        x = [xi + y for xi, y in zip(x, _mm1_many(x, p, NN))]
    for log2 in range(INV_BASE_LOG2, CHUNK_LOG2):
        off = jnp.logical_and(same(log2 + 1), jnp.logical_not(same(log2)))
        a_off = [jnp.where(off, a, 0.0) for a in a_list]
        x = [xi - y for xi, y in zip(x, _mm1_many(x, _mm1_many(a_off, x, NN), NN))]
    return x


def _transposes(xs, eye):
    e = eye.astype(MXU_DTYPE)
    parts = [_split(x, 2) for x in xs]
    return [_dot(p[0], e, TN) + _dot(p[1], e, TN) for p in parts]


def _load_heads(ref, nch):
    return [ref[c * CHUNK:(c + 1) * CHUNK, h * DH:(h + 1) * DH] for c in range(nch) for h in range(NH)]


def delta_fwd(qkv, bg, name, push=None):
    t = qkv.shape[0]
    nc = t // CHUNK
    nch = _pick(nc, DELTA_CHUNKS)
    rows = nch * CHUNK
    ng = nc // nch
    npush = 0 if push is None else len(push)

    def body(*refs):
        q_ref, k_ref, v_ref, bg_ref = refs[:4]
        x_refs = refs[4:4 + npush]
        o_ref, sh_ref, mi_ref, u_ref, w_ref = refs[4 + npush:9 + npush]
        got_refs = refs[9 + npush:9 + 2 * npush]
        s_ref = refs[9 + 2 * npush]
        n = pl.program_id(0)
        if npush:
            copies = _push_copies(x_refs, got_refs, *refs[10 + 2 * npush:], scatter=False)

            @pl.when(n == 0)
            def _():
                _push_start(copies)

        @pl.when(n == 0)
        def _():
            s_ref[...] = jnp.zeros_like(s_ref)

        ii, jj, low, strict = _chunk_masks()
        eye = jnp.where(ii == jj, 1.0, 0.0)
        q, k, v = _load_heads(q_ref, nch), _load_heads(k_ref, nch), _load_heads(v_ref, nch)
        bgs = [bg_ref[c * CHUNK:(c + 1) * CHUNK, :] for c in range(nch)]
        tm_ = _delta_terms(q, k, v, bgs, nch, low, strict)
        m_inv = _unit_lower_inverses(tm_["a"], ii, jj, eye)
        uw = _mm3_many(m_inv, _side_by_side(tm_["vb"], tm_["kbg"]), NN)
        u = [x[:, :DH] for x in uw]
        w = [x[:, DH:] for x in uw]
        for i, (c, h) in enumerate(tm_["idx"]):
            mi_ref[c, h] = m_inv[i]
            u_ref[c * CHUNK:(c + 1) * CHUNK, h * DH:(h + 1) * DH] = u[i]
            w_ref[c * CHUNK:(c + 1) * CHUNK, h * DH:(h + 1) * DH] = w[i]
        wq = _stack_rows(w, tm_["qd"])
        s = [s_ref[h] for h in range(NH)]
        for c in range(nch):
            pr = range(c * NH, (c + 1) * NH)
            wqs = [_mm1(wq[i], s[i - c * NH], NN) for i in pr]
            ws = [x[:CHUNK] for x in wqs]
            qs = [x[CHUNK:] for x in wqs]
            vn = [u[i] - x for i, x in zip(pr, ws)]
            pv = [_mm1(tm_["p"][i], x, NN) for i, x in zip(pr, vn)]
            kv = [_mm1(tm_["kd"][i], x, TN) for i, x in zip(pr, vn)]
            for h in range(NH):
                o_ref[c * CHUNK:(c + 1) * CHUNK, h * DH:(h + 1) * DH] = qs[h] + pv[h]
                sh_ref[c, h] = s[h]
                s[h] = tm_["gl"][c * NH + h] * s[h] + kv[h]
        for h in range(NH):
            s_ref[h] = s[h]
        if npush:
            @pl.when(n == ng - 1)
            def _():
                _push_finish(copies)

    col = lambda c: pl.BlockSpec((rows, DQ), lambda n: (n, c))
    any_spec = pl.BlockSpec(memory_space=pl.ANY)
    pushed = [] if push is None else list(push)
    outs = pl.pallas_call(
        body, grid=(ng,),
        in_specs=[col(0), col(1), col(2), pl.BlockSpec((rows, GATE_W), lambda n: (n, 0))] + [any_spec] * npush,
        out_specs=[col(0), pl.BlockSpec((nch, NH, DH, DH), lambda n: (n, 0, 0, 0)),
                   pl.BlockSpec((nch, NH, CHUNK, CHUNK), lambda n: (n, 0, 0, 0)), col(0), col(0)]
        + [any_spec] * npush,
        out_shape=[jax.ShapeDtypeStruct((t, DQ), F32), jax.ShapeDtypeStruct((nc, NH, DH, DH), F32),
                   jax.ShapeDtypeStruct((nc, NH, CHUNK, CHUNK), F32), jax.ShapeDtypeStruct((t, DQ), F32),
                   jax.ShapeDtypeStruct((t, DQ), F32)] + _push_out_shapes(pushed, scatter=False),
        scratch_shapes=[pltpu.VMEM((NH, DH, DH), F32)] + (_push_sems(npush) if npush else []),
        compiler_params=_cp(("arbitrary",)), name=name)(qkv, qkv, qkv, bg, *pushed)
    return outs[:5], outs[5:]


def delta_bwd(qkv, bg, do, s_hist, m_hist, u_all, w_all, name, push=None):
    t = qkv.shape[0]
    nc = t // CHUNK
    nch = _pick(nc, DELTA_CHUNKS)
    rows = nch * CHUNK
    ng = nc // nch
    npush = 0 if push is None else len(push)

    def body(*refs):
        q_ref, k_ref, v_ref, bg_ref, do_ref, sh_ref, mi_ref, u_ref, w_ref = refs[:9]
        x_refs = refs[9:9 + npush]
        dqkv_ref, dbg_ref = refs[9 + npush:11 + npush]
        got_refs = refs[11 + npush:11 + 2 * npush]
        ds_ref = refs[11 + 2 * npush]
        n = pl.program_id(0)
        if npush:
            copies = _push_copies(x_refs, got_refs, *refs[12 + 2 * npush:], scatter=True)

            @pl.when(n == 0)
            def _():
                _push_start(copies)

        @pl.when(n == 0)
        def _():
            ds_ref[...] = jnp.zeros_like(ds_ref)

        ii, jj, low, strict = _chunk_masks()
        eye = jnp.where(ii == jj, 1.0, 0.0)
        q, k, v = _load_heads(q_ref, nch), _load_heads(k_ref, nch), _load_heads(v_ref, nch)
        d_o = _load_heads(do_ref, nch)
        bgs = [bg_ref[c * CHUNK:(c + 1) * CHUNK, :] for c in range(nch)]
        tm_ = _delta_terms(q, k, v, bgs, nch, low, strict)
        idx, lane, rowi = tm_["idx"], tm_["lane"], tm_["rowi"]
        beta, dm, eg, ekl, gl = tm_["beta"], tm_["dm"], tm_["eg"], tm_["ekl"], tm_["gl"]
        kb, kbg, qd, kd, a_mat, p_mat = tm_["kb"], tm_["kbg"], tm_["qd"], tm_["kd"], tm_["a"], tm_["p"]
        s = [sh_ref[c, h] for c, h in idx]
        m_inv = [mi_ref[c, h] for c, h in idx]
        u, w = _load_heads(u_ref, nch), _load_heads(w_ref, nch)
        ws = _mm1_many(w, s, NN)
        vn = [x - y for x, y in zip(u, ws)]
        qp_do = _mm1_many(_side_by_side(qd, p_mat), d_o, TN)
        qdo = [x[:DH] for x in qp_do]
        pdo = [x[DH:] for x in qp_do]
        dqd = _mm1_many(d_o, s, NT)
        dp = [jnp.where(low, x, 0.0) for x in _mm1_many(d_o, vn, NT)]

        nprob = len(idx)
        dvn, dkd, dgl = [None] * nprob, [None] * nprob, [None] * nprob
        ds = [ds_ref[h] for h in range(NH)]
        for c in reversed(range(nch)):
            pr = list(range(c * NH, (c + 1) * NH))
            kds = [_mm1(kd[i], ds[i - c * NH], NN) for i in pr]
            for i, x in zip(pr, kds):
                dvn[i] = pdo[i] + x
            wdv = [_mm1(w[i], dvn[i], TN) for i in pr]
            for i in pr:
                h = i - c * NH
                dkd[i] = _mm1(vn[i], ds[h], NT)
                dgl[i] = jnp.sum(jnp.sum(s[i] * ds[h], axis=1, keepdims=True), axis=0, keepdims=True)
                ds[h] = qdo[i] + gl[i] * ds[h] - wdv[h]
        for h in range(NH):
            ds_ref[h] = ds[h]

        dw = [-x for x in _mm1_many(dvn, s, NT)]
        dvb_dkbg = _mm3_many(m_inv, _side_by_side(dvn, dw), TN)
        dvb = [x[:, :DH] for x in dvb_dkbg]
        dkbg = [x[:, DH:] for x in dvb_dkbg]
        da = [-jnp.where(strict, x, 0.0)
              for x in _mm1_many(dvb_dkbg, _side_by_side(u, w), NT)]
        gm = [x * d for x, d in zip(da, dm)]
        hm = [x * d for x, d in zip(dp, dm)]
        gh = _stack_rows(gm, hm)
        gh_k = _mm1_many(gh, k, NN)
        gk = [x[:CHUNK] for x in gh_k]
        hk = [x[CHUNK:] for x in gh_k]
        gkb_hq = _mm1_many(gh, _stack_rows(kb, q), TN)
        em = [x * a + y * p for x, a, y, p in zip(da, a_mat, dp, p_mat)]
        em_t = _transposes(em, eye)
        dbeta_all = [jnp.zeros((CHUNK, GATE_W), F32) for _ in range(nch)]
        dgam_all = [jnp.zeros((CHUNK, GATE_W), F32) for _ in range(nch)]
        for i, (c, h) in enumerate(idx):
            dkb = gk[i] + dkbg[i] * eg[i]
            dk = gkb_hq[i] + dkd[i] * ekl[i] + beta[i] * dkb
            dq = hk[i] + dqd[i] * eg[i]
            dkd_kd = jnp.sum(dkd[i] * kd[i], axis=1, keepdims=True)
            dgam = (jnp.sum(em[i], axis=1, keepdims=True) - jnp.sum(em_t[i], axis=1, keepdims=True)
                    + jnp.sum(dqd[i] * qd[i], axis=1, keepdims=True) - dkd_kd
                    + jnp.sum(dkbg[i] * kbg[i], axis=1, keepdims=True))
            tail = jnp.sum(dkd_kd, axis=0, keepdims=True) + dgl[i] * gl[i]
            dgam = dgam + jnp.where(rowi == CHUNK - 1, tail, 0.0)
            dbeta = jnp.sum(dkb * k[i], axis=1, keepdims=True) + jnp.sum(dvb[i] * v[i], axis=1, keepdims=True)
            rs = slice(c * CHUNK, (c + 1) * CHUNK)
            dqkv_ref[rs, h * DH:(h + 1) * DH] = dq
            dqkv_ref[rs, DQ + h * DH:DQ + (h + 1) * DH] = dk
            dqkv_ref[rs, 2 * DQ + h * DH:2 * DQ + (h + 1) * DH] = beta[i] * dvb[i]
            dbeta_all[c] = dbeta_all[c] + jnp.where(lane == h, dbeta, 0.0)
            dgam_all[c] = dgam_all[c] + jnp.where(lane == NH + h, dgam, 0.0)
        upf = jnp.where(ii <= jj, 1.0, 0.0)
        for c in range(nch):
            dg_all = _mmx(upf, dgam_all[c], NN)
            dbg_ref[c * CHUNK:(c + 1) * CHUNK, :] = jnp.where(lane < NH, dbeta_all[c], dg_all)
        if npush:
            @pl.when(n == ng - 1)
            def _():
                _push_finish(copies)

    col = lambda c: pl.BlockSpec((rows, DQ), lambda n: (ng - 1 - n, c))
    gate = pl.BlockSpec((rows, GATE_W), lambda n: (ng - 1 - n, 0))
    any_spec = pl.BlockSpec(memory_space=pl.ANY)
    pushed = [] if push is None else list(push)
    outs = pl.pallas_call(
        body, grid=(ng,),
        in_specs=[col(0), col(1), col(2), gate, col(0),
                  pl.BlockSpec((nch, NH, DH, DH), lambda n: (ng - 1 - n, 0, 0, 0)),
                  pl.BlockSpec((nch, NH, CHUNK, CHUNK), lambda n: (ng - 1 - n, 0, 0, 0)), col(0), col(0)]
        + [any_spec] * npush,
        out_specs=[pl.BlockSpec((rows, 3 * DQ), lambda n: (ng - 1 - n, 0)), gate] + [any_spec] * npush,
        out_shape=[jax.ShapeDtypeStruct((t, 3 * DQ), F32), jax.ShapeDtypeStruct((t, GATE_W), F32)]
        + _push_out_shapes(pushed, scatter=True),
        scratch_shapes=[pltpu.VMEM((NH, DH, DH), F32)] + (_push_sems(npush) if npush else []),
        compiler_params=_cp(("arbitrary",)), name=name)(qkv, qkv, qkv, bg, do, s_hist, m_hist, u_all, w_all, *pushed)
    return outs[0], outs[1], outs[2:]


def dn_post_fwd(ybuf, o, proj, nw, name):
    t = o.shape[0]
    tm = _row_tile(t)

    def body(y_in, o_ref, z_ref, nw_ref, y_ref):
        del y_in
        nwv = nw_ref[...]
        for h in range(NH):
            sl = slice(h * DH, (h + 1) * DH)
            oh = o_ref[:, sl]
            z = z_ref[:, sl]
            r = lax.rsqrt(jnp.mean(oh * oh, axis=-1, keepdims=True) + NORM_EPS)
            y_ref[:, sl] = (oh * r * nwv * (z * _sigmoid(z))).astype(y_ref.dtype)

    return pl.pallas_call(
        body, grid=(t // tm,),
        in_specs=[pl.BlockSpec(memory_space=pl.ANY), pl.BlockSpec((tm, DQ), lambda i: (i, 0)),
                  pl.BlockSpec((tm, DQ), lambda i: (i, 5)), pl.BlockSpec((1, DH), lambda i: (0, 0))],
        out_specs=pl.BlockSpec((tm, DQ), lambda i: (i, 1)),
        out_shape=jax.ShapeDtypeStruct(ybuf.shape, ybuf.dtype), input_output_aliases={0: 0},
        compiler_params=_cp(("parallel",)), name=name)(ybuf, o, proj, nw)


def dn_post_bwd(dproj, dy, o, proj, nw, name):
    t = o.shape[0]
    tm = _row_tile(t)

    def body(dp_in, dy_ref, o_ref, z_ref, nw_ref, do_ref, dp_ref, dnw_ref):
        del dp_in
        i = pl.program_id(0)
        nwv = nw_ref[...]
        acc = jnp.zeros((1, DH), F32)
        for h in range(NH):
            sl = slice(h * DH, (h + 1) * DH)
            oh = o_ref[:, sl]
            z = z_ref[:, sl]
            dyh = dy_ref[:, sl]
            r = lax.rsqrt(jnp.mean(oh * oh, axis=-1, keepdims=True) + NORM_EPS)
            xh = oh * r
            sg = _sigmoid(z)
            sz = z * sg
            dxh = dyh * nwv * sz
            do_ref[:, sl] = r * (dxh - xh * jnp.mean(dxh * xh, axis=-1, keepdims=True))
            dp_ref[:, sl] = (dyh * xh * nwv * (sg * (1.0 + z * (1.0 - sg)))).astype(dp_ref.dtype)
            acc = acc + jnp.sum(dyh * xh * sz, axis=0, keepdims=True)

        @pl.when(i == 0)
        def _():
            dnw_ref[...] = jnp.zeros_like(dnw_ref)

        dnw_ref[...] += acc

    vec = pl.BlockSpec((1, DH), lambda i: (0, 0))
    return pl.pallas_call(
        body, grid=(t // tm,),
        in_specs=[pl.BlockSpec(memory_space=pl.ANY), pl.BlockSpec((tm, DQ), lambda i: (i, 1)),
                  pl.BlockSpec((tm, DQ), lambda i: (i, 0)), pl.BlockSpec((tm, DQ), lambda i: (i, 5)), vec],
        out_specs=[pl.BlockSpec((tm, DQ), lambda i: (i, 0)), pl.BlockSpec((tm, DQ), lambda i: (i, 5)), vec],
        out_shape=[jax.ShapeDtypeStruct((t, DQ), F32), jax.ShapeDtypeStruct(dproj.shape, dproj.dtype),
                   jax.ShapeDtypeStruct((1, DH), F32)],
        input_output_aliases={0: 1}, compiler_params=_cp(("arbitrary",)), name=name)(dproj, dy, o, proj, nw)


def _shifted(first, second, s, lane):
    if s == 0:
        return first
    return jnp.where(lane < LANES - s, pltpu.roll(first, LANES - s, 1), pltpu.roll(second, LANES - s, 1))


def unshard_cols(g8, w, widths, name):
    _, r, wp = g8.shape
    rb = _pick(r, (256, 128, 64, 32, 16))

    def body(g_ref, *o_refs):
        lane = lax.broadcasted_iota(jnp.int32, (rb, LANES), 1)
        zeros = jnp.zeros((rb, LANES), F32)

        def src(j, ta):
            if j >= NDEV or ta * LANES >= wp:
                return zeros
            return g_ref[j, :, ta * LANES:(ta + 1) * LANES].astype(F32)

        base = 0
        for o_ref, width in zip(o_refs, widths):
            for b in range(width // LANES):
                c0 = base + b * LANES
                if c0 >= NDEV * w:
                    tile = zeros
                else:
                    j0, o0 = divmod(c0, w)
                    n0 = min(w - o0, LANES)
                    ta, s = divmod(o0, LANES)
                    tile = _shifted(src(j0, ta), src(j0, ta + 1), s, lane)
                    if n0 < LANES:
                        nxt = pltpu.roll(src(j0 + 1, 0), n0, 1) if j0 + 1 < NDEV else zeros
                        tile = jnp.where(lane < n0, tile, nxt)
                o_ref[:, b * LANES:(b + 1) * LANES] = tile.astype(o_ref.dtype)
            base += width

    return pl.pallas_call(
        body, grid=(r // rb,), in_specs=[pl.BlockSpec((NDEV, rb, wp), lambda i: (0, i, 0))],
        out_specs=[pl.BlockSpec((rb, width), lambda i: (i, 0)) for width in widths],
        out_shape=[jax.ShapeDtypeStruct((r, width), g8.dtype) for width in widths],
        compiler_params=_cp(("parallel",)), name=name)(g8)


def shard_cols(parts, w, name):
    r = parts[0].shape[0]
    wp = _lane_pad(w)
    rb = _pick(r, (256, 128, 64, 32, 16))
    tiles_of = [p.shape[1] // LANES for p in parts]

    def body(*refs):
        p_refs, o_ref = refs[:-1], refs[-1]
        lane = lax.broadcasted_iota(jnp.int32, (rb, LANES), 1)
        zeros = jnp.zeros((rb, LANES), F32)

        def glob(tile_idx):
            for p_ref, n_tiles in zip(p_refs, tiles_of):
                if tile_idx < n_tiles:
                    return p_ref[:, tile_idx * LANES:(tile_idx + 1) * LANES].astype(F32)
                tile_idx -= n_tiles
            return zeros

        for j in range(NDEV):
            for a in range(wp // LANES):
                nv = min(w - a * LANES, LANES)
                tb, s = divmod(w * j + a * LANES, LANES)
                tile = _shifted(glob(tb), glob(tb + 1), s, lane)
                if nv < LANES:
                    tile = jnp.where(lane < nv, tile, 0.0)
                o_ref[j, :, a * LANES:(a + 1) * LANES] = tile.astype(o_ref.dtype)

    return pl.pallas_call(
        body, grid=(r // rb,), in_specs=[pl.BlockSpec((rb, p.shape[1]), lambda i: (i, 0)) for p in parts],
        out_specs=pl.BlockSpec((NDEV, rb, wp), lambda i: (0, i, 0)),
        out_shape=jax.ShapeDtypeStruct((NDEV, r, wp), GRAD_DTYPE),
        compiler_params=_cp(("parallel",)), name=name)(*parts)


def _me_and_peers():
    mx, my, mc = lax.axis_index("x"), lax.axis_index("y"), lax.axis_index("c")
    me = 4 * mx + 2 * my + mc
    peers = []
    for kk in range(1, NDEV):
        px = 1 - mx if kk & 4 else mx
        py = 1 - my if kk & 2 else my
        pc = 1 - mc if kk & 1 else mc
        peers.append(((px, py, pc), 4 * px + 2 * py + pc))
    return me, peers


def _push_copies(x_refs, o_refs, send_sems, recv_sems, local_sems, scatter):
    me, peers = _me_and_peers()
    npeer = NDEV - 1
    local, sends, recvs = [], [], []
    for a, (x_ref, o_ref) in enumerate(zip(x_refs, o_refs)):
        local.append(pltpu.make_async_copy(x_ref.at[me] if scatter else x_ref, o_ref.at[me], local_sems.at[a]))
        for kk, (peer, pidx) in enumerate(peers):
            src = x_ref.at[pidx] if scatter else x_ref
            sems = dict(send_sem=send_sems.at[a * npeer + kk], recv_sem=recv_sems.at[a * npeer + kk],
                        device_id=peer, device_id_type=pl.DeviceIdType.MESH)
            sends.append(pltpu.make_async_remote_copy(src_ref=src, dst_ref=o_ref.at[me], **sems))
            recvs.append(pltpu.make_async_remote_copy(src_ref=src, dst_ref=o_ref.at[pidx], **sems))
    return local, sends, recvs


def _push_start(copies):
    local, sends, _ = copies
    for cp in local + sends:
        cp.start()


def _push_finish(copies):
    local, sends, recvs = copies
    for cp in recvs:
        cp.wait_recv()
    for cp in sends:
        cp.wait_send()
    for cp in local:
        cp.wait()


def _push_out_shapes(xs, scatter):
    return [jax.ShapeDtypeStruct(x.shape if scatter else (NDEV,) + x.shape, x.dtype) for x in xs]


def _push_sems(n):
    return [pltpu.SemaphoreType.DMA((n * (NDEV - 1),)), pltpu.SemaphoreType.DMA((n * (NDEV - 1),)),
            pltpu.SemaphoreType.DMA((n,))]


def _push_to_all(xs, name, scatter):
    n = len(xs)

    def body(*refs):
        copies = _push_copies(refs[:n], refs[n:2 * n], *refs[2 * n:], scatter=scatter)
        _push_start(copies)
        _push_finish(copies)

    any_spec = pl.BlockSpec(memory_space=pl.ANY)
    return pl.pallas_call(
        body, in_specs=[any_spec] * n, out_specs=[any_spec] * n, out_shape=_push_out_shapes(xs, scatter),
        scratch_shapes=_push_sems(n), name=name)(*xs)


def adamw(recv, w, m, v, name):
    rows, cols = w.shape
    c1 = 1.0 - ADAM_B1 ** ADAM_STEP
    c2 = 1.0 - ADAM_B2 ** ADAM_STEP
    cap = ADAM_BLOCK_BYTES // (NDEV * cols * 4)
    rb = _pick(rows, [p for p in (2048, 1024, 512, 256, 128, 64, 32, 16, 8) if p <= cap])

    def body(r_ref, w_ref, m_ref, v_ref, g_ref, d_ref, m2_ref, v2_ref):
        g = r_ref[0].astype(F32)
        for j in range(1, NDEV):
            g = g + r_ref[j].astype(F32)
        m2 = ADAM_B1 * m_ref[...] + (1.0 - ADAM_B1) * g
        v2 = ADAM_B2 * v_ref[...] + (1.0 - ADAM_B2) * (g * g)
        g_ref[...] = g
        m2_ref[...] = m2
        v2_ref[...] = v2
        d_ref[...] = -ADAM_LR * ((m2 / c1) / (jnp.sqrt(v2 / c2) + ADAM_EPS) + ADAM_WD * w_ref[...])

    blk = pl.BlockSpec((rb, cols), lambda i: (i, 0))
    return pl.pallas_call(
        body, grid=(rows // rb,),
        in_specs=[pl.BlockSpec((NDEV, rb, cols), lambda i: (0, i, 0)), blk, blk, blk],
        out_specs=[blk, blk, blk, blk], out_shape=[jax.ShapeDtypeStruct((rows, cols), F32)] * 4,
        compiler_params=_cp(("parallel",)), name=name)(recv, w, m, v)


SMALL_SHARDED = ("meta_tokens", "conv_dw_w", "dn_conv_w")
SMALL_REPLICATED = ("norm_mix_w", "conv_dw_b", "conv_ln_w", "conv_ln_b", "dn_A_log", "dn_dt_bias", "dn_norm_w",
                    "norm_ffn_w", "final_norm_w")
PARAM_ORDER = ("meta_tokens", "norm_mix_w", "w_in", "conv_dw_w", "conv_dw_b", "conv_ln_w", "conv_ln_b", "dn_conv_w",
               "dn_A_log", "dn_dt_bias", "dn_norm_w", "w_out", "norm_ffn_w", "ffn_w_gu", "ffn_w_down", "final_norm_w")


def _pack_small(parts, axis):
    flat = jnp.concatenate(parts, axis=axis)
    n = flat.shape[axis]
    total = -(-n // (8 * LANES)) * (8 * LANES)
    pad = [(0, 0)] * flat.ndim
    pad[axis] = (0, total - n)
    flat = jnp.pad(flat, pad)
    return flat.reshape(flat.shape[:axis] + (total // LANES, LANES))


def _unshard_last(g8):
    moved = jnp.moveaxis(g8, 0, -2)
    return moved.reshape(moved.shape[:-2] + (-1,))


def _per_destination_last(full):
    split = full.reshape(full.shape[:-1] + (NDEV, full.shape[-1] // NDEV))
    return jnp.moveaxis(split, -2, 0).reshape(NDEV, -1)


def _lane_row(vec4, width):
    return jnp.pad(vec4, (NH, width - 2 * NH))[None]


def kernel(x, meta_tokens, norm_mix_w, w_in, conv_dw_w, conv_dw_b, conv_ln_w, conv_ln_b, dn_conv_w, dn_A_log, dn_dt_bias, dn_norm_w, w_out, norm_ffn_w, ffn_w_gu, ffn_w_down, final_norm_w, loss_target, m_meta_tokens, m_norm_mix_w, m_w_in, m_conv_dw_w, m_conv_dw_b, m_conv_ln_w, m_conv_ln_b, m_dn_conv_w, m_dn_A_log, m_dn_dt_bias, m_dn_norm_w, m_w_out, m_norm_ffn_w, m_ffn_w_gu, m_ffn_w_down, m_final_norm_w, v_meta_tokens, v_norm_mix_w, v_w_in, v_conv_dw_w, v_conv_dw_b, v_conv_ln_w, v_conv_ln_b, v_dn_conv_w, v_dn_A_log, v_dn_dt_bias, v_dn_norm_w, v_w_out, v_norm_ffn_w, v_ffn_w_gu, v_ffn_w_down, v_final_norm_w):
    weights = dict(meta_tokens=meta_tokens, norm_mix_w=norm_mix_w, w_in=w_in, conv_dw_w=conv_dw_w, conv_dw_b=conv_dw_b,
                   conv_ln_w=conv_ln_w, conv_ln_b=conv_ln_b, dn_conv_w=dn_conv_w, dn_A_log=dn_A_log,
                   dn_dt_bias=dn_dt_bias, dn_norm_w=dn_norm_w, w_out=w_out, norm_ffn_w=norm_ffn_w, ffn_w_gu=ffn_w_gu,
                   ffn_w_down=ffn_w_down, final_norm_w=final_norm_w)
    m_in = dict(meta_tokens=m_meta_tokens, norm_mix_w=m_norm_mix_w, w_in=m_w_in, conv_dw_w=m_conv_dw_w,
                conv_dw_b=m_conv_dw_b, conv_ln_w=m_conv_ln_w, conv_ln_b=m_conv_ln_b, dn_conv_w=m_dn_conv_w,
                dn_A_log=m_dn_A_log, dn_dt_bias=m_dn_dt_bias, dn_norm_w=m_dn_norm_w, w_out=m_w_out,
                norm_ffn_w=m_norm_ffn_w, ffn_w_gu=m_ffn_w_gu, ffn_w_down=m_ffn_w_down, final_norm_w=m_final_norm_w)
    v_in = dict(meta_tokens=v_meta_tokens, norm_mix_w=v_norm_mix_w, w_in=v_w_in, conv_dw_w=v_conv_dw_w,
                conv_dw_b=v_conv_dw_b, conv_ln_w=v_conv_ln_w, conv_ln_b=v_conv_ln_b, dn_conv_w=v_dn_conv_w,
                dn_A_log=v_dn_A_log, dn_dt_bias=v_dn_dt_bias, dn_norm_w=v_dn_norm_w, w_out=v_w_out,
                norm_ffn_w=v_norm_ffn_w, ffn_w_gu=v_ffn_w_gu, ffn_w_down=v_ffn_w_down, final_norm_w=v_final_norm_w)

    depth = w_in.shape[0]
    seq = x.shape[1]
    t = _padded_rows(seq)
    rows_d = depth * D
    win_w, gu_w = w_in.shape[2], ffn_w_gu.shape[2]
    win_wp, gu_wp = _lane_pad(win_w), _lane_pad(gu_w)

    def pad_cols(a, wp):
        return jnp.pad(a, ((0, 0), (0, 0), (0, wp - a.shape[2]))).reshape(rows_d, wp)

    def rows2d(a):
        return a.reshape(-1, a.shape[2])

    small_shards = [weights[n] for n in SMALL_SHARDED]
    win_p = pad_cols(w_in, win_wp).astype(MXU_DTYPE).reshape(depth, D, win_wp)
    gu_p = pad_cols(ffn_w_gu, gu_wp).astype(MXU_DTYPE).reshape(depth, D, gu_wp)
    wout_b, wdown_b = w_out.astype(MXU_DTYPE), ffn_w_down.astype(MXU_DTYPE)

    got = [dict() for _ in range(depth)]
    got[0]["win"], g_small = _push_to_all([win_p[0], _pack_small([s.reshape(-1) for s in small_shards], 0)],
                                          "gather_first", scatter=False)
    wts = []
    small_flat, off, small_full = g_small.reshape(NDEV, -1), 0, {}
    for n, s in zip(SMALL_SHARDED, small_shards):
        small_full[n] = _unshard_last(small_flat[:, off:off + s.size].reshape((NDEV,) + s.shape))
        off += s.size
    cdw32 = jnp.pad(small_full["conv_dw_w"], ((0, 0), (0, CONV_HALO - CONV_W), (0, 0)))
    dcw8 = jnp.pad(small_full["dn_conv_w"], ((0, 0), (0, DN_HALO - DN_W), (0, 0)))

    h = jnp.concatenate([jnp.zeros((FRONT, D), F32), small_full["meta_tokens"], x[0],
                         jnp.zeros((t - HEAD - seq, D), F32)], axis=0)
    tgt = jnp.pad(loss_target[0], ((HEAD, t - HEAD - seq), (0, 0)))

    saved = []
    for l in range(depth):
        nmw, nfw = norm_mix_w[l][None], norm_ffn_w[l][None]
        cdb, clw, clb = conv_dw_b[l][None], conv_ln_w[l][None], conv_ln_b[l][None]
        alog, dtb, dnw = _lane_row(dn_A_log[l], GATE_W), _lane_row(dn_dt_bias[l], GATE_W), dn_norm_w[l][None]
        w_main, w_gate_cols = unshard_cols(got[l]["win"], win_w, [PROJ_MAIN, GATE_W], "unshard_w_in")
        proj = mm([h], w_main, a_fn=_rms_apply, a_vecs=[nmw], name="mm_proj")
        pg = mm([h], w_gate_cols, a_fn=_rms_apply, a_vecs=[nmw], name="mm_proj_gate")
        own = [wout_b[l], wdown_b[l]] + ([gu_p[0]] if l == 0 else [])
        ybuf, u1, came = conv_fwd(proj, cdw32[l], cdb, clw, clb, seq, "conv_fwd_gather", push=own)
        got[l]["out"], got[l]["down"] = came[:2]
        if l == 0:
            got[0]["gu"] = came[2]
        qkv = dn_pre_fwd(proj, dcw8[l], "dn_pre_fwd")
        bg = gate_fwd(pg, alog, dtb, seq, "gate_fwd")
        if l + 1 < depth:
            (o, s_hist, m_hist, u_all, w_all), (got[l + 1]["win"], got[l + 1]["gu"]) = delta_fwd(
                qkv, bg, "delta_fwd_gather", push=[win_p[l + 1], gu_p[l + 1]])
        else:
            (o, s_hist, m_hist, u_all, w_all), _ = delta_fwd(qkv, bg, "delta_fwd")
        w_g, w_u = unshard_cols(got[l]["gu"], gu_w, [DFF, DFF], "unshard_w_gu")
        wl = dict(main=w_main, gate=w_gate_cols, wg=w_g, wu=w_u, out=got[l]["out"].reshape(D, D),
                  down=got[l]["down"].reshape(DFF, D))
        wts.append(wl)
        ybuf = dn_post_fwd(ybuf, o, proj, dnw, "dn_post_fwd")
        h_mid = mm(ybuf, wl["out"], extras=[h], out_fn=_add, name="mm_out")
        gate = mm([h_mid], wl["wg"], a_fn=_rms_apply, a_vecs=[nfw], name="mm_gate")
        up = mm([h_mid], wl["wu"], a_fn=_rms_apply, a_vecs=[nfw], name="mm_up")
        h_out = mm([gate, up], wl["down"], a_fn=_swiglu, extras=[h_mid], out_fn=_add, tile_cap=ROW_TILE, name="mm_down")
        saved.append(dict(h=h, proj=proj, pg=pg, ybuf=ybuf, u1=u1, qkv=qkv, bg=bg, o=o, s_hist=s_hist,
                          m_hist=m_hist, u_all=u_all, w_all=w_all, h_mid=h_mid, gate=gate, up=up,
                          nmw=nmw, nfw=nfw, clw=clw, clb=clb, alog=alog, dtb=dtb, dnw=dnw))
        h = h_out

    dh, loss_part, d_final = loss_bwd(h, tgt, final_norm_w[None], seq, "loss_bwd")
    loss = lax.psum(loss_part[0, 0], MESH_AXES)

    per_layer = ("norm_mix_w", "conv_dw_w", "conv_dw_b", "conv_ln_w", "conv_ln_b", "dn_conv_w", "dn_A_log", "dn_dt_bias",
                 "dn_norm_w", "norm_ffn_w")
    grads = {n: [None] * depth for n in per_layer}
    received = [None] * depth
    pending = None
    dw_mm = lambda a, b, name, **kw: mm(a, b, ta=True, out_dtypes=(GRAD_DTYPE,), name=name, **kw)
    for l in reversed(range(depth)):
        s, wl = saved[l], wts[l]
        dgate, dup = mm(dh, wl["down"], tb=True, extras=[s["gate"], s["up"]], out_fn=_swiglu_bwd,
                        out_dtypes=(MXU_DTYPE, MXU_DTYPE), tile_cap=ROW_TILE, name="mm_down_dx")
        d_down = dw_mm([s["gate"], s["up"]], dh, "mm_down_dw", a_fn=_swiglu, tile_cap=ROW_TILE)
        dhn2 = mm(dup, wl["wu"], tb=True, plus=[(dgate, wl["wg"])], name="mm_gate_up_dx")
        dh_mid, dnfw = rms_bwd(dhn2, s["h_mid"], s["nfw"], dh, "rms_ffn_bwd")
        d_wg = dw_mm([s["h_mid"]], dgate, "mm_gate_dw", a_fn=_rms_apply, a_vecs=[s["nfw"]])
        d_wu = dw_mm([s["h_mid"]], dup, "mm_up_dw", a_fn=_rms_apply, a_vecs=[s["nfw"]])
        dy = mm(dh_mid, wl["out"], tb=True, name="mm_out_dx")
        d_out = dw_mm(s["ybuf"], dh_mid, "mm_out_dw")
        early = [shard_cols([d_wg, d_wu], gu_w, "shard_w_gu"), d_out.reshape(NDEV, D // NDEV, D),
                 d_down.reshape(NDEV, DFF // NDEV, D)]
        (dproj, dcdw, dcdb, dclw, dclb), early_came = conv_bwd(
            dy, s["u1"], s["proj"], cdw32[l], s["clw"], s["clb"], seq, "conv_bwd" if l else "conv_bwd_exchange",
            push=() if l else early)
        do, dproj, ddnw = dn_post_bwd(dproj, dy, s["o"], s["proj"], s["dnw"], "dn_post_bwd")
        delta_args = (s["qkv"], s["bg"], do, s["s_hist"], s["m_hist"], s["u_all"], s["w_all"])
        if pending is None:
            dqkv, dbg, _ = delta_bwd(*delta_args, "delta_bwd")
        else:
            dqkv, dbg, received[l + 1] = delta_bwd(*delta_args, "delta_bwd_exchange", push=pending)
        dproj, ddcw = dn_pre_bwd(dproj, dqkv, s["proj"], dcw8[l], "dn_pre_bwd")
        dpg, dalog, ddtb = gate_bwd(dbg, s["pg"], s["alog"], s["dtb"], seq, "gate_bwd")
        dhn_gate = mm(dpg, wl["gate"], tb=True, name="mm_proj_gate_dx")
        dhn = mm(dproj, wl["main"], tb=True, extras=[dhn_gate], out_fn=_add, name="mm_proj_dx")
        d_main = dw_mm([s["h"]], dproj, "mm_proj_dw", a_fn=_rms_apply, a_vecs=[s["nmw"]])
        d_gate_cols = dw_mm([s["h"]], dpg, "mm_proj_gate_dw", a_fn=_rms_apply, a_vecs=[s["nmw"]])
        pending = [shard_cols([d_main, d_gate_cols], win_w, "shard_w_in")] + (early if l else [])
        dh, dnmw = rms_bwd(dhn, s["h"], s["nmw"], dh_mid, "rms_mix_bwd")
        grads["norm_mix_w"][l] = dnmw[0]
        grads["norm_ffn_w"][l] = dnfw[0]
        grads["conv_dw_w"][l] = dcdw[:CONV_W]
        grads["conv_dw_b"][l] = dcdb[0]
        grads["conv_ln_w"][l] = dclw[0]
        grads["conv_ln_b"][l] = dclb[0]
        grads["dn_conv_w"][l] = ddcw[:DN_W]
        grads["dn_A_log"][l] = dalog[0, NH:2 * NH]
        grads["dn_dt_bias"][l] = ddtb[0, NH:2 * NH]
        grads["dn_norm_w"][l] = ddnw[0]

    grad_x = dh[HEAD:HEAD + seq][None]
    full = {n: jnp.stack(g) for n, g in grads.items()}
    full["meta_tokens"] = dh[FRONT:HEAD]
    full["final_norm_w"] = d_final[0]

    send_small = _pack_small(
        [_per_destination_last(full[n]) for n in SMALL_SHARDED]
        + [jnp.broadcast_to(full[n].reshape(1, -1), (NDEV, full[n].size)) for n in SMALL_REPLICATED], 1)
    r_win0, r_small = _push_to_all(pending + [send_small], "exchange_last", scatter=True)
    received[0] = [r_win0, *early_came]
    r_win, r_gu, r_wout, r_down = (jnp.concatenate([received[l][k] for l in range(depth)], axis=1) for k in range(4))

    small_names = SMALL_SHARDED + SMALL_REPLICATED
    pack_local = lambda tree: _pack_small([tree[n].reshape(-1) for n in small_names], 0)
    results = {}

    def run_adamw(name, recv, prep, finish):
        outs = adamw(recv, prep(weights[name]), prep(m_in[name]), prep(v_in[name]), "adamw_" + name)
        results[name] = [finish(o) for o in outs]

    run_adamw("w_in", r_win, lambda a: pad_cols(a, win_wp),
              lambda o: o[:, :win_w].reshape(depth, D, win_w))
    run_adamw("ffn_w_gu", r_gu, lambda a: pad_cols(a, gu_wp), lambda o: o[:, :gu_w].reshape(depth, D, gu_w))
    run_adamw("w_out", r_wout, rows2d, lambda o: o.reshape(w_out.shape))
    run_adamw("ffn_w_down", r_down, rows2d, lambda o: o.reshape(ffn_w_down.shape))
    small_outs = adamw(r_small, pack_local(weights), pack_local(m_in), pack_local(v_in), "adamw_small")
    for kind in range(4):
        flat, off = small_outs[kind].reshape(-1), 0
        for n in small_names:
            wgt = weights[n]
            results.setdefault(n, [None] * 4)[kind] = flat[off:off + wgt.size].reshape(wgt.shape)
            off += wgt.size

    return (loss, grad_x, *[results[n][0] for n in PARAM_ORDER], *[results[n][1] for n in PARAM_ORDER],
            *[results[n][2] for n in PARAM_ORDER], *[results[n][3] for n in PARAM_ORDER])
```
